```python
import jax, jax.numpy as jnp
from jax import lax
import numpy as np

D_MODEL = 2048
BATCH = 8
SEQ = 4096
DEPTH = 2

MIX_WIDTH = D_MODEL
MLA_HEADS = D_MODEL // 256
QK_NOPE_DIM = 128
QK_ROPE_DIM = 64
QK_HEAD_DIM = QK_NOPE_DIM + QK_ROPE_DIM
V_HEAD_DIM = 128
Q_LORA_RANK = 512
KV_LORA_RANK = 256
ATTN_WIDTH = MLA_HEADS * V_HEAD_DIM
GM_WIDTH = MIX_WIDTH - ATTN_WIDTH
GM_GROUPS = D_MODEL // 256
GM_GROUP_DIM = GM_WIDTH // GM_GROUPS
CHUNK = 128
D_FF = 128 * ((8 * D_MODEL // 3 + 127) // 128)
PLE_DIM = 256
ROPE_BASE = 10000.0
EPS = 1e-6
Q_BLOCK = 128
IN_COLS = Q_LORA_RANK + KV_LORA_RANK + QK_ROPE_DIM + 2 * GM_WIDTH
SPLITS = (Q_LORA_RANK,
          Q_LORA_RANK + KV_LORA_RANK,
          Q_LORA_RANK + KV_LORA_RANK + QK_ROPE_DIM,
          Q_LORA_RANK + KV_LORA_RANK + QK_ROPE_DIM + GM_WIDTH)

kernel_name = "hybrid_mla_gmlp_macaron_ple"


def rms_norm(x, g):
    xf = x.astype(jnp.float32)
    y = xf * lax.rsqrt(jnp.mean(xf * xf, axis=-1, keepdims=True) + EPS)
    return (y * g.astype(jnp.float32)).astype(x.dtype)


def swiglu(x, w1, w3, w2):
    return (jax.nn.silu(x @ w1) * (x @ w3)) @ w2


def rope_tables(positions):
    inv_freq = ROPE_BASE ** (-jnp.arange(0, QK_ROPE_DIM, 2, dtype=jnp.float32) / QK_ROPE_DIM)
    ang = positions.astype(jnp.float32)[..., None] * inv_freq
    return jnp.cos(ang)[:, :, None, :], jnp.sin(ang)[:, :, None, :]


def apply_rope(x, cos, sin):
    x1, x2 = jnp.split(x.astype(jnp.float32), 2, axis=-1)
    out = jnp.concatenate([x1 * cos - x2 * sin, x2 * cos + x1 * sin], axis=-1)
    return out.astype(x.dtype)


def causal_block_attention(q, k, v):
    b, s, h, dqk = q.shape
    nb = s // Q_BLOCK
    scale = dqk ** -0.5
    qb = q.reshape(b, nb, Q_BLOCK, h, dqk).transpose(1, 0, 2, 3, 4)
    key_pos = jnp.arange(s)
    neg = jnp.finfo(jnp.float32).min

    def one_block(args):
        q_blk, blk = args
        q_pos = blk * Q_BLOCK + jnp.arange(Q_BLOCK)
        scores = jnp.einsum('bqhd,bkhd->bhqk', q_blk, k,
                            preferred_element_type=jnp.float32) * scale
        mask = key_pos[None, :] <= q_pos[:, None]
        scores = jnp.where(mask[None, None], scores, neg)
        probs = jax.nn.softmax(scores, axis=-1).astype(v.dtype)
        return jnp.einsum('bhqk,bkhd->bqhd', probs, v)

    out = lax.map(one_block, (qb, jnp.arange(nb)))
    return out.transpose(1, 0, 2, 3, 4).reshape(b, s, h, v.shape[-1])


def mla_mixer(c_q, c_kv, k_rope_raw, cos, sin, q_a_norm, w_uq, kv_a_norm, w_ukv, q_norm, k_norm):
    b, s, _ = c_q.shape
    q = (rms_norm(c_q, q_a_norm) @ w_uq).reshape(b, s, MLA_HEADS, QK_HEAD_DIM)
    kv = (rms_norm(c_kv, kv_a_norm) @ w_ukv).reshape(b, s, MLA_HEADS, QK_NOPE_DIM + V_HEAD_DIM)
    k_nope, v = jnp.split(kv, [QK_NOPE_DIM], axis=-1)
    k_rope = jnp.broadcast_to(k_rope_raw[:, :, None, :], (b, s, MLA_HEADS, QK_ROPE_DIM))
    k = jnp.concatenate([k_nope, k_rope], axis=-1)
    q = rms_norm(q, q_norm)
    k = rms_norm(k, k_norm)
    q = jnp.concatenate([q[..., :QK_NOPE_DIM], apply_rope(q[..., QK_NOPE_DIM:], cos, sin)], axis=-1)
    k = jnp.concatenate([k[..., :QK_NOPE_DIM], apply_rope(k[..., QK_NOPE_DIM:], cos, sin)], axis=-1)
    return causal_block_attention(q, k, v).reshape(b, s, ATTN_WIDTH)


def gmlp_mixer(u, v, v_norm, w_s, b_s):
    b, s, _ = u.shape
    u = jax.nn.gelu(u)
    v = rms_norm(jax.nn.gelu(v), v_norm)
    vc = v.reshape(b, s // CHUNK, CHUNK, GM_GROUPS, GM_GROUP_DIM)
    tril = jnp.tril(jnp.ones((CHUNK, CHUNK), dtype=bool))
    w_causal = jnp.where(tril[None], w_s, jnp.zeros_like(w_s))
    gate = jnp.einsum('gts,bcsgd->bctgd', w_causal, vc) + b_s.T[None, None, :, :, None]
    return u * gate.reshape(b, s, GM_WIDTH)


def _fwd_setup_inputs(seed: int = 0) -> dict:
    key = jax.random.key(seed)
    ks = jax.random.split(key, 32)
    f32 = jnp.float32

    def w(k, shape, fan_in):
        return jax.random.normal(k, shape, f32) * fan_in ** -0.5

    def g(k, shape):
        return 1.0 + 0.05 * jax.random.normal(k, shape, f32)

    L = DEPTH
    offsets = jax.random.randint(ks[2], (BATCH, 1), 0, 1024, dtype=jnp.int32)
    positions = (offsets + jnp.arange(SEQ, dtype=jnp.int32)[None, :]).astype(jnp.int32)
    return {
        "x": jax.random.normal(ks[0], (BATCH, SEQ, D_MODEL), f32),
        "p": jax.random.normal(ks[1], (DEPTH, BATCH, SEQ, PLE_DIM), f32),
        "positions": positions,
        "ffn_a_norm": g(ks[3], (L, D_MODEL)),
        "ffn_a_w1": w(ks[4], (L, D_MODEL, D_FF), D_MODEL),
        "ffn_a_w3": w(ks[5], (L, D_MODEL, D_FF), D_MODEL),
        "ffn_a_w2": w(ks[6], (L, D_FF, D_MODEL), D_FF),
        "mix_norm": g(ks[7], (L, D_MODEL)),
        "w_in": w(ks[8], (L, D_MODEL, IN_COLS), D_MODEL),
        "q_a_norm": g(ks[9], (L, Q_LORA_RANK)),
        "w_uq": w(ks[10], (L, Q_LORA_RANK, MLA_HEADS * QK_HEAD_DIM), Q_LORA_RANK),
        "kv_a_norm": g(ks[11], (L, KV_LORA_RANK)),
        "w_ukv": w(ks[12], (L, KV_LORA_RANK, MLA_HEADS * (QK_NOPE_DIM + V_HEAD_DIM)), KV_LORA_RANK),
        "q_norm": g(ks[13], (L, QK_HEAD_DIM)),
        "k_norm": g(ks[14], (L, QK_HEAD_DIM)),
        "gm_v_norm": g(ks[15], (L, GM_WIDTH)),
        "gm_ws": w(ks[16], (L, GM_GROUPS, CHUNK, CHUNK), CHUNK),
        "gm_bs": 1.0 + 0.1 * jax.random.normal(ks[17], (L, GM_GROUPS, CHUNK), f32),
        "attn_out_norm": g(ks[18], (L, ATTN_WIDTH)),
        "gm_out_norm": g(ks[19], (L, GM_WIDTH)),
        "w_out": w(ks[20], (L, MIX_WIDTH, D_MODEL), MIX_WIDTH),
        "ffn_b_norm": g(ks[21], (L, D_MODEL)),
        "ffn_b_w1": w(ks[22], (L, D_MODEL, D_FF), D_MODEL),
        "ffn_b_w3": w(ks[23], (L, D_MODEL, D_FF), D_MODEL),
        "ffn_b_w2": w(ks[24], (L, D_FF, D_MODEL), D_FF),
        "ple_gate_norm": g(ks[25], (L, D_MODEL)),
        "w_ple_gate": w(ks[26], (L, D_MODEL, D_MODEL), D_MODEL),
        "w_ple": w(ks[27], (L, PLE_DIM, D_MODEL), PLE_DIM),
        "ple_norm": g(ks[28], (L, D_MODEL)),
    }


def _fwd_reference(x, p, positions, ffn_a_norm, ffn_a_w1, ffn_a_w3, ffn_a_w2, mix_norm, w_in,
              q_a_norm, w_uq, kv_a_norm, w_ukv, q_norm, k_norm, gm_v_norm, gm_ws, gm_bs,
              attn_out_norm, gm_out_norm, w_out, ffn_b_norm, ffn_b_w1, ffn_b_w3, ffn_b_w2,
              ple_gate_norm, w_ple_gate, w_ple, ple_norm):
    cos, sin = rope_tables(positions)
    h = x
    for i in range(DEPTH):
        h = h + 0.5 * swiglu(rms_norm(h, ffn_a_norm[i]), ffn_a_w1[i], ffn_a_w3[i], ffn_a_w2[i])
        z = rms_norm(h, mix_norm[i]) @ w_in[i]
        c_q, c_kv, k_rope_raw, u, v = jnp.split(z, SPLITS, axis=-1)
        a_out = mla_mixer(c_q, c_kv, k_rope_raw, cos, sin, q_a_norm[i], w_uq[i],
                          kv_a_norm[i], w_ukv[i], q_norm[i], k_norm[i])
        g_out = gmlp_mixer(u, v, gm_v_norm[i], gm_ws[i], gm_bs[i])
        mixed = jnp.concatenate([rms_norm(a_out, attn_out_norm[i]),
                                 rms_norm(g_out, gm_out_norm[i])], axis=-1)
        h = h + mixed @ w_out[i]
        h = h + 0.5 * swiglu(rms_norm(h, ffn_b_norm[i]), ffn_b_w1[i], ffn_b_w3[i], ffn_b_w2[i])
        e = rms_norm(p[i] @ w_ple[i], ple_norm[i])
        gate = jax.nn.sigmoid(rms_norm(h, ple_gate_norm[i]) @ w_ple_gate[i])
        h = h + gate * e
    return h


import jax as _jax
import jax.numpy as _jnp

TWIN_FORMAT = 'train_step'
FWD_PARAMS = ['x', 'p', 'positions', 'ffn_a_norm', 'ffn_a_w1', 'ffn_a_w3', 'ffn_a_w2', 'mix_norm', 'w_in', 'q_a_norm', 'w_uq', 'kv_a_norm', 'w_ukv', 'q_norm', 'k_norm', 'gm_v_norm', 'gm_ws', 'gm_bs', 'attn_out_norm', 'gm_out_norm', 'w_out', 'ffn_b_norm', 'ffn_b_w1', 'ffn_b_w3', 'ffn_b_w2', 'ple_gate_norm', 'w_ple_gate', 'w_ple', 'ple_norm']
TWIN_WEIGHTS = ['ffn_a_norm', 'ffn_a_w1', 'ffn_a_w3', 'ffn_a_w2', 'mix_norm', 'w_in', 'q_a_norm', 'w_uq', 'kv_a_norm', 'w_ukv', 'q_norm', 'k_norm', 'gm_v_norm', 'gm_ws', 'gm_bs', 'attn_out_norm', 'gm_out_norm', 'w_out', 'ffn_b_norm', 'ffn_b_w1', 'ffn_b_w3', 'ffn_b_w2', 'ple_gate_norm', 'w_ple_gate', 'w_ple', 'ple_norm']
TWIN_DIFF_INPUT = 'x'
TWIN_INPUTS = ['x', 'p', 'positions', 'ffn_a_norm', 'ffn_a_w1', 'ffn_a_w3', 'ffn_a_w2', 'mix_norm', 'w_in', 'q_a_norm', 'w_uq', 'kv_a_norm', 'w_ukv', 'q_norm', 'k_norm', 'gm_v_norm', 'gm_ws', 'gm_bs', 'attn_out_norm', 'gm_out_norm', 'w_out', 'ffn_b_norm', 'ffn_b_w1', 'ffn_b_w3', 'ffn_b_w2', 'ple_gate_norm', 'w_ple_gate', 'w_ple', 'ple_norm', 'loss_target', 'm_ffn_a_norm', 'm_ffn_a_w1', 'm_ffn_a_w3', 'm_ffn_a_w2', 'm_mix_norm', 'm_w_in', 'm_q_a_norm', 'm_w_uq', 'm_kv_a_norm', 'm_w_ukv', 'm_q_norm', 'm_k_norm', 'm_gm_v_norm', 'm_gm_ws', 'm_gm_bs', 'm_attn_out_norm', 'm_gm_out_norm', 'm_w_out', 'm_ffn_b_norm', 'm_ffn_b_w1', 'm_ffn_b_w3', 'm_ffn_b_w2', 'm_ple_gate_norm', 'm_w_ple_gate', 'm_w_ple', 'm_ple_norm', 'v_ffn_a_norm', 'v_ffn_a_w1', 'v_ffn_a_w3', 'v_ffn_a_w2', 'v_mix_norm', 'v_w_in', 'v_q_a_norm', 'v_w_uq', 'v_kv_a_norm', 'v_w_ukv', 'v_q_norm', 'v_k_norm', 'v_gm_v_norm', 'v_gm_ws', 'v_gm_bs', 'v_attn_out_norm', 'v_gm_out_norm', 'v_w_out', 'v_ffn_b_norm', 'v_ffn_b_w1', 'v_ffn_b_w3', 'v_ffn_b_w2', 'v_ple_gate_norm', 'v_w_ple_gate', 'v_w_ple', 'v_ple_norm']
TWIN_OUTPUTS = ['loss', 'grad_x', 'grad_ffn_a_norm', 'grad_ffn_a_w1', 'grad_ffn_a_w3', 'grad_ffn_a_w2', 'grad_mix_norm', 'grad_w_in', 'grad_q_a_norm', 'grad_w_uq', 'grad_kv_a_norm', 'grad_w_ukv', 'grad_q_norm', 'grad_k_norm', 'grad_gm_v_norm', 'grad_gm_ws', 'grad_gm_bs', 'grad_attn_out_norm', 'grad_gm_out_norm', 'grad_w_out', 'grad_ffn_b_norm', 'grad_ffn_b_w1', 'grad_ffn_b_w3', 'grad_ffn_b_w2', 'grad_ple_gate_norm', 'grad_w_ple_gate', 'grad_w_ple', 'grad_ple_norm', 'delta_ffn_a_norm', 'delta_ffn_a_w1', 'delta_ffn_a_w3', 'delta_ffn_a_w2', 'delta_mix_norm', 'delta_w_in', 'delta_q_a_norm', 'delta_w_uq', 'delta_kv_a_norm', 'delta_w_ukv', 'delta_q_norm', 'delta_k_norm', 'delta_gm_v_norm', 'delta_gm_ws', 'delta_gm_bs', 'delta_attn_out_norm', 'delta_gm_out_norm', 'delta_w_out', 'delta_ffn_b_norm', 'delta_ffn_b_w1', 'delta_ffn_b_w3', 'delta_ffn_b_w2', 'delta_ple_gate_norm', 'delta_w_ple_gate', 'delta_w_ple', 'delta_ple_norm', 'new_m_ffn_a_norm', 'new_m_ffn_a_w1', 'new_m_ffn_a_w3', 'new_m_ffn_a_w2', 'new_m_mix_norm', 'new_m_w_in', 'new_m_q_a_norm', 'new_m_w_uq', 'new_m_kv_a_norm', 'new_m_w_ukv', 'new_m_q_norm', 'new_m_k_norm', 'new_m_gm_v_norm', 'new_m_gm_ws', 'new_m_gm_bs', 'new_m_attn_out_norm', 'new_m_gm_out_norm', 'new_m_w_out', 'new_m_ffn_b_norm', 'new_m_ffn_b_w1', 'new_m_ffn_b_w3', 'new_m_ffn_b_w2', 'new_m_ple_gate_norm', 'new_m_w_ple_gate', 'new_m_w_ple', 'new_m_ple_norm', 'new_v_ffn_a_norm', 'new_v_ffn_a_w1', 'new_v_ffn_a_w3', 'new_v_ffn_a_w2', 'new_v_mix_norm', 'new_v_w_in', 'new_v_q_a_norm', 'new_v_w_uq', 'new_v_kv_a_norm', 'new_v_w_ukv', 'new_v_q_norm', 'new_v_k_norm', 'new_v_gm_v_norm', 'new_v_gm_ws', 'new_v_gm_bs', 'new_v_attn_out_norm', 'new_v_gm_out_norm', 'new_v_w_out', 'new_v_ffn_b_norm', 'new_v_ffn_b_w1', 'new_v_ffn_b_w3', 'new_v_ffn_b_w2', 'new_v_ple_gate_norm', 'new_v_w_ple_gate', 'new_v_w_ple', 'new_v_ple_norm']
TWIN_LEAF_KINDS = {'loss': 'loss', 'grad_x': 'grad_x', 'grad_ffn_a_norm': 'grad_w', 'grad_ffn_a_w1': 'grad_w', 'grad_ffn_a_w3': 'grad_w', 'grad_ffn_a_w2': 'grad_w', 'grad_mix_norm': 'grad_w', 'grad_w_in': 'grad_w', 'grad_q_a_norm': 'grad_w', 'grad_w_uq': 'grad_w', 'grad_kv_a_norm': 'grad_w', 'grad_w_ukv': 'grad_w', 'grad_q_norm': 'grad_w', 'grad_k_norm': 'grad_w', 'grad_gm_v_norm': 'grad_w', 'grad_gm_ws': 'grad_w', 'grad_gm_bs': 'grad_w', 'grad_attn_out_norm': 'grad_w', 'grad_gm_out_norm': 'grad_w', 'grad_w_out': 'grad_w', 'grad_ffn_b_norm': 'grad_w', 'grad_ffn_b_w1': 'grad_w', 'grad_ffn_b_w3': 'grad_w', 'grad_ffn_b_w2': 'grad_w', 'grad_ple_gate_norm': 'grad_w', 'grad_w_ple_gate': 'grad_w', 'grad_w_ple': 'grad_w', 'grad_ple_norm': 'grad_w', 'delta_ffn_a_norm': 'delta_w', 'delta_ffn_a_w1': 'delta_w', 'delta_ffn_a_w3': 'delta_w', 'delta_ffn_a_w2': 'delta_w', 'delta_mix_norm': 'delta_w', 'delta_w_in': 'delta_w', 'delta_q_a_norm': 'delta_w', 'delta_w_uq': 'delta_w', 'delta_kv_a_norm': 'delta_w', 'delta_w_ukv': 'delta_w', 'delta_q_norm': 'delta_w', 'delta_k_norm': 'delta_w', 'delta_gm_v_norm': 'delta_w', 'delta_gm_ws': 'delta_w', 'delta_gm_bs': 'delta_w', 'delta_attn_out_norm': 'delta_w', 'delta_gm_out_norm': 'delta_w', 'delta_w_out': 'delta_w', 'delta_ffn_b_norm': 'delta_w', 'delta_ffn_b_w1': 'delta_w', 'delta_ffn_b_w3': 'delta_w', 'delta_ffn_b_w2': 'delta_w', 'delta_ple_gate_norm': 'delta_w', 'delta_w_ple_gate': 'delta_w', 'delta_w_ple': 'delta_w', 'delta_ple_norm': 'delta_w', 'new_m_ffn_a_norm': 'new_m', 'new_m_ffn_a_w1': 'new_m', 'new_m_ffn_a_w3': 'new_m', 'new_m_ffn_a_w2': 'new_m', 'new_m_mix_norm': 'new_m', 'new_m_w_in': 'new_m', 'new_m_q_a_norm': 'new_m', 'new_m_w_uq': 'new_m', 'new_m_kv_a_norm': 'new_m', 'new_m_w_ukv': 'new_m', 'new_m_q_norm': 'new_m', 'new_m_k_norm': 'new_m', 'new_m_gm_v_norm': 'new_m', 'new_m_gm_ws': 'new_m', 'new_m_gm_bs': 'new_m', 'new_m_attn_out_norm': 'new_m', 'new_m_gm_out_norm': 'new_m', 'new_m_w_out': 'new_m', 'new_m_ffn_b_norm': 'new_m', 'new_m_ffn_b_w1': 'new_m', 'new_m_ffn_b_w3': 'new_m', 'new_m_ffn_b_w2': 'new_m', 'new_m_ple_gate_norm': 'new_m', 'new_m_w_ple_gate': 'new_m', 'new_m_w_ple': 'new_m', 'new_m_ple_norm': 'new_m', 'new_v_ffn_a_norm': 'new_v', 'new_v_ffn_a_w1': 'new_v', 'new_v_ffn_a_w3': 'new_v', 'new_v_ffn_a_w2': 'new_v', 'new_v_mix_norm': 'new_v', 'new_v_w_in': 'new_v', 'new_v_q_a_norm': 'new_v', 'new_v_w_uq': 'new_v', 'new_v_kv_a_norm': 'new_v', 'new_v_w_ukv': 'new_v', 'new_v_q_norm': 'new_v', 'new_v_k_norm': 'new_v', 'new_v_gm_v_norm': 'new_v', 'new_v_gm_ws': 'new_v', 'new_v_gm_bs': 'new_v', 'new_v_attn_out_norm': 'new_v', 'new_v_gm_out_norm': 'new_v', 'new_v_w_out': 'new_v', 'new_v_ffn_b_norm': 'new_v', 'new_v_ffn_b_w1': 'new_v', 'new_v_ffn_b_w3': 'new_v', 'new_v_ffn_b_w2': 'new_v', 'new_v_ple_gate_norm': 'new_v', 'new_v_w_ple_gate': 'new_v', 'new_v_w_ple': 'new_v', 'new_v_ple_norm': 'new_v'}


def _forward(args):
    return _fwd_reference(*[args[k] for k in FWD_PARAMS])


def _output_shape():
    out = _jax.eval_shape(lambda: _forward(_fwd_setup_inputs(0)))
    return out.shape, out.dtype

N_MICROBATCH = 1
ADAM_LR = 0.001
ADAM_B1 = 0.9
ADAM_B2 = 0.999
ADAM_EPS = 1e-08
ADAM_WD = 0.01
ADAM_STEP = 10
PER_EXAMPLE_BATCH_AXIS = {'x': 0, 'p': 1, 'positions': 0, 'loss_target': 0}
SHARED_INPUTS = []
_WEIGHT_DTYPES = {'ffn_a_norm': _jnp.float32, 'ffn_a_w1': _jnp.float32, 'ffn_a_w3': _jnp.float32, 'ffn_a_w2': _jnp.float32, 'mix_norm': _jnp.float32, 'w_in': _jnp.float32, 'q_a_norm': _jnp.float32, 'w_uq': _jnp.float32, 'kv_a_norm': _jnp.float32, 'w_ukv': _jnp.float32, 'q_norm': _jnp.float32, 'k_norm': _jnp.float32, 'gm_v_norm': _jnp.float32, 'gm_ws': _jnp.float32, 'gm_bs': _jnp.float32, 'attn_out_norm': _jnp.float32, 'gm_out_norm': _jnp.float32, 'w_out': _jnp.float32, 'ffn_b_norm': _jnp.float32, 'ffn_b_w1': _jnp.float32, 'ffn_b_w3': _jnp.float32, 'ffn_b_w2': _jnp.float32, 'ple_gate_norm': _jnp.float32, 'w_ple_gate': _jnp.float32, 'w_ple': _jnp.float32, 'ple_norm': _jnp.float32}
MOMENT_SCALE = {'ffn_a_norm': 3.436602e+00, 'ffn_a_w1': 4.788921e-01, 'ffn_a_w3': 5.002466e-01, 'ffn_a_w2': 8.050034e-01, 'mix_norm': 4.234702e+00, 'w_in': 3.637660e+00, 'q_a_norm': 5.742858e-01, 'w_uq': 2.993803e-01, 'kv_a_norm': 1.259302e+01, 'w_ukv': 4.550384e+00, 'q_norm': 1.303791e+00, 'k_norm': 1.382350e+00, 'gm_v_norm': 4.008014e-01, 'gm_ws': 1.767952e-01, 'gm_bs': 4.349977e-01, 'attn_out_norm': 1.834291e+01, 'gm_out_norm': 1.722887e+01, 'w_out': 5.948643e+00, 'ffn_b_norm': 3.199023e+00, 'ffn_b_w1': 1.855242e-01, 'ffn_b_w3': 2.727686e-01, 'ffn_b_w2': 4.493331e-01, 'ple_gate_norm': 6.302049e-01, 'w_ple_gate': 3.876429e-01, 'w_ple': 1.543688e-01, 'ple_norm': 4.751459e+00}


def _to_microbatches(a, axis):
    t = _jnp.moveaxis(a, axis, 0)
    t = t.reshape((N_MICROBATCH, t.shape[0] // N_MICROBATCH) + t.shape[1:])
    return _jnp.moveaxis(t, 1, axis + 1)


def setup_inputs(seed: int = 0) -> dict:
    inp = _fwd_setup_inputs(seed)
    key = _jax.random.fold_in(_jax.random.key(seed), 7919)
    shape, _ = _output_shape()
    out = dict(inp)
    out["loss_target"] = _jax.random.normal(_jax.random.fold_in(key, 0), shape, _jnp.float32)
    for i, name in enumerate(TWIN_WEIGHTS):
        w = inp[name].astype(_jnp.float32)
        if MOMENT_SCALE is None:
            s = _jnp.sqrt(_jnp.mean(_jnp.square(w)) + 1e-30)
        else:
            s = MOMENT_SCALE[name]
        km, kv = _jax.random.split(_jax.random.fold_in(key, i + 1))
        out[name] = w
        out["m_" + name] = s * _jax.random.normal(km, w.shape, _jnp.float32)
        out["v_" + name] = (s * s) * _jax.random.uniform(kv, w.shape, _jnp.float32, 0.5, 1.5)
    if N_MICROBATCH > 1:
        for name, axis in PER_EXAMPLE_BATCH_AXIS.items():
            out[name] = _to_microbatches(out[name], axis)
    return {'x': out['x'], 'p': out['p'], 'positions': out['positions'], 'ffn_a_norm': out['ffn_a_norm'], 'ffn_a_w1': out['ffn_a_w1'], 'ffn_a_w3': out['ffn_a_w3'], 'ffn_a_w2': out['ffn_a_w2'], 'mix_norm': out['mix_norm'], 'w_in': out['w_in'], 'q_a_norm': out['q_a_norm'], 'w_uq': out['w_uq'], 'kv_a_norm': out['kv_a_norm'], 'w_ukv': out['w_ukv'], 'q_norm': out['q_norm'], 'k_norm': out['k_norm'], 'gm_v_norm': out['gm_v_norm'], 'gm_ws': out['gm_ws'], 'gm_bs': out['gm_bs'], 'attn_out_norm': out['attn_out_norm'], 'gm_out_norm': out['gm_out_norm'], 'w_out': out['w_out'], 'ffn_b_norm': out['ffn_b_norm'], 'ffn_b_w1': out['ffn_b_w1'], 'ffn_b_w3': out['ffn_b_w3'], 'ffn_b_w2': out['ffn_b_w2'], 'ple_gate_norm': out['ple_gate_norm'], 'w_ple_gate': out['w_ple_gate'], 'w_ple': out['w_ple'], 'ple_norm': out['ple_norm'], 'loss_target': out['loss_target'], 'm_ffn_a_norm': out['m_ffn_a_norm'], 'm_ffn_a_w1': out['m_ffn_a_w1'], 'm_ffn_a_w3': out['m_ffn_a_w3'], 'm_ffn_a_w2': out['m_ffn_a_w2'], 'm_mix_norm': out['m_mix_norm'], 'm_w_in': out['m_w_in'], 'm_q_a_norm': out['m_q_a_norm'], 'm_w_uq': out['m_w_uq'], 'm_kv_a_norm': out['m_kv_a_norm'], 'm_w_ukv': out['m_w_ukv'], 'm_q_norm': out['m_q_norm'], 'm_k_norm': out['m_k_norm'], 'm_gm_v_norm': out['m_gm_v_norm'], 'm_gm_ws': out['m_gm_ws'], 'm_gm_bs': out['m_gm_bs'], 'm_attn_out_norm': out['m_attn_out_norm'], 'm_gm_out_norm': out['m_gm_out_norm'], 'm_w_out': out['m_w_out'], 'm_ffn_b_norm': out['m_ffn_b_norm'], 'm_ffn_b_w1': out['m_ffn_b_w1'], 'm_ffn_b_w3': out['m_ffn_b_w3'], 'm_ffn_b_w2': out['m_ffn_b_w2'], 'm_ple_gate_norm': out['m_ple_gate_norm'], 'm_w_ple_gate': out['m_w_ple_gate'], 'm_w_ple': out['m_w_ple'], 'm_ple_norm': out['m_ple_norm'], 'v_ffn_a_norm': out['v_ffn_a_norm'], 'v_ffn_a_w1': out['v_ffn_a_w1'], 'v_ffn_a_w3': out['v_ffn_a_w3'], 'v_ffn_a_w2': out['v_ffn_a_w2'], 'v_mix_norm': out['v_mix_norm'], 'v_w_in': out['v_w_in'], 'v_q_a_norm': out['v_q_a_norm'], 'v_w_uq': out['v_w_uq'], 'v_kv_a_norm': out['v_kv_a_norm'], 'v_w_ukv': out['v_w_ukv'], 'v_q_norm': out['v_q_norm'], 'v_k_norm': out['v_k_norm'], 'v_gm_v_norm': out['v_gm_v_norm'], 'v_gm_ws': out['v_gm_ws'], 'v_gm_bs': out['v_gm_bs'], 'v_attn_out_norm': out['v_attn_out_norm'], 'v_gm_out_norm': out['v_gm_out_norm'], 'v_w_out': out['v_w_out'], 'v_ffn_b_norm': out['v_ffn_b_norm'], 'v_ffn_b_w1': out['v_ffn_b_w1'], 'v_ffn_b_w3': out['v_ffn_b_w3'], 'v_ffn_b_w2': out['v_ffn_b_w2'], 'v_ple_gate_norm': out['v_ple_gate_norm'], 'v_w_ple_gate': out['v_w_ple_gate'], 'v_w_ple': out['v_w_ple'], 'v_ple_norm': out['v_ple_norm']}


def _loss(weights, diff, rest, loss_target):
    with _jax.named_scope("forward"):
        args = {**rest, TWIN_DIFF_INPUT: diff, **{k: w.astype(_WEIGHT_DTYPES[k]) for k, w in weights.items()}}
        y = _forward(args)
    with _jax.named_scope("loss_head"):
        err = _jnp.square(y.astype(_jnp.float32) - loss_target)
        return 0.5 * _jnp.sum(_jnp.mean(err, axis=-1)) if err.ndim else 0.5 * err


def _adamw(w, g, m, v):
    m = ADAM_B1 * m + (1.0 - ADAM_B1) * g
    v = ADAM_B2 * v + (1.0 - ADAM_B2) * _jnp.square(g)
    m_hat = m / (1.0 - ADAM_B1 ** ADAM_STEP)
    v_hat = v / (1.0 - ADAM_B2 ** ADAM_STEP)
    delta = -ADAM_LR * (m_hat / (_jnp.sqrt(v_hat) + ADAM_EPS) + ADAM_WD * w)
    return delta, m, v


def reference(x, p, positions, ffn_a_norm, ffn_a_w1, ffn_a_w3, ffn_a_w2, mix_norm, w_in, q_a_norm, w_uq, kv_a_norm, w_ukv, q_norm, k_norm, gm_v_norm, gm_ws, gm_bs, attn_out_norm, gm_out_norm, w_out, ffn_b_norm, ffn_b_w1, ffn_b_w3, ffn_b_w2, ple_gate_norm, w_ple_gate, w_ple, ple_norm, loss_target, m_ffn_a_norm, m_ffn_a_w1, m_ffn_a_w3, m_ffn_a_w2, m_mix_norm, m_w_in, m_q_a_norm, m_w_uq, m_kv_a_norm, m_w_ukv, m_q_norm, m_k_norm, m_gm_v_norm, m_gm_ws, m_gm_bs, m_attn_out_norm, m_gm_out_norm, m_w_out, m_ffn_b_norm, m_ffn_b_w1, m_ffn_b_w3, m_ffn_b_w2, m_ple_gate_norm, m_w_ple_gate, m_w_ple, m_ple_norm, v_ffn_a_norm, v_ffn_a_w1, v_ffn_a_w3, v_ffn_a_w2, v_mix_norm, v_w_in, v_q_a_norm, v_w_uq, v_kv_a_norm, v_w_ukv, v_q_norm, v_k_norm, v_gm_v_norm, v_gm_ws, v_gm_bs, v_attn_out_norm, v_gm_out_norm, v_w_out, v_ffn_b_norm, v_ffn_b_w1, v_ffn_b_w3, v_ffn_b_w2, v_ple_gate_norm, v_w_ple_gate, v_w_ple, v_ple_norm):
    given = dict(x=x, p=p, positions=positions, ffn_a_norm=ffn_a_norm, ffn_a_w1=ffn_a_w1, ffn_a_w3=ffn_a_w3, ffn_a_w2=ffn_a_w2, mix_norm=mix_norm, w_in=w_in, q_a_norm=q_a_norm, w_uq=w_uq, kv_a_norm=kv_a_norm, w_ukv=w_ukv, q_norm=q_norm, k_norm=k_norm, gm_v_norm=gm_v_norm, gm_ws=gm_ws, gm_bs=gm_bs, attn_out_norm=attn_out_norm, gm_out_norm=gm_out_norm, w_out=w_out, ffn_b_norm=ffn_b_norm, ffn_b_w1=ffn_b_w1, ffn_b_w3=ffn_b_w3, ffn_b_w2=ffn_b_w2, ple_gate_norm=ple_gate_norm, w_ple_gate=w_ple_gate, w_ple=w_ple, ple_norm=ple_norm, loss_target=loss_target, m_ffn_a_norm=m_ffn_a_norm, m_ffn_a_w1=m_ffn_a_w1, m_ffn_a_w3=m_ffn_a_w3, m_ffn_a_w2=m_ffn_a_w2, m_mix_norm=m_mix_norm, m_w_in=m_w_in, m_q_a_norm=m_q_a_norm, m_w_uq=m_w_uq, m_kv_a_norm=m_kv_a_norm, m_w_ukv=m_w_ukv, m_q_norm=m_q_norm, m_k_norm=m_k_norm, m_gm_v_norm=m_gm_v_norm, m_gm_ws=m_gm_ws, m_gm_bs=m_gm_bs, m_attn_out_norm=m_attn_out_norm, m_gm_out_norm=m_gm_out_norm, m_w_out=m_w_out, m_ffn_b_norm=m_ffn_b_norm, m_ffn_b_w1=m_ffn_b_w1, m_ffn_b_w3=m_ffn_b_w3, m_ffn_b_w2=m_ffn_b_w2, m_ple_gate_norm=m_ple_gate_norm, m_w_ple_gate=m_w_ple_gate, m_w_ple=m_w_ple, m_ple_norm=m_ple_norm, v_ffn_a_norm=v_ffn_a_norm, v_ffn_a_w1=v_ffn_a_w1, v_ffn_a_w3=v_ffn_a_w3, v_ffn_a_w2=v_ffn_a_w2, v_mix_norm=v_mix_norm, v_w_in=v_w_in, v_q_a_norm=v_q_a_norm, v_w_uq=v_w_uq, v_kv_a_norm=v_kv_a_norm, v_w_ukv=v_w_ukv, v_q_norm=v_q_norm, v_k_norm=v_k_norm, v_gm_v_norm=v_gm_v_norm, v_gm_ws=v_gm_ws, v_gm_bs=v_gm_bs, v_attn_out_norm=v_attn_out_norm, v_gm_out_norm=v_gm_out_norm, v_w_out=v_w_out, v_ffn_b_norm=v_ffn_b_norm, v_ffn_b_w1=v_ffn_b_w1, v_ffn_b_w3=v_ffn_b_w3, v_ffn_b_w2=v_ffn_b_w2, v_ple_gate_norm=v_ple_gate_norm, v_w_ple_gate=v_w_ple_gate, v_w_ple=v_w_ple, v_ple_norm=v_ple_norm)
    weights = {n: given[n] for n in TWIN_WEIGHTS}
    shared = {n: given[n] for n in SHARED_INPUTS}
    per_example = {n: given[n] for n in ['x', 'p', 'positions']}
    grad_fn = _jax.value_and_grad(_loss, argnums=(0, 1))

    def one_microbatch(ex, loss_target):
        ex = dict(ex)
        diff = ex.pop(TWIN_DIFF_INPUT)
        return grad_fn(weights, diff, {**shared, **ex}, loss_target)

    if N_MICROBATCH == 1:
        loss, (grad_w, grad_x) = one_microbatch(per_example, given["loss_target"])
    else:
        def body(carry, xs):
            loss_sum, grad_sum = carry
            l_k, (gw_k, gx_k) = one_microbatch(xs[0], xs[1])
            with _jax.named_scope("update"):
                return (loss_sum + l_k, _jax.tree.map(_jnp.add, grad_sum, gw_k)), gx_k

        init = (_jnp.zeros((), _jnp.float32), _jax.tree.map(_jnp.zeros_like, weights))
        (loss, grad_w), grad_x = _jax.lax.scan(body, init, (per_example, given["loss_target"]))
    with _jax.named_scope("update"):
        delta_w, new_m, new_v = {}, {}, {}
        for n in TWIN_WEIGHTS:
            delta_w[n], new_m[n], new_v[n] = _adamw(weights[n], grad_w[n], given["m_" + n], given["v_" + n])
    return (loss, grad_x, *[grad_w[n] for n in TWIN_WEIGHTS], *[delta_w[n] for n in TWIN_WEIGHTS],
            *[new_m[n] for n in TWIN_WEIGHTS], *[new_v[n] for n in TWIN_WEIGHTS])
```

```python
import functools

import jax
import jax.numpy as jnp
from jax import lax
from jax.experimental import pallas as pl
from jax.experimental.pallas import tpu as pltpu

F32, BF16 = jnp.float32, jnp.bfloat16
EPS = 1e-6
N_DEV = 8
HEADS = 8
NOPE, ROPE, QK, VD = 128, 64, 192, 128
CHUNK = 128
GROUPS = 8
LANE = 128
ROPE_BASE = 10000.0
ADAM_LR, ADAM_B1, ADAM_B2, ADAM_EPS, ADAM_WD, ADAM_STEP = 0.001, 0.9, 0.999, 1e-08, 0.01, 10
AXES = ("x", "y", "c")
MESH = pl.DeviceIdType.MESH
ANY = pl.BlockSpec(memory_space=pl.ANY)

WEIGHTS = ['ffn_a_norm', 'ffn_a_w1', 'ffn_a_w3', 'ffn_a_w2', 'mix_norm', 'w_in', 'q_a_norm', 'w_uq', 'kv_a_norm',
           'w_ukv', 'q_norm', 'k_norm', 'gm_v_norm', 'gm_ws', 'gm_bs', 'attn_out_norm', 'gm_out_norm', 'w_out',
           'ffn_b_norm', 'ffn_b_w1', 'ffn_b_w3', 'ffn_b_w2', 'ple_gate_norm', 'w_ple_gate', 'w_ple', 'ple_norm']
BIG = ['ffn_a_w1', 'ffn_a_w3', 'ffn_a_w2', 'w_in', 'w_uq', 'w_ukv', 'w_out', 'ffn_b_w1', 'ffn_b_w3', 'ffn_b_w2',
       'w_ple_gate', 'w_ple']
SMALL = [n for n in WEIGHTS if n not in BIG]


def _pcall(body, **kw):
    return pl.pallas_call(body, **kw)


def _pick(n, cands):
    for c in cands:
        if n % c == 0:
            return c
    return n


def _rms(x, g):
    return x * lax.rsqrt(jnp.mean(x * x, axis=-1, keepdims=True) + EPS) * g


@jax.custom_vjp
def _bdot(x, w):
    return jnp.dot(x.astype(BF16), w.astype(BF16), preferred_element_type=F32)


def _bdot_fwd(x, w):
    return _bdot(x, w), (x, w)


def _bdot_bwd(res, dy):
    x, w = res
    dyb = dy.astype(BF16)
    dx = lax.dot_general(dyb, w.astype(BF16), (((1,), (1,)), ((), ())), preferred_element_type=F32)
    dw = lax.dot_general(x.astype(BF16), dyb, (((0,), (0,)), ((), ())), preferred_element_type=F32)
    return dx.astype(x.dtype), dw.astype(w.dtype)


_bdot.defvjp(_bdot_fwd, _bdot_bwd)


def _hdot(x, p):
    return jnp.dot(x, p, precision=lax.Precision.HIGHEST, preferred_element_type=F32)


def _flip(v, bit):
    return 1 - v if bit else v


def _all_gather(name, x):
    def body(x_ref, out_ref, send_sems, recv_sems, local_sem):
        cx, cy, cc = lax.axis_index("x"), lax.axis_index("y"), lax.axis_index("c")
        me, sibling = (cx, cy, cc), (cx, cy, 1 - cc)
        chips = [(1 - cx, cy), (cx, 1 - cy), (1 - cx, 1 - cy)]

        def slot(px, py, pc):
            return out_ref.at[4 * px + 2 * py + pc]

        def copy(k, block, to, src=None):
            return pltpu.make_async_remote_copy(
                src_ref=slot(*block) if src is None else src, dst_ref=slot(*block),
                send_sem=send_sems.at[k], recv_sem=recv_sems.at[k], device_id=to, device_id_type=MESH)

        mine = pltpu.make_async_copy(x_ref, slot(*me), local_sem)
        mine.start()
        first = [copy(0, me, sibling, src=x_ref)]
        first += [copy(1 + j, me, (*chip, cc), src=x_ref) for j, chip in enumerate(chips)]
        for cp in first:
            cp.start()
        passed = [copy(4 + j, (*chip, cc), sibling) for j, chip in enumerate(chips)]
        for j, chip in enumerate(chips):
            copy(1 + j, (*chip, cc), me).wait_recv()
            passed[j].start()
        copy(0, sibling, me).wait_recv()
        for j, chip in enumerate(chips):
            copy(4 + j, (*chip, 1 - cc), me).wait_recv()
        for cp in first + passed:
            cp.wait_send()
        mine.wait()

    return _pcall(
        body, name=name, out_shape=jax.ShapeDtypeStruct((N_DEV,) + x.shape, x.dtype),
        in_specs=[ANY], out_specs=ANY,
        scratch_shapes=[pltpu.SemaphoreType.DMA((7,)), pltpu.SemaphoreType.DMA((7,)), pltpu.SemaphoreType.DMA],
    )(x)


def _reduce_scatter(name, g):
    def body(g_ref, out_ref, send_sems, recv_sems, local_sem):
        cx, cy, cc = lax.axis_index("x"), lax.axis_index("y"), lax.axis_index("c")
        me = 4 * cx + 2 * cy + cc
        mine = pltpu.make_async_copy(g_ref.at[me], out_ref.at[me], local_sem)
        mine.start()
        copies = []
        for k in range(1, N_DEV):
            px, py, pc = _flip(cx, k & 4), _flip(cy, k & 2), _flip(cc, k & 1)
            peer = 4 * px + 2 * py + pc
            copies.append((
                pltpu.make_async_remote_copy(
                    src_ref=g_ref.at[peer], dst_ref=out_ref.at[me], send_sem=send_sems.at[k - 1],
                    recv_sem=recv_sems.at[k - 1], device_id=(px, py, pc), device_id_type=MESH),
                pltpu.make_async_remote_copy(
                    src_ref=g_ref.at[peer], dst_ref=out_ref.at[peer], send_sem=send_sems.at[k - 1],
                    recv_sem=recv_sems.at[k - 1], device_id=(px, py, pc), device_id_type=MESH)))
        for send, _ in copies:
            send.start()
        for _, recv in copies:
            recv.wait_recv()
        for send, _ in copies:
            send.wait_send()
        mine.wait()

    return _pcall(
        body, name=name, out_shape=jax.ShapeDtypeStruct(g.shape, g.dtype),
        in_specs=[ANY], out_specs=ANY,
        scratch_shapes=[pltpu.SemaphoreType.DMA((7,)), pltpu.SemaphoreType.DMA((7,)), pltpu.SemaphoreType.DMA],
    )(g)


_DIMS = {"nn": (((1,), (0,)), ((), ())), "nt": (((1,), (1,)), ((), ())), "tn": (((0,), (0,)), ((), ()))}


def _mm(name, a, b, mode, out_dtype=F32, res=None, alpha=1.0):
    if mode == "tn":
        K, M = a.shape
        N = b.shape[1]
    else:
        M, K = a.shape
        N = b.shape[0] if mode == "nt" else b.shape[1]
    tn = _pick(N, (1024, 512, 256))
    tm = _pick(M, (1024, 512, 256, 128) if tn <= 1024 else (512, 256, 128))
    tk = _pick(K, (512, 256, 128))
    nk = K // tk
    a_spec = pl.BlockSpec((tk, tm), lambda i, j, k: (k, i)) if mode == "tn" else pl.BlockSpec((tm, tk), lambda i, j, k: (i, k))
    b_spec = pl.BlockSpec((tn, tk), lambda i, j, k: (j, k)) if mode == "nt" else pl.BlockSpec((tk, tn), lambda i, j, k: (k, j))
    o_spec = pl.BlockSpec((tm, tn), lambda i, j, k: (i, j))
    dims = _DIMS[mode]

    def body(*refs):
        if res is None:
            a_ref, b_ref, o_ref, acc = refs
        else:
            a_ref, b_ref, r_ref, o_ref, acc = refs
        k = pl.program_id(2)

        @pl.when(k == 0)
        def _():
            acc[...] = jnp.zeros_like(acc)

        acc[...] += lax.dot_general(a_ref[...].astype(BF16), b_ref[...].astype(BF16), dims, preferred_element_type=F32)

        @pl.when(k == nk - 1)
        def _():
            r = acc[...] * alpha if alpha != 1.0 else acc[...]
            if res is not None:
                r = r_ref[...] + r
            o_ref[...] = r.astype(o_ref.dtype)

    ins = [a, b] + ([] if res is None else [res])
    specs = [a_spec, b_spec] + ([] if res is None else [o_spec])
    return _pcall(body, name=name, grid=(M // tm, N // tn, nk), in_specs=specs, out_specs=o_spec,
                  out_shape=jax.ShapeDtypeStruct((M, N), out_dtype),
                  scratch_shapes=[pltpu.VMEM((tm, tn), F32)])(*ins)


def _mm_tn_layer(name, a3, b3, layer, into=None, alpha=1.0):
    ga, T, M = a3.shape
    gb, _, N = b3.shape
    G = max(ga, gb)
    tm = _pick(M, (1024, 512)) if N <= 1024 else M
    tk = _pick(T, (512, 256, 128))
    nk = T // tk
    a_spec = pl.BlockSpec((None, tk, tm), (lambda g, i, k: (g, k, i)) if ga > 1 else (lambda g, i, k: (0, k, i)))
    b_spec = pl.BlockSpec((None, tk, N), (lambda g, i, k: (g, k, 0)) if gb > 1 else (lambda g, i, k: (0, k, 0)))
    o_spec = pl.BlockSpec((None, None, tm, N), lambda g, i, k: (g, layer, i, 0))

    def body(*refs):
        a_ref, b_ref = refs[0], refs[1]
        o_ref, acc = refs[-2], refs[-1]
        k = pl.program_id(2)

        @pl.when(k == 0)
        def _():
            acc[...] = jnp.zeros_like(acc)

        acc[...] += lax.dot_general(a_ref[...].astype(BF16), b_ref[...].astype(BF16), _DIMS["tn"], preferred_element_type=F32)

        @pl.when(k == nk - 1)
        def _():
            o_ref[...] = (acc[...] * alpha if alpha != 1.0 else acc[...]).astype(o_ref.dtype)

    ins, specs, alias = [a3, b3], [a_spec, b_spec], {}
    if into is not None:
        ins.append(into)
        specs.append(ANY)
        alias = {2: 0}
    return _pcall(body, name=name, grid=(G, M // tm, nk), in_specs=specs, out_specs=o_spec,
                  out_shape=jax.ShapeDtypeStruct((G, 2, M, N), BF16), input_output_aliases=alias,
                  scratch_shapes=[pltpu.VMEM((tm, N), F32)])(*ins)


def _rowwise(name, fn, ins, outs, T, tm):
    in_specs = []
    for arr, spec in ins:
        if spec == "row":
            in_specs.append(pl.BlockSpec((tm, arr.shape[1]), lambda i: (i, 0)))
        elif spec == "full":
            in_specs.append(pl.BlockSpec(arr.shape, lambda i, _n=arr.ndim: (0,) * _n))
        elif spec[0] == "cols":
            _, off, width = spec
            in_specs.append(pl.BlockSpec((tm, width), lambda i, _b=off // width: (i, _b)))
        else:
            _, layer = spec
            in_specs.append(pl.BlockSpec((arr.shape[0], None) + arr.shape[2:],
                                         lambda i, _l=layer, _n=arr.ndim: (0, _l) + (0,) * (_n - 2)))
    out_specs, out_shapes = [], []
    for shape, dtype, spec in outs:
        out_shapes.append(jax.ShapeDtypeStruct(shape, dtype))
        if spec == "row":
            out_specs.append(pl.BlockSpec((tm, shape[1]), lambda i: (i, 0)))
        else:
            out_specs.append(pl.BlockSpec(shape, lambda i, _n=len(shape): (0,) * _n))
    n_in = len(ins)

    def body(*refs):
        res = fn(*[r[...] for r in refs[:n_in]])
        i = pl.program_id(0)
        for r, (_, _, spec), val in zip(refs[n_in:], outs, res):
            if spec == "acc":
                @pl.when(i == 0)
                def _(r=r):
                    r[...] = jnp.zeros_like(r)
                r[...] += val.astype(r.dtype)
            else:
                r[...] = val.astype(r.dtype)

    return _pcall(body, name=name, grid=(T // tm,), in_specs=in_specs, out_specs=out_specs, out_shape=out_shapes)(
        *[a for a, _ in ins])


def _rms_fwd(name, h, g):
    T, D = h.shape
    return _rowwise(name, lambda hv, gv: (_rms(hv, gv),), [(h, "row"), (g, "full")], [((T, D), BF16, "row")], T,
                    _pick(T, (512, 256, 128)))[0]


def _rms_bwd(name, h, g, dxn, dh_in):
    T, D = h.shape

    def fn(hv, gv, dv, dh0):
        _, vjp = jax.vjp(_rms, hv, gv)
        dh, dg = vjp(dv.astype(F32))
        return dh0 + dh, dg

    return _rowwise(name, fn, [(h, "row"), (g, "full"), (dxn, "row"), (dh_in, "row")],
                    [((T, D), F32, "row"), ((1, D), F32, "acc")], T, _pick(T, (256, 128)))


def _ffn_fwd(name, xn, h, w1, w3, w2, layer):
    T, D = xn.shape
    F8 = w1.shape[-1]
    tm = _pick(T, (512, 256, 128))
    wspec = lambda r, c: pl.BlockSpec((None, None, r, c), lambda i, d: (d, layer, 0, 0))
    row = pl.BlockSpec((tm, D), lambda i, d: (i, 0))
    hid = pl.BlockSpec((None, tm, F8), lambda i, d: (d, i, 0))

    def body(xn_ref, h_ref, w1_ref, w3_ref, w2_ref, out_ref, h1_ref, h3_ref, acc):
        d = pl.program_id(1)

        @pl.when(d == 0)
        def _():
            acc[...] = jnp.zeros_like(acc)

        x = xn_ref[...]
        h1 = jnp.dot(x, w1_ref[...], preferred_element_type=F32)
        h3 = jnp.dot(x, w3_ref[...], preferred_element_type=F32)
        h1_ref[...] = h1.astype(BF16)
        h3_ref[...] = h3.astype(BF16)
        act = (h1 * jax.nn.sigmoid(h1) * h3).astype(BF16)
        acc[...] += jnp.dot(act, w2_ref[...], preferred_element_type=F32)

        @pl.when(d == N_DEV - 1)
        def _():
            out_ref[...] = h_ref[...] + 0.5 * acc[...]

    return _pcall(body, name=name, grid=(T // tm, N_DEV),
                  in_specs=[row, row, wspec(D, F8), wspec(D, F8), wspec(F8, D)],
                  out_specs=[row, hid, hid],
                  out_shape=[jax.ShapeDtypeStruct((T, D), F32), jax.ShapeDtypeStruct((N_DEV, T, F8), BF16),
                             jax.ShapeDtypeStruct((N_DEV, T, F8), BF16)],
                  scratch_shapes=[pltpu.VMEM((tm, D), F32)])(xn, h, w1, w3, w2)


def _ffn_bwd(name, dy, h1, h3, w1, w3, w2, layer):
    T, D = dy.shape
    F8 = w1.shape[-1]
    tm = _pick(T, (512, 256, 128))
    wspec = lambda r, c: pl.BlockSpec((None, None, r, c), lambda i, d: (d, layer, 0, 0))
    row = pl.BlockSpec((tm, D), lambda i, d: (i, 0))
    hid = pl.BlockSpec((None, tm, F8), lambda i, d: (d, i, 0))

    def body(dy_ref, h1_ref, h3_ref, w1_ref, w3_ref, w2_ref, dxn_ref, dh1_ref, dh3_ref, act_ref, dyb, acc):
        d = pl.program_id(1)

        @pl.when(d == 0)
        def _():
            acc[...] = jnp.zeros_like(acc)
            dyb[...] = dy_ref[...].astype(BF16)

        dact = 0.5 * lax.dot_general(dyb[...], w2_ref[...], _DIMS["nt"], preferred_element_type=F32)
        h1 = h1_ref[...].astype(F32)
        h3 = h3_ref[...].astype(F32)
        sig = jax.nn.sigmoid(h1)
        silu = h1 * sig
        dh1 = (dact * h3 * (sig * (1.0 + h1 * (1.0 - sig)))).astype(BF16)
        dh3 = (dact * silu).astype(BF16)
        dh1_ref[...] = dh1
        dh3_ref[...] = dh3
        act_ref[...] = (silu * h3).astype(BF16)
        acc[...] += (lax.dot_general(dh1, w1_ref[...], _DIMS["nt"], preferred_element_type=F32)
                     + lax.dot_general(dh3, w3_ref[...], _DIMS["nt"], preferred_element_type=F32))

        @pl.when(d == N_DEV - 1)
        def _():
            dxn_ref[...] = acc[...]

    hshape = jax.ShapeDtypeStruct((N_DEV, T, F8), BF16)
    return _pcall(body, name=name, grid=(T // tm, N_DEV),
                  in_specs=[row, hid, hid, wspec(D, F8), wspec(D, F8), wspec(F8, D)],
                  out_specs=[row, hid, hid, hid],
                  out_shape=[jax.ShapeDtypeStruct((T, D), F32), hshape, hshape, hshape],
                  scratch_shapes=[pltpu.VMEM((tm, D), BF16), pltpu.VMEM((tm, D), F32)])(dy, h1, h3, w1, w3, w2)


def _rot_matrix():
    i = jnp.arange(QK)[:, None]
    j = jnp.arange(QK)[None, :]
    half = ROPE // 2
    first = (j >= NOPE) & (j < NOPE + half) & (i == j + half)
    second = (j >= NOPE + half) & (i == j - half)
    return jnp.where(first, -1.0, jnp.where(second, 1.0, 0.0)).astype(F32)


def _mla_fn(cq, ckv, kr128, cos, sin, rot, qa_g, kva_g, qn_g, kn_g, w_uq, w_ukv):
    cqn = _rms(cq, qa_g)
    ckvn = _rms(ckv, kva_g)
    kr = kr128[:, :ROPE]
    qs, ks, vs = [], [], []
    for h in range(HEADS):
        qh = _rms(_bdot(cqn, w_uq[h]), qn_g)
        qs.append(qh * cos + _hdot(qh, rot) * sin)
        kvh = _bdot(ckvn, w_ukv[h])
        kh = _rms(jnp.concatenate([kvh[:, :NOPE], kr], axis=-1), kn_g)
        ks.append(kh * cos + _hdot(kh, rot) * sin)
        vs.append(kvh[:, NOPE:])
    return qs, ks, vs


def _mla_specs(z, tabs, small, w_uq, w_ukv, layer, tm, offs):
    o_cq, o_ckv, o_kr = offs
    row = lambda w: pl.BlockSpec((tm, w), lambda i: (i, 0))
    col = lambda off, w: pl.BlockSpec((tm, w), lambda i: (i, off // w))
    full2 = lambda a: pl.BlockSpec(a.shape, lambda i: (0, 0))
    wsp = lambda a: pl.BlockSpec((HEADS, None) + a.shape[2:], lambda i: (0, layer, 0, 0))
    cq_w, ckv_w = w_uq.shape[2], w_ukv.shape[2]
    ins = [z, z, z, tabs[0], tabs[1], tabs[2]] + list(small) + [w_uq, w_ukv]
    specs = ([col(o_cq, cq_w), col(o_ckv, ckv_w), col(o_kr, LANE), row(QK), row(QK), full2(tabs[2])]
             + [full2(s) for s in small] + [wsp(w_uq), wsp(w_ukv)])
    return ins, specs


def _mla_prep_fwd(name, z, tabs, small, w_uq, w_ukv, layer, offs):
    T = z.shape[0]
    tm = _pick(T, (256, 128))
    ins, specs = _mla_specs(z, tabs, small, w_uq, w_ukv, layer, tm, offs)
    head = lambda w: pl.BlockSpec((HEADS, tm, w), lambda i: (0, i, 0))

    def body(*refs):
        vals = [r[...] for r in refs[:12]]
        q_ref, k_ref, v_ref = refs[12:]
        qs, ks, vs = _mla_fn(*vals)
        for h in range(HEADS):
            q_ref[h] = qs[h].astype(BF16)
            k_ref[h] = ks[h].astype(BF16)
            v_ref[h] = vs[h].astype(BF16)

    return _pcall(body, name=name, grid=(T // tm,), in_specs=specs, out_specs=[head(QK), head(QK), head(VD)],
                  out_shape=[jax.ShapeDtypeStruct((HEADS, T, QK), BF16), jax.ShapeDtypeStruct((HEADS, T, QK), BF16),
                             jax.ShapeDtypeStruct((HEADS, T, VD), BF16)])(*ins)


def _mla_prep_bwd(name, z, tabs, small, w_uq, w_ukv, layer, offs, dq, dk, dv):
    T = z.shape[0]
    tm = _pick(T, (256, 128))
    ins, specs = _mla_specs(z, tabs, small, w_uq, w_ukv, layer, tm, offs)
    head = lambda w: pl.BlockSpec((HEADS, tm, w), lambda i: (0, i, 0))
    ins += [dq, dk, dv]
    specs += [head(QK), head(QK), head(VD)]
    cq_w, ckv_w = w_uq.shape[2], w_ukv.shape[2]
    acc_shapes = [s.shape for s in small] + [(HEADS,) + w_uq.shape[2:], (HEADS,) + w_ukv.shape[2:]]
    row_shapes = [(T, cq_w), (T, ckv_w), (T, LANE)]
    out_shape = [jax.ShapeDtypeStruct(s, BF16) for s in row_shapes] + [jax.ShapeDtypeStruct(s, F32) for s in acc_shapes]
    out_specs = ([pl.BlockSpec((tm, s[1]), lambda i: (i, 0)) for s in row_shapes]
                 + [pl.BlockSpec(s, lambda i, _n=len(s): (0,) * _n) for s in acc_shapes])

    def body(*refs):
        cq, ckv, kr128, cos, sin, rot, qa_g, kva_g, qn_g, kn_g, w_uq_v, w_ukv_v = [r[...] for r in refs[:12]]
        dq_ref, dk_ref, dv_ref = refs[12:15]
        outs = refs[15:]
        f = lambda a, b, c, g1, g2, g3, g4, wq, wkv: _mla_fn(a, b, c, cos, sin, rot, g1, g2, g3, g4, wq, wkv)
        _, vjp = jax.vjp(f, cq, ckv, kr128, qa_g, kva_g, qn_g, kn_g, w_uq_v.astype(F32), w_ukv_v.astype(F32))
        cts = ([dq_ref[h] for h in range(HEADS)], [dk_ref[h] for h in range(HEADS)], [dv_ref[h] for h in range(HEADS)])
        grads = vjp(cts)
        i = pl.program_id(0)
        for n, (r, gval) in enumerate(zip(outs, grads)):
            if n < 3:
                r[...] = gval.astype(r.dtype)
            else:
                @pl.when(i == 0)
                def _(r=r):
                    r[...] = jnp.zeros_like(r)
                r[...] += gval

    return _pcall(body, name=name, grid=(T // tm,), in_specs=specs, out_specs=out_specs, out_shape=out_shape)(*ins)


NEG = -1e30


def _attn_tiles(T):
    t = _pick(T, (512, 256, 128))
    return t, T // t


def _attn_fwd(name, q, k, v):
    H, T, _ = q.shape
    t, n = _attn_tiles(T)
    scale = QK ** -0.5

    def body(q_ref, k_ref, v_ref, o_ref, lse_ref, m_s, l_s, acc):
        qi, ki = pl.program_id(1), pl.program_id(2)

        @pl.when(ki == 0)
        def _():
            m_s[...] = jnp.full_like(m_s, NEG)
            l_s[...] = jnp.zeros_like(l_s)
            acc[...] = jnp.zeros_like(acc)

        @pl.when(ki <= qi)
        def _():
            s = lax.dot_general(q_ref[...], k_ref[...], _DIMS["nt"], preferred_element_type=F32) * scale
            rows = qi * t + lax.broadcasted_iota(jnp.int32, (t, t), 0)
            cols = ki * t + lax.broadcasted_iota(jnp.int32, (t, t), 1)
            s = jnp.where(cols <= rows, s, NEG)
            m_new = jnp.maximum(m_s[...], jnp.max(s, axis=-1, keepdims=True))
            alpha = jnp.exp(m_s[...] - m_new)
            p = jnp.exp(s - m_new)
            l_s[...] = alpha * l_s[...] + jnp.sum(p, axis=-1, keepdims=True)
            acc[...] = alpha * acc[...] + jnp.dot(p.astype(BF16), v_ref[...], preferred_element_type=F32)
            m_s[...] = m_new

        @pl.when(ki == qi)
        def _():
            o_ref[...] = acc[...] / l_s[...]
            lse_ref[...] = m_s[...] + jnp.log(l_s[...])

    kv = lambda w: pl.BlockSpec((None, t, w), lambda h, qi, ki: (h, jnp.minimum(ki, qi), 0))
    return _pcall(body, name=name, grid=(H, n, n),
                  in_specs=[pl.BlockSpec((None, t, QK), lambda h, qi, ki: (h, qi, 0)), kv(QK), kv(VD)],
                  out_specs=[pl.BlockSpec((t, VD), lambda h, qi, ki: (qi, h)),
                             pl.BlockSpec((None, t, 1), lambda h, qi, ki: (h, qi, 0))],
                  out_shape=[jax.ShapeDtypeStruct((T, H * VD), F32), jax.ShapeDtypeStruct((H, T, 1), F32)],
                  scratch_shapes=[pltpu.VMEM((t, 1), F32), pltpu.VMEM((t, 1), F32), pltpu.VMEM((t, VD), F32)])(q, k, v)


def _attn_bwd(name, q, k, v, o, lse, do):
    H, T, _ = q.shape
    t, n = _attn_tiles(T)
    scale = QK ** -0.5

    def body(q_ref, k_ref, v_ref, o_ref, lse_ref, do_ref, dq_ref, dk_ref, dv_ref, dk_acc, dv_acc):
        ki, qi = pl.program_id(1), pl.program_id(2)

        @pl.when((ki == 0) & (qi == 0))
        def _():
            dq_ref[...] = jnp.zeros_like(dq_ref)

        @pl.when(qi == 0)
        def _():
            dk_acc[...] = jnp.zeros_like(dk_acc)
            dv_acc[...] = jnp.zeros_like(dv_acc)

        @pl.when(qi >= ki)
        def _():
            qv, kv_, dov = q_ref[...], k_ref[...], do_ref[...]
            s = lax.dot_general(qv, kv_, _DIMS["nt"], preferred_element_type=F32) * scale
            rows = qi * t + lax.broadcasted_iota(jnp.int32, (t, t), 0)
            cols = ki * t + lax.broadcasted_iota(jnp.int32, (t, t), 1)
            p = jnp.where(cols <= rows, jnp.exp(s - lse_ref[...]), 0.0)
            dob = dov.astype(BF16)
            delta = jnp.sum(o_ref[...] * dov, axis=-1, keepdims=True)
            dv_acc[...] += lax.dot_general(p.astype(BF16), dob, _DIMS["tn"], preferred_element_type=F32)
            dp = lax.dot_general(dob, v_ref[...], _DIMS["nt"], preferred_element_type=F32)
            ds = (p * (dp - delta) * scale).astype(BF16)
            rs = pl.ds(pl.multiple_of(qi * t, t), t)
            dq_ref[rs, :] += jnp.dot(ds, kv_, preferred_element_type=F32)
            dk_acc[...] += lax.dot_general(ds, qv, _DIMS["tn"], preferred_element_type=F32)

        @pl.when(qi == n - 1)
        def _():
            dk_ref[...] = dk_acc[...]
            dv_ref[...] = dv_acc[...]

    qrow = lambda w: pl.BlockSpec((None, t, w), lambda h, ki, qi: (h, jnp.maximum(qi, ki), 0))
    krow = lambda w: pl.BlockSpec((None, t, w), lambda h, ki, qi: (h, ki, 0))
    wide = pl.BlockSpec((t, VD), lambda h, ki, qi: (jnp.maximum(qi, ki), h))
    return _pcall(body, name=name, grid=(H, n, n),
                  in_specs=[qrow(QK), krow(QK), krow(VD), wide, qrow(1), wide],
                  out_specs=[pl.BlockSpec((None, T, QK), lambda h, ki, qi: (h, 0, 0)), krow(QK), krow(VD)],
                  out_shape=[jax.ShapeDtypeStruct((H, T, QK), F32), jax.ShapeDtypeStruct((H, T, QK), F32),
                             jax.ShapeDtypeStruct((H, T, VD), F32)],
                  scratch_shapes=[pltpu.VMEM((t, QK), F32), pltpu.VMEM((t, VD), F32)])(q, k, v, o, lse, do)


def _tril():
    return lax.broadcasted_iota(jnp.int32, (CHUNK, CHUNK), 1) <= lax.broadcasted_iota(jnp.int32, (CHUNK, CHUNK), 0)


@jax.custom_vjp
def _gm_gate(v, ws, b_t):
    wc = jnp.where(_tril()[None], ws, 0.0).astype(BF16)
    vb = v.astype(BF16)
    rows = []
    for c in range(v.shape[0] // CHUNK):
        cols = []
        for g in range(GROUPS):
            vc = vb[c * CHUNK:(c + 1) * CHUNK, g * LANE:(g + 1) * LANE]
            cols.append(jnp.dot(wc[g], vc, preferred_element_type=F32) + jnp.broadcast_to(b_t[:, g:g + 1], (CHUNK, LANE)))
        rows.append(jnp.concatenate(cols, axis=-1))
    return jnp.concatenate(rows, axis=0)


def _gm_gate_fwd(v, ws, b_t):
    return _gm_gate(v, ws, b_t), (v, ws)


def _gm_gate_bwd(res, dgate):
    v, ws = res
    tril = _tril()
    wc = jnp.where(tril[None], ws, 0.0).astype(BF16)
    vb = v.astype(BF16)
    dgb = dgate.astype(BF16)
    dws = [jnp.zeros((CHUNK, CHUNK), F32) for _ in range(GROUPS)]
    db = jnp.zeros((CHUNK, GROUPS), F32)
    lane_g = lax.broadcasted_iota(jnp.int32, (1, GROUPS), 1)
    rows = []
    for c in range(v.shape[0] // CHUNK):
        cols = []
        for g in range(GROUPS):
            sl = (slice(c * CHUNK, (c + 1) * CHUNK), slice(g * LANE, (g + 1) * LANE))
            cols.append(lax.dot_general(wc[g], dgb[sl], _DIMS["tn"], preferred_element_type=F32))
            dws[g] = dws[g] + lax.dot_general(dgb[sl], vb[sl], _DIMS["nt"], preferred_element_type=F32)
            db = db + jnp.sum(dgate[sl], axis=1, keepdims=True) * (lane_g == g).astype(F32)
        rows.append(jnp.concatenate(cols, axis=-1))
    dws = jnp.stack([jnp.where(tril, d, 0.0) for d in dws])
    return jnp.concatenate(rows, axis=0), dws, db


_gm_gate.defvjp(_gm_gate_fwd, _gm_gate_bwd)


def _mix_fn(a_out, zu, zv, aon_g, gon_g, vn_g, ws, b_t):
    u = jax.nn.gelu(zu)
    vv = _rms(jax.nn.gelu(zv), vn_g)
    g_out = u * _gm_gate(vv, ws, b_t)
    return jnp.concatenate([_rms(a_out, aon_g), _rms(g_out, gon_g)], axis=-1)


def _mix_ins(a_out, z, small, offs):
    gw = a_out.shape[1]
    return [(a_out, "row"), (z, ("cols", offs[0], gw)), (z, ("cols", offs[1], gw))] + [(s, "full") for s in small]


def _mix_fwd(name, a_out, z, small, offs):
    T, gw = a_out.shape
    return _rowwise(name, lambda *a: (_mix_fn(*a),), _mix_ins(a_out, z, small, offs), [((T, 2 * gw), BF16, "row")],
                    T, _pick(T, (256, 128)))[0]


def _mix_bwd(name, a_out, z, small, offs, dmixed):
    T, gw = a_out.shape

    def fn(*a):
        _, vjp = jax.vjp(_mix_fn, *a[:-1])
        return vjp(a[-1].astype(F32))

    outs = [((T, gw), F32, "row"), ((T, gw), BF16, "row"), ((T, gw), BF16, "row")] + [(s.shape, F32, "acc") for s in small]
    return _rowwise(name, fn, _mix_ins(a_out, z, small, offs) + [(dmixed, "row")], outs, T, _pick(T, (256, 128)))


def _ple_fn(gl, pe, g):
    return jax.nn.sigmoid(gl) * _rms(pe, g)


def _ple_fwd(name, h, gl, pe, g):
    T, D = h.shape
    return _rowwise(name, lambda hv, a, b, c: (hv + _ple_fn(a, b, c),),
                    [(h, "row"), (gl, "row"), (pe, "row"), (g, "full")], [((T, D), F32, "row")], T, _pick(T, (256, 128)))[0]


def _ple_bwd(name, gl, pe, g, dh):
    T, D = gl.shape

    def fn(a, b, c, d):
        _, vjp = jax.vjp(_ple_fn, a, b, c)
        return vjp(d)

    return _rowwise(name, fn, [(gl, "row"), (pe, "row"), (g, "full"), (dh, "row")],
                    [((T, D), BF16, "row"), ((T, D), BF16, "row"), ((1, D), F32, "acc")], T, _pick(T, (256, 128)))


def _loss(name, y, target):
    T, D = y.shape

    def fn(yv, tv):
        err = yv - tv
        part = 0.5 * jnp.sum(jnp.mean(err * err, axis=-1, keepdims=True), axis=0, keepdims=True)
        return err * (1.0 / D), jnp.broadcast_to(part, (8, LANE))

    return _rowwise(name, fn, [(y, "row"), (target, "row")], [((T, D), F32, "row"), ((8, LANE), F32, "acc")], T,
                    _pick(T, (512, 256, 128)))


def _adamw_sum(name, parts, w, m, v):
    R, C = w.shape
    tr = _pick(R, (256, 128, 64, 32, 16, 8)) if C > 512 else _pick(R, (512, 256, 128, 64, 32, 16, 8))
    c1 = 1.0 - ADAM_B1 ** ADAM_STEP
    c2 = 1.0 - ADAM_B2 ** ADAM_STEP

    def body(p_ref, w_ref, m_ref, v_ref, g_out, d_out, m_out, v_out):
        g = p_ref[0].astype(F32)
        for s in range(1, N_DEV):
            g = g + p_ref[s].astype(F32)
        m2 = ADAM_B1 * m_ref[...] + (1.0 - ADAM_B1) * g
        v2 = ADAM_B2 * v_ref[...] + (1.0 - ADAM_B2) * (g * g)
        g_out[...] = g
        m_out[...] = m2
        v_out[...] = v2
        d_out[...] = -ADAM_LR * ((m2 / c1) / (jnp.sqrt(v2 / c2) + ADAM_EPS) + ADAM_WD * w_ref[...])

    blk = pl.BlockSpec((tr, C), lambda i: (i, 0))
    sd = jax.ShapeDtypeStruct((R, C), F32)
    return _pcall(body, name=name, grid=(R // tr,),
                  in_specs=[pl.BlockSpec((N_DEV, tr, C), lambda i: (0, i, 0)), blk, blk, blk],
                  out_specs=[blk, blk, blk, blk], out_shape=[sd, sd, sd, sd])(parts, w, m, v)


def _unshard_cols(g):
    _, L, K, n = g.shape
    return g.transpose(1, 2, 0, 3).reshape(L, K, N_DEV * n)


def _shard_cols(full):
    L, K, N = full.shape
    return full.reshape(L, K, N_DEV, N // N_DEV).transpose(2, 0, 1, 3)


def _unshard_rows(g):
    _, L, k, N = g.shape
    return g.transpose(1, 0, 2, 3).reshape(L, N_DEV * k, N)


def _shard_rows(full):
    L, K, N = full.shape
    return full.reshape(L, N_DEV, K // N_DEV, N).transpose(1, 0, 2, 3)


def _step(x, p, positions, target, w, m, v):
    T, D = x.shape[1], x.shape[2]
    L = p.shape[0]
    x2, target2 = x[0], target[0]
    q_rank, kv_rank = w['w_uq'].shape[1], w['w_ukv'].shape[1]
    gw = w['gm_v_norm'].shape[1]

    inv_freq = ROPE_BASE ** (-jnp.arange(0, ROPE, 2, dtype=F32) / ROPE)
    ang = positions[0].astype(F32)[:, None] * inv_freq
    cos = jnp.concatenate([jnp.ones((T, NOPE), F32), jnp.cos(ang), jnp.cos(ang)], axis=-1)
    sin = jnp.concatenate([jnp.zeros((T, NOPE), F32), jnp.sin(ang), jnp.sin(ang)], axis=-1)
    tabs = (cos, sin, _rot_matrix())

    gat = {n: _all_gather("ag_" + n, w[n].astype(BF16)) for n in BIG}
    w_in_full = _unshard_cols(gat['w_in'])
    s0, s1, s2, s3 = q_rank, q_rank + kv_rank, q_rank + kv_rank + ROPE, q_rank + kv_rank + ROPE + gw
    w_in_perm = jnp.concatenate([w_in_full[..., s2:s3], w_in_full[..., s3:], w_in_full[..., :s0], w_in_full[..., s0:s1],
                                 w_in_full[..., s1:s2], jnp.zeros((L, D, LANE - ROPE), BF16)], axis=-1)
    o_u, o_v, o_cq, o_ckv, o_kr = 0, gw, 2 * gw, 2 * gw + q_rank, 2 * gw + q_rank + kv_rank
    w_out_full = _unshard_rows(gat['w_out'])
    w_gate_full = _unshard_rows(gat['w_ple_gate'])
    w_ple_full = _unshard_cols(gat['w_ple'])
    w_uq_g, w_ukv_g = gat['w_uq'], gat['w_ukv']

    def g2(name, l):
        return w[name][l][None, :]

    gm_bt = [w['gm_bs'][l].T for l in range(L)]

    saved = []
    h = x2
    for l in range(L):
        s = {}
        s['h0'] = h
        s['xn_a'] = _rms_fwd(f"rms_a{l}", h, g2('ffn_a_norm', l))
        h, s['a_h1'], s['a_h3'] = _ffn_fwd(f"ffn_a_fwd{l}", s['xn_a'], h, gat['ffn_a_w1'], gat['ffn_a_w3'], gat['ffn_a_w2'], l)
        s['h1'] = h
        s['n'] = _rms_fwd(f"rms_mix{l}", h, g2('mix_norm', l))
        s['z'] = _mm(f"w_in{l}", s['n'], w_in_perm[l], "nn")
        s['mla_small'] = [g2('q_a_norm', l), g2('kv_a_norm', l), g2('q_norm', l), g2('k_norm', l)]
        s['q'], s['k'], s['v'] = _mla_prep_fwd(f"mla_prep{l}", s['z'], tabs, s['mla_small'], w_uq_g, w_ukv_g, l,
                                               (o_cq, o_ckv, o_kr))
        s['a_out'], s['lse'] = _attn_fwd(f"attn_fwd{l}", s['q'], s['k'], s['v'])
        s['mix_small'] = [g2('attn_out_norm', l), g2('gm_out_norm', l), g2('gm_v_norm', l), w['gm_ws'][l], gm_bt[l]]
        s['mixed'] = _mix_fwd(f"mix_fwd{l}", s['a_out'], s['z'], s['mix_small'], (o_u, o_v))
        h = _mm(f"w_out{l}", s['mixed'], w_out_full[l], "nn", res=h)
        s['h2'] = h
        s['xn_b'] = _rms_fwd(f"rms_b{l}", h, g2('ffn_b_norm', l))
        h, s['b_h1'], s['b_h3'] = _ffn_fwd(f"ffn_b_fwd{l}", s['xn_b'], h, gat['ffn_b_w1'], gat['ffn_b_w3'], gat['ffn_b_w2'], l)
        s['h3'] = h
        s['xn_g'] = _rms_fwd(f"rms_g{l}", h, g2('ple_gate_norm', l))
        s['gl'] = _mm(f"w_gate{l}", s['xn_g'], w_gate_full[l], "nn")
        s['p'] = p[l, 0]
        s['pe'] = _mm(f"w_ple{l}", s['p'], w_ple_full[l], "nn")
        h = _ple_fwd(f"ple_fwd{l}", h, s['gl'], s['pe'], g2('ple_norm', l))
        saved.append(s)

    dh, loss_part = _loss("loss", h, target2)
    loss = lax.psum(loss_part[0, 0], AXES)

    gsmall = {n: [None] * L for n in SMALL}
    gfull = {n: [None] * L for n in ('w_in', 'w_uq', 'w_ukv', 'w_out', 'w_ple_gate', 'w_ple')}
    gffn = {n: None for n in ('ffn_a_w1', 'ffn_a_w3', 'ffn_a_w2', 'ffn_b_w1', 'ffn_b_w3', 'ffn_b_w2')}

    def ffn_backward(tag, l, dh, xn, h_in, h1, h3, norm_name):
        pre = 'ffn_' + tag
        dxn, dh1, dh3, act = _ffn_bwd(f"{pre}_bwd{l}", dh, h1, h3, gat[pre + '_w1'], gat[pre + '_w3'], gat[pre + '_w2'], l)
        gffn[pre + '_w1'] = _mm_tn_layer(f"{pre}_dw1_{l}", xn[None], dh1, l, gffn[pre + '_w1'])
        gffn[pre + '_w3'] = _mm_tn_layer(f"{pre}_dw3_{l}", xn[None], dh3, l, gffn[pre + '_w3'])
        gffn[pre + '_w2'] = _mm_tn_layer(f"{pre}_dw2_{l}", act, dh[None], l, gffn[pre + '_w2'], alpha=0.5)
        dh, gsmall[norm_name][l] = _rms_bwd(f"rms_{tag}_bwd{l}", h_in, g2(norm_name, l), dxn, dh)
        return dh

    for l in reversed(range(L)):
        s = saved[l]
        d_gl, d_pe, gsmall['ple_norm'][l] = _ple_bwd(f"ple_bwd{l}", s['gl'], s['pe'], g2('ple_norm', l), dh)
        gfull['w_ple'][l] = _mm(f"dw_ple{l}", s['p'], d_pe, "tn")
        gfull['w_ple_gate'][l] = _mm(f"dw_gate{l}", s['xn_g'], d_gl, "tn")
        d_xng = _mm(f"d_xng{l}", d_gl, w_gate_full[l], "nt")
        dh, gsmall['ple_gate_norm'][l] = _rms_bwd(f"rms_g_bwd{l}", s['h3'], g2('ple_gate_norm', l), d_xng, dh)
        dh = ffn_backward('b', l, dh, s['xn_b'], s['h2'], s['b_h1'], s['b_h3'], 'ffn_b_norm')
        gfull['w_out'][l] = _mm(f"dw_out{l}", s['mixed'], dh, "tn")
        d_mixed = _mm(f"d_mixed{l}", dh, w_out_full[l], "nt")
        mix = _mix_bwd(f"mix_bwd{l}", s['a_out'], s['z'], s['mix_small'], (o_u, o_v), d_mixed)
        d_a_out, d_u, d_v = mix[:3]
        gsmall['attn_out_norm'][l], gsmall['gm_out_norm'][l], gsmall['gm_v_norm'][l], gsmall['gm_ws'][l] = mix[3:7]
        gsmall['gm_bs'][l] = mix[7].T
        dq, dk, dv = _attn_bwd(f"attn_bwd{l}", s['q'], s['k'], s['v'], s['a_out'], s['lse'], d_a_out)
        mla = _mla_prep_bwd(f"mla_prep_bwd{l}", s['z'], tabs, s['mla_small'], w_uq_g, w_ukv_g, l, (o_cq, o_ckv, o_kr),
                            dq, dk, dv)
        d_cq, d_ckv, d_kr = mla[:3]
        gsmall['q_a_norm'][l], gsmall['kv_a_norm'][l], gsmall['q_norm'][l], gsmall['k_norm'][l] = mla[3:7]
        gfull['w_uq'][l], gfull['w_ukv'][l] = mla[7], mla[8]
        dz = jnp.concatenate([d_u, d_v, d_cq, d_ckv, d_kr], axis=-1)
        g_in = _mm(f"dw_in{l}", s['n'], dz, "tn")
        gfull['w_in'][l] = jnp.concatenate([g_in[:, o_cq:o_cq + q_rank], g_in[:, o_ckv:o_ckv + kv_rank],
                                            g_in[:, o_kr:o_kr + ROPE], g_in[:, o_u:o_u + gw], g_in[:, o_v:o_v + gw]], axis=-1)
        d_n = _mm(f"d_n{l}", dz, w_in_perm[l], "nt")
        dh, gsmall['mix_norm'][l] = _rms_bwd(f"rms_mix_bwd{l}", s['h1'], g2('mix_norm', l), d_n, dh)
        dh = ffn_backward('a', l, dh, s['xn_a'], s['h0'], s['a_h1'], s['a_h3'], 'ffn_a_norm')
    grad_x = dh[None]

    parts = dict(gffn)
    parts['w_in'] = _shard_cols(jnp.stack(gfull['w_in'])).astype(BF16)
    parts['w_uq'] = jnp.stack(gfull['w_uq'], axis=1).astype(BF16)
    parts['w_ukv'] = jnp.stack(gfull['w_ukv'], axis=1).astype(BF16)
    parts['w_out'] = _shard_rows(jnp.stack(gfull['w_out'])).astype(BF16)
    parts['w_ple_gate'] = _shard_rows(jnp.stack(gfull['w_ple_gate'])).astype(BF16)
    parts['w_ple'] = _shard_cols(jnp.stack(gfull['w_ple'])).astype(BF16)

    out = {}
    for n in BIG:
        shp = w[n].shape
        recv = _reduce_scatter("rs_" + n, parts[n])
        flat = lambda a: a.reshape(shp[0] * shp[1], shp[2])
        res = _adamw_sum("adamw_" + n, recv.reshape(N_DEV, shp[0] * shp[1], shp[2]), flat(w[n]), flat(m[n]), flat(v[n]))
        out[n] = [r.reshape(shp) for r in res]

    sizes = [w[n].size for n in SMALL]
    total = sum(sizes)
    padded = -(-total // (8 * LANE)) * (8 * LANE)

    def pack(d):
        flat = jnp.concatenate([d[n].reshape(-1) for n in SMALL] + [jnp.zeros((padded - total,), F32)])
        return flat.reshape(padded // LANE, LANE)

    gs = pack({n: jnp.stack([gsmall[n][l].reshape(w[n].shape[1:]) for l in range(L)]) for n in SMALL})
    res = _adamw_sum("adamw_small", _all_gather("ag_small_grads", gs), pack(w), pack(m), pack(v))
    off = 0
    for n, sz in zip(SMALL, sizes):
        out[n] = [r.reshape(-1)[off:off + sz].reshape(w[n].shape) for r in res]
        off += sz

    return (loss, grad_x, *[out[n][0] for n in WEIGHTS], *[out[n][1] for n in WEIGHTS],
            *[out[n][2] for n in WEIGHTS], *[out[n][3] for n in WEIGHTS])


def kernel(x, p, positions, ffn_a_norm, ffn_a_w1, ffn_a_w3, ffn_a_w2, mix_norm, w_in, q_a_norm, w_uq, kv_a_norm, w_ukv, q_norm, k_norm, gm_v_norm, gm_ws, gm_bs, attn_out_norm, gm_out_norm, w_out, ffn_b_norm, ffn_b_w1, ffn_b_w3, ffn_b_w2, ple_gate_norm, w_ple_gate, w_ple, ple_norm, loss_target, m_ffn_a_norm, m_ffn_a_w1, m_ffn_a_w3, m_ffn_a_w2, m_mix_norm, m_w_in, m_q_a_norm, m_w_uq, m_kv_a_norm, m_w_ukv, m_q_norm, m_k_norm, m_gm_v_norm, m_gm_ws, m_gm_bs, m_attn_out_norm, m_gm_out_norm, m_w_out, m_ffn_b_norm, m_ffn_b_w1, m_ffn_b_w3, m_ffn_b_w2, m_ple_gate_norm, m_w_ple_gate, m_w_ple, m_ple_norm, v_ffn_a_norm, v_ffn_a_w1, v_ffn_a_w3, v_ffn_a_w2, v_mix_norm, v_w_in, v_q_a_norm, v_w_uq, v_kv_a_norm, v_w_ukv, v_q_norm, v_k_norm, v_gm_v_norm, v_gm_ws, v_gm_bs, v_attn_out_norm, v_gm_out_norm, v_w_out, v_ffn_b_norm, v_ffn_b_w1, v_ffn_b_w3, v_ffn_b_w2, v_ple_gate_norm, v_w_ple_gate, v_w_ple, v_ple_norm):
    args = locals()
    w = {n: args[n] for n in WEIGHTS}
    m = {n: args["m_" + n] for n in WEIGHTS}
    v = {n: args["v_" + n] for n in WEIGHTS}
    return _step(x, p, positions, loss_target, w, m, v)
```

```python
import functools

import jax
import jax.numpy as jnp
from jax import lax
from jax.experimental import pallas as pl
from jax.experimental.pallas import tpu as pltpu

F32, BF16 = jnp.float32, jnp.bfloat16
EPS = 1e-6
N_DEV = 8
HEADS = 8
NOPE, ROPE, QK, VD = 128, 64, 192, 128
CHUNK = 128
GROUPS = 8
LANE = 128
ROPE_BASE = 10000.0
ADAM_LR, ADAM_B1, ADAM_B2, ADAM_EPS, ADAM_WD, ADAM_STEP = 0.001, 0.9, 0.999, 1e-08, 0.01, 10
AXES = ("x", "y", "c")
MESH = pl.DeviceIdType.MESH
ANY = pl.BlockSpec(memory_space=pl.ANY)

WEIGHTS = ['ffn_a_norm', 'ffn_a_w1', 'ffn_a_w3', 'ffn_a_w2', 'mix_norm', 'w_in', 'q_a_norm', 'w_uq', 'kv_a_norm',
           'w_ukv', 'q_norm', 'k_norm', 'gm_v_norm', 'gm_ws', 'gm_bs', 'attn_out_norm', 'gm_out_norm', 'w_out',
           'ffn_b_norm', 'ffn_b_w1', 'ffn_b_w3', 'ffn_b_w2', 'ple_gate_norm', 'w_ple_gate', 'w_ple', 'ple_norm']
BIG = ['ffn_a_w1', 'ffn_a_w3', 'ffn_a_w2', 'w_in', 'w_uq', 'w_ukv', 'w_out', 'ffn_b_w1', 'ffn_b_w3', 'ffn_b_w2',
       'w_ple_gate', 'w_ple']
SMALL = [n for n in WEIGHTS if n not in BIG]


def _pcall(body, **kw):
    return pl.pallas_call(body, **kw)


def _pick(n, cands):
    for c in cands:
        if n % c == 0:
            return c
    return n


def _rms(x, g):
    return x * lax.rsqrt(jnp.mean(x * x, axis=-1, keepdims=True) + EPS) * g


@jax.custom_vjp
def _bdot(x, w):
    return jnp.dot(x.astype(BF16), w.astype(BF16), preferred_element_type=F32)


def _bdot_fwd(x, w):
    return _bdot(x, w), (x, w)


def _bdot_bwd(res, dy):
    x, w = res
    dyb = dy.astype(BF16)
    dx = lax.dot_general(dyb, w.astype(BF16), (((1,), (1,)), ((), ())), preferred_element_type=F32)
    dw = lax.dot_general(x.astype(BF16), dyb, (((0,), (0,)), ((), ())), preferred_element_type=F32)
    return dx.astype(x.dtype), dw.astype(w.dtype)


_bdot.defvjp(_bdot_fwd, _bdot_bwd)


def _hdot(x, p):
    return jnp.dot(x, p, precision=lax.Precision.HIGHEST, preferred_element_type=F32)


def _flip(v, bit):
    return 1 - v if bit else v


def _all_gather(name, x):
    def body(x_ref, out_ref, send_sems, recv_sems, local_sem):
        cx, cy, cc = lax.axis_index("x"), lax.axis_index("y"), lax.axis_index("c")
        me, sibling = (cx, cy, cc), (cx, cy, 1 - cc)
        chips = [(1 - cx, cy), (cx, 1 - cy), (1 - cx, 1 - cy)]

        def slot(px, py, pc):
            return out_ref.at[4 * px + 2 * py + pc]

        def copy(k, block, to, src=None):
            return pltpu.make_async_remote_copy(
                src_ref=slot(*block) if src is None else src, dst_ref=slot(*block),
                send_sem=send_sems.at[k], recv_sem=recv_sems.at[k], device_id=to, device_id_type=MESH)

        mine = pltpu.make_async_copy(x_ref, slot(*me), local_sem)
        mine.start()
        first = [copy(0, me, sibling, src=x_ref)]
        first += [copy(1 + j, me, (*chip, cc), src=x_ref) for j, chip in enumerate(chips)]
        for cp in first:
            cp.start()
        passed = [copy(4 + j, (*chip, cc), sibling) for j, chip in enumerate(chips)]
        for j, chip in enumerate(chips):
            copy(1 + j, (*chip, cc), me).wait_recv()
            passed[j].start()
        copy(0, sibling, me).wait_recv()
        for j, chip in enumerate(chips):
            copy(4 + j, (*chip, 1 - cc), me).wait_recv()
        for cp in first + passed:
            cp.wait_send()
        mine.wait()

    return _pcall(
        body, name=name, out_shape=jax.ShapeDtypeStruct((N_DEV,) + x.shape, x.dtype),
        in_specs=[ANY], out_specs=ANY,
        scratch_shapes=[pltpu.SemaphoreType.DMA((7,)), pltpu.SemaphoreType.DMA((7,)), pltpu.SemaphoreType.DMA],
    )(x)


HBM = pl.BlockSpec(memory_space=pltpu.HBM)
SEM = pl.BlockSpec(memory_space=pltpu.SEMAPHORE)
EFFECT = pltpu.SideEffectType.DATAFLOW_SIDE_EFFECTING
PEERS = N_DEV - 1


def _my_index():
    return 4 * lax.axis_index("x") + 2 * lax.axis_index("y") + lax.axis_index("c")


def _landing(own):
    zone = lax.empty((N_DEV,) + own.shape, own.dtype)
    return lax.dynamic_update_slice(zone, own[None], (_my_index(),) + (0,) * own.ndim)


def _peer_copies(src_refs, land_refs, send_sems, recv_sems, scatter):
    cx, cy, cc = lax.axis_index("x"), lax.axis_index("y"), lax.axis_index("c")
    me = 4 * cx + 2 * cy + cc
    out = []
    for t, (src, land) in enumerate(zip(src_refs, land_refs)):
        for k in range(1, N_DEV):
            px, py, pc = _flip(cx, k & 4), _flip(cy, k & 2), _flip(cc, k & 1)
            peer = 4 * px + 2 * py + pc
            s = src.at[peer] if scatter else src
            kw = dict(send_sem=send_sems.at[PEERS * t + k - 1], recv_sem=recv_sems.at[PEERS * t + k - 1],
                      device_id=(px, py, pc), device_id_type=MESH)
            out.append((pltpu.make_async_remote_copy(src_ref=s, dst_ref=land.at[me], **kw),
                        pltpu.make_async_remote_copy(src_ref=s, dst_ref=land.at[peer], **kw)))
    return out


def _xchg_start(name, srcs, lands, scatter, after):
    n, na = len(srcs), len(after)

    def body(*refs):
        src_refs, land_refs = refs[:n], refs[n:2 * n]
        send_sems, recv_sems = refs[2 * n + na], refs[2 * n + na + 1]
        token = refs[-1]
        for send, _ in _peer_copies(src_refs, land_refs, send_sems, recv_sems, scatter):
            send.start()
        token[...] = jnp.zeros_like(token)

    thru = [pltpu.HBM(a.shape, a.dtype) for a in list(srcs) + list(lands)]
    res = _pcall(
        body, name=name,
        out_shape=(pltpu.SemaphoreType.DMA((PEERS * n,)), pltpu.SemaphoreType.DMA((PEERS * n,)), *thru,
                   jax.ShapeDtypeStruct((8, LANE), F32)),
        in_specs=[HBM] * (2 * n) + [ANY] * na,
        out_specs=(SEM, SEM, *([HBM] * (2 * n)), pl.BlockSpec(memory_space=pltpu.VMEM)),
        input_output_aliases={i: 2 + i for i in range(2 * n)},
        compiler_params=pltpu.CompilerParams(has_side_effects=EFFECT),
    )(*[pltpu.with_memory_space_constraint(a, pltpu.HBM) for a in list(srcs) + list(lands)], *after)
    return dict(send=res[0], recv=res[1], srcs=list(res[2:2 + n]), lands=list(res[2 + n:2 + 2 * n]), token=res[-1],
                scatter=scatter)


def _xchg_wait(name, st, after):
    n = len(st['srcs'])
    scatter = st['scatter']

    def body(*refs):
        src_refs, land_refs = refs[:n], refs[n:2 * n]
        send_sems, recv_sems = refs[2 * n], refs[2 * n + 1]
        for _, back in _peer_copies(src_refs, land_refs, send_sems, recv_sems, scatter):
            back.wait_send()
            back.wait_recv()

    bufs = st['srcs'] + st['lands']
    res = _pcall(
        body, name=name, out_shape=tuple(pltpu.HBM(a.shape, a.dtype) for a in bufs),
        in_specs=[HBM] * (2 * n) + [SEM, SEM] + [ANY] * len(after), out_specs=tuple([HBM] * (2 * n)),
        input_output_aliases={i: i for i in range(2 * n)},
        compiler_params=pltpu.CompilerParams(has_side_effects=EFFECT),
    )(*bufs, st['send'], st['recv'], *after)
    return list(res[n:])


_DIMS = {"nn": (((1,), (0,)), ((), ())), "nt": (((1,), (1,)), ((), ())), "tn": (((0,), (0,)), ((), ()))}


def _mm(name, a, b, mode, out_dtype=F32, res=None, alpha=1.0, deps=()):
    if mode == "tn":
        K, M = a.shape
        N = b.shape[1]
    else:
        M, K = a.shape
        N = b.shape[0] if mode == "nt" else b.shape[1]
    tn = _pick(N, (1024, 512, 256))
    tm = _pick(M, (1024, 512, 256, 128) if tn <= 1024 else (512, 256, 128))
    tk = _pick(K, (512, 256, 128))
    nk = K // tk
    a_spec = pl.BlockSpec((tk, tm), lambda i, j, k: (k, i)) if mode == "tn" else pl.BlockSpec((tm, tk), lambda i, j, k: (i, k))
    b_spec = pl.BlockSpec((tn, tk), lambda i, j, k: (j, k)) if mode == "nt" else pl.BlockSpec((tk, tn), lambda i, j, k: (k, j))
    o_spec = pl.BlockSpec((tm, tn), lambda i, j, k: (i, j))
    dims = _DIMS[mode]

    def body(*refs):
        a_ref, b_ref, r_ref = refs[0], refs[1], refs[2]
        o_ref, acc = refs[-2], refs[-1]
        k = pl.program_id(2)

        @pl.when(k == 0)
        def _():
            acc[...] = jnp.zeros_like(acc)

        acc[...] += lax.dot_general(a_ref[...].astype(BF16), b_ref[...].astype(BF16), dims, preferred_element_type=F32)

        @pl.when(k == nk - 1)
        def _():
            r = acc[...] * alpha if alpha != 1.0 else acc[...]
            if res is not None:
                r = r_ref[...] + r
            o_ref[...] = r.astype(o_ref.dtype)

    ins = [a, b] + ([] if res is None else [res]) + list(deps)
    specs = [a_spec, b_spec] + ([] if res is None else [o_spec]) + [ANY] * len(deps)
    return _pcall(body, name=name, grid=(M // tm, N // tn, nk), in_specs=specs, out_specs=o_spec,
                  out_shape=jax.ShapeDtypeStruct((M, N), out_dtype),
                  scratch_shapes=[pltpu.VMEM((tm, tn), F32)])(*ins)


def _mm_tn_batch(name, a3, b3, alpha=1.0):
    ga, T, M = a3.shape
    gb, _, N = b3.shape
    G = max(ga, gb)
    tm = _pick(M, (1024, 512)) if N <= 1024 else M
    tk = _pick(T, (512, 256, 128))
    nk = T // tk
    a_spec = pl.BlockSpec((None, tk, tm), (lambda g, i, k: (g, k, i)) if ga > 1 else (lambda g, i, k: (0, k, i)))
    b_spec = pl.BlockSpec((None, tk, N), (lambda g, i, k: (g, k, 0)) if gb > 1 else (lambda g, i, k: (0, k, 0)))
    o_spec = pl.BlockSpec((None, tm, N), lambda g, i, k: (g, i, 0))

    def body(a_ref, b_ref, o_ref, acc):
        k = pl.program_id(2)

        @pl.when(k == 0)
        def _():
            acc[...] = jnp.zeros_like(acc)

        acc[...] += lax.dot_general(a_ref[...].astype(BF16), b_ref[...].astype(BF16), _DIMS["tn"], preferred_element_type=F32)

        @pl.when(k == nk - 1)
        def _():
            o_ref[...] = (acc[...] * alpha if alpha != 1.0 else acc[...]).astype(o_ref.dtype)

    return _pcall(body, name=name, grid=(G, M // tm, nk), in_specs=[a_spec, b_spec], out_specs=o_spec,
                  out_shape=jax.ShapeDtypeStruct((G, M, N), BF16),
                  scratch_shapes=[pltpu.VMEM((tm, N), F32)])(a3, b3)


def _rowwise(name, fn, ins, outs, T, tm, deps=()):
    in_specs = []
    for arr, spec in ins:
        if spec == "row":
            in_specs.append(pl.BlockSpec((tm, arr.shape[1]), lambda i: (i, 0)))
        elif spec == "full":
            in_specs.append(pl.BlockSpec(arr.shape, lambda i, _n=arr.ndim: (0,) * _n))
        else:
            _, off, width = spec
            in_specs.append(pl.BlockSpec((tm, width), lambda i, _b=off // width: (i, _b)))
    in_specs += [ANY] * len(deps)
    out_specs, out_shapes = [], []
    for shape, dtype, spec in outs:
        out_shapes.append(jax.ShapeDtypeStruct(shape, dtype))
        if spec == "row":
            out_specs.append(pl.BlockSpec((tm, shape[1]), lambda i: (i, 0)))
        else:
            out_specs.append(pl.BlockSpec(shape, lambda i, _n=len(shape): (0,) * _n))
    n_in = len(ins)

    def body(*refs):
        res = fn(*[r[...] for r in refs[:n_in]])
        i = pl.program_id(0)
        for r, (_, _, spec), val in zip(refs[n_in + len(deps):], outs, res):
            if spec == "acc":
                @pl.when(i == 0)
                def _(r=r):
                    r[...] = jnp.zeros_like(r)
                r[...] += val.astype(r.dtype)
            else:
                r[...] = val.astype(r.dtype)

    return _pcall(body, name=name, grid=(T // tm,), in_specs=in_specs, out_specs=out_specs, out_shape=out_shapes)(
        *[a for a, _ in ins], *deps)


def _rms_fwd(name, h, g, deps=()):
    T, D = h.shape
    return _rowwise(name, lambda hv, gv: (_rms(hv, gv),), [(h, "row"), (g, "full")], [((T, D), BF16, "row")], T,
                    _pick(T, (512, 256, 128)), deps)[0]


def _rms_bwd(name, h, g, dxn, dh_in):
    T, D = h.shape

    def fn(hv, gv, dv, dh0):
        _, vjp = jax.vjp(_rms, hv, gv)
        dh, dg = vjp(dv.astype(F32))
        return dh0 + dh, dg

    return _rowwise(name, fn, [(h, "row"), (g, "full"), (dxn, "row"), (dh_in, "row")],
                    [((T, D), F32, "row"), ((1, D), F32, "acc")], T, _pick(T, (256, 128)))


def _ffn_fwd(name, xn, h, w1, w3, w2):
    T, D = xn.shape
    F8 = w1.shape[-1]
    tm = _pick(T, (512, 256, 128))
    wspec = lambda r, c: pl.BlockSpec((None, r, c), lambda i, d: (d, 0, 0))
    row = pl.BlockSpec((tm, D), lambda i, d: (i, 0))
    hid = pl.BlockSpec((None, tm, F8), lambda i, d: (d, i, 0))

    def body(xn_ref, h_ref, w1_ref, w3_ref, w2_ref, out_ref, h1_ref, h3_ref, acc):
        d = pl.program_id(1)

        @pl.when(d == 0)
        def _():
            acc[...] = jnp.zeros_like(acc)

        x = xn_ref[...]
        h1 = jnp.dot(x, w1_ref[...], preferred_element_type=F32)
        h3 = jnp.dot(x, w3_ref[...], preferred_element_type=F32)
        h1_ref[...] = h1.astype(BF16)
        h3_ref[...] = h3.astype(BF16)
        act = (h1 * jax.nn.sigmoid(h1) * h3).astype(BF16)
        acc[...] += jnp.dot(act, w2_ref[...], preferred_element_type=F32)

        @pl.when(d == N_DEV - 1)
        def _():
            out_ref[...] = h_ref[...] + 0.5 * acc[...]

    return _pcall(body, name=name, grid=(T // tm, N_DEV),
                  in_specs=[row, row, wspec(D, F8), wspec(D, F8), wspec(F8, D)],
                  out_specs=[row, hid, hid],
                  out_shape=[jax.ShapeDtypeStruct((T, D), F32), jax.ShapeDtypeStruct((N_DEV, T, F8), BF16),
                             jax.ShapeDtypeStruct((N_DEV, T, F8), BF16)],
                  scratch_shapes=[pltpu.VMEM((tm, D), F32)])(xn, h, w1, w3, w2)


def _ffn_bwd(name, dy, h1, h3, w1, w3, w2, deps=()):
    T, D = dy.shape
    F8 = w1.shape[-1]
    tm = _pick(T, (512, 256, 128))
    wspec = lambda r, c: pl.BlockSpec((None, r, c), lambda i, d: (d, 0, 0))
    row = pl.BlockSpec((tm, D), lambda i, d: (i, 0))
    hid = pl.BlockSpec((None, tm, F8), lambda i, d: (d, i, 0))

    def body(*refs):
        dy_ref, h1_ref, h3_ref, w1_ref, w3_ref, w2_ref = refs[:6]
        dxn_ref, dh1_ref, dh3_ref, act_ref, dyb, acc = refs[6 + len(deps):]
        d = pl.program_id(1)

        @pl.when(d == 0)
        def _():
            acc[...] = jnp.zeros_like(acc)
            dyb[...] = dy_ref[...].astype(BF16)

        dact = 0.5 * lax.dot_general(dyb[...], w2_ref[...], _DIMS["nt"], preferred_element_type=F32)
        h1 = h1_ref[...].astype(F32)
        h3 = h3_ref[...].astype(F32)
        sig = jax.nn.sigmoid(h1)
        silu = h1 * sig
        dh1 = (dact * h3 * (sig * (1.0 + h1 * (1.0 - sig)))).astype(BF16)
        dh3 = (dact * silu).astype(BF16)
        dh1_ref[...] = dh1
        dh3_ref[...] = dh3
        act_ref[...] = (silu * h3).astype(BF16)
        acc[...] += (lax.dot_general(dh1, w1_ref[...], _DIMS["nt"], preferred_element_type=F32)
                     + lax.dot_general(dh3, w3_ref[...], _DIMS["nt"], preferred_element_type=F32))

        @pl.when(d == N_DEV - 1)
        def _():
            dxn_ref[...] = acc[...]

    hshape = jax.ShapeDtypeStruct((N_DEV, T, F8), BF16)
    return _pcall(body, name=name, grid=(T // tm, N_DEV),
                  in_specs=[row, hid, hid, wspec(D, F8), wspec(D, F8), wspec(F8, D)] + [ANY] * len(deps),
                  out_specs=[row, hid, hid, hid],
                  out_shape=[jax.ShapeDtypeStruct((T, D), F32), hshape, hshape, hshape],
                  scratch_shapes=[pltpu.VMEM((tm, D), BF16), pltpu.VMEM((tm, D), F32)])(dy, h1, h3, w1, w3, w2, *deps)


def _rot_matrix():
    i = jnp.arange(QK)[:, None]
    j = jnp.arange(QK)[None, :]
    half = ROPE // 2
    first = (j >= NOPE) & (j < NOPE + half) & (i == j + half)
    second = (j >= NOPE + half) & (i == j - half)
    return jnp.where(first, -1.0, jnp.where(second, 1.0, 0.0)).astype(F32)


def _mla_fn(cq, ckv, kr128, cos, sin, rot, qa_g, kva_g, qn_g, kn_g, w_uq, w_ukv):
    cqn = _rms(cq, qa_g)
    ckvn = _rms(ckv, kva_g)
    kr = kr128[:, :ROPE]
    qs, ks, vs = [], [], []
    for h in range(HEADS):
        qh = _rms(_bdot(cqn, w_uq[h]), qn_g)
        qs.append(qh * cos + _hdot(qh, rot) * sin)
        kvh = _bdot(ckvn, w_ukv[h])
        kh = _rms(jnp.concatenate([kvh[:, :NOPE], kr], axis=-1), kn_g)
        ks.append(kh * cos + _hdot(kh, rot) * sin)
        vs.append(kvh[:, NOPE:])
    return qs, ks, vs


def _mla_specs(z, tabs, small, w_uq, w_ukv, tm, offs):
    o_cq, o_ckv, o_kr = offs
    row = lambda w: pl.BlockSpec((tm, w), lambda i: (i, 0))
    col = lambda off, w: pl.BlockSpec((tm, w), lambda i: (i, off // w))
    full2 = lambda a: pl.BlockSpec(a.shape, lambda i: (0, 0))
    wsp = lambda a: pl.BlockSpec(a.shape, lambda i: (0, 0, 0))
    cq_w, ckv_w = w_uq.shape[1], w_ukv.shape[1]
    ins = [z, z, z, tabs[0], tabs[1], tabs[2]] + list(small) + [w_uq, w_ukv]
    specs = ([col(o_cq, cq_w), col(o_ckv, ckv_w), col(o_kr, LANE), row(QK), row(QK), full2(tabs[2])]
             + [full2(s) for s in small] + [wsp(w_uq), wsp(w_ukv)])
    return ins, specs


def _mla_prep_fwd(name, z, tabs, small, w_uq, w_ukv, offs):
    T = z.shape[0]
    tm = _pick(T, (256, 128))
    ins, specs = _mla_specs(z, tabs, small, w_uq, w_ukv, tm, offs)
    head = lambda w: pl.BlockSpec((HEADS, tm, w), lambda i: (0, i, 0))

    def body(*refs):
        vals = [r[...] for r in refs[:12]]
        q_ref, k_ref, v_ref = refs[12:]
        qs, ks, vs = _mla_fn(*vals)
        for h in range(HEADS):
            q_ref[h] = qs[h].astype(BF16)
            k_ref[h] = ks[h].astype(BF16)
            v_ref[h] = vs[h].astype(BF16)

    return _pcall(body, name=name, grid=(T // tm,), in_specs=specs, out_specs=[head(QK), head(QK), head(VD)],
                  out_shape=[jax.ShapeDtypeStruct((HEADS, T, QK), BF16), jax.ShapeDtypeStruct((HEADS, T, QK), BF16),
                             jax.ShapeDtypeStruct((HEADS, T, VD), BF16)])(*ins)


def _mla_prep_bwd(name, z, tabs, small, w_uq, w_ukv, offs, dq, dk, dv):
    T = z.shape[0]
    tm = _pick(T, (256, 128))
    ins, specs = _mla_specs(z, tabs, small, w_uq, w_ukv, tm, offs)
    head = lambda w: pl.BlockSpec((HEADS, tm, w), lambda i: (0, i, 0))
    ins += [dq, dk, dv]
    specs += [head(QK), head(QK), head(VD)]
    cq_w, ckv_w = w_uq.shape[1], w_ukv.shape[1]
    acc_shapes = [s.shape for s in small] + [w_uq.shape, w_ukv.shape]
    row_shapes = [(T, cq_w), (T, ckv_w), (T, LANE)]
    out_shape = [jax.ShapeDtypeStruct(s, BF16) for s in row_shapes] + [jax.ShapeDtypeStruct(s, F32) for s in acc_shapes]
    out_specs = ([pl.BlockSpec((tm, s[1]), lambda i: (i, 0)) for s in row_shapes]
                 + [pl.BlockSpec(s, lambda i, _n=len(s): (0,) * _n) for s in acc_shapes])

    def body(*refs):
        cq, ckv, kr128, cos, sin, rot, qa_g, kva_g, qn_g, kn_g, w_uq_v, w_ukv_v = [r[...] for r in refs[:12]]
        dq_ref, dk_ref, dv_ref = refs[12:15]
        outs = refs[15:]
        f = lambda a, b, c, g1, g2, g3, g4, wq, wkv: _mla_fn(a, b, c, cos, sin, rot, g1, g2, g3, g4, wq, wkv)
        _, vjp = jax.vjp(f, cq, ckv, kr128, qa_g, kva_g, qn_g, kn_g, w_uq_v.astype(F32), w_ukv_v.astype(F32))
        cts = ([dq_ref[h] for h in range(HEADS)], [dk_ref[h] for h in range(HEADS)], [dv_ref[h] for h in range(HEADS)])
        grads = vjp(cts)
        i = pl.program_id(0)
        for n, (r, gval) in enumerate(zip(outs, grads)):
            if n < 3:
                r[...] = gval.astype(r.dtype)
            else:
                @pl.when(i == 0)
                def _(r=r):
                    r[...] = jnp.zeros_like(r)
                r[...] += gval

    return _pcall(body, name=name, grid=(T // tm,), in_specs=specs, out_specs=out_specs, out_shape=out_shape)(*ins)


NEG = -1e30


def _tri(t):
    return lax.broadcasted_iota(jnp.int32, (t, t), 1) <= lax.broadcasted_iota(jnp.int32, (t, t), 0)


def _attn_tiles(T):
    t = _pick(T, (512, 256, 128))
    return t, T // t


def _attn_fwd(name, q, k, v):
    H, T, _ = q.shape
    t, n = _attn_tiles(T)
    scale = QK ** -0.5

    def body(q_ref, k_ref, v_ref, o_ref, lse_ref, m_s, l_s, acc):
        qi, ki = pl.program_id(1), pl.program_id(2)

        @pl.when(ki == 0)
        def _():
            m_s[...] = jnp.full_like(m_s, NEG)
            l_s[...] = jnp.zeros_like(l_s)
            acc[...] = jnp.zeros_like(acc)

        def tile(diagonal):
            s = lax.dot_general(q_ref[...], k_ref[...], _DIMS["nt"], preferred_element_type=F32) * scale
            if diagonal:
                s = jnp.where(_tri(t), s, NEG)
            m_new = jnp.maximum(m_s[...], jnp.max(s, axis=-1, keepdims=True))
            alpha = jnp.exp(m_s[...] - m_new)
            p = jnp.exp(s - m_new)
            l_s[...] = alpha * l_s[...] + jnp.sum(p, axis=-1, keepdims=True)
            acc[...] = alpha * acc[...] + jnp.dot(p.astype(BF16), v_ref[...], preferred_element_type=F32)
            m_s[...] = m_new

        @pl.when(ki < qi)
        def _():
            tile(False)

        @pl.when(ki == qi)
        def _():
            tile(True)
            o_ref[...] = acc[...] / l_s[...]
            lse_ref[...] = m_s[...] + jnp.log(l_s[...])

    kv = lambda w: pl.BlockSpec((None, t, w), lambda h, qi, ki: (h, jnp.minimum(ki, qi), 0))
    return _pcall(body, name=name, grid=(H, n, n),
                  in_specs=[pl.BlockSpec((None, t, QK), lambda h, qi, ki: (h, qi, 0)), kv(QK), kv(VD)],
                  out_specs=[pl.BlockSpec((t, VD), lambda h, qi, ki: (qi, h)),
                             pl.BlockSpec((None, t, 1), lambda h, qi, ki: (h, qi, 0))],
                  out_shape=[jax.ShapeDtypeStruct((T, H * VD), F32), jax.ShapeDtypeStruct((H, T, 1), F32)],
                  scratch_shapes=[pltpu.VMEM((t, 1), F32), pltpu.VMEM((t, 1), F32), pltpu.VMEM((t, VD), F32)])(q, k, v)


def _attn_bwd(name, q, k, v, o, lse, do):
    H, T, _ = q.shape
    t, n = _attn_tiles(T)
    scale = QK ** -0.5

    def body(q_ref, k_ref, v_ref, o_ref, lse_ref, do_ref, dq_ref, dk_ref, dv_ref, dk_acc, dv_acc):
        ki, qi = pl.program_id(1), pl.program_id(2)

        @pl.when((ki == 0) & (qi == 0))
        def _():
            dq_ref[...] = jnp.zeros_like(dq_ref)

        @pl.when(qi == 0)
        def _():
            dk_acc[...] = jnp.zeros_like(dk_acc)
            dv_acc[...] = jnp.zeros_like(dv_acc)

        def tile(diagonal):
            qv, kv_, dov = q_ref[...], k_ref[...], do_ref[...]
            s = lax.dot_general(qv, kv_, _DIMS["nt"], preferred_element_type=F32) * scale
            p = jnp.exp(s - lse_ref[...])
            if diagonal:
                p = jnp.where(_tri(t), p, 0.0)
            dob = dov.astype(BF16)
            delta = jnp.sum(o_ref[...] * dov, axis=-1, keepdims=True)
            dv_acc[...] += lax.dot_general(p.astype(BF16), dob, _DIMS["tn"], preferred_element_type=F32)
            dp = lax.dot_general(dob, v_ref[...], _DIMS["nt"], preferred_element_type=F32)
            ds = (p * (dp - delta) * scale).astype(BF16)
            rs = pl.ds(pl.multiple_of(qi * t, t), t)
            dq_ref[rs, :] += jnp.dot(ds, kv_, preferred_element_type=F32)
            dk_acc[...] += lax.dot_general(ds, qv, _DIMS["tn"], preferred_element_type=F32)

        @pl.when(qi > ki)
        def _():
            tile(False)

        @pl.when(qi == ki)
        def _():
            tile(True)

        @pl.when(qi == n - 1)
        def _():
            dk_ref[...] = dk_acc[...]
            dv_ref[...] = dv_acc[...]

    qrow = lambda w: pl.BlockSpec((None, t, w), lambda h, ki, qi: (h, jnp.maximum(qi, ki), 0))
    krow = lambda w: pl.BlockSpec((None, t, w), lambda h, ki, qi: (h, ki, 0))
    wide = pl.BlockSpec((t, VD), lambda h, ki, qi: (jnp.maximum(qi, ki), h))
    return _pcall(body, name=name, grid=(H, n, n),
                  in_specs=[qrow(QK), krow(QK), krow(VD), wide, qrow(1), wide],
                  out_specs=[pl.BlockSpec((None, T, QK), lambda h, ki, qi: (h, 0, 0)), krow(QK), krow(VD)],
                  out_shape=[jax.ShapeDtypeStruct((H, T, QK), F32), jax.ShapeDtypeStruct((H, T, QK), F32),
                             jax.ShapeDtypeStruct((H, T, VD), F32)],
                  scratch_shapes=[pltpu.VMEM((t, QK), F32), pltpu.VMEM((t, VD), F32)])(q, k, v, o, lse, do)


def _tril():
    return lax.broadcasted_iota(jnp.int32, (CHUNK, CHUNK), 1) <= lax.broadcasted_iota(jnp.int32, (CHUNK, CHUNK), 0)


@jax.custom_vjp
def _gm_gate(v, ws, b_t):
    wc = jnp.where(_tril()[None], ws, 0.0).astype(BF16)
    vb = v.astype(BF16)
    rows = []
    for c in range(v.shape[0] // CHUNK):
        cols = []
        for g in range(GROUPS):
            vc = vb[c * CHUNK:(c + 1) * CHUNK, g * LANE:(g + 1) * LANE]
            cols.append(jnp.dot(wc[g], vc, preferred_element_type=F32) + jnp.broadcast_to(b_t[:, g:g + 1], (CHUNK, LANE)))
        rows.append(jnp.concatenate(cols, axis=-1))
    return jnp.concatenate(rows, axis=0)


def _gm_gate_fwd(v, ws, b_t):
    return _gm_gate(v, ws, b_t), (v, ws)


def _gm_gate_bwd(res, dgate):
    v, ws = res
    tril = _tril()
    wc = jnp.where(tril[None], ws, 0.0).astype(BF16)
    vb = v.astype(BF16)
    dgb = dgate.astype(BF16)
    dws = [jnp.zeros((CHUNK, CHUNK), F32) for _ in range(GROUPS)]
    db = jnp.zeros((CHUNK, GROUPS), F32)
    lane_g = lax.broadcasted_iota(jnp.int32, (1, GROUPS), 1)
    rows = []
    for c in range(v.shape[0] // CHUNK):
        cols = []
        for g in range(GROUPS):
            sl = (slice(c * CHUNK, (c + 1) * CHUNK), slice(g * LANE, (g + 1) * LANE))
            cols.append(lax.dot_general(wc[g], dgb[sl], _DIMS["tn"], preferred_element_type=F32))
            dws[g] = dws[g] + lax.dot_general(dgb[sl], vb[sl], _DIMS["nt"], preferred_element_type=F32)
            db = db + jnp.sum(dgate[sl], axis=1, keepdims=True) * (lane_g == g).astype(F32)
        rows.append(jnp.concatenate(cols, axis=-1))
    dws = jnp.stack([jnp.where(tril, d, 0.0) for d in dws])
    return jnp.concatenate(rows, axis=0), dws, db


_gm_gate.defvjp(_gm_gate_fwd, _gm_gate_bwd)


def _mix_fn(a_out, zu, zv, aon_g, gon_g, vn_g, ws, b_t):
    u = jax.nn.gelu(zu)
    vv = _rms(jax.nn.gelu(zv), vn_g)
    g_out = u * _gm_gate(vv, ws, b_t)
    return jnp.concatenate([_rms(a_out, aon_g), _rms(g_out, gon_g)], axis=-1)


def _mix_ins(a_out, z, small, offs):
    gw = a_out.shape[1]
    return [(a_out, "row"), (z, ("cols", offs[0], gw)), (z, ("cols", offs[1], gw))] + [(s, "full") for s in small]


def _mix_fwd(name, a_out, z, small, offs):
    T, gw = a_out.shape
    return _rowwise(name, lambda *a: (_mix_fn(*a),), _mix_ins(a_out, z, small, offs), [((T, 2 * gw), BF16, "row")],
                    T, _pick(T, (256, 128)))[0]


def _mix_bwd(name, a_out, z, small, offs, dmixed):
    T, gw = a_out.shape

    def fn(*a):
        _, vjp = jax.vjp(_mix_fn, *a[:-1])
        return vjp(a[-1].astype(F32))

    outs = [((T, gw), F32, "row"), ((T, gw), BF16, "row"), ((T, gw), BF16, "row")] + [(s.shape, F32, "acc") for s in small]
    return _rowwise(name, fn, _mix_ins(a_out, z, small, offs) + [(dmixed, "row")], outs, T, _pick(T, (256, 128)))


def _ple_fn(gl, pe, g):
    return jax.nn.sigmoid(gl) * _rms(pe, g)


def _ple_fwd(name, h, gl, pe, g):
    T, D = h.shape
    return _rowwise(name, lambda hv, a, b, c: (hv + _ple_fn(a, b, c),),
                    [(h, "row"), (gl, "row"), (pe, "row"), (g, "full")], [((T, D), F32, "row")], T, _pick(T, (256, 128)))[0]


def _ple_bwd(name, gl, pe, g, dh, deps=()):
    T, D = gl.shape

    def fn(a, b, c, d):
        _, vjp = jax.vjp(_ple_fn, a, b, c)
        return vjp(d)

    return _rowwise(name, fn, [(gl, "row"), (pe, "row"), (g, "full"), (dh, "row")],
                    [((T, D), BF16, "row"), ((T, D), BF16, "row"), ((1, D), F32, "acc")], T, _pick(T, (256, 128)), deps)


def _loss(name, y, target):
    T, D = y.shape

    def fn(yv, tv):
        err = yv - tv
        part = 0.5 * jnp.sum(jnp.mean(err * err, axis=-1, keepdims=True), axis=0, keepdims=True)
        return err * (1.0 / D), jnp.broadcast_to(part, (8, LANE))

    return _rowwise(name, fn, [(y, "row"), (target, "row")], [((T, D), F32, "row"), ((8, LANE), F32, "acc")], T,
                    _pick(T, (512, 256, 128)))


ADAMW_BLOCK_ELEMS = 128 * 1024


def _adamw_sum(name, parts, w, m, v):
    L, R, C = w.shape
    tr = _pick(R, [c for c in (512, 256, 128, 64, 32, 16, 8) if c * C <= ADAMW_BLOCK_ELEMS])
    nr = R // tr
    c1 = 1.0 - ADAM_B1 ** ADAM_STEP
    c2 = 1.0 - ADAM_B2 ** ADAM_STEP

    def body(*refs):
        p_refs = refs[:L]
        w_ref, m_ref, v_ref, g_out, d_out, m_out, v_out = refs[L:]
        layer = pl.program_id(0)

        def part(s):
            val = p_refs[0][s].astype(F32)
            for j in range(1, L):
                val = jnp.where(layer == j, p_refs[j][s].astype(F32), val)
            return val

        g = part(0)
        for s in range(1, N_DEV):
            g = g + part(s)
        m2 = ADAM_B1 * m_ref[...] + (1.0 - ADAM_B1) * g
        v2 = ADAM_B2 * v_ref[...] + (1.0 - ADAM_B2) * (g * g)
        g_out[...] = g
        m_out[...] = m2
        v_out[...] = v2
        d_out[...] = -ADAM_LR * ((m2 / c1) / (jnp.sqrt(v2 / c2) + ADAM_EPS) + ADAM_WD * w_ref[...])

    def part_spec(j):
        return pl.BlockSpec((N_DEV, tr, C), lambda l, i: (0, jnp.where(l == j, i, jnp.where(l < j, 0, nr - 1)), 0))

    blk = pl.BlockSpec((None, tr, C), lambda l, i: (l, i, 0))
    sd = jax.ShapeDtypeStruct((L, R, C), F32)
    return _pcall(body, name=name, grid=(L, nr),
                  in_specs=[part_spec(j) for j in range(L)] + [blk, blk, blk],
                  out_specs=[blk, blk, blk, blk], out_shape=[sd, sd, sd, sd])(*parts, w, m, v)


def _unshard_cols(g):
    _, K, n = g.shape
    return g.transpose(1, 0, 2).reshape(K, N_DEV * n)


def _shard_cols(full):
    K, N = full.shape
    return full.reshape(K, N_DEV, N // N_DEV).transpose(1, 0, 2)


def _unshard_rows(g):
    _, k, N = g.shape
    return g.reshape(N_DEV * k, N)


def _shard_rows(full):
    K, N = full.shape
    return full.reshape(N_DEV, K // N_DEV, N)


STAGES = (('ffn_a_w1', 'ffn_a_w3', 'ffn_a_w2'), ('w_in', 'w_uq', 'w_ukv', 'w_out'),
          ('ffn_b_w1', 'ffn_b_w3', 'ffn_b_w2'), ('w_ple_gate', 'w_ple'))


def _step(x, p, positions, target, w, m, v):
    T, D = x.shape[1], x.shape[2]
    L = p.shape[0]
    x2, target2 = x[0], target[0]
    q_rank, kv_rank = w['w_uq'].shape[1], w['w_ukv'].shape[1]
    gw = w['gm_v_norm'].shape[1]

    inv_freq = ROPE_BASE ** (-jnp.arange(0, ROPE, 2, dtype=F32) / ROPE)
    ang = positions[0].astype(F32)[:, None] * inv_freq
    cos = jnp.concatenate([jnp.ones((T, NOPE), F32), jnp.cos(ang), jnp.cos(ang)], axis=-1)
    sin = jnp.concatenate([jnp.zeros((T, NOPE), F32), jnp.sin(ang), jnp.sin(ang)], axis=-1)
    tabs = (cos, sin, _rot_matrix())

    groups = [(l, names) for l in range(L) for names in STAGES]

    def ag_start(k, after):
        l, names = groups[k]
        shards = [w[n][l].astype(BF16) for n in names]
        return _xchg_start(f"ag_start{k}", shards, [_landing(s) for s in shards], False, after)

    ag = {0: ag_start(0, [])}
    ag[1] = ag_start(1, [ag[0]['token']])

    def fetch(k, after):
        got = _xchg_wait(f"ag_wait{k}", ag[k], after)
        deps = []
        if k + 2 < len(groups):
            ag[k + 2] = ag_start(k + 2, [got[0]])
            deps = [ag[k + 2]['token']]
        return dict(zip(groups[k][1], got)), deps

    s0, s1, s2, s3 = q_rank, q_rank + kv_rank, q_rank + kv_rank + ROPE, q_rank + kv_rank + ROPE + gw
    o_u, o_v, o_cq, o_ckv, o_kr = 0, gw, 2 * gw, 2 * gw + q_rank, 2 * gw + q_rank + kv_rank
    kr_pad = 2 * LANE - ROPE

    def g2(name, l):
        return w[name][l][None, :]

    gm_bt = [w['gm_bs'][l].T for l in range(L)]

    saved = []
    h = x2
    for l in range(L):
        s = {}
        wa, deps = fetch(4 * l, [h])
        s['wa'] = wa
        s['h0'] = h
        s['xn_a'] = _rms_fwd(f"rms_a{l}", h, g2('ffn_a_norm', l), deps)
        h, s['a_h1'], s['a_h3'] = _ffn_fwd(f"ffn_a_fwd{l}", s['xn_a'], h, wa['ffn_a_w1'], wa['ffn_a_w3'], wa['ffn_a_w2'])
        wm, deps = fetch(4 * l + 1, [h])
        w_in_full = _unshard_cols(wm['w_in'])
        s['w_in'] = jnp.concatenate([w_in_full[:, s2:s3], w_in_full[:, s3:], w_in_full[:, :s0], w_in_full[:, s0:s1],
                                     w_in_full[:, s1:s2], jnp.zeros((D, kr_pad), BF16)], axis=-1)
        s['w_out'] = _unshard_rows(wm['w_out'])
        s['w_uq'], s['w_ukv'] = wm['w_uq'], wm['w_ukv']
        s['h1'] = h
        s['n'] = _rms_fwd(f"rms_mix{l}", h, g2('mix_norm', l), deps)
        s['z'] = _mm(f"w_in{l}", s['n'], s['w_in'], "nn")
        s['mla_small'] = [g2('q_a_norm', l), g2('kv_a_norm', l), g2('q_norm', l), g2('k_norm', l)]
        s['q'], s['k'], s['v'] = _mla_prep_fwd(f"mla_prep{l}", s['z'], tabs, s['mla_small'], s['w_uq'], s['w_ukv'],
                                               (o_cq, o_ckv, o_kr))
        s['a_out'], s['lse'] = _attn_fwd(f"attn_fwd{l}", s['q'], s['k'], s['v'])
        s['mix_small'] = [g2('attn_out_norm', l), g2('gm_out_norm', l), g2('gm_v_norm', l), w['gm_ws'][l], gm_bt[l]]
        s['mixed'] = _mix_fwd(f"mix_fwd{l}", s['a_out'], s['z'], s['mix_small'], (o_u, o_v))
        h = _mm(f"w_out{l}", s['mixed'], s['w_out'], "nn", res=h)
        wb, deps = fetch(4 * l + 2, [h])
        s['wb'] = wb
        s['h2'] = h
        s['xn_b'] = _rms_fwd(f"rms_b{l}", h, g2('ffn_b_norm', l), deps)
        h, s['b_h1'], s['b_h3'] = _ffn_fwd(f"ffn_b_fwd{l}", s['xn_b'], h, wb['ffn_b_w1'], wb['ffn_b_w3'], wb['ffn_b_w2'])
        wp, deps = fetch(4 * l + 3, [h])
        s['w_gate'] = _unshard_rows(wp['w_ple_gate'])
        s['h3'] = h
        s['xn_g'] = _rms_fwd(f"rms_g{l}", h, g2('ple_gate_norm', l), deps)
        s['gl'] = _mm(f"w_gate{l}", s['xn_g'], s['w_gate'], "nn")
        s['p'] = p[l, 0]
        s['pe'] = _mm(f"w_ple{l}", s['p'], _unshard_cols(wp['w_ple']), "nn")
        h = _ple_fwd(f"ple_fwd{l}", h, s['gl'], s['pe'], g2('ple_norm', l))
        saved.append(s)

    dh, loss_part = _loss("loss", h, target2)
    loss = lax.psum(loss_part[0, 0], AXES)

    gsmall = {n: [None] * L for n in SMALL}
    rs = {}

    def rs_start(k, grads):
        lands = [lax.dynamic_update_slice(lax.empty(g.shape, g.dtype),
                                          lax.dynamic_index_in_dim(g, _my_index(), 0, keepdims=True),
                                          (_my_index(),) + (0,) * (g.ndim - 1)) for g in grads]
        rs[k] = _xchg_start(f"rs_start{k}", grads, lands, True, [])
        return [rs[k]['token']]

    def ffn_backward(tag, l, k, dh, xn, h_in, h1, h3, wts, norm_name, deps):
        pre = 'ffn_' + tag
        dxn, dh1, dh3, act = _ffn_bwd(f"{pre}_bwd{l}", dh, h1, h3, wts[pre + '_w1'], wts[pre + '_w3'], wts[pre + '_w2'], deps)
        grads = [_mm_tn_batch(f"{pre}_dw1_{l}", xn[None], dh1), _mm_tn_batch(f"{pre}_dw3_{l}", xn[None], dh3),
                 _mm_tn_batch(f"{pre}_dw2_{l}", act, dh[None], alpha=0.5)]
        dh, gsmall[norm_name][l] = _rms_bwd(f"rms_{tag}_bwd{l}", h_in, g2(norm_name, l), dxn, dh)
        return dh, rs_start(k, grads)

    deps = []
    for l in reversed(range(L)):
        s = saved[l]
        d_gl, d_pe, gsmall['ple_norm'][l] = _ple_bwd(f"ple_bwd{l}", s['gl'], s['pe'], g2('ple_norm', l), dh, deps)
        g_ple = _mm(f"dw_ple{l}", s['p'], d_pe, "tn", out_dtype=BF16)
        g_gate = _mm(f"dw_gate{l}", s['xn_g'], d_gl, "tn", out_dtype=BF16)
        d_xng = _mm(f"d_xng{l}", d_gl, s['w_gate'], "nt")
        dh, gsmall['ple_gate_norm'][l] = _rms_bwd(f"rms_g_bwd{l}", s['h3'], g2('ple_gate_norm', l), d_xng, dh)
        deps = rs_start(4 * l + 3, [_shard_rows(g_gate), _shard_cols(g_ple)])
        dh, deps = ffn_backward('b', l, 4 * l + 2, dh, s['xn_b'], s['h2'], s['b_h1'], s['b_h3'], s['wb'], 'ffn_b_norm', deps)
        g_out = _mm(f"dw_out{l}", s['mixed'], dh, "tn", out_dtype=BF16, deps=deps)
        d_mixed = _mm(f"d_mixed{l}", dh, s['w_out'], "nt")
        mix = _mix_bwd(f"mix_bwd{l}", s['a_out'], s['z'], s['mix_small'], (o_u, o_v), d_mixed)
        d_a_out, d_u, d_v = mix[:3]
        gsmall['attn_out_norm'][l], gsmall['gm_out_norm'][l], gsmall['gm_v_norm'][l], gsmall['gm_ws'][l] = mix[3:7]
        gsmall['gm_bs'][l] = mix[7].T
        dq, dk, dv = _attn_bwd(f"attn_bwd{l}", s['q'], s['k'], s['v'], s['a_out'], s['lse'], d_a_out)
        mla = _mla_prep_bwd(f"mla_prep_bwd{l}", s['z'], tabs, s['mla_small'], s['w_uq'], s['w_ukv'], (o_cq, o_ckv, o_kr),
                            dq, dk, dv)
        d_cq, d_ckv, d_kr = mla[:3]
        gsmall['q_a_norm'][l], gsmall['kv_a_norm'][l], gsmall['q_norm'][l], gsmall['k_norm'][l] = mla[3:7]
        dz = jnp.concatenate([d_u, d_v, d_cq, d_ckv, d_kr, jnp.zeros((T, LANE), BF16)], axis=-1)
        g_in = _mm(f"dw_in{l}", s['n'], dz, "tn", out_dtype=BF16)
        g_in = jnp.concatenate([g_in[:, o_cq:o_cq + q_rank], g_in[:, o_ckv:o_ckv + kv_rank],
                                g_in[:, o_kr:o_kr + ROPE], g_in[:, o_u:o_u + gw], g_in[:, o_v:o_v + gw]], axis=-1)
        d_n = _mm(f"d_n{l}", dz, s['w_in'], "nt")
        dh, gsmall['mix_norm'][l] = _rms_bwd(f"rms_mix_bwd{l}", s['h1'], g2('mix_norm', l), d_n, dh)
        deps = rs_start(4 * l + 1, [_shard_cols(g_in), mla[7].astype(BF16), mla[8].astype(BF16), _shard_rows(g_out)])
        dh, deps = ffn_backward('a', l, 4 * l, dh, s['xn_a'], s['h0'], s['a_h1'], s['a_h3'], s['wa'], 'ffn_a_norm', deps)
    grad_x = dh[None]

    out = {}
    sizes = [w[n].size for n in SMALL]
    total = sum(sizes)
    padded = -(-total // (512 * LANE)) * (512 * LANE)

    def pack(d):
        flat = jnp.concatenate([d[n].reshape(-1) for n in SMALL] + [jnp.zeros((padded - total,), F32)])
        return flat.reshape(1, padded // LANE, LANE)

    gs = pack({n: jnp.stack([gsmall[n][l].reshape(w[n].shape[1:]) for l in range(L)]) for n in SMALL})
    res = _adamw_sum("adamw_small", [_all_gather("ag_small_grads", gs[0])], pack(w), pack(m), pack(v))
    off = 0
    for n, sz in zip(SMALL, sizes):
        out[n] = [r.reshape(-1)[off:off + sz].reshape(w[n].shape) for r in res]
        off += sz

    after = [dh, res[0]]
    for stage in (3, 2, 1, 0):
        recv = {}
        for l in reversed(range(L)):
            got = _xchg_wait(f"rs_wait{4 * l + stage}", rs[4 * l + stage], after)
            recv.update({(n, l): a for n, a in zip(STAGES[stage], got)})
        for n in STAGES[stage]:
            out[n] = _adamw_sum("adamw_" + n, [recv[(n, l)] for l in range(L)], w[n], m[n], v[n])
            after = [out[n][0]]

    return (loss, grad_x, *[out[n][0] for n in WEIGHTS], *[out[n][1] for n in WEIGHTS],
            *[out[n][2] for n in WEIGHTS], *[out[n][3] for n in WEIGHTS])


def kernel(x, p, positions, ffn_a_norm, ffn_a_w1, ffn_a_w3, ffn_a_w2, mix_norm, w_in, q_a_norm, w_uq, kv_a_norm, w_ukv, q_norm, k_norm, gm_v_norm, gm_ws, gm_bs, attn_out_norm, gm_out_norm, w_out, ffn_b_norm, ffn_b_w1, ffn_b_w3, ffn_b_w2, ple_gate_norm, w_ple_gate, w_ple, ple_norm, loss_target, m_ffn_a_norm, m_ffn_a_w1, m_ffn_a_w3, m_ffn_a_w2, m_mix_norm, m_w_in, m_q_a_norm, m_w_uq, m_kv_a_norm, m_w_ukv, m_q_norm, m_k_norm, m_gm_v_norm, m_gm_ws, m_gm_bs, m_attn_out_norm, m_gm_out_norm, m_w_out, m_ffn_b_norm, m_ffn_b_w1, m_ffn_b_w3, m_ffn_b_w2, m_ple_gate_norm, m_w_ple_gate, m_w_ple, m_ple_norm, v_ffn_a_norm, v_ffn_a_w1, v_ffn_a_w3, v_ffn_a_w2, v_mix_norm, v_w_in, v_q_a_norm, v_w_uq, v_kv_a_norm, v_w_ukv, v_q_norm, v_k_norm, v_gm_v_norm, v_gm_ws, v_gm_bs, v_attn_out_norm, v_gm_out_norm, v_w_out, v_ffn_b_norm, v_ffn_b_w1, v_ffn_b_w3, v_ffn_b_w2, v_ple_gate_norm, v_w_ple_gate, v_w_ple, v_ple_norm):
    args = locals()
    w = {n: args[n] for n in WEIGHTS}
    m = {n: args["m_" + n] for n in WEIGHTS}
    v = {n: args["v_" + n] for n in WEIGHTS}
    return _step(x, p, positions, loss_target, w, m, v)
```

```python
import functools

import jax
import jax.numpy as jnp
from jax import lax
from jax.experimental import pallas as pl
from jax.experimental.pallas import tpu as pltpu

F32, BF16 = jnp.float32, jnp.bfloat16
EPS = 1e-6
N_DEV = 8
HEADS = 8
NOPE, ROPE, QK, VD = 128, 64, 192, 128
CHUNK = 128
GROUPS = 8
LANE = 128
ROPE_BASE = 10000.0
ADAM_LR, ADAM_B1, ADAM_B2, ADAM_EPS, ADAM_WD, ADAM_STEP = 0.001, 0.9, 0.999, 1e-08, 0.01, 10
AXES = ("x", "y", "c")
MESH = pl.DeviceIdType.MESH
ANY = pl.BlockSpec(memory_space=pl.ANY)

WEIGHTS = ['ffn_a_norm', 'ffn_a_w1', 'ffn_a_w3', 'ffn_a_w2', 'mix_norm', 'w_in', 'q_a_norm', 'w_uq', 'kv_a_norm',
           'w_ukv', 'q_norm', 'k_norm', 'gm_v_norm', 'gm_ws', 'gm_bs', 'attn_out_norm', 'gm_out_norm', 'w_out',
           'ffn_b_norm', 'ffn_b_w1', 'ffn_b_w3', 'ffn_b_w2', 'ple_gate_norm', 'w_ple_gate', 'w_ple', 'ple_norm']
BIG = ['ffn_a_w1', 'ffn_a_w3', 'ffn_a_w2', 'w_in', 'w_uq', 'w_ukv', 'w_out', 'ffn_b_w1', 'ffn_b_w3', 'ffn_b_w2',
       'w_ple_gate', 'w_ple']
SMALL = [n for n in WEIGHTS if n not in BIG]


def _pcall(body, **kw):
    return pl.pallas_call(body, **kw)


def _pick(n, cands):
    for c in cands:
        if n % c == 0:
            return c
    return n


def _rms(x, g):
    return x * lax.rsqrt(jnp.mean(x * x, axis=-1, keepdims=True) + EPS) * g


@jax.custom_vjp
def _bdot(x, w):
    return jnp.dot(x.astype(BF16), w.astype(BF16), preferred_element_type=F32)


def _bdot_fwd(x, w):
    return _bdot(x, w), (x, w)


def _bdot_bwd(res, dy):
    x, w = res
    dyb = dy.astype(BF16)
    dx = lax.dot_general(dyb, w.astype(BF16), (((1,), (1,)), ((), ())), preferred_element_type=F32)
    dw = lax.dot_general(x.astype(BF16), dyb, (((0,), (0,)), ((), ())), preferred_element_type=F32)
    return dx.astype(x.dtype), dw.astype(w.dtype)


_bdot.defvjp(_bdot_fwd, _bdot_bwd)


def _hdot(x, p):
    return jnp.dot(x, p, precision=lax.Precision.HIGHEST, preferred_element_type=F32)


def _flip(v, bit):
    return 1 - v if bit else v


def _all_gather(name, x):
    def body(x_ref, out_ref, send_sems, recv_sems, local_sem):
        cx, cy, cc = lax.axis_index("x"), lax.axis_index("y"), lax.axis_index("c")
        me, sibling = (cx, cy, cc), (cx, cy, 1 - cc)
        chips = [(1 - cx, cy), (cx, 1 - cy), (1 - cx, 1 - cy)]

        def slot(px, py, pc):
            return out_ref.at[4 * px + 2 * py + pc]

        def copy(k, block, to, src=None):
            return pltpu.make_async_remote_copy(
                src_ref=slot(*block) if src is None else src, dst_ref=slot(*block),
                send_sem=send_sems.at[k], recv_sem=recv_sems.at[k], device_id=to, device_id_type=MESH)

        mine = pltpu.make_async_copy(x_ref, slot(*me), local_sem)
        mine.start()
        first = [copy(0, me, sibling, src=x_ref)]
        first += [copy(1 + j, me, (*chip, cc), src=x_ref) for j, chip in enumerate(chips)]
        for cp in first:
            cp.start()
        passed = [copy(4 + j, (*chip, cc), sibling) for j, chip in enumerate(chips)]
        for j, chip in enumerate(chips):
            copy(1 + j, (*chip, cc), me).wait_recv()
            passed[j].start()
        copy(0, sibling, me).wait_recv()
        for j, chip in enumerate(chips):
            copy(4 + j, (*chip, 1 - cc), me).wait_recv()
        for cp in first + passed:
            cp.wait_send()
        mine.wait()

    return _pcall(
        body, name=name, out_shape=jax.ShapeDtypeStruct((N_DEV,) + x.shape, x.dtype),
        in_specs=[ANY], out_specs=ANY,
        scratch_shapes=[pltpu.SemaphoreType.DMA((7,)), pltpu.SemaphoreType.DMA((7,)), pltpu.SemaphoreType.DMA],
    )(x)


HBM = pl.BlockSpec(memory_space=pltpu.HBM)
SEM = pl.BlockSpec(memory_space=pltpu.SEMAPHORE)
EFFECT = pltpu.SideEffectType.DATAFLOW_SIDE_EFFECTING
PEERS = N_DEV - 1


def _my_index():
    return 4 * lax.axis_index("x") + 2 * lax.axis_index("y") + lax.axis_index("c")


def _landing(own):
    zone = lax.empty((N_DEV,) + own.shape, own.dtype)
    return lax.dynamic_update_slice(zone, own[None], (_my_index(),) + (0,) * own.ndim)


COPIES = {"gather": PEERS, "scatter": PEERS, "chips": 4, "forward": 3}


def _copy_plan(kind, src_refs, land_refs, send_sems, recv_sems):
    cx, cy, cc = lax.axis_index("x"), lax.axis_index("y"), lax.axis_index("c")
    me = 4 * cx + 2 * cy + cc
    per = COPIES[kind]
    out = []
    for t, land in enumerate(land_refs):
        def pair(i, src, to_slot, from_slot, dev):
            kw = dict(send_sem=send_sems.at[per * t + i], recv_sem=recv_sems.at[per * t + i], device_id=dev,
                      device_id_type=MESH)
            out.append((pltpu.make_async_remote_copy(src_ref=src, dst_ref=land.at[to_slot], **kw),
                        pltpu.make_async_remote_copy(src_ref=src, dst_ref=land.at[from_slot], **kw)))

        if kind in ("gather", "scatter"):
            for k in range(1, N_DEV):
                px, py, pc = _flip(cx, k & 4), _flip(cy, k & 2), _flip(cc, k & 1)
                peer = 4 * px + 2 * py + pc
                pair(k - 1, src_refs[t].at[peer] if kind == "scatter" else src_refs[t], me, peer, (px, py, pc))
        elif kind == "chips":
            pair(0, src_refs[t], me, me + 1 - 2 * cc, (cx, cy, 1 - cc))
            for j in range(1, 4):
                px, py = _flip(cx, j & 2), _flip(cy, j & 1)
                pair(j, src_refs[t], me, 4 * px + 2 * py + cc, (px, py, cc))
        else:
            for j in range(1, 4):
                px, py = _flip(cx, j & 2), _flip(cy, j & 1)
                mine, theirs = 4 * px + 2 * py + cc, 4 * px + 2 * py + 1 - cc
                pair(j - 1, land.at[mine], mine, theirs, (cx, cy, 1 - cc))
    return out


def _xchg_start(name, kind, srcs, lands, after):
    ns, nb, na = len(srcs), len(srcs) + len(lands), len(after)
    n_sems = COPIES[kind] * len(lands)

    def body(*refs):
        send_sems, recv_sems = refs[nb + na], refs[nb + na + 1]
        for send, _ in _copy_plan(kind, refs[:ns], refs[ns:nb], send_sems, recv_sems):
            send.start()
        refs[-1][...] = jnp.zeros_like(refs[-1])

    bufs = list(srcs) + list(lands)
    res = _pcall(
        body, name=name,
        out_shape=(pltpu.SemaphoreType.DMA((n_sems,)), pltpu.SemaphoreType.DMA((n_sems,)),
                   *[pltpu.HBM(a.shape, a.dtype) for a in bufs], jax.ShapeDtypeStruct((8, LANE), F32)),
        in_specs=[HBM] * nb + [ANY] * na,
        out_specs=(SEM, SEM, *([HBM] * nb), pl.BlockSpec(memory_space=pltpu.VMEM)),
        input_output_aliases={i: 2 + i for i in range(nb)},
        compiler_params=pltpu.CompilerParams(has_side_effects=EFFECT),
    )(*[pltpu.with_memory_space_constraint(a, pltpu.HBM) for a in bufs], *after)
    return dict(kind=kind, send=res[0], recv=res[1], srcs=list(res[2:2 + ns]), lands=list(res[2 + ns:2 + nb]),
                token=res[-1])


def _xchg_wait(name, st, after):
    ns, nb = len(st['srcs']), len(st['srcs']) + len(st['lands'])

    def body(*refs):
        for _, back in _copy_plan(st['kind'], refs[:ns], refs[ns:nb], refs[nb], refs[nb + 1]):
            back.wait_send()
            back.wait_recv()

    bufs = st['srcs'] + st['lands']
    res = _pcall(
        body, name=name, out_shape=tuple(pltpu.HBM(a.shape, a.dtype) for a in bufs),
        in_specs=[HBM] * nb + [SEM, SEM] + [ANY] * len(after), out_specs=tuple([HBM] * nb),
        input_output_aliases={i: i for i in range(nb)},
        compiler_params=pltpu.CompilerParams(has_side_effects=EFFECT),
    )(*bufs, st['send'], st['recv'], *after)
    return list(res[ns:])


_DIMS = {"nn": (((1,), (0,)), ((), ())), "nt": (((1,), (1,)), ((), ())), "tn": (((0,), (0,)), ((), ()))}


def _mm(name, a, b, mode, out_dtype=F32, res=None, alpha=1.0, deps=()):
    if mode == "tn":
        K, M = a.shape
        N = b.shape[1]
    else:
        M, K = a.shape
        N = b.shape[0] if mode == "nt" else b.shape[1]
    tn = _pick(N, (1024, 512, 256))
    tm = _pick(M, (1024, 512, 256, 128) if tn <= 1024 else (512, 256, 128))
    tk = _pick(K, (512, 256, 128))
    nk = K // tk
    a_spec = pl.BlockSpec((tk, tm), lambda i, j, k: (k, i)) if mode == "tn" else pl.BlockSpec((tm, tk), lambda i, j, k: (i, k))
    b_spec = pl.BlockSpec((tn, tk), lambda i, j, k: (j, k)) if mode == "nt" else pl.BlockSpec((tk, tn), lambda i, j, k: (k, j))
    o_spec = pl.BlockSpec((tm, tn), lambda i, j, k: (i, j))
    dims = _DIMS[mode]

    def body(*refs):
        a_ref, b_ref, r_ref = refs[0], refs[1], refs[2]
        o_ref, acc = refs[-2], refs[-1]
        k = pl.program_id(2)

        @pl.when(k == 0)
        def _():
            acc[...] = jnp.zeros_like(acc)

        acc[...] += lax.dot_general(a_ref[...].astype(BF16), b_ref[...].astype(BF16), dims, preferred_element_type=F32)

        @pl.when(k == nk - 1)
        def _():
            r = acc[...] * alpha if alpha != 1.0 else acc[...]
            if res is not None:
                r = r_ref[...] + r
            o_ref[...] = r.astype(o_ref.dtype)

    ins = [a, b] + ([] if res is None else [res]) + list(deps)
    specs = [a_spec, b_spec] + ([] if res is None else [o_spec]) + [ANY] * len(deps)
    return _pcall(body, name=name, grid=(M // tm, N // tn, nk), in_specs=specs, out_specs=o_spec,
                  out_shape=jax.ShapeDtypeStruct((M, N), out_dtype),
                  scratch_shapes=[pltpu.VMEM((tm, tn), F32)])(*ins)


def _mm_tn_batch(name, a3, b3, alpha=1.0, deps=()):
    ga, T, M = a3.shape
    gb, _, N = b3.shape
    G = max(ga, gb)
    tm = _pick(M, (1024, 512)) if N <= 1024 else M
    tk = _pick(T, (512, 256, 128))
    nk = T // tk
    a_spec = pl.BlockSpec((None, tk, tm), (lambda g, i, k: (g, k, i)) if ga > 1 else (lambda g, i, k: (0, k, i)))
    b_spec = pl.BlockSpec((None, tk, N), (lambda g, i, k: (g, k, 0)) if gb > 1 else (lambda g, i, k: (0, k, 0)))
    o_spec = pl.BlockSpec((None, tm, N), lambda g, i, k: (g, i, 0))

    def body(*refs):
        a_ref, b_ref, o_ref, acc = refs[0], refs[1], refs[-2], refs[-1]
        k = pl.program_id(2)

        @pl.when(k == 0)
        def _():
            acc[...] = jnp.zeros_like(acc)

        acc[...] += lax.dot_general(a_ref[...].astype(BF16), b_ref[...].astype(BF16), _DIMS["tn"], preferred_element_type=F32)

        @pl.when(k == nk - 1)
        def _():
            o_ref[...] = (acc[...] * alpha if alpha != 1.0 else acc[...]).astype(o_ref.dtype)

    return _pcall(body, name=name, grid=(G, M // tm, nk), in_specs=[a_spec, b_spec] + [ANY] * len(deps),
                  out_specs=o_spec, out_shape=jax.ShapeDtypeStruct((G, M, N), BF16),
                  scratch_shapes=[pltpu.VMEM((tm, N), F32)])(a3, b3, *deps)


def _rowwise(name, fn, ins, outs, T, tm, deps=()):
    in_specs = []
    for arr, spec in ins:
        if spec == "row":
            in_specs.append(pl.BlockSpec((tm, arr.shape[1]), lambda i: (i, 0)))
        elif spec == "full":
            in_specs.append(pl.BlockSpec(arr.shape, lambda i, _n=arr.ndim: (0,) * _n))
        else:
            _, off, width = spec
            in_specs.append(pl.BlockSpec((tm, width), lambda i, _b=off // width: (i, _b)))
    in_specs += [ANY] * len(deps)
    out_specs, out_shapes = [], []
    for shape, dtype, spec in outs:
        out_shapes.append(jax.ShapeDtypeStruct(shape, dtype))
        if spec == "row":
            out_specs.append(pl.BlockSpec((tm, shape[1]), lambda i: (i, 0)))
        else:
            out_specs.append(pl.BlockSpec(shape, lambda i, _n=len(shape): (0,) * _n))
    n_in = len(ins)

    def body(*refs):
        res = fn(*[r[...] for r in refs[:n_in]])
        i = pl.program_id(0)
        for r, (_, _, spec), val in zip(refs[n_in + len(deps):], outs, res):
            if spec == "acc":
                @pl.when(i == 0)
                def _(r=r):
                    r[...] = jnp.zeros_like(r)
                r[...] += val.astype(r.dtype)
            else:
                r[...] = val.astype(r.dtype)

    return _pcall(body, name=name, grid=(T // tm,), in_specs=in_specs, out_specs=out_specs, out_shape=out_shapes)(
        *[a for a, _ in ins], *deps)


def _rms_fwd(name, h, g, deps=()):
    T, D = h.shape
    return _rowwise(name, lambda hv, gv: (_rms(hv, gv),), [(h, "row"), (g, "full")], [((T, D), BF16, "row")], T,
                    _pick(T, (512, 256, 128)), deps)[0]


def _rms_bwd(name, h, g, dxn, dh_in, deps=()):
    T, D = h.shape

    def fn(hv, gv, dv, dh0):
        _, vjp = jax.vjp(_rms, hv, gv)
        dh, dg = vjp(dv.astype(F32))
        return dh0 + dh, dg

    return _rowwise(name, fn, [(h, "row"), (g, "full"), (dxn, "row"), (dh_in, "row")],
                    [((T, D), F32, "row"), ((1, D), F32, "acc")], T, _pick(T, (256, 128)), deps)


def _ffn_fwd(name, xn, h, w1, w3, w2):
    T, D = xn.shape
    F8 = w1.shape[-1]
    tm = _pick(T, (512, 256, 128))
    wspec = lambda r, c: pl.BlockSpec((None, r, c), lambda i, d: (d, 0, 0))
    row = pl.BlockSpec((tm, D), lambda i, d: (i, 0))
    hid = pl.BlockSpec((None, tm, F8), lambda i, d: (d, i, 0))

    def body(xn_ref, h_ref, w1_ref, w3_ref, w2_ref, out_ref, h1_ref, h3_ref, acc):
        d = pl.program_id(1)

        @pl.when(d == 0)
        def _():
            acc[...] = jnp.zeros_like(acc)

        x = xn_ref[...]
        h1 = jnp.dot(x, w1_ref[...], preferred_element_type=F32)
        h3 = jnp.dot(x, w3_ref[...], preferred_element_type=F32)
        h1_ref[...] = h1.astype(BF16)
        h3_ref[...] = h3.astype(BF16)
        act = (h1 * jax.nn.sigmoid(h1) * h3).astype(BF16)
        acc[...] += jnp.dot(act, w2_ref[...], preferred_element_type=F32)

        @pl.when(d == N_DEV - 1)
        def _():
            out_ref[...] = h_ref[...] + 0.5 * acc[...]

    return _pcall(body, name=name, grid=(T // tm, N_DEV),
                  in_specs=[row, row, wspec(D, F8), wspec(D, F8), wspec(F8, D)],
                  out_specs=[row, hid, hid],
                  out_shape=[jax.ShapeDtypeStruct((T, D), F32), jax.ShapeDtypeStruct((N_DEV, T, F8), BF16),
                             jax.ShapeDtypeStruct((N_DEV, T, F8), BF16)],
                  scratch_shapes=[pltpu.VMEM((tm, D), F32)])(xn, h, w1, w3, w2)


def _ffn_bwd(name, dy, h1, h3, w1, w3, w2, deps=()):
    T, D = dy.shape
    F8 = w1.shape[-1]
    tm = _pick(T, (512, 256, 128))
    wspec = lambda r, c: pl.BlockSpec((None, r, c), lambda i, d: (d, 0, 0))
    row = pl.BlockSpec((tm, D), lambda i, d: (i, 0))
    hid = pl.BlockSpec((None, tm, F8), lambda i, d: (d, i, 0))

    def body(*refs):
        dy_ref, h1_ref, h3_ref, w1_ref, w3_ref, w2_ref = refs[:6]
        dxn_ref, dh1_ref, dh3_ref, act_ref, dyb, acc = refs[6 + len(deps):]
        d = pl.program_id(1)

        @pl.when(d == 0)
        def _():
            acc[...] = jnp.zeros_like(acc)
            dyb[...] = dy_ref[...].astype(BF16)

        dact = 0.5 * lax.dot_general(dyb[...], w2_ref[...], _DIMS["nt"], preferred_element_type=F32)
        h1 = h1_ref[...].astype(F32)
        h3 = h3_ref[...].astype(F32)
        sig = jax.nn.sigmoid(h1)
        silu = h1 * sig
        dh1 = (dact * h3 * (sig * (1.0 + h1 * (1.0 - sig)))).astype(BF16)
        dh3 = (dact * silu).astype(BF16)
        dh1_ref[...] = dh1
        dh3_ref[...] = dh3
        act_ref[...] = (silu * h3).astype(BF16)
        acc[...] += (lax.dot_general(dh1, w1_ref[...], _DIMS["nt"], preferred_element_type=F32)
                     + lax.dot_general(dh3, w3_ref[...], _DIMS["nt"], preferred_element_type=F32))

        @pl.when(d == N_DEV - 1)
        def _():
            dxn_ref[...] = acc[...]

    hshape = jax.ShapeDtypeStruct((N_DEV, T, F8), BF16)
    return _pcall(body, name=name, grid=(T // tm, N_DEV),
                  in_specs=[row, hid, hid, wspec(D, F8), wspec(D, F8), wspec(F8, D)] + [ANY] * len(deps),
                  out_specs=[row, hid, hid, hid],
                  out_shape=[jax.ShapeDtypeStruct((T, D), F32), hshape, hshape, hshape],
                  scratch_shapes=[pltpu.VMEM((tm, D), BF16), pltpu.VMEM((tm, D), F32)])(dy, h1, h3, w1, w3, w2, *deps)


def _rot_matrix():
    i = jnp.arange(QK)[:, None]
    j = jnp.arange(QK)[None, :]
    half = ROPE // 2
    first = (j >= NOPE) & (j < NOPE + half) & (i == j + half)
    second = (j >= NOPE + half) & (i == j - half)
    return jnp.where(first, -1.0, jnp.where(second, 1.0, 0.0)).astype(F32)


def _mla_fn(cq, ckv, kr128, cos, sin, rot, qa_g, kva_g, qn_g, kn_g, w_uq, w_ukv):
    cqn = _rms(cq, qa_g)
    ckvn = _rms(ckv, kva_g)
    kr = kr128[:, :ROPE]
    qs, ks, vs = [], [], []
    for h in range(HEADS):
        qh = _rms(_bdot(cqn, w_uq[h]), qn_g)
        qs.append(qh * cos + _hdot(qh, rot) * sin)
        kvh = _bdot(ckvn, w_ukv[h])
        kh = _rms(jnp.concatenate([kvh[:, :NOPE], kr], axis=-1), kn_g)
        ks.append(kh * cos + _hdot(kh, rot) * sin)
        vs.append(kvh[:, NOPE:])
    return qs, ks, vs


def _mla_specs(z, tabs, small, w_uq, w_ukv, tm, offs):
    o_cq, o_ckv, o_kr = offs
    row = lambda w: pl.BlockSpec((tm, w), lambda i: (i, 0))
    col = lambda off, w: pl.BlockSpec((tm, w), lambda i: (i, off // w))
    full2 = lambda a: pl.BlockSpec(a.shape, lambda i: (0, 0))
    wsp = lambda a: pl.BlockSpec(a.shape, lambda i: (0, 0, 0))
    cq_w, ckv_w = w_uq.shape[1], w_ukv.shape[1]
    ins = [z, z, z, tabs[0], tabs[1], tabs[2]] + list(small) + [w_uq, w_ukv]
    specs = ([col(o_cq, cq_w), col(o_ckv, ckv_w), col(o_kr, LANE), row(QK), row(QK), full2(tabs[2])]
             + [full2(s) for s in small] + [wsp(w_uq), wsp(w_ukv)])
    return ins, specs


def _mla_prep_fwd(name, z, tabs, small, w_uq, w_ukv, offs):
    T = z.shape[0]
    tm = _pick(T, (256, 128))
    ins, specs = _mla_specs(z, tabs, small, w_uq, w_ukv, tm, offs)
    head = lambda w: pl.BlockSpec((HEADS, tm, w), lambda i: (0, i, 0))

    def body(*refs):
        vals = [r[...] for r in refs[:12]]
        q_ref, k_ref, v_ref = refs[12:]
        qs, ks, vs = _mla_fn(*vals)
        for h in range(HEADS):
            q_ref[h] = qs[h].astype(BF16)
            k_ref[h] = ks[h].astype(BF16)
            v_ref[h] = vs[h].astype(BF16)

    return _pcall(body, name=name, grid=(T // tm,), in_specs=specs, out_specs=[head(QK), head(QK), head(VD)],
                  out_shape=[jax.ShapeDtypeStruct((HEADS, T, QK), BF16), jax.ShapeDtypeStruct((HEADS, T, QK), BF16),
                             jax.ShapeDtypeStruct((HEADS, T, VD), BF16)])(*ins)


def _mla_prep_bwd(name, z, tabs, small, w_uq, w_ukv, offs, dq, dk, dv):
    T = z.shape[0]
    tm = _pick(T, (256, 128))
    ins, specs = _mla_specs(z, tabs, small, w_uq, w_ukv, tm, offs)
    head = lambda w: pl.BlockSpec((HEADS, tm, w), lambda i: (0, i, 0))
    ins += [dq, dk, dv]
    specs += [head(QK), head(QK), head(VD)]
    cq_w, ckv_w = w_uq.shape[1], w_ukv.shape[1]
    acc_shapes = [s.shape for s in small] + [w_uq.shape, w_ukv.shape]
    row_shapes = [(T, cq_w), (T, ckv_w), (T, LANE)]
    out_shape = [jax.ShapeDtypeStruct(s, BF16) for s in row_shapes] + [jax.ShapeDtypeStruct(s, F32) for s in acc_shapes]
    out_specs = ([pl.BlockSpec((tm, s[1]), lambda i: (i, 0)) for s in row_shapes]
                 + [pl.BlockSpec(s, lambda i, _n=len(s): (0,) * _n) for s in acc_shapes])

    def body(*refs):
        cq, ckv, kr128, cos, sin, rot, qa_g, kva_g, qn_g, kn_g, w_uq_v, w_ukv_v = [r[...] for r in refs[:12]]
        dq_ref, dk_ref, dv_ref = refs[12:15]
        outs = refs[15:]
        f = lambda a, b, c, g1, g2, g3, g4, wq, wkv: _mla_fn(a, b, c, cos, sin, rot, g1, g2, g3, g4, wq, wkv)
        _, vjp = jax.vjp(f, cq, ckv, kr128, qa_g, kva_g, qn_g, kn_g, w_uq_v.astype(F32), w_ukv_v.astype(F32))
        cts = ([dq_ref[h] for h in range(HEADS)], [dk_ref[h] for h in range(HEADS)], [dv_ref[h] for h in range(HEADS)])
        grads = vjp(cts)
        i = pl.program_id(0)
        for n, (r, gval) in enumerate(zip(outs, grads)):
            if n < 3:
                r[...] = gval.astype(r.dtype)
            else:
                @pl.when(i == 0)
                def _(r=r):
                    r[...] = jnp.zeros_like(r)
                r[...] += gval

    return _pcall(body, name=name, grid=(T // tm,), in_specs=specs, out_specs=out_specs, out_shape=out_shape)(*ins)


NEG = -1e30


def _tri(t):
    return lax.broadcasted_iota(jnp.int32, (t, t), 1) <= lax.broadcasted_iota(jnp.int32, (t, t), 0)


def _attn_tiles(T):
    t = _pick(T, (1024, 512, 256, 128))
    return t, T // t


def _attn_fwd(name, q, k, v):
    H, T, _ = q.shape
    t, n = _attn_tiles(T)
    scale = QK ** -0.5

    def body(q_ref, k_ref, v_ref, o_ref, lse_ref, m_s, l_s, acc):
        qi, ki = pl.program_id(1), pl.program_id(2)

        @pl.when(ki == 0)
        def _():
            m_s[...] = jnp.full_like(m_s, NEG)
            l_s[...] = jnp.zeros_like(l_s)
            acc[...] = jnp.zeros_like(acc)

        def tile(diagonal):
            s = lax.dot_general(q_ref[...], k_ref[...], _DIMS["nt"], preferred_element_type=F32) * scale
            if diagonal:
                s = jnp.where(_tri(t), s, NEG)
            m_new = jnp.maximum(m_s[...], jnp.max(s, axis=-1, keepdims=True))
            alpha = jnp.exp(m_s[...] - m_new)
            p = jnp.exp(s - m_new)
            l_s[...] = alpha * l_s[...] + jnp.sum(p, axis=-1, keepdims=True)
            acc[...] = alpha * acc[...] + jnp.dot(p.astype(BF16), v_ref[...], preferred_element_type=F32)
            m_s[...] = m_new

        @pl.when(ki < qi)
        def _():
            tile(False)

        @pl.when(ki == qi)
        def _():
            tile(True)
            o_ref[...] = acc[...] / l_s[...]
            lse_ref[...] = m_s[...] + jnp.log(l_s[...])

    kv = lambda w: pl.BlockSpec((None, t, w), lambda h, qi, ki: (h, jnp.minimum(ki, qi), 0))
    return _pcall(body, name=name, grid=(H, n, n),
                  in_specs=[pl.BlockSpec((None, t, QK), lambda h, qi, ki: (h, qi, 0)), kv(QK), kv(VD)],
                  out_specs=[pl.BlockSpec((t, VD), lambda h, qi, ki: (qi, h)),
                             pl.BlockSpec((None, t, 1), lambda h, qi, ki: (h, qi, 0))],
                  out_shape=[jax.ShapeDtypeStruct((T, H * VD), F32), jax.ShapeDtypeStruct((H, T, 1), F32)],
                  scratch_shapes=[pltpu.VMEM((t, 1), F32), pltpu.VMEM((t, 1), F32), pltpu.VMEM((t, VD), F32)])(q, k, v)


def _attn_bwd(name, q, k, v, o, lse, do):
    H, T, _ = q.shape
    t, n = _attn_tiles(T)
    scale = QK ** -0.5

    def body(q_ref, k_ref, v_ref, o_ref, lse_ref, do_ref, dq_ref, dk_ref, dv_ref, dk_acc, dv_acc):
        ki, qi = pl.program_id(1), pl.program_id(2)

        @pl.when((ki == 0) & (qi == 0))
        def _():
            dq_ref[...] = jnp.zeros_like(dq_ref)

        @pl.when(qi == 0)
        def _():
            dk_acc[...] = jnp.zeros_like(dk_acc)
            dv_acc[...] = jnp.zeros_like(dv_acc)

        def tile(diagonal):
            qv, kv_, dov = q_ref[...], k_ref[...], do_ref[...]
            s = lax.dot_general(qv, kv_, _DIMS["nt"], preferred_element_type=F32) * scale
            p = jnp.exp(s - lse_ref[...])
            if diagonal:
                p = jnp.where(_tri(t), p, 0.0)
            dob = dov.astype(BF16)
            delta = jnp.sum(o_ref[...] * dov, axis=-1, keepdims=True)
            dv_acc[...] += lax.dot_general(p.astype(BF16), dob, _DIMS["tn"], preferred_element_type=F32)
            dp = lax.dot_general(dob, v_ref[...], _DIMS["nt"], preferred_element_type=F32)
            ds = (p * (dp - delta) * scale).astype(BF16)
            rs = pl.ds(pl.multiple_of(qi * t, t), t)
            dq_ref[rs, :] += jnp.dot(ds, kv_, preferred_element_type=F32)
            dk_acc[...] += lax.dot_general(ds, qv, _DIMS["tn"], preferred_element_type=F32)

        @pl.when(qi > ki)
        def _():
            tile(False)

        @pl.when(qi == ki)
        def _():
            tile(True)

        @pl.when(qi == n - 1)
        def _():
            dk_ref[...] = dk_acc[...]
            dv_ref[...] = dv_acc[...]

    qrow = lambda w: pl.BlockSpec((None, t, w), lambda h, ki, qi: (h, jnp.maximum(qi, ki), 0))
    krow = lambda w: pl.BlockSpec((None, t, w), lambda h, ki, qi: (h, ki, 0))
    wide = pl.BlockSpec((t, VD), lambda h, ki, qi: (jnp.maximum(qi, ki), h))
    return _pcall(body, name=name, grid=(H, n, n),
                  in_specs=[qrow(QK), krow(QK), krow(VD), wide, qrow(1), wide],
                  out_specs=[pl.BlockSpec((None, T, QK), lambda h, ki, qi: (h, 0, 0)), krow(QK), krow(VD)],
                  out_shape=[jax.ShapeDtypeStruct((H, T, QK), F32), jax.ShapeDtypeStruct((H, T, QK), F32),
                             jax.ShapeDtypeStruct((H, T, VD), F32)],
                  scratch_shapes=[pltpu.VMEM((t, QK), F32), pltpu.VMEM((t, VD), F32)])(q, k, v, o, lse, do)


def _tril():
    return lax.broadcasted_iota(jnp.int32, (CHUNK, CHUNK), 1) <= lax.broadcasted_iota(jnp.int32, (CHUNK, CHUNK), 0)


@jax.custom_vjp
def _gm_gate(v, ws, b_t):
    wc = jnp.where(_tril()[None], ws, 0.0).astype(BF16)
    vb = v.astype(BF16)
    rows = []
    for c in range(v.shape[0] // CHUNK):
        cols = []
        for g in range(GROUPS):
            vc = vb[c * CHUNK:(c + 1) * CHUNK, g * LANE:(g + 1) * LANE]
            cols.append(jnp.dot(wc[g], vc, preferred_element_type=F32) + jnp.broadcast_to(b_t[:, g:g + 1], (CHUNK, LANE)))
        rows.append(jnp.concatenate(cols, axis=-1))
    return jnp.concatenate(rows, axis=0)


def _gm_gate_fwd(v, ws, b_t):
    return _gm_gate(v, ws, b_t), (v, ws)


def _gm_gate_bwd(res, dgate):
    v, ws = res
    tril = _tril()
    wc = jnp.where(tril[None], ws, 0.0).astype(BF16)
    vb = v.astype(BF16)
    dgb = dgate.astype(BF16)
    dws = [jnp.zeros((CHUNK, CHUNK), F32) for _ in range(GROUPS)]
    db = jnp.zeros((CHUNK, GROUPS), F32)
    lane_g = lax.broadcasted_iota(jnp.int32, (1, GROUPS), 1)
    rows = []
    for c in range(v.shape[0] // CHUNK):
        cols = []
        for g in range(GROUPS):
            sl = (slice(c * CHUNK, (c + 1) * CHUNK), slice(g * LANE, (g + 1) * LANE))
            cols.append(lax.dot_general(wc[g], dgb[sl], _DIMS["tn"], preferred_element_type=F32))
            dws[g] = dws[g] + lax.dot_general(dgb[sl], vb[sl], _DIMS["nt"], preferred_element_type=F32)
            db = db + jnp.sum(dgate[sl], axis=1, keepdims=True) * (lane_g == g).astype(F32)
        rows.append(jnp.concatenate(cols, axis=-1))
    dws = jnp.stack([jnp.where(tril, d, 0.0) for d in dws])
    return jnp.concatenate(rows, axis=0), dws, db


_gm_gate.defvjp(_gm_gate_fwd, _gm_gate_bwd)


def _mix_fn(a_out, zu, zv, aon_g, gon_g, vn_g, ws, b_t):
    u = jax.nn.gelu(zu)
    vv = _rms(jax.nn.gelu(zv), vn_g)
    g_out = u * _gm_gate(vv, ws, b_t)
    return jnp.concatenate([_rms(a_out, aon_g), _rms(g_out, gon_g)], axis=-1)


def _mix_ins(a_out, z, small, offs):
    gw = a_out.shape[1]
    return [(a_out, "row"), (z, ("cols", offs[0], gw)), (z, ("cols", offs[1], gw))] + [(s, "full") for s in small]


def _mix_fwd(name, a_out, z, small, offs):
    T, gw = a_out.shape
    return _rowwise(name, lambda *a: (_mix_fn(*a),), _mix_ins(a_out, z, small, offs), [((T, 2 * gw), BF16, "row")],
                    T, _pick(T, (256, 128)))[0]


def _mix_bwd(name, a_out, z, small, offs, dmixed):
    T, gw = a_out.shape

    def fn(*a):
        _, vjp = jax.vjp(_mix_fn, *a[:-1])
        return vjp(a[-1].astype(F32))

    outs = [((T, gw), F32, "row"), ((T, gw), BF16, "row"), ((T, gw), BF16, "row")] + [(s.shape, F32, "acc") for s in small]
    return _rowwise(name, fn, _mix_ins(a_out, z, small, offs) + [(dmixed, "row")], outs, T, _pick(T, (256, 128)))


def _ple_fn(gl, pe, g):
    return jax.nn.sigmoid(gl) * _rms(pe, g)


def _ple_fwd(name, h, gl, pe, g):
    T, D = h.shape
    return _rowwise(name, lambda hv, a, b, c: (hv + _ple_fn(a, b, c),),
                    [(h, "row"), (gl, "row"), (pe, "row"), (g, "full")], [((T, D), F32, "row")], T, _pick(T, (256, 128)))[0]


def _ple_bwd(name, gl, pe, g, dh, deps=()):
    T, D = gl.shape

    def fn(a, b, c, d):
        _, vjp = jax.vjp(_ple_fn, a, b, c)
        return vjp(d)

    return _rowwise(name, fn, [(gl, "row"), (pe, "row"), (g, "full"), (dh, "row")],
                    [((T, D), BF16, "row"), ((T, D), BF16, "row"), ((1, D), F32, "acc")], T, _pick(T, (256, 128)), deps)


def _loss(name, y, target):
    T, D = y.shape

    def fn(yv, tv):
        err = yv - tv
        part = 0.5 * jnp.sum(jnp.mean(err * err, axis=-1, keepdims=True), axis=0, keepdims=True)
        return err * (1.0 / D), jnp.broadcast_to(part, (8, LANE))

    return _rowwise(name, fn, [(y, "row"), (target, "row")], [((T, D), F32, "row"), ((8, LANE), F32, "acc")], T,
                    _pick(T, (512, 256, 128)))


ADAMW_BLOCK_ELEMS = 128 * 1024


def _adamw_sum(name, parts, w, m, v):
    L, R, C = w.shape
    tiles = [(r, c) for r in (R, 512, 256, 128, 64, 32, 16) for c in (C, 1024, 512, 256, 128)
             if R % r == 0 and C % c == 0 and r * c <= ADAMW_BLOCK_ELEMS]
    tr, tc = max(tiles, key=lambda rc: (rc[0] * rc[1], rc[1]))
    nr, nc = R // tr, C // tc
    c1 = 1.0 - ADAM_B1 ** ADAM_STEP
    c2 = 1.0 - ADAM_B2 ** ADAM_STEP

    def body(*refs):
        p_refs = refs[:L]
        w_ref, m_ref, v_ref, g_out, d_out, m_out, v_out = refs[L:]
        layer = pl.program_id(0)

        def part(s):
            val = p_refs[0][s].astype(F32)
            for j in range(1, L):
                val = jnp.where(layer == j, p_refs[j][s].astype(F32), val)
            return val

        g = part(0)
        for s in range(1, N_DEV):
            g = g + part(s)
        m2 = ADAM_B1 * m_ref[...] + (1.0 - ADAM_B1) * g
        v2 = ADAM_B2 * v_ref[...] + (1.0 - ADAM_B2) * (g * g)
        g_out[...] = g
        m_out[...] = m2
        v_out[...] = v2
        d_out[...] = -ADAM_LR * ((m2 / c1) / (jnp.sqrt(v2 / c2) + ADAM_EPS) + ADAM_WD * w_ref[...])

    def part_spec(j):
        def index(l, i, k):
            before, mine = l < j, l == j
            return (0, jnp.where(mine, i, jnp.where(before, 0, nr - 1)), jnp.where(mine, k, jnp.where(before, 0, nc - 1)))
        return pl.BlockSpec((N_DEV, tr, tc), index)

    blk = pl.BlockSpec((None, tr, tc), lambda l, i, k: (l, i, k))
    sd = jax.ShapeDtypeStruct((L, R, C), F32)
    return _pcall(body, name=name, grid=(L, nr, nc),
                  in_specs=[part_spec(j) for j in range(L)] + [blk, blk, blk],
                  out_specs=[blk, blk, blk, blk], out_shape=[sd, sd, sd, sd])(*parts, w, m, v)


def _unshard_cols(g):
    _, K, n = g.shape
    return g.transpose(1, 0, 2).reshape(K, N_DEV * n)


def _shard_cols(full):
    K, N = full.shape
    return full.reshape(K, N_DEV, N // N_DEV).transpose(1, 0, 2)


def _unshard_rows(g):
    _, k, N = g.shape
    return g.reshape(N_DEV * k, N)


def _shard_rows(full):
    K, N = full.shape
    return full.reshape(N_DEV, K // N_DEV, N)


STAGES = (('ffn_a_w1', 'ffn_a_w3', 'ffn_a_w2'), ('w_in', 'w_uq', 'w_ukv', 'w_out'),
          ('ffn_b_w1', 'ffn_b_w3', 'ffn_b_w2'), ('w_ple_gate', 'w_ple'))
TRANSPOSED = ('ffn_a_w1', 'ffn_a_w3', 'ffn_b_w1', 'ffn_b_w3', 'w_in', 'w_uq')


def _step(x, p, positions, target, w, m, v):
    T, D = x.shape[1], x.shape[2]
    L = p.shape[0]
    x2, target2 = x[0], target[0]
    q_rank, kv_rank = w['w_uq'].shape[1], w['w_ukv'].shape[1]
    gw = w['gm_v_norm'].shape[1]

    inv_freq = ROPE_BASE ** (-jnp.arange(0, ROPE, 2, dtype=F32) / ROPE)
    ang = positions[0].astype(F32)[:, None] * inv_freq
    cos = jnp.concatenate([jnp.ones((T, NOPE), F32), jnp.cos(ang), jnp.cos(ang)], axis=-1)
    sin = jnp.concatenate([jnp.zeros((T, NOPE), F32), jnp.sin(ang), jnp.sin(ang)], axis=-1)
    tabs = (cos, sin, _rot_matrix())

    groups = [(l, names) for l in range(L) for names in STAGES]

    def ag_start(k, after):
        l, names = groups[k]
        shards = [w[n][l].astype(BF16) for n in names]
        return _xchg_start(f"ag_chips{k}", "chips", shards, [_landing(s) for s in shards], after)

    ag = {0: ag_start(0, [])}
    ag[1] = ag_start(1, [ag[0]['token']])

    def fetch(k, after):
        lands = _xchg_wait(f"ag_landed{k}", ag[k], after)
        fw = _xchg_start(f"ag_forward{k}", "forward", [], lands, [])
        deps = [fw['token']]
        if k + 2 < len(groups):
            ag[k + 2] = ag_start(k + 2, deps)
            deps = [ag[k + 2]['token']]
        return fw, deps

    def gathered(k, fw, after):
        return dict(zip(groups[k][1], _xchg_wait(f"ag_wait{k}", fw, after)))

    s0, s1, s2, s3 = q_rank, q_rank + kv_rank, q_rank + kv_rank + ROPE, q_rank + kv_rank + ROPE + gw
    o_u, o_v, o_cq, o_ckv, o_kr = 0, gw, 2 * gw, 2 * gw + q_rank, 2 * gw + q_rank + kv_rank
    kr_pad = 2 * LANE - ROPE

    def g2(name, l):
        return w[name][l][None, :]

    gm_bt = [w['gm_bs'][l].T for l in range(L)]

    saved = []
    h = x2
    for l in range(L):
        s = {}
        fw, deps = fetch(4 * l, [h])
        s['h0'] = h
        s['xn_a'] = _rms_fwd(f"rms_a{l}", h, g2('ffn_a_norm', l), deps)
        wa = s['wa'] = gathered(4 * l, fw, [s['xn_a']])
        h, s['a_h1'], s['a_h3'] = _ffn_fwd(f"ffn_a_fwd{l}", s['xn_a'], h, wa['ffn_a_w1'], wa['ffn_a_w3'], wa['ffn_a_w2'])
        fw, deps = fetch(4 * l + 1, [h])
        s['h1'] = h
        s['n'] = _rms_fwd(f"rms_mix{l}", h, g2('mix_norm', l), deps)
        wm = gathered(4 * l + 1, fw, [s['n']])
        w_in_full = _unshard_cols(wm['w_in'])
        s['w_in'] = jnp.concatenate([w_in_full[:, s2:s3], w_in_full[:, s3:], w_in_full[:, :s0], w_in_full[:, s0:s1],
                                     w_in_full[:, s1:s2], jnp.zeros((D, kr_pad), BF16)], axis=-1)
        s['w_out'] = _unshard_rows(wm['w_out'])
        s['w_uq'], s['w_ukv'] = wm['w_uq'], wm['w_ukv']
        s['z'] = _mm(f"w_in{l}", s['n'], s['w_in'], "nn")
        s['mla_small'] = [g2('q_a_norm', l), g2('kv_a_norm', l), g2('q_norm', l), g2('k_norm', l)]
        s['q'], s['k'], s['v'] = _mla_prep_fwd(f"mla_prep{l}", s['z'], tabs, s['mla_small'], s['w_uq'], s['w_ukv'],
                                               (o_cq, o_ckv, o_kr))
        s['a_out'], s['lse'] = _attn_fwd(f"attn_fwd{l}", s['q'], s['k'], s['v'])
        s['mix_small'] = [g2('attn_out_norm', l), g2('gm_out_norm', l), g2('gm_v_norm', l), w['gm_ws'][l], gm_bt[l]]
        s['mixed'] = _mix_fwd(f"mix_fwd{l}", s['a_out'], s['z'], s['mix_small'], (o_u, o_v))
        h = _mm(f"w_out{l}", s['mixed'], s['w_out'], "nn", res=h)
        fw, deps = fetch(4 * l + 2, [h])
        s['h2'] = h
        s['xn_b'] = _rms_fwd(f"rms_b{l}", h, g2('ffn_b_norm', l), deps)
        wb = s['wb'] = gathered(4 * l + 2, fw, [s['xn_b']])
        h, s['b_h1'], s['b_h3'] = _ffn_fwd(f"ffn_b_fwd{l}", s['xn_b'], h, wb['ffn_b_w1'], wb['ffn_b_w3'], wb['ffn_b_w2'])
        fw, deps = fetch(4 * l + 3, [h])
        s['h3'] = h
        s['xn_g'] = _rms_fwd(f"rms_g{l}", h, g2('ple_gate_norm', l), deps)
        wp = gathered(4 * l + 3, fw, [s['xn_g']])
        s['w_gate'] = _unshard_rows(wp['w_ple_gate'])
        s['gl'] = _mm(f"w_gate{l}", s['xn_g'], s['w_gate'], "nn")
        s['p'] = p[l, 0]
        s['pe'] = _mm(f"w_ple{l}", s['p'], _unshard_cols(wp['w_ple']), "nn")
        h = _ple_fwd(f"ple_fwd{l}", h, s['gl'], s['pe'], g2('ple_norm', l))
        saved.append(s)

    dh, loss_part = _loss("loss", h, target2)
    loss = lax.psum(loss_part[0, 0], AXES)

    gsmall = {n: [None] * L for n in SMALL}
    rs, where = {}, {}

    def rs_start(key, l, named):
        grads = [g for _, g in named]
        lands = [lax.dynamic_update_slice(lax.empty(g.shape, g.dtype),
                                          lax.dynamic_index_in_dim(g, _my_index(), 0, keepdims=True),
                                          (_my_index(),) + (0,) * (g.ndim - 1)) for g in grads]
        rs[key] = _xchg_start("rs_start_" + key, "scatter", grads, lands, [])
        where.update({(n, l): (key, i) for i, (n, _) in enumerate(named)})
        return [rs[key]['token']]

    def ffn_backward(tag, l, dh, xn, h_in, h1, h3, wts, norm_name, deps):
        pre = 'ffn_' + tag
        dxn, dh1, dh3, act = _ffn_bwd(f"{pre}_bwd{l}", dh, h1, h3, wts[pre + '_w1'], wts[pre + '_w3'], wts[pre + '_w2'], deps)
        g1 = _mm_tn_batch(f"{pre}_dw1_{l}", dh1, xn[None])
        deps = rs_start(f"{pre}_w1_{l}", l, [(pre + '_w1', g1)])
        g3 = _mm_tn_batch(f"{pre}_dw3_{l}", dh3, xn[None], deps=deps)
        deps = rs_start(f"{pre}_w3_{l}", l, [(pre + '_w3', g3)])
        g2_ = _mm_tn_batch(f"{pre}_dw2_{l}", act, dh[None], alpha=0.5, deps=deps)
        deps = rs_start(f"{pre}_w2_{l}", l, [(pre + '_w2', g2_)])
        return _rms_bwd(f"rms_{tag}_bwd{l}", h_in, g2(norm_name, l), dxn, dh, deps)

    deps = []
    for l in reversed(range(L)):
        s = saved[l]
        d_gl, d_pe, gsmall['ple_norm'][l] = _ple_bwd(f"ple_bwd{l}", s['gl'], s['pe'], g2('ple_norm', l), dh, deps)
        g_ple = _mm(f"dw_ple{l}", s['p'], d_pe, "tn", out_dtype=BF16)
        g_gate = _mm(f"dw_gate{l}", s['xn_g'], d_gl, "tn", out_dtype=BF16)
        d_xng = _mm(f"d_xng{l}", d_gl, s['w_gate'], "nt")
        dh, gsmall['ple_gate_norm'][l] = _rms_bwd(f"rms_g_bwd{l}", s['h3'], g2('ple_gate_norm', l), d_xng, dh)
        deps = rs_start(f"ple_{l}", l, [('w_ple_gate', _shard_rows(g_gate)), ('w_ple', _shard_cols(g_ple))])
        dh, gsmall['ffn_b_norm'][l] = ffn_backward('b', l, dh, s['xn_b'], s['h2'], s['b_h1'], s['b_h3'], s['wb'],
                                                   'ffn_b_norm', deps)
        g_out = _mm(f"dw_out{l}", s['mixed'], dh, "tn", out_dtype=BF16)
        d_mixed = _mm(f"d_mixed{l}", dh, s['w_out'], "nt")
        mix = _mix_bwd(f"mix_bwd{l}", s['a_out'], s['z'], s['mix_small'], (o_u, o_v), d_mixed)
        d_a_out, d_u, d_v = mix[:3]
        gsmall['attn_out_norm'][l], gsmall['gm_out_norm'][l], gsmall['gm_v_norm'][l], gsmall['gm_ws'][l] = mix[3:7]
        gsmall['gm_bs'][l] = mix[7].T
        dq, dk, dv = _attn_bwd(f"attn_bwd{l}", s['q'], s['k'], s['v'], s['a_out'], s['lse'], d_a_out)
        mla = _mla_prep_bwd(f"mla_prep_bwd{l}", s['z'], tabs, s['mla_small'], s['w_uq'], s['w_ukv'], (o_cq, o_ckv, o_kr),
                            dq, dk, dv)
        d_cq, d_ckv, d_kr = mla[:3]
        gsmall['q_a_norm'][l], gsmall['kv_a_norm'][l], gsmall['q_norm'][l], gsmall['k_norm'][l] = mla[3:7]
        dz = jnp.concatenate([d_u, d_v, d_cq, d_ckv, d_kr, jnp.zeros((T, LANE), BF16)], axis=-1)
        g_in = _mm(f"dw_in{l}", dz, s['n'], "tn", out_dtype=BF16)
        g_in = jnp.concatenate([g_in[o_cq:o_cq + q_rank], g_in[o_ckv:o_ckv + kv_rank], g_in[o_kr:o_kr + ROPE],
                                g_in[o_u:o_u + gw], g_in[o_v:o_v + gw]], axis=0)
        d_n = _mm(f"d_n{l}", dz, s['w_in'], "nt")
        dh, gsmall['mix_norm'][l] = _rms_bwd(f"rms_mix_bwd{l}", s['h1'], g2('mix_norm', l), d_n, dh)
        deps = rs_start(f"mix_{l}", l, [('w_in', _shard_rows(g_in)), ('w_uq', mla[7].transpose(0, 2, 1).astype(BF16)),
                                        ('w_ukv', mla[8].astype(BF16)), ('w_out', _shard_rows(g_out))])
        dh, gsmall['ffn_a_norm'][l] = ffn_backward('a', l, dh, s['xn_a'], s['h0'], s['a_h1'], s['a_h3'], s['wa'],
                                                   'ffn_a_norm', deps)
        deps = []
    grad_x = dh[None]

    out = {}
    sizes = [w[n].size for n in SMALL]
    total = sum(sizes)
    padded = -(-total // (512 * LANE)) * (512 * LANE)

    def pack(d):
        flat = jnp.concatenate([d[n].reshape(-1) for n in SMALL] + [jnp.zeros((padded - total,), F32)])
        return flat.reshape(1, padded // LANE, LANE)

    gs = pack({n: jnp.stack([gsmall[n][l].reshape(w[n].shape[1:]) for l in range(L)]) for n in SMALL})
    small = _xchg_start("small_start", "gather", [gs[0]], [_landing(gs[0])], [])

    after = [dh, small['token']]
    landed = {}

    def partials(n, l):
        key, i = where[(n, l)]
        if key not in landed:
            landed[key] = _xchg_wait("rs_wait_" + key, rs[key], after)
        return landed[key][i]

    swap = lambda a: a.transpose(0, 2, 1)
    for stage in (3, 2, 1, 0):
        for n in STAGES[stage]:
            parts = [partials(n, l) for l in reversed(range(L))][::-1]
            if n in TRANSPOSED:
                out[n] = [swap(r) for r in _adamw_sum("adamw_" + n, parts, swap(w[n]), swap(m[n]), swap(v[n]))]
            else:
                out[n] = _adamw_sum("adamw_" + n, parts, w[n], m[n], v[n])
            after = [out[n][0]]

    res = _adamw_sum("adamw_small", _xchg_wait("small_wait", small, after), pack(w), pack(m), pack(v))
    off = 0
    for n, sz in zip(SMALL, sizes):
        out[n] = [r.reshape(-1)[off:off + sz].reshape(w[n].shape) for r in res]
        off += sz

    return (loss, grad_x, *[out[n][0] for n in WEIGHTS], *[out[n][1] for n in WEIGHTS],
            *[out[n][2] for n in WEIGHTS], *[out[n][3] for n in WEIGHTS])


def kernel(x, p, positions, ffn_a_norm, ffn_a_w1, ffn_a_w3, ffn_a_w2, mix_norm, w_in, q_a_norm, w_uq, kv_a_norm, w_ukv, q_norm, k_norm, gm_v_norm, gm_ws, gm_bs, attn_out_norm, gm_out_norm, w_out, ffn_b_norm, ffn_b_w1, ffn_b_w3, ffn_b_w2, ple_gate_norm, w_ple_gate, w_ple, ple_norm, loss_target, m_ffn_a_norm, m_ffn_a_w1, m_ffn_a_w3, m_ffn_a_w2, m_mix_norm, m_w_in, m_q_a_norm, m_w_uq, m_kv_a_norm, m_w_ukv, m_q_norm, m_k_norm, m_gm_v_norm, m_gm_ws, m_gm_bs, m_attn_out_norm, m_gm_out_norm, m_w_out, m_ffn_b_norm, m_ffn_b_w1, m_ffn_b_w3, m_ffn_b_w2, m_ple_gate_norm, m_w_ple_gate, m_w_ple, m_ple_norm, v_ffn_a_norm, v_ffn_a_w1, v_ffn_a_w3, v_ffn_a_w2, v_mix_norm, v_w_in, v_q_a_norm, v_w_uq, v_kv_a_norm, v_w_ukv, v_q_norm, v_k_norm, v_gm_v_norm, v_gm_ws, v_gm_bs, v_attn_out_norm, v_gm_out_norm, v_w_out, v_ffn_b_norm, v_ffn_b_w1, v_ffn_b_w3, v_ffn_b_w2, v_ple_gate_norm, v_w_ple_gate, v_w_ple, v_ple_norm):
    args = locals()
    w = {n: args[n] for n in WEIGHTS}
    m = {n: args["m_" + n] for n in WEIGHTS}
    v = {n: args["v_" + n] for n in WEIGHTS}
    return _step(x, p, positions, loss_target, w, m, v)
```

```python
import functools

import jax
import jax.numpy as jnp
from jax import lax
from jax.experimental import pallas as pl
from jax.experimental.pallas import tpu as pltpu

F32, BF16 = jnp.float32, jnp.bfloat16
EPS = 1e-6
N_DEV = 8
HEADS = 8
NOPE, ROPE, QK, VD = 128, 64, 192, 128
CHUNK = 128
GROUPS = 8
LANE = 128
ROPE_BASE = 10000.0
ADAM_LR, ADAM_B1, ADAM_B2, ADAM_EPS, ADAM_WD, ADAM_STEP = 0.001, 0.9, 0.999, 1e-08, 0.01, 10
AXES = ("x", "y", "c")
MESH = pl.DeviceIdType.MESH
ANY = pl.BlockSpec(memory_space=pl.ANY)

WEIGHTS = ['ffn_a_norm', 'ffn_a_w1', 'ffn_a_w3', 'ffn_a_w2', 'mix_norm', 'w_in', 'q_a_norm', 'w_uq', 'kv_a_norm',
           'w_ukv', 'q_norm', 'k_norm', 'gm_v_norm', 'gm_ws', 'gm_bs', 'attn_out_norm', 'gm_out_norm', 'w_out',
           'ffn_b_norm', 'ffn_b_w1', 'ffn_b_w3', 'ffn_b_w2', 'ple_gate_norm', 'w_ple_gate', 'w_ple', 'ple_norm']
BIG = ['ffn_a_w1', 'ffn_a_w3', 'ffn_a_w2', 'w_in', 'w_uq', 'w_ukv', 'w_out', 'ffn_b_w1', 'ffn_b_w3', 'ffn_b_w2',
       'w_ple_gate', 'w_ple']
SMALL = [n for n in WEIGHTS if n not in BIG]


def _pcall(body, **kw):
    return pl.pallas_call(body, **kw)


def _pick(n, cands):
    for c in cands:
        if n % c == 0:
            return c
    return n


def _rms(x, g):
    return x * lax.rsqrt(jnp.mean(x * x, axis=-1, keepdims=True) + EPS) * g


@jax.custom_vjp
def _bdot(x, w):
    return jnp.dot(x.astype(BF16), w.astype(BF16), preferred_element_type=F32)


def _bdot_fwd(x, w):
    return _bdot(x, w), (x, w)


def _bdot_bwd(res, dy):
    x, w = res
    dyb = dy.astype(BF16)
    dx = lax.dot_general(dyb, w.astype(BF16), (((1,), (1,)), ((), ())), preferred_element_type=F32)
    dw = lax.dot_general(x.astype(BF16), dyb, (((0,), (0,)), ((), ())), preferred_element_type=F32)
    return dx.astype(x.dtype), dw.astype(w.dtype)


_bdot.defvjp(_bdot_fwd, _bdot_bwd)


def _split_dot(x, p, dims):
    hi = x.astype(BF16)
    lo = (x - hi.astype(F32)).astype(BF16)
    return (lax.dot_general(hi, p, dims, preferred_element_type=F32)
            + lax.dot_general(lo, p, dims, preferred_element_type=F32))


@jax.custom_vjp
def _permute(x, p):
    return _split_dot(x, p, _DIMS["nn"])


def _permute_fwd(x, p):
    return _permute(x, p), p


def _permute_bwd(p, ct):
    return _split_dot(ct, p, _DIMS["nt"]), jnp.zeros_like(p)


_permute.defvjp(_permute_fwd, _permute_bwd)


def _flip(v, bit):
    return 1 - v if bit else v


HBM = pl.BlockSpec(memory_space=pltpu.HBM)
SEM = pl.BlockSpec(memory_space=pltpu.SEMAPHORE)
EFFECT = pltpu.SideEffectType.DATAFLOW_SIDE_EFFECTING
PEERS = N_DEV - 1


def _my_index():
    return 4 * lax.axis_index("x") + 2 * lax.axis_index("y") + lax.axis_index("c")


def _landing(own):
    zone = lax.empty((N_DEV,) + own.shape, own.dtype)
    return lax.dynamic_update_slice(zone, own[None], (_my_index(),) + (0,) * own.ndim)


COPIES = {"gather": PEERS, "scatter": PEERS, "chips": 4, "forward": 3}


def _copy_plan(kind, src_refs, land_refs, send_sems, recv_sems):
    cx, cy, cc = lax.axis_index("x"), lax.axis_index("y"), lax.axis_index("c")
    me = 4 * cx + 2 * cy + cc
    per = COPIES[kind]
    out = []
    for t, land in enumerate(land_refs):
        def pair(i, src, to_slot, from_slot, dev):
            kw = dict(send_sem=send_sems.at[per * t + i], recv_sem=recv_sems.at[per * t + i], device_id=dev,
                      device_id_type=MESH)
            out.append((pltpu.make_async_remote_copy(src_ref=src, dst_ref=land.at[to_slot], **kw),
                        pltpu.make_async_remote_copy(src_ref=src, dst_ref=land.at[from_slot], **kw)))

        if kind in ("gather", "scatter"):
            for k in range(1, N_DEV):
                px, py, pc = _flip(cx, k & 4), _flip(cy, k & 2), _flip(cc, k & 1)
                peer = 4 * px + 2 * py + pc
                pair(k - 1, src_refs[t].at[peer] if kind == "scatter" else src_refs[t], me, peer, (px, py, pc))
        elif kind == "chips":
            pair(0, src_refs[t], me, me + 1 - 2 * cc, (cx, cy, 1 - cc))
            for j in range(1, 4):
                px, py = _flip(cx, j & 2), _flip(cy, j & 1)
                pair(j, src_refs[t], me, 4 * px + 2 * py + cc, (px, py, cc))
        else:
            for j in range(1, 4):
                px, py = _flip(cx, j & 2), _flip(cy, j & 1)
                mine, theirs = 4 * px + 2 * py + cc, 4 * px + 2 * py + 1 - cc
                pair(j - 1, land.at[mine], mine, theirs, (cx, cy, 1 - cc))
    return out


def _xchg_start(name, kind, srcs, lands, after):
    ns, nb, na = len(srcs), len(srcs) + len(lands), len(after)
    n_sems = COPIES[kind] * len(lands)

    def body(*refs):
        send_sems, recv_sems = refs[nb + na], refs[nb + na + 1]
        for send, _ in _copy_plan(kind, refs[:ns], refs[ns:nb], send_sems, recv_sems):
            send.start()
        refs[-1][...] = jnp.zeros_like(refs[-1])

    bufs = list(srcs) + list(lands)
    res = _pcall(
        body, name=name,
        out_shape=(pltpu.SemaphoreType.DMA((n_sems,)), pltpu.SemaphoreType.DMA((n_sems,)),
                   *[pltpu.HBM(a.shape, a.dtype) for a in bufs], jax.ShapeDtypeStruct((8, LANE), F32)),
        in_specs=[HBM] * nb + [ANY] * na,
        out_specs=(SEM, SEM, *([HBM] * nb), pl.BlockSpec(memory_space=pltpu.VMEM)),
        input_output_aliases={i: 2 + i for i in range(nb)},
        compiler_params=pltpu.CompilerParams(has_side_effects=EFFECT),
    )(*[pltpu.with_memory_space_constraint(a, pltpu.HBM) for a in bufs], *after)
    return dict(kind=kind, send=res[0], recv=res[1], srcs=list(res[2:2 + ns]), lands=list(res[2 + ns:2 + nb]),
                token=res[-1])


def _xchg_wait(name, st, after):
    ns, nb = len(st['srcs']), len(st['srcs']) + len(st['lands'])

    def body(*refs):
        for _, back in _copy_plan(st['kind'], refs[:ns], refs[ns:nb], refs[nb], refs[nb + 1]):
            back.wait_send()
            back.wait_recv()

    bufs = st['srcs'] + st['lands']
    res = _pcall(
        body, name=name, out_shape=tuple(pltpu.HBM(a.shape, a.dtype) for a in bufs),
        in_specs=[HBM] * nb + [SEM, SEM] + [ANY] * len(after), out_specs=tuple([HBM] * nb),
        input_output_aliases={i: i for i in range(nb)},
        compiler_params=pltpu.CompilerParams(has_side_effects=EFFECT),
    )(*bufs, st['send'], st['recv'], *after)
    return list(res[ns:])


_DIMS = {"nn": (((1,), (0,)), ((), ())), "nt": (((1,), (1,)), ((), ())), "tn": (((0,), (0,)), ((), ()))}


MM_WHOLE_K = 2048


def _mm(name, a, b, mode, out_dtype=F32, res=None, alpha=1.0, deps=()):
    if mode == "tn":
        K, M = a.shape
        N = b.shape[1]
    else:
        M, K = a.shape
        N = b.shape[0] if mode == "nt" else b.shape[1]
    tn = _pick(N, (1024, 512, 256))
    tm = _pick(M, (1024, 512, 256, 128) if tn <= 1024 else (512, 256, 128))
    tk = K if K <= MM_WHOLE_K else _pick(K, (1024, 512, 256, 128))
    nk = K // tk
    a_spec = pl.BlockSpec((tk, tm), lambda i, j, k: (k, i)) if mode == "tn" else pl.BlockSpec((tm, tk), lambda i, j, k: (i, k))
    b_spec = pl.BlockSpec((tn, tk), lambda i, j, k: (j, k)) if mode == "nt" else pl.BlockSpec((tk, tn), lambda i, j, k: (k, j))
    o_spec = pl.BlockSpec((tm, tn), lambda i, j, k: (i, j))
    dims = _DIMS[mode]

    def body(*refs):
        a_ref, b_ref, r_ref = refs[0], refs[1], refs[2]
        part = lax.dot_general(a_ref[...].astype(BF16), b_ref[...].astype(BF16), dims, preferred_element_type=F32)

        def finish(o_ref, r):
            r = r * alpha if alpha != 1.0 else r
            if res is not None:
                r = r_ref[...] + r
            o_ref[...] = r.astype(o_ref.dtype)

        if nk == 1:
            finish(refs[-1], part)
            return
        o_ref, acc = refs[-2], refs[-1]
        k = pl.program_id(2)

        @pl.when(k == 0)
        def _():
            acc[...] = part

        @pl.when(k > 0)
        def _():
            acc[...] += part

        @pl.when(k == nk - 1)
        def _():
            finish(o_ref, acc[...])

    ins = [a, b] + ([] if res is None else [res]) + list(deps)
    specs = [a_spec, b_spec] + ([] if res is None else [o_spec]) + [ANY] * len(deps)
    return _pcall(body, name=name, grid=(M // tm, N // tn, nk), in_specs=specs, out_specs=o_spec,
                  out_shape=jax.ShapeDtypeStruct((M, N), out_dtype),
                  scratch_shapes=[] if nk == 1 else [pltpu.VMEM((tm, tn), F32)])(*ins)


def _mm_tn_batch(name, a3, b3, alpha=1.0, deps=()):
    ga, T, M = a3.shape
    gb, _, N = b3.shape
    G = max(ga, gb)
    tm = _pick(M, (1024, 512)) if N <= 1024 else M
    tk = _pick(T, (1024, 512, 256, 128))
    nk = T // tk
    a_spec = pl.BlockSpec((None, tk, tm), (lambda g, i, k: (g, k, i)) if ga > 1 else (lambda g, i, k: (0, k, i)))
    b_spec = pl.BlockSpec((None, tk, N), (lambda g, i, k: (g, k, 0)) if gb > 1 else (lambda g, i, k: (0, k, 0)))
    o_spec = pl.BlockSpec((None, tm, N), lambda g, i, k: (g, i, 0))

    def body(*refs):
        a_ref, b_ref, o_ref, acc = refs[0], refs[1], refs[-2], refs[-1]
        k = pl.program_id(2)
        part = lax.dot_general(a_ref[...].astype(BF16), b_ref[...].astype(BF16), _DIMS["tn"], preferred_element_type=F32)

        @pl.when(k == 0)
        def _():
            acc[...] = part

        @pl.when(k > 0)
        def _():
            acc[...] += part

        @pl.when(k == nk - 1)
        def _():
            o_ref[...] = (acc[...] * alpha if alpha != 1.0 else acc[...]).astype(o_ref.dtype)

    return _pcall(body, name=name, grid=(G, M // tm, nk), in_specs=[a_spec, b_spec] + [ANY] * len(deps),
                  out_specs=o_spec, out_shape=jax.ShapeDtypeStruct((G, M, N), BF16),
                  scratch_shapes=[pltpu.VMEM((tm, N), F32)])(a3, b3, *deps)


def _rowwise(name, fn, ins, outs, T, tm, deps=()):
    in_specs = []
    for arr, spec in ins:
        if spec == "row":
            in_specs.append(pl.BlockSpec((tm, arr.shape[1]), lambda i: (i, 0)))
        elif spec == "full":
            in_specs.append(pl.BlockSpec(arr.shape, lambda i, _n=arr.ndim: (0,) * _n))
        else:
            _, off, width = spec
            in_specs.append(pl.BlockSpec((tm, width), lambda i, _b=off // width: (i, _b)))
    in_specs += [ANY] * len(deps)
    out_specs, out_shapes = [], []
    for shape, dtype, spec in outs:
        out_shapes.append(jax.ShapeDtypeStruct(shape, dtype))
        if spec == "row":
            out_specs.append(pl.BlockSpec((tm, shape[1]), lambda i: (i, 0)))
        else:
            out_specs.append(pl.BlockSpec(shape, lambda i, _n=len(shape): (0,) * _n))
    n_in = len(ins)

    def body(*refs):
        res = fn(*[r[...] for r in refs[:n_in]])
        i = pl.program_id(0)
        for r, (_, _, spec), val in zip(refs[n_in + len(deps):], outs, res):
            if spec == "acc":
                @pl.when(i == 0)
                def _(r=r):
                    r[...] = jnp.zeros_like(r)
                r[...] += val.astype(r.dtype)
            else:
                r[...] = val.astype(r.dtype)

    return _pcall(body, name=name, grid=(T // tm,), in_specs=in_specs, out_specs=out_specs, out_shape=out_shapes)(
        *[a for a, _ in ins], *deps)


def _rms_fwd(name, h, g, deps=()):
    T, D = h.shape
    return _rowwise(name, lambda hv, gv: (_rms(hv, gv),), [(h, "row"), (g, "full")], [((T, D), BF16, "row")], T,
                    _pick(T, (512, 256, 128)), deps)[0]


def _rms_bwd(name, h, g, dxn, dh_in, deps=()):
    T, D = h.shape

    def fn(hv, gv, dv, dh0):
        _, vjp = jax.vjp(_rms, hv, gv)
        dh, dg = vjp(dv.astype(F32))
        return dh0 + dh, dg

    return _rowwise(name, fn, [(h, "row"), (g, "full"), (dxn, "row"), (dh_in, "row")],
                    [((T, D), F32, "row"), ((1, D), F32, "acc")], T, _pick(T, (256, 128)), deps)


def _ffn_fwd(name, xn, h, w1, w3, w2):
    T, D = xn.shape
    F8 = w1.shape[-1]
    tm = _pick(T, (512, 256, 128))
    wspec = lambda r, c: pl.BlockSpec((None, r, c), lambda i, d: (d, 0, 0))
    row = pl.BlockSpec((tm, D), lambda i, d: (i, 0))
    hid = pl.BlockSpec((None, tm, F8), lambda i, d: (d, i, 0))

    def body(xn_ref, h_ref, w1_ref, w3_ref, w2_ref, out_ref, h1_ref, h3_ref, acc):
        d = pl.program_id(1)

        @pl.when(d == 0)
        def _():
            acc[...] = jnp.zeros_like(acc)

        x = xn_ref[...]
        h1 = jnp.dot(x, w1_ref[...], preferred_element_type=F32)
        h3 = jnp.dot(x, w3_ref[...], preferred_element_type=F32)
        h1_ref[...] = h1.astype(BF16)
        h3_ref[...] = h3.astype(BF16)
        act = (h1 * jax.nn.sigmoid(h1) * h3).astype(BF16)
        acc[...] += jnp.dot(act, w2_ref[...], preferred_element_type=F32)

        @pl.when(d == N_DEV - 1)
        def _():
            out_ref[...] = h_ref[...] + 0.5 * acc[...]

    return _pcall(body, name=name, grid=(T // tm, N_DEV),
                  in_specs=[row, row, wspec(D, F8), wspec(D, F8), wspec(F8, D)],
                  out_specs=[row, hid, hid],
                  out_shape=[jax.ShapeDtypeStruct((T, D), F32), jax.ShapeDtypeStruct((N_DEV, T, F8), BF16),
                             jax.ShapeDtypeStruct((N_DEV, T, F8), BF16)],
                  scratch_shapes=[pltpu.VMEM((tm, D), F32)])(xn, h, w1, w3, w2)


def _ffn_bwd(name, dy, h1, h3, w1t, w3t, w2t, deps=()):
    T, D = dy.shape
    F8 = w2t.shape[-1]
    tm = _pick(T, (512, 256, 128))
    wspec = lambda r, c: pl.BlockSpec((None, r, c), lambda i, d: (d, 0, 0))
    row = pl.BlockSpec((tm, D), lambda i, d: (i, 0))
    hid = pl.BlockSpec((None, tm, F8), lambda i, d: (d, i, 0))

    def body(*refs):
        dy_ref, h1_ref, h3_ref, w1_ref, w3_ref, w2_ref = refs[:6]
        dxn_ref, dyb, dh1_ref, dh3_ref, act_ref, acc = refs[6 + len(deps):]
        d = pl.program_id(1)

        @pl.when(d == 0)
        def _():
            acc[...] = jnp.zeros_like(acc)
            dyb[...] = dy_ref[...].astype(BF16)

        dact = 0.5 * jnp.dot(dyb[...], w2_ref[...], preferred_element_type=F32)
        h1 = h1_ref[...].astype(F32)
        h3 = h3_ref[...].astype(F32)
        sig = jax.nn.sigmoid(h1)
        silu = h1 * sig
        dh1 = (dact * h3 * (sig * (1.0 + h1 * (1.0 - sig)))).astype(BF16)
        dh3 = (dact * silu).astype(BF16)
        dh1_ref[...] = dh1
        dh3_ref[...] = dh3
        act_ref[...] = (silu * h3).astype(BF16)
        acc[...] += (jnp.dot(dh1, w1_ref[...], preferred_element_type=F32)
                     + jnp.dot(dh3, w3_ref[...], preferred_element_type=F32))

        @pl.when(d == N_DEV - 1)
        def _():
            dxn_ref[...] = acc[...]

    hshape = jax.ShapeDtypeStruct((N_DEV, T, F8), BF16)
    return _pcall(body, name=name, grid=(T // tm, N_DEV),
                  in_specs=[row, hid, hid, wspec(F8, D), wspec(F8, D), wspec(D, F8)] + [ANY] * len(deps),
                  out_specs=[row, row, hid, hid, hid],
                  out_shape=[jax.ShapeDtypeStruct((T, D), F32), jax.ShapeDtypeStruct((T, D), BF16), hshape, hshape, hshape],
                  scratch_shapes=[pltpu.VMEM((tm, D), F32)])(dy, h1, h3, w1t, w3t, w2t, *deps)


def _rot_matrix():
    i = jnp.arange(QK)[:, None]
    j = jnp.arange(QK)[None, :]
    half = ROPE // 2
    first = (j >= NOPE) & (j < NOPE + half) & (i == j + half)
    second = (j >= NOPE + half) & (i == j - half)
    return jnp.where(first, -1.0, jnp.where(second, 1.0, 0.0)).astype(F32)


def _mla_fn(cq, ckv, kr128, cos, sin, rot, qa_g, kva_g, qn_g, kn_g, w_uq, w_ukv):
    cqn = _rms(cq, qa_g)
    ckvn = _rms(ckv, kva_g)
    kr = kr128[:, :ROPE]
    qs, ks, vs = [], [], []
    for h in range(HEADS):
        qh = _rms(_bdot(cqn, w_uq[h]), qn_g)
        qs.append(qh * cos + _permute(qh, rot) * sin)
        kvh = _bdot(ckvn, w_ukv[h])
        kh = _rms(jnp.concatenate([kvh[:, :NOPE], kr], axis=-1), kn_g)
        ks.append(kh * cos + _permute(kh, rot) * sin)
        vs.append(kvh[:, NOPE:])
    return qs, ks, vs


def _mla_specs(z, tabs, small, w_uq, w_ukv, tm, offs):
    o_cq, o_ckv, o_kr = offs
    row = lambda w: pl.BlockSpec((tm, w), lambda i: (i, 0))
    col = lambda off, w: pl.BlockSpec((tm, w), lambda i: (i, off // w))
    full2 = lambda a: pl.BlockSpec(a.shape, lambda i: (0, 0))
    wsp = lambda a: pl.BlockSpec(a.shape, lambda i: (0, 0, 0))
    cq_w, ckv_w = w_uq.shape[1], w_ukv.shape[1]
    ins = [z, z, z, tabs[0], tabs[1], tabs[2]] + list(small) + [w_uq, w_ukv]
    specs = ([col(o_cq, cq_w), col(o_ckv, ckv_w), col(o_kr, LANE), row(QK), row(QK), full2(tabs[2])]
             + [full2(s) for s in small] + [wsp(w_uq), wsp(w_ukv)])
    return ins, specs


def _mla_prep_fwd(name, z, tabs, small, w_uq, w_ukv, offs):
    T = z.shape[0]
    tm = _pick(T, (256, 128))
    ins, specs = _mla_specs(z, tabs, small, w_uq, w_ukv, tm, offs)
    head = lambda w: pl.BlockSpec((HEADS, tm, w), lambda i: (0, i, 0))

    def body(*refs):
        vals = [r[...] for r in refs[:12]]
        q_ref, k_ref, v_ref = refs[12:]
        qs, ks, vs = _mla_fn(*vals)
        for h in range(HEADS):
            q_ref[h] = qs[h].astype(BF16)
            k_ref[h] = ks[h].astype(BF16)
            v_ref[h] = vs[h].astype(BF16)

    return _pcall(body, name=name, grid=(T // tm,), in_specs=specs, out_specs=[head(QK), head(QK), head(VD)],
                  out_shape=[jax.ShapeDtypeStruct((HEADS, T, QK), BF16), jax.ShapeDtypeStruct((HEADS, T, QK), BF16),
                             jax.ShapeDtypeStruct((HEADS, T, VD), BF16)])(*ins)


def _mla_prep_bwd(name, z, tabs, small, w_uq, w_ukv, offs, dq, dk, dv):
    T = z.shape[0]
    tm = _pick(T, (256, 128))
    ins, specs = _mla_specs(z, tabs, small, w_uq, w_ukv, tm, offs)
    head = lambda w: pl.BlockSpec((HEADS, tm, w), lambda i: (0, i, 0))
    ins += [dq, dk, dv]
    specs += [head(QK), head(QK), head(VD)]
    cq_w, ckv_w = w_uq.shape[1], w_ukv.shape[1]
    acc_shapes = [s.shape for s in small] + [w_uq.shape, w_ukv.shape]
    row_shapes = [(T, cq_w), (T, ckv_w), (T, LANE)]
    out_shape = [jax.ShapeDtypeStruct(s, BF16) for s in row_shapes] + [jax.ShapeDtypeStruct(s, F32) for s in acc_shapes]
    out_specs = ([pl.BlockSpec((tm, s[1]), lambda i: (i, 0)) for s in row_shapes]
                 + [pl.BlockSpec(s, lambda i, _n=len(s): (0,) * _n) for s in acc_shapes])

    def body(*refs):
        cq, ckv, kr128, cos, sin, rot, qa_g, kva_g, qn_g, kn_g, w_uq_v, w_ukv_v = [r[...] for r in refs[:12]]
        dq_ref, dk_ref, dv_ref = refs[12:15]
        outs = refs[15:]
        f = lambda a, b, c, g1, g2, g3, g4, wq, wkv: _mla_fn(a, b, c, cos, sin, rot, g1, g2, g3, g4, wq, wkv)
        _, vjp = jax.vjp(f, cq, ckv, kr128, qa_g, kva_g, qn_g, kn_g, w_uq_v.astype(F32), w_ukv_v.astype(F32))
        cts = ([dq_ref[h] for h in range(HEADS)], [dk_ref[h] for h in range(HEADS)], [dv_ref[h] for h in range(HEADS)])
        grads = vjp(cts)
        i = pl.program_id(0)
        for n, (r, gval) in enumerate(zip(outs, grads)):
            if n < 3:
                r[...] = gval.astype(r.dtype)
            else:
                @pl.when(i == 0)
                def _(r=r):
                    r[...] = jnp.zeros_like(r)
                r[...] += gval

    return _pcall(body, name=name, grid=(T // tm,), in_specs=specs, out_specs=out_specs, out_shape=out_shape)(*ins)


NEG = -1e30


def _tri(t):
    return lax.broadcasted_iota(jnp.int32, (t, t), 1) <= lax.broadcasted_iota(jnp.int32, (t, t), 0)


def _attn_tiles(T):
    t = _pick(T, (1024, 512, 256, 128))
    return t, T // t


def _attn_fwd(name, q, k, v):
    H, T, _ = q.shape
    t, n = _attn_tiles(T)
    scale = QK ** -0.5

    def body(q_ref, k_ref, v_ref, o_ref, lse_ref, m_s, l_s, acc):
        qi, ki = pl.program_id(1), pl.program_id(2)

        @pl.when(ki == 0)
        def _():
            m_s[...] = jnp.full_like(m_s, NEG)
            l_s[...] = jnp.zeros_like(l_s)
            acc[...] = jnp.zeros_like(acc)

        def tile(diagonal):
            s = lax.dot_general(q_ref[...], k_ref[...], _DIMS["nt"], preferred_element_type=F32) * scale
            if diagonal:
                s = jnp.where(_tri(t), s, NEG)
            m_new = jnp.maximum(m_s[...], jnp.max(s, axis=-1, keepdims=True))
            alpha = jnp.exp(m_s[...] - m_new)
            p = jnp.exp(s - m_new)
            l_s[...] = alpha * l_s[...] + jnp.sum(p, axis=-1, keepdims=True)
            acc[...] = alpha * acc[...] + jnp.dot(p.astype(BF16), v_ref[...], preferred_element_type=F32)
            m_s[...] = m_new

        @pl.when(ki < qi)
        def _():
            tile(False)

        @pl.when(ki == qi)
        def _():
            tile(True)
            o_ref[...] = acc[...] / l_s[...]
            lse_ref[...] = m_s[...] + jnp.log(l_s[...])

    kv = lambda w: pl.BlockSpec((None, t, w), lambda h, qi, ki: (h, jnp.minimum(ki, qi), 0))
    return _pcall(body, name=name, grid=(H, n, n),
                  in_specs=[pl.BlockSpec((None, t, QK), lambda h, qi, ki: (h, qi, 0)), kv(QK), kv(VD)],
                  out_specs=[pl.BlockSpec((t, VD), lambda h, qi, ki: (qi, h)),
                             pl.BlockSpec((None, t, 1), lambda h, qi, ki: (h, qi, 0))],
                  out_shape=[jax.ShapeDtypeStruct((T, H * VD), F32), jax.ShapeDtypeStruct((H, T, 1), F32)],
                  scratch_shapes=[pltpu.VMEM((t, 1), F32), pltpu.VMEM((t, 1), F32), pltpu.VMEM((t, VD), F32)])(q, k, v)


def _attn_bwd(name, q, k, v, o, lse, do):
    H, T, _ = q.shape
    t, n = _attn_tiles(T)
    scale = QK ** -0.5

    def body(q_ref, k_ref, v_ref, o_ref, lse_ref, do_ref, dq_ref, dk_ref, dv_ref, dk_acc, dv_acc):
        ki, qi = pl.program_id(1), pl.program_id(2)

        @pl.when((ki == 0) & (qi == 0))
        def _():
            dq_ref[...] = jnp.zeros_like(dq_ref)

        @pl.when(qi == 0)
        def _():
            dk_acc[...] = jnp.zeros_like(dk_acc)
            dv_acc[...] = jnp.zeros_like(dv_acc)

        def tile(diagonal):
            qv, kv_, dov = q_ref[...], k_ref[...], do_ref[...]
            s = lax.dot_general(qv, kv_, _DIMS["nt"], preferred_element_type=F32) * scale
            p = jnp.exp(s - lse_ref[...])
            if diagonal:
                p = jnp.where(_tri(t), p, 0.0)
            dob = dov.astype(BF16)
            delta = jnp.sum(o_ref[...] * dov, axis=-1, keepdims=True)
            dv_acc[...] += lax.dot_general(p.astype(BF16), dob, _DIMS["tn"], preferred_element_type=F32)
            dp = lax.dot_general(dob, v_ref[...], _DIMS["nt"], preferred_element_type=F32)
            ds = (p * (dp - delta) * scale).astype(BF16)
            rs = pl.ds(pl.multiple_of(qi * t, t), t)
            dq_ref[rs, :] += jnp.dot(ds, kv_, preferred_element_type=F32)
            dk_acc[...] += lax.dot_general(ds, qv, _DIMS["tn"], preferred_element_type=F32)

        @pl.when(qi > ki)
        def _():
            tile(False)

        @pl.when(qi == ki)
        def _():
            tile(True)

        @pl.when(qi == n - 1)
        def _():
            dk_ref[...] = dk_acc[...]
            dv_ref[...] = dv_acc[...]

    qrow = lambda w: pl.BlockSpec((None, t, w), lambda h, ki, qi: (h, jnp.maximum(qi, ki), 0))
    krow = lambda w: pl.BlockSpec((None, t, w), lambda h, ki, qi: (h, ki, 0))
    wide = pl.BlockSpec((t, VD), lambda h, ki, qi: (jnp.maximum(qi, ki), h))
    return _pcall(body, name=name, grid=(H, n, n),
                  in_specs=[qrow(QK), krow(QK), krow(VD), wide, qrow(1), wide],
                  out_specs=[pl.BlockSpec((None, T, QK), lambda h, ki, qi: (h, 0, 0)), krow(QK), krow(VD)],
                  out_shape=[jax.ShapeDtypeStruct((H, T, QK), F32), jax.ShapeDtypeStruct((H, T, QK), F32),
                             jax.ShapeDtypeStruct((H, T, VD), F32)],
                  scratch_shapes=[pltpu.VMEM((t, QK), F32), pltpu.VMEM((t, VD), F32)])(q, k, v, o, lse, do)


def _tril():
    return lax.broadcasted_iota(jnp.int32, (CHUNK, CHUNK), 1) <= lax.broadcasted_iota(jnp.int32, (CHUNK, CHUNK), 0)


@jax.custom_vjp
def _gm_gate(v, ws, b_t):
    wc = jnp.where(_tril()[None], ws, 0.0).astype(BF16)
    vb = v.astype(BF16)
    rows = []
    for c in range(v.shape[0] // CHUNK):
        cols = []
        for g in range(GROUPS):
            vc = vb[c * CHUNK:(c + 1) * CHUNK, g * LANE:(g + 1) * LANE]
            cols.append(jnp.dot(wc[g], vc, preferred_element_type=F32) + jnp.broadcast_to(b_t[:, g:g + 1], (CHUNK, LANE)))
        rows.append(jnp.concatenate(cols, axis=-1))
    return jnp.concatenate(rows, axis=0)


def _gm_gate_fwd(v, ws, b_t):
    return _gm_gate(v, ws, b_t), (v, ws)


def _gm_gate_bwd(res, dgate):
    v, ws = res
    tril = _tril()
    wc = jnp.where(tril[None], ws, 0.0).astype(BF16)
    vb = v.astype(BF16)
    dgb = dgate.astype(BF16)
    dws = [jnp.zeros((CHUNK, CHUNK), F32) for _ in range(GROUPS)]
    db = jnp.zeros((CHUNK, GROUPS), F32)
    lane_g = lax.broadcasted_iota(jnp.int32, (1, GROUPS), 1)
    rows = []
    for c in range(v.shape[0] // CHUNK):
        cols = []
        for g in range(GROUPS):
            sl = (slice(c * CHUNK, (c + 1) * CHUNK), slice(g * LANE, (g + 1) * LANE))
            cols.append(lax.dot_general(wc[g], dgb[sl], _DIMS["tn"], preferred_element_type=F32))
            dws[g] = dws[g] + lax.dot_general(dgb[sl], vb[sl], _DIMS["nt"], preferred_element_type=F32)
            db = db + jnp.sum(dgate[sl], axis=1, keepdims=True) * (lane_g == g).astype(F32)
        rows.append(jnp.concatenate(cols, axis=-1))
    dws = jnp.stack([jnp.where(tril, d, 0.0) for d in dws])
    return jnp.concatenate(rows, axis=0), dws, db


_gm_gate.defvjp(_gm_gate_fwd, _gm_gate_bwd)


def _mix_fn(a_out, zu, zv, aon_g, gon_g, vn_g, ws, b_t):
    u = jax.nn.gelu(zu)
    vv = _rms(jax.nn.gelu(zv), vn_g)
    g_out = u * _gm_gate(vv, ws, b_t)
    return jnp.concatenate([_rms(a_out, aon_g), _rms(g_out, gon_g)], axis=-1)


def _mix_ins(a_out, z, small, offs):
    gw = a_out.shape[1]
    return [(a_out, "row"), (z, ("cols", offs[0], gw)), (z, ("cols", offs[1], gw))] + [(s, "full") for s in small]


def _mix_fwd(name, a_out, z, small, offs):
    T, gw = a_out.shape
    return _rowwise(name, lambda *a: (_mix_fn(*a),), _mix_ins(a_out, z, small, offs), [((T, 2 * gw), BF16, "row")],
                    T, _pick(T, (256, 128)))[0]


def _mix_bwd(name, a_out, z, small, offs, dmixed):
    T, gw = a_out.shape

    def fn(*a):
        _, vjp = jax.vjp(_mix_fn, *a[:-1])
        return vjp(a[-1].astype(F32))

    outs = [((T, gw), F32, "row"), ((T, gw), BF16, "row"), ((T, gw), BF16, "row")] + [(s.shape, F32, "acc") for s in small]
    return _rowwise(name, fn, _mix_ins(a_out, z, small, offs) + [(dmixed, "row")], outs, T, _pick(T, (256, 128)))


def _ple_fn(gl, pe, g):
    return jax.nn.sigmoid(gl) * _rms(pe, g)


def _ple_fwd(name, h, gl, pe, g):
    T, D = h.shape
    return _rowwise(name, lambda hv, a, b, c: (hv + _ple_fn(a, b, c),),
                    [(h, "row"), (gl, "row"), (pe, "row"), (g, "full")], [((T, D), F32, "row")], T, _pick(T, (256, 128)))[0]


def _ple_bwd(name, gl, pe, g, dh, deps=()):
    T, D = gl.shape

    def fn(a, b, c, d):
        _, vjp = jax.vjp(_ple_fn, a, b, c)
        return vjp(d)

    return _rowwise(name, fn, [(gl, "row"), (pe, "row"), (g, "full"), (dh, "row")],
                    [((T, D), BF16, "row"), ((T, D), BF16, "row"), ((1, D), F32, "acc")], T, _pick(T, (256, 128)), deps)


def _loss(name, y, target):
    T, D = y.shape

    def fn(yv, tv):
        err = yv - tv
        part = 0.5 * jnp.sum(jnp.mean(err * err, axis=-1, keepdims=True), axis=0, keepdims=True)
        return err * (1.0 / D), jnp.broadcast_to(part, (8, LANE))

    return _rowwise(name, fn, [(y, "row"), (target, "row")], [((T, D), F32, "row"), ((8, LANE), F32, "acc")], T,
                    _pick(T, (512, 256, 128)))


ADAMW_BLOCK_ELEMS = 128 * 1024


def _adamw_sum(name, parts, w, m, v):
    L, R, C = w.shape
    tiles = [(r, c) for r in (R, 512, 256, 128, 64, 32, 16) for c in (C, 1024, 512, 256, 128)
             if R % r == 0 and C % c == 0 and r * c <= ADAMW_BLOCK_ELEMS]
    tr, tc = max(tiles, key=lambda rc: (rc[0] * rc[1], rc[1]))
    nr, nc = R // tr, C // tc
    c1 = 1.0 - ADAM_B1 ** ADAM_STEP
    c2 = 1.0 - ADAM_B2 ** ADAM_STEP

    def body(*refs):
        p_refs = refs[:L]
        w_ref, m_ref, v_ref, g_out, d_out, m_out, v_out = refs[L:]
        layer = pl.program_id(0)

        def part(s):
            val = p_refs[0][s].astype(F32)
            for j in range(1, L):
                val = jnp.where(layer == j, p_refs[j][s].astype(F32), val)
            return val

        g = part(0)
        for s in range(1, N_DEV):
            g = g + part(s)
        m2 = ADAM_B1 * m_ref[...] + (1.0 - ADAM_B1) * g
        v2 = ADAM_B2 * v_ref[...] + (1.0 - ADAM_B2) * (g * g)
        g_out[...] = g
        m_out[...] = m2
        v_out[...] = v2
        d_out[...] = -ADAM_LR * ((m2 / c1) / (jnp.sqrt(v2 / c2) + ADAM_EPS) + ADAM_WD * w_ref[...])

    def part_spec(j):
        def index(l, i, k):
            before, mine = l < j, l == j
            return (0, jnp.where(mine, i, jnp.where(before, 0, nr - 1)), jnp.where(mine, k, jnp.where(before, 0, nc - 1)))
        return pl.BlockSpec((N_DEV, tr, tc), index)

    blk = pl.BlockSpec((None, tr, tc), lambda l, i, k: (l, i, k))
    sd = jax.ShapeDtypeStruct((L, R, C), F32)
    return _pcall(body, name=name, grid=(L, nr, nc),
                  in_specs=[part_spec(j) for j in range(L)] + [blk, blk, blk],
                  out_specs=[blk, blk, blk, blk], out_shape=[sd, sd, sd, sd])(*parts, w, m, v)


def _unshard_cols(g):
    _, K, n = g.shape
    return g.transpose(1, 0, 2).reshape(K, N_DEV * n)


def _shard_cols(full):
    K, N = full.shape
    return full.reshape(K, N_DEV, N // N_DEV).transpose(1, 0, 2)


def _unshard_rows(g):
    _, k, N = g.shape
    return g.reshape(N_DEV * k, N)


def _shard_rows(full):
    K, N = full.shape
    return full.reshape(N_DEV, K // N_DEV, N)


STAGES = (('ffn_a_w1', 'ffn_a_w3', 'ffn_a_w2'), ('w_in', 'w_uq', 'w_ukv', 'w_out'),
          ('ffn_b_w1', 'ffn_b_w3', 'ffn_b_w2'), ('w_ple_gate', 'w_ple'))
TRANSPOSED = ('ffn_a_w1', 'ffn_a_w3', 'ffn_b_w1', 'ffn_b_w3', 'w_in', 'w_uq')


def _step(x, p, positions, target, w, m, v):
    T, D = x.shape[1], x.shape[2]
    L = p.shape[0]
    x2, target2 = x[0], target[0]
    q_rank, kv_rank = w['w_uq'].shape[1], w['w_ukv'].shape[1]
    gw = w['gm_v_norm'].shape[1]

    inv_freq = ROPE_BASE ** (-jnp.arange(0, ROPE, 2, dtype=F32) / ROPE)
    ang = positions[0].astype(F32)[:, None] * inv_freq
    cos = jnp.concatenate([jnp.ones((T, NOPE), F32), jnp.cos(ang), jnp.cos(ang)], axis=-1)
    sin = jnp.concatenate([jnp.zeros((T, NOPE), F32), jnp.sin(ang), jnp.sin(ang)], axis=-1)
    tabs = (cos, sin, _rot_matrix().astype(BF16))

    groups = [(l, names) for l in range(L) for names in STAGES]

    def ag_start(k, after):
        l, names = groups[k]
        shards = [w[n][l].astype(BF16) for n in names]
        return _xchg_start(f"ag_chips{k}", "chips", shards, [_landing(s) for s in shards], after)

    ag = {0: ag_start(0, [])}
    ag[1] = ag_start(1, [ag[0]['token']])

    def fetch(k, after):
        lands = _xchg_wait(f"ag_landed{k}", ag[k], after)
        fw = _xchg_start(f"ag_forward{k}", "forward", [], lands, [])
        deps = [fw['token']]
        if k + 2 < len(groups):
            ag[k + 2] = ag_start(k + 2, deps)
            deps = [ag[k + 2]['token']]
        return fw, deps

    def gathered(k, fw, after):
        return dict(zip(groups[k][1], _xchg_wait(f"ag_wait{k}", fw, after)))

    s0, s1, s2, s3 = q_rank, q_rank + kv_rank, q_rank + kv_rank + ROPE, q_rank + kv_rank + ROPE + gw
    o_u, o_v, o_cq, o_ckv, o_kr = 0, gw, 2 * gw, 2 * gw + q_rank, 2 * gw + q_rank + kv_rank
    kr_pad = 2 * LANE - ROPE

    def g2(name, l):
        return w[name][l][None, :]

    gm_bt = [w['gm_bs'][l].T for l in range(L)]

    saved = []
    h = x2
    for l in range(L):
        s = {}
        fw, deps = fetch(4 * l, [h])
        s['h0'] = h
        s['xn_a'] = _rms_fwd(f"rms_a{l}", h, g2('ffn_a_norm', l), deps)
        wa = s['wa'] = gathered(4 * l, fw, [s['xn_a']])
        h, s['a_h1'], s['a_h3'] = _ffn_fwd(f"ffn_a_fwd{l}", s['xn_a'], h, wa['ffn_a_w1'], wa['ffn_a_w3'], wa['ffn_a_w2'])
        fw, deps = fetch(4 * l + 1, [h])
        s['h1'] = h
        s['n'] = _rms_fwd(f"rms_mix{l}", h, g2('mix_norm', l), deps)
        wm = gathered(4 * l + 1, fw, [s['n']])
        w_in_full = _unshard_cols(wm['w_in'])
        s['w_in'] = jnp.concatenate([w_in_full[:, s2:s3], w_in_full[:, s3:], w_in_full[:, :s0], w_in_full[:, s0:s1],
                                     w_in_full[:, s1:s2], jnp.zeros((D, kr_pad), BF16)], axis=-1)
        s['w_out'] = _unshard_rows(wm['w_out'])
        s['w_uq'], s['w_ukv'] = wm['w_uq'], wm['w_ukv']
        s['z'] = _mm(f"w_in{l}", s['n'], s['w_in'], "nn")
        s['mla_small'] = [g2('q_a_norm', l), g2('kv_a_norm', l), g2('q_norm', l), g2('k_norm', l)]
        s['q'], s['k'], s['v'] = _mla_prep_fwd(f"mla_prep{l}", s['z'], tabs, s['mla_small'], s['w_uq'], s['w_ukv'],
                                               (o_cq, o_ckv, o_kr))
        s['a_out'], s['lse'] = _attn_fwd(f"attn_fwd{l}", s['q'], s['k'], s['v'])
        s['mix_small'] = [g2('attn_out_norm', l), g2('gm_out_norm', l), g2('gm_v_norm', l), w['gm_ws'][l], gm_bt[l]]
        s['mixed'] = _mix_fwd(f"mix_fwd{l}", s['a_out'], s['z'], s['mix_small'], (o_u, o_v))
        h = _mm(f"w_out{l}", s['mixed'], s['w_out'], "nn", res=h)
        fw, deps = fetch(4 * l + 2, [h])
        s['h2'] = h
        s['xn_b'] = _rms_fwd(f"rms_b{l}", h, g2('ffn_b_norm', l), deps)
        wb = s['wb'] = gathered(4 * l + 2, fw, [s['xn_b']])
        h, s['b_h1'], s['b_h3'] = _ffn_fwd(f"ffn_b_fwd{l}", s['xn_b'], h, wb['ffn_b_w1'], wb['ffn_b_w3'], wb['ffn_b_w2'])
        fw, deps = fetch(4 * l + 3, [h])
        s['h3'] = h
        s['xn_g'] = _rms_fwd(f"rms_g{l}", h, g2('ple_gate_norm', l), deps)
        wp = gathered(4 * l + 3, fw, [s['xn_g']])
        s['w_gate'] = _unshard_rows(wp['w_ple_gate'])
        s['gl'] = _mm(f"w_gate{l}", s['xn_g'], s['w_gate'], "nn")
        s['p'] = p[l, 0]
        s['pe'] = _mm(f"w_ple{l}", s['p'], _unshard_cols(wp['w_ple']), "nn")
        h = _ple_fwd(f"ple_fwd{l}", h, s['gl'], s['pe'], g2('ple_norm', l))
        saved.append(s)

    dh, loss_part = _loss("loss", h, target2)
    loss = lax.psum(loss_part[0, 0], AXES)

    gsmall = {n: [None] * L for n in SMALL}
    rs, where = {}, {}

    def rs_start(key, l, named):
        grads = [g for _, g in named]
        lands = [lax.dynamic_update_slice(lax.empty(g.shape, g.dtype),
                                          lax.dynamic_index_in_dim(g, _my_index(), 0, keepdims=True),
                                          (_my_index(),) + (0,) * (g.ndim - 1)) for g in grads]
        rs[key] = _xchg_start("rs_start_" + key, "scatter", grads, lands, [])
        where.update({(n, l): (key, i) for i, (n, _) in enumerate(named)})
        return [rs[key]['token']]

    def ffn_backward(tag, l, dh, xn, h_in, h1, h3, wts, norm_name, deps):
        pre = 'ffn_' + tag
        w1t, w3t, w2t = [jnp.swapaxes(wts[pre + sfx], 1, 2) for sfx in ('_w1', '_w3', '_w2')]
        dxn, dhb, dh1, dh3, act = _ffn_bwd(f"{pre}_bwd{l}", dh, h1, h3, w1t, w3t, w2t, deps)
        g1 = _mm_tn_batch(f"{pre}_dw1_{l}", dh1, xn[None])
        deps = rs_start(f"{pre}_w1_{l}", l, [(pre + '_w1', g1)])
        g3 = _mm_tn_batch(f"{pre}_dw3_{l}", dh3, xn[None], deps=deps)
        deps = rs_start(f"{pre}_w3_{l}", l, [(pre + '_w3', g3)])
        g2_ = _mm_tn_batch(f"{pre}_dw2_{l}", act, dhb[None], alpha=0.5, deps=deps)
        deps = rs_start(f"{pre}_w2_{l}", l, [(pre + '_w2', g2_)])
        return _rms_bwd(f"rms_{tag}_bwd{l}", h_in, g2(norm_name, l), dxn, dh, deps)

    deps = []
    for l in reversed(range(L)):
        s = saved[l]
        d_gl, d_pe, gsmall['ple_norm'][l] = _ple_bwd(f"ple_bwd{l}", s['gl'], s['pe'], g2('ple_norm', l), dh, deps)
        g_ple = _mm(f"dw_ple{l}", s['p'], d_pe, "tn", out_dtype=BF16)
        g_gate = _mm(f"dw_gate{l}", s['xn_g'], d_gl, "tn", out_dtype=BF16)
        d_xng = _mm(f"d_xng{l}", d_gl, s['w_gate'], "nt")
        dh, gsmall['ple_gate_norm'][l] = _rms_bwd(f"rms_g_bwd{l}", s['h3'], g2('ple_gate_norm', l), d_xng, dh)
        deps = rs_start(f"ple_{l}", l, [('w_ple_gate', _shard_rows(g_gate)), ('w_ple', _shard_cols(g_ple))])
        dh, gsmall['ffn_b_norm'][l] = ffn_backward('b', l, dh, s['xn_b'], s['h2'], s['b_h1'], s['b_h3'], s['wb'],
                                                   'ffn_b_norm', deps)
        g_out = _mm(f"dw_out{l}", s['mixed'], dh, "tn", out_dtype=BF16)
        d_mixed = _mm(f"d_mixed{l}", dh, s['w_out'], "nt")
        mix = _mix_bwd(f"mix_bwd{l}", s['a_out'], s['z'], s['mix_small'], (o_u, o_v), d_mixed)
        d_a_out, d_u, d_v = mix[:3]
        gsmall['attn_out_norm'][l], gsmall['gm_out_norm'][l], gsmall['gm_v_norm'][l], gsmall['gm_ws'][l] = mix[3:7]
        gsmall['gm_bs'][l] = mix[7].T
        dq, dk, dv = _attn_bwd(f"attn_bwd{l}", s['q'], s['k'], s['v'], s['a_out'], s['lse'], d_a_out)
        mla = _mla_prep_bwd(f"mla_prep_bwd{l}", s['z'], tabs, s['mla_small'], s['w_uq'], s['w_ukv'], (o_cq, o_ckv, o_kr),
                            dq, dk, dv)
        d_cq, d_ckv, d_kr = mla[:3]
        gsmall['q_a_norm'][l], gsmall['kv_a_norm'][l], gsmall['q_norm'][l], gsmall['k_norm'][l] = mla[3:7]
        dz = jnp.concatenate([d_u, d_v, d_cq, d_ckv, d_kr, jnp.zeros((T, LANE), BF16)], axis=-1)
        g_in = _mm(f"dw_in{l}", dz, s['n'], "tn", out_dtype=BF16)
        g_in = jnp.concatenate([g_in[o_cq:o_cq + q_rank], g_in[o_ckv:o_ckv + kv_rank], g_in[o_kr:o_kr + ROPE],
                                g_in[o_u:o_u + gw], g_in[o_v:o_v + gw]], axis=0)
        d_n = _mm(f"d_n{l}", dz, s['w_in'], "nt")
        dh, gsmall['mix_norm'][l] = _rms_bwd(f"rms_mix_bwd{l}", s['h1'], g2('mix_norm', l), d_n, dh)
        deps = rs_start(f"mix_{l}", l, [('w_in', _shard_rows(g_in)), ('w_uq', mla[7].transpose(0, 2, 1).astype(BF16)),
                                        ('w_ukv', mla[8].astype(BF16)), ('w_out', _shard_rows(g_out))])
        dh, gsmall['ffn_a_norm'][l] = ffn_backward('a', l, dh, s['xn_a'], s['h0'], s['a_h1'], s['a_h3'], s['wa'],
                                                   'ffn_a_norm', deps)
        deps = []
    grad_x = dh[None]

    out = {}
    sizes = [w[n].size for n in SMALL]
    total = sum(sizes)
    padded = -(-total // (512 * LANE)) * (512 * LANE)

    def pack(d):
        flat = jnp.concatenate([d[n].reshape(-1) for n in SMALL] + [jnp.zeros((padded - total,), F32)])
        return flat.reshape(1, padded // LANE, LANE)

    gs = pack({n: jnp.stack([gsmall[n][l].reshape(w[n].shape[1:]) for l in range(L)]) for n in SMALL})
    small = _xchg_start("small_start", "gather", [gs[0]], [_landing(gs[0])], [])

    after = [dh, small['token']]
    landed = {}

    def partials(n, l):
        key, i = where[(n, l)]
        if key not in landed:
            landed[key] = _xchg_wait("rs_wait_" + key, rs[key], after)
        return landed[key][i]

    swap = lambda a: a.transpose(0, 2, 1)
    for stage in (3, 2, 1, 0):
        for n in STAGES[stage]:
            parts = [partials(n, l) for l in reversed(range(L))][::-1]
            if n in TRANSPOSED:
                out[n] = [swap(r) for r in _adamw_sum("adamw_" + n, parts, swap(w[n]), swap(m[n]), swap(v[n]))]
            else:
                out[n] = _adamw_sum("adamw_" + n, parts, w[n], m[n], v[n])
            after = [out[n][0]]

    res = _adamw_sum("adamw_small", _xchg_wait("small_wait", small, after), pack(w), pack(m), pack(v))
    off = 0
    for n, sz in zip(SMALL, sizes):
        out[n] = [r.reshape(-1)[off:off + sz].reshape(w[n].shape) for r in res]
        off += sz

    return (loss, grad_x, *[out[n][0] for n in WEIGHTS], *[out[n][1] for n in WEIGHTS],
            *[out[n][2] for n in WEIGHTS], *[out[n][3] for n in WEIGHTS])


def kernel(x, p, positions, ffn_a_norm, ffn_a_w1, ffn_a_w3, ffn_a_w2, mix_norm, w_in, q_a_norm, w_uq, kv_a_norm, w_ukv, q_norm, k_norm, gm_v_norm, gm_ws, gm_bs, attn_out_norm, gm_out_norm, w_out, ffn_b_norm, ffn_b_w1, ffn_b_w3, ffn_b_w2, ple_gate_norm, w_ple_gate, w_ple, ple_norm, loss_target, m_ffn_a_norm, m_ffn_a_w1, m_ffn_a_w3, m_ffn_a_w2, m_mix_norm, m_w_in, m_q_a_norm, m_w_uq, m_kv_a_norm, m_w_ukv, m_q_norm, m_k_norm, m_gm_v_norm, m_gm_ws, m_gm_bs, m_attn_out_norm, m_gm_out_norm, m_w_out, m_ffn_b_norm, m_ffn_b_w1, m_ffn_b_w3, m_ffn_b_w2, m_ple_gate_norm, m_w_ple_gate, m_w_ple, m_ple_norm, v_ffn_a_norm, v_ffn_a_w1, v_ffn_a_w3, v_ffn_a_w2, v_mix_norm, v_w_in, v_q_a_norm, v_w_uq, v_kv_a_norm, v_w_ukv, v_q_norm, v_k_norm, v_gm_v_norm, v_gm_ws, v_gm_bs, v_attn_out_norm, v_gm_out_norm, v_w_out, v_ffn_b_norm, v_ffn_b_w1, v_ffn_b_w3, v_ffn_b_w2, v_ple_gate_norm, v_w_ple_gate, v_w_ple, v_ple_norm):
    args = locals()
    w = {n: args[n] for n in WEIGHTS}
    m = {n: args["m_" + n] for n in WEIGHTS}
    v = {n: args["v_" + n] for n in WEIGHTS}
    return _step(x, p, positions, loss_target, w, m, v)
```

```python
import functools

import jax
import jax.numpy as jnp
from jax import lax
from jax.experimental import pallas as pl
from jax.experimental.pallas import tpu as pltpu

F32, BF16 = jnp.float32, jnp.bfloat16
EPS = 1e-6
N_DEV = 8
HEADS = 8
NOPE, ROPE, QK, VD = 128, 64, 192, 128
CHUNK = 128
GROUPS = 8
LANE = 128
ROPE_BASE = 10000.0
ADAM_LR, ADAM_B1, ADAM_B2, ADAM_EPS, ADAM_WD, ADAM_STEP = 0.001, 0.9, 0.999, 1e-08, 0.01, 10
AXES = ("x", "y", "c")
MESH = pl.DeviceIdType.MESH
ANY = pl.BlockSpec(memory_space=pl.ANY)

WEIGHTS = ['ffn_a_norm', 'ffn_a_w1', 'ffn_a_w3', 'ffn_a_w2', 'mix_norm', 'w_in', 'q_a_norm', 'w_uq', 'kv_a_norm',
           'w_ukv', 'q_norm', 'k_norm', 'gm_v_norm', 'gm_ws', 'gm_bs', 'attn_out_norm', 'gm_out_norm', 'w_out',
           'ffn_b_norm', 'ffn_b_w1', 'ffn_b_w3', 'ffn_b_w2', 'ple_gate_norm', 'w_ple_gate', 'w_ple', 'ple_norm']
BIG = ['ffn_a_w1', 'ffn_a_w3', 'ffn_a_w2', 'w_in', 'w_uq', 'w_ukv', 'w_out', 'ffn_b_w1', 'ffn_b_w3', 'ffn_b_w2',
       'w_ple_gate', 'w_ple']
SMALL = [n for n in WEIGHTS if n not in BIG]


def _pcall(body, **kw):
    return pl.pallas_call(body, **kw)


def _pick(n, cands):
    for c in cands:
        if n % c == 0:
            return c
    return n


def _rms(x, g):
    return x * lax.rsqrt(jnp.mean(x * x, axis=-1, keepdims=True) + EPS) * g


@jax.custom_vjp
def _bdot(x, w):
    return jnp.dot(x.astype(BF16), w.astype(BF16), preferred_element_type=F32)


def _bdot_fwd(x, w):
    return _bdot(x, w), (x, w)


def _bdot_bwd(res, dy):
    x, w = res
    dyb = dy.astype(BF16)
    dx = lax.dot_general(dyb, w.astype(BF16), (((1,), (1,)), ((), ())), preferred_element_type=F32)
    dw = lax.dot_general(x.astype(BF16), dyb, (((0,), (0,)), ((), ())), preferred_element_type=F32)
    return dx.astype(x.dtype), dw.astype(w.dtype)


_bdot.defvjp(_bdot_fwd, _bdot_bwd)


def _split_dot(x, p, dims):
    hi = x.astype(BF16)
    lo = (x - hi.astype(F32)).astype(BF16)
    return (lax.dot_general(hi, p, dims, preferred_element_type=F32)
            + lax.dot_general(lo, p, dims, preferred_element_type=F32))


@jax.custom_vjp
def _permute(x, p):
    return _split_dot(x, p, _DIMS["nn"])


def _permute_fwd(x, p):
    return _permute(x, p), p


def _permute_bwd(p, ct):
    return _split_dot(ct, p, _DIMS["nt"]), jnp.zeros_like(p)


_permute.defvjp(_permute_fwd, _permute_bwd)


def _flip(v, bit):
    return 1 - v if bit else v


HBM = pl.BlockSpec(memory_space=pltpu.HBM)
SEM = pl.BlockSpec(memory_space=pltpu.SEMAPHORE)
EFFECT = pltpu.SideEffectType.DATAFLOW_SIDE_EFFECTING
PEERS = N_DEV - 1


def _my_index():
    return 4 * lax.axis_index("x") + 2 * lax.axis_index("y") + lax.axis_index("c")


def _landing(own):
    zone = lax.empty((N_DEV,) + own.shape, own.dtype)
    return lax.dynamic_update_slice(zone, own[None], (_my_index(),) + (0,) * own.ndim)


COPIES = {"gather": PEERS, "scatter": PEERS, "chips": 4, "forward": 3}


def _copy_plan(kind, src_refs, land_refs, send_sems, recv_sems):
    cx, cy, cc = lax.axis_index("x"), lax.axis_index("y"), lax.axis_index("c")
    me = 4 * cx + 2 * cy + cc
    per = COPIES[kind]
    out = []
    for t, land in enumerate(land_refs):
        def pair(i, src, to_slot, from_slot, dev):
            kw = dict(send_sem=send_sems.at[per * t + i], recv_sem=recv_sems.at[per * t + i], device_id=dev,
                      device_id_type=MESH)
            out.append((pltpu.make_async_remote_copy(src_ref=src, dst_ref=land.at[to_slot], **kw),
                        pltpu.make_async_remote_copy(src_ref=src, dst_ref=land.at[from_slot], **kw)))

        if kind in ("gather", "scatter"):
            for k in range(1, N_DEV):
                px, py, pc = _flip(cx, k & 4), _flip(cy, k & 2), _flip(cc, k & 1)
                peer = 4 * px + 2 * py + pc
                pair(k - 1, src_refs[t].at[peer] if kind == "scatter" else src_refs[t], me, peer, (px, py, pc))
        elif kind == "chips":
            pair(0, src_refs[t], me, me + 1 - 2 * cc, (cx, cy, 1 - cc))
            for j in range(1, 4):
                px, py = _flip(cx, j & 2), _flip(cy, j & 1)
                pair(j, src_refs[t], me, 4 * px + 2 * py + cc, (px, py, cc))
        else:
            for j in range(1, 4):
                px, py = _flip(cx, j & 2), _flip(cy, j & 1)
                mine, theirs = 4 * px + 2 * py + cc, 4 * px + 2 * py + 1 - cc
                pair(j - 1, land.at[mine], mine, theirs, (cx, cy, 1 - cc))
    return out


def _xchg_start(name, kind, srcs, lands, after):
    ns, nb, na = len(srcs), len(srcs) + len(lands), len(after)
    n_sems = COPIES[kind] * len(lands)

    def body(*refs):
        send_sems, recv_sems = refs[nb + na], refs[nb + na + 1]
        for send, _ in _copy_plan(kind, refs[:ns], refs[ns:nb], send_sems, recv_sems):
            send.start()
        refs[-1][...] = jnp.zeros_like(refs[-1])

    bufs = list(srcs) + list(lands)
    res = _pcall(
        body, name=name,
        out_shape=(pltpu.SemaphoreType.DMA((n_sems,)), pltpu.SemaphoreType.DMA((n_sems,)),
                   *[pltpu.HBM(a.shape, a.dtype) for a in bufs], jax.ShapeDtypeStruct((8, LANE), F32)),
        in_specs=[HBM] * nb + [ANY] * na,
        out_specs=(SEM, SEM, *([HBM] * nb), pl.BlockSpec(memory_space=pltpu.VMEM)),
        input_output_aliases={i: 2 + i for i in range(nb)},
        compiler_params=pltpu.CompilerParams(has_side_effects=EFFECT),
    )(*[pltpu.with_memory_space_constraint(a, pltpu.HBM) for a in bufs], *after)
    return dict(kind=kind, send=res[0], recv=res[1], srcs=list(res[2:2 + ns]), lands=list(res[2 + ns:2 + nb]),
                token=res[-1])


def _xchg_wait(name, st, after):
    ns, nb = len(st['srcs']), len(st['srcs']) + len(st['lands'])

    def body(*refs):
        for _, back in _copy_plan(st['kind'], refs[:ns], refs[ns:nb], refs[nb], refs[nb + 1]):
            back.wait_send()
            back.wait_recv()

    bufs = st['srcs'] + st['lands']
    res = _pcall(
        body, name=name, out_shape=tuple(pltpu.HBM(a.shape, a.dtype) for a in bufs),
        in_specs=[HBM] * nb + [SEM, SEM] + [ANY] * len(after), out_specs=tuple([HBM] * nb),
        input_output_aliases={i: i for i in range(nb)},
        compiler_params=pltpu.CompilerParams(has_side_effects=EFFECT),
    )(*bufs, st['send'], st['recv'], *after)
    return list(res[ns:])


_DIMS = {"nn": (((1,), (0,)), ((), ())), "nt": (((1,), (1,)), ((), ())), "tn": (((0,), (0,)), ((), ()))}


MM_WHOLE_K = 2048


def _mm(name, a, b, mode, out_dtype=F32, res=None, alpha=1.0, deps=()):
    if mode == "tn":
        K, M = a.shape
        N = b.shape[1]
    else:
        M, K = a.shape
        N = b.shape[0] if mode == "nt" else b.shape[1]
    tn = _pick(N, (1024, 512, 256))
    tm = _pick(M, (1024, 512, 256, 128) if tn <= 1024 else (512, 256, 128))
    tk = K if K <= MM_WHOLE_K else _pick(K, (1024, 512, 256, 128))
    nk = K // tk
    a_spec = pl.BlockSpec((tk, tm), lambda i, j, k: (k, i)) if mode == "tn" else pl.BlockSpec((tm, tk), lambda i, j, k: (i, k))
    b_spec = pl.BlockSpec((tn, tk), lambda i, j, k: (j, k)) if mode == "nt" else pl.BlockSpec((tk, tn), lambda i, j, k: (k, j))
    o_spec = pl.BlockSpec((tm, tn), lambda i, j, k: (i, j))
    dims = _DIMS[mode]

    def body(*refs):
        a_ref, b_ref, r_ref = refs[0], refs[1], refs[2]
        part = lax.dot_general(a_ref[...].astype(BF16), b_ref[...].astype(BF16), dims, preferred_element_type=F32)

        def finish(o_ref, r):
            r = r * alpha if alpha != 1.0 else r
            if res is not None:
                r = r_ref[...] + r
            o_ref[...] = r.astype(o_ref.dtype)

        if nk == 1:
            finish(refs[-1], part)
            return
        o_ref, acc = refs[-2], refs[-1]
        k = pl.program_id(2)

        @pl.when(k == 0)
        def _():
            acc[...] = part

        @pl.when(k > 0)
        def _():
            acc[...] += part

        @pl.when(k == nk - 1)
        def _():
            finish(o_ref, acc[...])

    ins = [a, b] + ([] if res is None else [res]) + list(deps)
    specs = [a_spec, b_spec] + ([] if res is None else [o_spec]) + [ANY] * len(deps)
    return _pcall(body, name=name, grid=(M // tm, N // tn, nk), in_specs=specs, out_specs=o_spec,
                  out_shape=jax.ShapeDtypeStruct((M, N), out_dtype),
                  scratch_shapes=[] if nk == 1 else [pltpu.VMEM((tm, tn), F32)])(*ins)


def _mm_tn_batch(name, a3, b3, alpha=1.0, deps=()):
    ga, T, M = a3.shape
    gb, _, N = b3.shape
    G = max(ga, gb)
    tm = _pick(M, (1024, 512)) if N <= 1024 else M
    tk = _pick(T, (1024, 512, 256, 128))
    nk = T // tk
    a_spec = pl.BlockSpec((None, tk, tm), (lambda g, i, k: (g, k, i)) if ga > 1 else (lambda g, i, k: (0, k, i)))
    b_spec = pl.BlockSpec((None, tk, N), (lambda g, i, k: (g, k, 0)) if gb > 1 else (lambda g, i, k: (0, k, 0)))
    o_spec = pl.BlockSpec((None, tm, N), lambda g, i, k: (g, i, 0))

    def body(*refs):
        a_ref, b_ref, o_ref, acc = refs[0], refs[1], refs[-2], refs[-1]
        k = pl.program_id(2)
        part = lax.dot_general(a_ref[...].astype(BF16), b_ref[...].astype(BF16), _DIMS["tn"], preferred_element_type=F32)

        @pl.when(k == 0)
        def _():
            acc[...] = part

        @pl.when(k > 0)
        def _():
            acc[...] += part

        @pl.when(k == nk - 1)
        def _():
            o_ref[...] = (acc[...] * alpha if alpha != 1.0 else acc[...]).astype(o_ref.dtype)

    return _pcall(body, name=name, grid=(G, M // tm, nk), in_specs=[a_spec, b_spec] + [ANY] * len(deps),
                  out_specs=o_spec, out_shape=jax.ShapeDtypeStruct((G, M, N), BF16),
                  scratch_shapes=[pltpu.VMEM((tm, N), F32)])(a3, b3, *deps)


def _rowwise(name, fn, ins, outs, T, tm, deps=()):
    in_specs = []
    for arr, spec in ins:
        if spec == "row":
            in_specs.append(pl.BlockSpec((tm, arr.shape[1]), lambda i: (i, 0)))
        elif spec == "full":
            in_specs.append(pl.BlockSpec(arr.shape, lambda i, _n=arr.ndim: (0,) * _n))
        else:
            _, off, width = spec
            in_specs.append(pl.BlockSpec((tm, width), lambda i, _b=off // width: (i, _b)))
    in_specs += [ANY] * len(deps)
    out_specs, out_shapes = [], []
    for shape, dtype, spec in outs:
        out_shapes.append(jax.ShapeDtypeStruct(shape, dtype))
        if spec == "row":
            out_specs.append(pl.BlockSpec((tm, shape[1]), lambda i: (i, 0)))
        else:
            out_specs.append(pl.BlockSpec(shape, lambda i, _n=len(shape): (0,) * _n))
    n_in = len(ins)

    def body(*refs):
        res = fn(*[r[...] for r in refs[:n_in]])
        i = pl.program_id(0)
        for r, (_, _, spec), val in zip(refs[n_in + len(deps):], outs, res):
            if spec == "acc":
                @pl.when(i == 0)
                def _(r=r):
                    r[...] = jnp.zeros_like(r)
                r[...] += val.astype(r.dtype)
            else:
                r[...] = val.astype(r.dtype)

    return _pcall(body, name=name, grid=(T // tm,), in_specs=in_specs, out_specs=out_specs, out_shape=out_shapes)(
        *[a for a, _ in ins], *deps)


def _rms_fwd(name, h, g, deps=()):
    T, D = h.shape
    return _rowwise(name, lambda hv, gv: (_rms(hv, gv),), [(h, "row"), (g, "full")], [((T, D), BF16, "row")], T,
                    _pick(T, (512, 256, 128)), deps)[0]


def _rms_bwd(name, h, g, dxn, dh_in, deps=()):
    T, D = h.shape

    def fn(hv, gv, dv, dh0):
        _, vjp = jax.vjp(_rms, hv, gv)
        dh, dg = vjp(dv.astype(F32))
        return dh0 + dh, dg

    return _rowwise(name, fn, [(h, "row"), (g, "full"), (dxn, "row"), (dh_in, "row")],
                    [((T, D), F32, "row"), ((1, D), F32, "acc")], T, _pick(T, (256, 128)), deps)


def _ffn_fwd(name, xn, h, w1, w3, w2):
    T, D = xn.shape
    F8 = w1.shape[-1]
    tm = _pick(T, (512, 256, 128))
    wspec = lambda r, c: pl.BlockSpec((None, r, c), lambda i, d: (d, 0, 0))
    row = pl.BlockSpec((tm, D), lambda i, d: (i, 0))
    hid = pl.BlockSpec((None, tm, F8), lambda i, d: (d, i, 0))

    def body(xn_ref, h_ref, w1_ref, w3_ref, w2_ref, out_ref, h1_ref, h3_ref, acc):
        d = pl.program_id(1)

        @pl.when(d == 0)
        def _():
            acc[...] = jnp.zeros_like(acc)

        x = xn_ref[...]
        h1 = jnp.dot(x, w1_ref[...], preferred_element_type=F32)
        h3 = jnp.dot(x, w3_ref[...], preferred_element_type=F32)
        h1_ref[...] = h1.astype(BF16)
        h3_ref[...] = h3.astype(BF16)
        act = (h1 * jax.nn.sigmoid(h1) * h3).astype(BF16)
        acc[...] += jnp.dot(act, w2_ref[...], preferred_element_type=F32)

        @pl.when(d == N_DEV - 1)
        def _():
            out_ref[...] = h_ref[...] + 0.5 * acc[...]

    return _pcall(body, name=name, grid=(T // tm, N_DEV),
                  in_specs=[row, row, wspec(D, F8), wspec(D, F8), wspec(F8, D)],
                  out_specs=[row, hid, hid],
                  out_shape=[jax.ShapeDtypeStruct((T, D), F32), jax.ShapeDtypeStruct((N_DEV, T, F8), BF16),
                             jax.ShapeDtypeStruct((N_DEV, T, F8), BF16)],
                  scratch_shapes=[pltpu.VMEM((tm, D), F32)])(xn, h, w1, w3, w2)


def _ffn_bwd(name, dy, h1, h3, w1, w3, w2, deps=()):
    T, D = dy.shape
    F8 = w1.shape[-1]
    tm = _pick(T, (512, 256, 128))
    wspec = lambda r, c: pl.BlockSpec((None, r, c), lambda i, d: (d, 0, 0))
    row = pl.BlockSpec((tm, D), lambda i, d: (i, 0))
    hid = pl.BlockSpec((None, tm, F8), lambda i, d: (d, i, 0))

    def body(*refs):
        dy_ref, h1_ref, h3_ref, w1_ref, w3_ref, w2_ref = refs[:6]
        dxn_ref, dyb, dh1_ref, dh3_ref, act_ref, acc = refs[6 + len(deps):]
        d = pl.program_id(1)

        @pl.when(d == 0)
        def _():
            acc[...] = jnp.zeros_like(acc)
            dyb[...] = dy_ref[...].astype(BF16)

        dact = 0.5 * lax.dot_general(dyb[...], w2_ref[...], _DIMS["nt"], preferred_element_type=F32)
        h1 = h1_ref[...].astype(F32)
        h3 = h3_ref[...].astype(F32)
        sig = jax.nn.sigmoid(h1)
        silu = h1 * sig
        dh1 = (dact * h3 * (sig * (1.0 + h1 * (1.0 - sig)))).astype(BF16)
        dh3 = (dact * silu).astype(BF16)
        dh1_ref[...] = dh1
        dh3_ref[...] = dh3
        act_ref[...] = (silu * h3).astype(BF16)
        acc[...] += (lax.dot_general(dh1, w1_ref[...], _DIMS["nt"], preferred_element_type=F32)
                     + lax.dot_general(dh3, w3_ref[...], _DIMS["nt"], preferred_element_type=F32))

        @pl.when(d == N_DEV - 1)
        def _():
            dxn_ref[...] = acc[...]

    hshape = jax.ShapeDtypeStruct((N_DEV, T, F8), BF16)
    return _pcall(body, name=name, grid=(T // tm, N_DEV),
                  in_specs=[row, hid, hid, wspec(D, F8), wspec(D, F8), wspec(F8, D)] + [ANY] * len(deps),
                  out_specs=[row, row, hid, hid, hid],
                  out_shape=[jax.ShapeDtypeStruct((T, D), F32), jax.ShapeDtypeStruct((T, D), BF16), hshape, hshape, hshape],
                  scratch_shapes=[pltpu.VMEM((tm, D), F32)])(dy, h1, h3, w1, w3, w2, *deps)


def _rot_matrix():
    i = jnp.arange(QK)[:, None]
    j = jnp.arange(QK)[None, :]
    half = ROPE // 2
    first = (j >= NOPE) & (j < NOPE + half) & (i == j + half)
    second = (j >= NOPE + half) & (i == j - half)
    return jnp.where(first, -1.0, jnp.where(second, 1.0, 0.0)).astype(F32)


def _mla_fn(cq, ckv, kr128, cos, sin, rot, qa_g, kva_g, qn_g, kn_g, w_uq, w_ukv):
    cqn = _rms(cq, qa_g)
    ckvn = _rms(ckv, kva_g)
    kr = kr128[:, :ROPE]
    qs, ks, vs = [], [], []
    for h in range(HEADS):
        qh = _rms(_bdot(cqn, w_uq[h]), qn_g)
        qs.append(qh * cos + _permute(qh, rot) * sin)
        kvh = _bdot(ckvn, w_ukv[h])
        kh = _rms(jnp.concatenate([kvh[:, :NOPE], kr], axis=-1), kn_g)
        ks.append(kh * cos + _permute(kh, rot) * sin)
        vs.append(kvh[:, NOPE:])
    return qs, ks, vs


def _mla_specs(z, tabs, small, w_uq, w_ukv, tm, offs):
    o_cq, o_ckv, o_kr = offs
    row = lambda w: pl.BlockSpec((tm, w), lambda i: (i, 0))
    col = lambda off, w: pl.BlockSpec((tm, w), lambda i: (i, off // w))
    full2 = lambda a: pl.BlockSpec(a.shape, lambda i: (0, 0))
    wsp = lambda a: pl.BlockSpec(a.shape, lambda i: (0, 0, 0))
    cq_w, ckv_w = w_uq.shape[1], w_ukv.shape[1]
    ins = [z, z, z, tabs[0], tabs[1], tabs[2]] + list(small) + [w_uq, w_ukv]
    specs = ([col(o_cq, cq_w), col(o_ckv, ckv_w), col(o_kr, LANE), row(QK), row(QK), full2(tabs[2])]
             + [full2(s) for s in small] + [wsp(w_uq), wsp(w_ukv)])
    return ins, specs


def _mla_prep_fwd(name, z, tabs, small, w_uq, w_ukv, offs):
    T = z.shape[0]
    tm = _pick(T, (256, 128))
    ins, specs = _mla_specs(z, tabs, small, w_uq, w_ukv, tm, offs)
    head = lambda w: pl.BlockSpec((HEADS, tm, w), lambda i: (0, i, 0))

    def body(*refs):
        vals = [r[...] for r in refs[:12]]
        q_ref, k_ref, v_ref = refs[12:]
        qs, ks, vs = _mla_fn(*vals)
        for h in range(HEADS):
            q_ref[h] = qs[h].astype(BF16)
            k_ref[h] = ks[h].astype(BF16)
            v_ref[h] = vs[h].astype(BF16)

    return _pcall(body, name=name, grid=(T // tm,), in_specs=specs, out_specs=[head(QK), head(QK), head(VD)],
                  out_shape=[jax.ShapeDtypeStruct((HEADS, T, QK), BF16), jax.ShapeDtypeStruct((HEADS, T, QK), BF16),
                             jax.ShapeDtypeStruct((HEADS, T, VD), BF16)])(*ins)


def _mla_prep_bwd(name, z, tabs, small, w_uq, w_ukv, offs, dq, dk, dv):
    T = z.shape[0]
    tm = _pick(T, (256, 128))
    ins, specs = _mla_specs(z, tabs, small, w_uq, w_ukv, tm, offs)
    head = lambda w: pl.BlockSpec((HEADS, tm, w), lambda i: (0, i, 0))
    ins += [dq, dk, dv]
    specs += [head(QK), head(QK), head(VD)]
    cq_w, ckv_w = w_uq.shape[1], w_ukv.shape[1]
    acc_shapes = [s.shape for s in small] + [w_uq.shape, w_ukv.shape]
    row_shapes = [(T, cq_w), (T, ckv_w), (T, LANE)]
    out_shape = [jax.ShapeDtypeStruct(s, BF16) for s in row_shapes] + [jax.ShapeDtypeStruct(s, F32) for s in acc_shapes]
    out_specs = ([pl.BlockSpec((tm, s[1]), lambda i: (i, 0)) for s in row_shapes]
                 + [pl.BlockSpec(s, lambda i, _n=len(s): (0,) * _n) for s in acc_shapes])

    def body(*refs):
        cq, ckv, kr128, cos, sin, rot, qa_g, kva_g, qn_g, kn_g, w_uq_v, w_ukv_v = [r[...] for r in refs[:12]]
        dq_ref, dk_ref, dv_ref = refs[12:15]
        outs = refs[15:]
        f = lambda a, b, c, g1, g2, g3, g4, wq, wkv: _mla_fn(a, b, c, cos, sin, rot, g1, g2, g3, g4, wq, wkv)
        _, vjp = jax.vjp(f, cq, ckv, kr128, qa_g, kva_g, qn_g, kn_g, w_uq_v.astype(F32), w_ukv_v.astype(F32))
        cts = ([dq_ref[h] for h in range(HEADS)], [dk_ref[h] for h in range(HEADS)], [dv_ref[h] for h in range(HEADS)])
        grads = vjp(cts)
        i = pl.program_id(0)
        for n, (r, gval) in enumerate(zip(outs, grads)):
            if n < 3:
                r[...] = gval.astype(r.dtype)
            else:
                @pl.when(i == 0)
                def _(r=r):
                    r[...] = jnp.zeros_like(r)
                r[...] += gval

    return _pcall(body, name=name, grid=(T // tm,), in_specs=specs, out_specs=out_specs, out_shape=out_shape)(*ins)


NEG = -1e30


def _tri(t):
    return lax.broadcasted_iota(jnp.int32, (t, t), 1) <= lax.broadcasted_iota(jnp.int32, (t, t), 0)


def _attn_tiles(T):
    t = _pick(T, (1024, 512, 256, 128))
    return t, T // t


def _attn_fwd(name, q, k, v):
    H, T, _ = q.shape
    t, n = _attn_tiles(T)
    scale = QK ** -0.5

    def body(q_ref, k_ref, v_ref, o_ref, lse_ref, m_s, l_s, acc):
        qi, ki = pl.program_id(1), pl.program_id(2)

        @pl.when(ki == 0)
        def _():
            m_s[...] = jnp.full_like(m_s, NEG)
            l_s[...] = jnp.zeros_like(l_s)
            acc[...] = jnp.zeros_like(acc)

        def tile(diagonal):
            s = lax.dot_general(q_ref[...], k_ref[...], _DIMS["nt"], preferred_element_type=F32) * scale
            if diagonal:
                s = jnp.where(_tri(t), s, NEG)
            m_new = jnp.maximum(m_s[...], jnp.max(s, axis=-1, keepdims=True))
            alpha = jnp.exp(m_s[...] - m_new)
            p = jnp.exp(s - m_new)
            l_s[...] = alpha * l_s[...] + jnp.sum(p, axis=-1, keepdims=True)
            acc[...] = alpha * acc[...] + jnp.dot(p.astype(BF16), v_ref[...], preferred_element_type=F32)
            m_s[...] = m_new

        @pl.when(ki < qi)
        def _():
            tile(False)

        @pl.when(ki == qi)
        def _():
            tile(True)
            o_ref[...] = acc[...] / l_s[...]
            lse_ref[...] = m_s[...] + jnp.log(l_s[...])

    kv = lambda w: pl.BlockSpec((None, t, w), lambda h, qi, ki: (h, jnp.minimum(ki, qi), 0))
    return _pcall(body, name=name, grid=(H, n, n),
                  in_specs=[pl.BlockSpec((None, t, QK), lambda h, qi, ki: (h, qi, 0)), kv(QK), kv(VD)],
                  out_specs=[pl.BlockSpec((t, VD), lambda h, qi, ki: (qi, h)),
                             pl.BlockSpec((None, t, 1), lambda h, qi, ki: (h, qi, 0))],
                  out_shape=[jax.ShapeDtypeStruct((T, H * VD), F32), jax.ShapeDtypeStruct((H, T, 1), F32)],
                  scratch_shapes=[pltpu.VMEM((t, 1), F32), pltpu.VMEM((t, 1), F32), pltpu.VMEM((t, VD), F32)])(q, k, v)


def _attn_bwd(name, q, k, v, o, lse, do):
    H, T, _ = q.shape
    t, n = _attn_tiles(T)
    scale = QK ** -0.5

    def body(q_ref, k_ref, v_ref, o_ref, lse_ref, do_ref, dq_ref, dk_ref, dv_ref, dk_acc, dv_acc):
        ki, qi = pl.program_id(1), pl.program_id(2)

        @pl.when((ki == 0) & (qi == 0))
        def _():
            dq_ref[...] = jnp.zeros_like(dq_ref)

        @pl.when(qi == 0)
        def _():
            dk_acc[...] = jnp.zeros_like(dk_acc)
            dv_acc[...] = jnp.zeros_like(dv_acc)

        def tile(diagonal):
            qv, kv_, dov = q_ref[...], k_ref[...], do_ref[...]
            s = lax.dot_general(qv, kv_, _DIMS["nt"], preferred_element_type=F32) * scale
            p = jnp.exp(s - lse_ref[...])
            if diagonal:
                p = jnp.where(_tri(t), p, 0.0)
            dob = dov.astype(BF16)
            delta = jnp.sum(o_ref[...] * dov, axis=-1, keepdims=True)
            dv_acc[...] += lax.dot_general(p.astype(BF16), dob, _DIMS["tn"], preferred_element_type=F32)
            dp = lax.dot_general(dob, v_ref[...], _DIMS["nt"], preferred_element_type=F32)
            ds = (p * (dp - delta) * scale).astype(BF16)
            rs = pl.ds(pl.multiple_of(qi * t, t), t)
            dq_ref[rs, :] += jnp.dot(ds, kv_, preferred_element_type=F32)
            dk_acc[...] += lax.dot_general(ds, qv, _DIMS["tn"], preferred_element_type=F32)

        @pl.when(qi > ki)
        def _():
            tile(False)

        @pl.when(qi == ki)
        def _():
            tile(True)

        @pl.when(qi == n - 1)
        def _():
            dk_ref[...] = dk_acc[...]
            dv_ref[...] = dv_acc[...]

    qrow = lambda w: pl.BlockSpec((None, t, w), lambda h, ki, qi: (h, jnp.maximum(qi, ki), 0))
    krow = lambda w: pl.BlockSpec((None, t, w), lambda h, ki, qi: (h, ki, 0))
    wide = pl.BlockSpec((t, VD), lambda h, ki, qi: (jnp.maximum(qi, ki), h))
    return _pcall(body, name=name, grid=(H, n, n),
                  in_specs=[qrow(QK), krow(QK), krow(VD), wide, qrow(1), wide],
                  out_specs=[pl.BlockSpec((None, T, QK), lambda h, ki, qi: (h, 0, 0)), krow(QK), krow(VD)],
                  out_shape=[jax.ShapeDtypeStruct((H, T, QK), F32), jax.ShapeDtypeStruct((H, T, QK), F32),
                             jax.ShapeDtypeStruct((H, T, VD), F32)],
                  scratch_shapes=[pltpu.VMEM((t, QK), F32), pltpu.VMEM((t, VD), F32)])(q, k, v, o, lse, do)


def _tril():
    return lax.broadcasted_iota(jnp.int32, (CHUNK, CHUNK), 1) <= lax.broadcasted_iota(jnp.int32, (CHUNK, CHUNK), 0)


@jax.custom_vjp
def _gm_gate(v, ws, b_t):
    wc = jnp.where(_tril()[None], ws, 0.0).astype(BF16)
    vb = v.astype(BF16)
    rows = []
    for c in range(v.shape[0] // CHUNK):
        cols = []
        for g in range(GROUPS):
            vc = vb[c * CHUNK:(c + 1) * CHUNK, g * LANE:(g + 1) * LANE]
            cols.append(jnp.dot(wc[g], vc, preferred_element_type=F32) + jnp.broadcast_to(b_t[:, g:g + 1], (CHUNK, LANE)))
        rows.append(jnp.concatenate(cols, axis=-1))
    return jnp.concatenate(rows, axis=0)


def _gm_gate_fwd(v, ws, b_t):
    return _gm_gate(v, ws, b_t), (v, ws)


def _gm_gate_bwd(res, dgate):
    v, ws = res
    tril = _tril()
    wc = jnp.where(tril[None], ws, 0.0).astype(BF16)
    vb = v.astype(BF16)
    dgb = dgate.astype(BF16)
    dws = [jnp.zeros((CHUNK, CHUNK), F32) for _ in range(GROUPS)]
    db = jnp.zeros((CHUNK, GROUPS), F32)
    lane_g = lax.broadcasted_iota(jnp.int32, (1, GROUPS), 1)
    rows = []
    for c in range(v.shape[0] // CHUNK):
        cols = []
        for g in range(GROUPS):
            sl = (slice(c * CHUNK, (c + 1) * CHUNK), slice(g * LANE, (g + 1) * LANE))
            cols.append(lax.dot_general(wc[g], dgb[sl], _DIMS["tn"], preferred_element_type=F32))
            dws[g] = dws[g] + lax.dot_general(dgb[sl], vb[sl], _DIMS["nt"], preferred_element_type=F32)
            db = db + jnp.sum(dgate[sl], axis=1, keepdims=True) * (lane_g == g).astype(F32)
        rows.append(jnp.concatenate(cols, axis=-1))
    dws = jnp.stack([jnp.where(tril, d, 0.0) for d in dws])
    return jnp.concatenate(rows, axis=0), dws, db


_gm_gate.defvjp(_gm_gate_fwd, _gm_gate_bwd)


def _mix_fn(a_out, zu, zv, aon_g, gon_g, vn_g, ws, b_t):
    u = jax.nn.gelu(zu)
    vv = _rms(jax.nn.gelu(zv), vn_g)
    g_out = u * _gm_gate(vv, ws, b_t)
    return jnp.concatenate([_rms(a_out, aon_g), _rms(g_out, gon_g)], axis=-1)


def _mix_ins(a_out, z, small, offs):
    gw = a_out.shape[1]
    return [(a_out, "row"), (z, ("cols", offs[0], gw)), (z, ("cols", offs[1], gw))] + [(s, "full") for s in small]


def _mix_fwd(name, a_out, z, small, offs):
    T, gw = a_out.shape
    return _rowwise(name, lambda *a: (_mix_fn(*a),), _mix_ins(a_out, z, small, offs), [((T, 2 * gw), BF16, "row")],
                    T, _pick(T, (256, 128)))[0]


def _mix_bwd(name, a_out, z, small, offs, dmixed):
    T, gw = a_out.shape

    def fn(*a):
        _, vjp = jax.vjp(_mix_fn, *a[:-1])
        return vjp(a[-1].astype(F32))

    outs = [((T, gw), F32, "row"), ((T, gw), BF16, "row"), ((T, gw), BF16, "row")] + [(s.shape, F32, "acc") for s in small]
    return _rowwise(name, fn, _mix_ins(a_out, z, small, offs) + [(dmixed, "row")], outs, T, _pick(T, (256, 128)))


def _ple_fn(gl, pe, g):
    return jax.nn.sigmoid(gl) * _rms(pe, g)


def _ple_fwd(name, h, gl, pe, g):
    T, D = h.shape
    return _rowwise(name, lambda hv, a, b, c: (hv + _ple_fn(a, b, c),),
                    [(h, "row"), (gl, "row"), (pe, "row"), (g, "full")], [((T, D), F32, "row")], T, _pick(T, (256, 128)))[0]


def _ple_bwd(name, gl, pe, g, dh, deps=()):
    T, D = gl.shape

    def fn(a, b, c, d):
        _, vjp = jax.vjp(_ple_fn, a, b, c)
        return vjp(d)

    return _rowwise(name, fn, [(gl, "row"), (pe, "row"), (g, "full"), (dh, "row")],
                    [((T, D), BF16, "row"), ((T, D), BF16, "row"), ((1, D), F32, "acc")], T, _pick(T, (256, 128)), deps)


def _loss(name, y, target):
    T, D = y.shape

    def fn(yv, tv):
        err = yv - tv
        part = 0.5 * jnp.sum(jnp.mean(err * err, axis=-1, keepdims=True), axis=0, keepdims=True)
        return err * (1.0 / D), jnp.broadcast_to(part, (8, LANE))

    return _rowwise(name, fn, [(y, "row"), (target, "row")], [((T, D), F32, "row"), ((8, LANE), F32, "acc")], T,
                    _pick(T, (512, 256, 128)))


ADAMW_BLOCK_ELEMS = 128 * 1024


def _adamw_sum(name, parts, w, m, v):
    L, R, C = w.shape
    tiles = [(r, c) for r in (R, 512, 256, 128, 64, 32, 16) for c in (C, 1024, 512, 256, 128)
             if R % r == 0 and C % c == 0 and r * c <= ADAMW_BLOCK_ELEMS]
    tr, tc = max(tiles, key=lambda rc: (rc[0] * rc[1], rc[1]))
    nr, nc = R // tr, C // tc
    c1 = 1.0 - ADAM_B1 ** ADAM_STEP
    c2 = 1.0 - ADAM_B2 ** ADAM_STEP

    def body(*refs):
        p_refs = refs[:L]
        w_ref, m_ref, v_ref, g_out, d_out, m_out, v_out = refs[L:]
        layer = pl.program_id(0)

        def part(s):
            val = p_refs[0][s].astype(F32)
            for j in range(1, L):
                val = jnp.where(layer == j, p_refs[j][s].astype(F32), val)
            return val

        g = part(0)
        for s in range(1, N_DEV):
            g = g + part(s)
        m2 = ADAM_B1 * m_ref[...] + (1.0 - ADAM_B1) * g
        v2 = ADAM_B2 * v_ref[...] + (1.0 - ADAM_B2) * (g * g)
        g_out[...] = g
        m_out[...] = m2
        v_out[...] = v2
        d_out[...] = -ADAM_LR * ((m2 / c1) / (jnp.sqrt(v2 / c2) + ADAM_EPS) + ADAM_WD * w_ref[...])

    def part_spec(j):
        def index(l, i, k):
            before, mine = l < j, l == j
            return (0, jnp.where(mine, i, jnp.where(before, 0, nr - 1)), jnp.where(mine, k, jnp.where(before, 0, nc - 1)))
        return pl.BlockSpec((N_DEV, tr, tc), index)

    blk = pl.BlockSpec((None, tr, tc), lambda l, i, k: (l, i, k))
    sd = jax.ShapeDtypeStruct((L, R, C), F32)
    return _pcall(body, name=name, grid=(L, nr, nc),
                  in_specs=[part_spec(j) for j in range(L)] + [blk, blk, blk],
                  out_specs=[blk, blk, blk, blk], out_shape=[sd, sd, sd, sd])(*parts, w, m, v)


def _unshard_cols(g):
    _, K, n = g.shape
    return g.transpose(1, 0, 2).reshape(K, N_DEV * n)


def _shard_cols(full):
    K, N = full.shape
    return full.reshape(K, N_DEV, N // N_DEV).transpose(1, 0, 2)


def _unshard_rows(g):
    _, k, N = g.shape
    return g.reshape(N_DEV * k, N)


def _shard_rows(full):
    K, N = full.shape
    return full.reshape(N_DEV, K // N_DEV, N)


STAGES = (('ffn_a_w1', 'ffn_a_w3', 'ffn_a_w2'), ('w_in', 'w_uq', 'w_ukv', 'w_out'),
          ('ffn_b_w1', 'ffn_b_w3', 'ffn_b_w2'), ('w_ple_gate', 'w_ple'))
TRANSPOSED = ('ffn_a_w1', 'ffn_a_w3', 'ffn_b_w1', 'ffn_b_w3', 'w_in', 'w_uq')


def _step(x, p, positions, target, w, m, v):
    T, D = x.shape[1], x.shape[2]
    L = p.shape[0]
    x2, target2 = x[0], target[0]
    q_rank, kv_rank = w['w_uq'].shape[1], w['w_ukv'].shape[1]
    gw = w['gm_v_norm'].shape[1]

    inv_freq = ROPE_BASE ** (-jnp.arange(0, ROPE, 2, dtype=F32) / ROPE)
    ang = positions[0].astype(F32)[:, None] * inv_freq
    cos = jnp.concatenate([jnp.ones((T, NOPE), F32), jnp.cos(ang), jnp.cos(ang)], axis=-1)
    sin = jnp.concatenate([jnp.zeros((T, NOPE), F32), jnp.sin(ang), jnp.sin(ang)], axis=-1)
    tabs = (cos, sin, _rot_matrix().astype(BF16))

    groups = [(l, names) for l in range(L) for names in STAGES]

    def ag_start(k, after):
        l, names = groups[k]
        shards = [w[n][l].astype(BF16) for n in names]
        return _xchg_start(f"ag_chips{k}", "chips", shards, [_landing(s) for s in shards], after)

    ag = {0: ag_start(0, [])}
    ag[1] = ag_start(1, [ag[0]['token']])

    def fetch(k, after):
        lands = _xchg_wait(f"ag_landed{k}", ag[k], after)
        fw = _xchg_start(f"ag_forward{k}", "forward", [], lands, [])
        deps = [fw['token']]
        for nxt in {0: (), 1: (2, 3)}.get(k, (k + 2,)):
            if nxt < len(groups):
                ag[nxt] = ag_start(nxt, deps)
                deps = [ag[nxt]['token']]
        return fw, deps

    def gathered(k, fw, after):
        return dict(zip(groups[k][1], _xchg_wait(f"ag_wait{k}", fw, after)))

    s0, s1, s2, s3 = q_rank, q_rank + kv_rank, q_rank + kv_rank + ROPE, q_rank + kv_rank + ROPE + gw
    o_u, o_v, o_cq, o_ckv, o_kr = 0, gw, 2 * gw, 2 * gw + q_rank, 2 * gw + q_rank + kv_rank
    kr_pad = 2 * LANE - ROPE

    def g2(name, l):
        return w[name][l][None, :]

    gm_bt = [w['gm_bs'][l].T for l in range(L)]

    saved = []
    h = x2
    for l in range(L):
        s = {}
        fw, deps = fetch(4 * l, [h])
        s['h0'] = h
        s['xn_a'] = _rms_fwd(f"rms_a{l}", h, g2('ffn_a_norm', l), deps)
        wa = s['wa'] = gathered(4 * l, fw, [s['xn_a']])
        h, s['a_h1'], s['a_h3'] = _ffn_fwd(f"ffn_a_fwd{l}", s['xn_a'], h, wa['ffn_a_w1'], wa['ffn_a_w3'], wa['ffn_a_w2'])
        fw, deps = fetch(4 * l + 1, [h])
        s['h1'] = h
        s['n'] = _rms_fwd(f"rms_mix{l}", h, g2('mix_norm', l), deps)
        wm = gathered(4 * l + 1, fw, [s['n']])
        w_in_full = _unshard_cols(wm['w_in'])
        s['w_in'] = jnp.concatenate([w_in_full[:, s2:s3], w_in_full[:, s3:], w_in_full[:, :s0], w_in_full[:, s0:s1],
                                     w_in_full[:, s1:s2], jnp.zeros((D, kr_pad), BF16)], axis=-1)
        s['w_out'] = _unshard_rows(wm['w_out'])
        s['w_uq'], s['w_ukv'] = wm['w_uq'], wm['w_ukv']
        s['z'] = _mm(f"w_in{l}", s['n'], s['w_in'], "nn")
        s['mla_small'] = [g2('q_a_norm', l), g2('kv_a_norm', l), g2('q_norm', l), g2('k_norm', l)]
        s['q'], s['k'], s['v'] = _mla_prep_fwd(f"mla_prep{l}", s['z'], tabs, s['mla_small'], s['w_uq'], s['w_ukv'],
                                               (o_cq, o_ckv, o_kr))
        s['a_out'], s['lse'] = _attn_fwd(f"attn_fwd{l}", s['q'], s['k'], s['v'])
        s['mix_small'] = [g2('attn_out_norm', l), g2('gm_out_norm', l), g2('gm_v_norm', l), w['gm_ws'][l], gm_bt[l]]
        s['mixed'] = _mix_fwd(f"mix_fwd{l}", s['a_out'], s['z'], s['mix_small'], (o_u, o_v))
        h = _mm(f"w_out{l}", s['mixed'], s['w_out'], "nn", res=h)
        fw, deps = fetch(4 * l + 2, [h])
        s['h2'] = h
        s['xn_b'] = _rms_fwd(f"rms_b{l}", h, g2('ffn_b_norm', l), deps)
        wb = s['wb'] = gathered(4 * l + 2, fw, [s['xn_b']])
        h, s['b_h1'], s['b_h3'] = _ffn_fwd(f"ffn_b_fwd{l}", s['xn_b'], h, wb['ffn_b_w1'], wb['ffn_b_w3'], wb['ffn_b_w2'])
        fw, deps = fetch(4 * l + 3, [h])
        s['h3'] = h
        s['xn_g'] = _rms_fwd(f"rms_g{l}", h, g2('ple_gate_norm', l), deps)
        wp = gathered(4 * l + 3, fw, [s['xn_g']])
        s['w_gate'] = _unshard_rows(wp['w_ple_gate'])
        s['gl'] = _mm(f"w_gate{l}", s['xn_g'], s['w_gate'], "nn")
        s['p'] = p[l, 0]
        s['pe'] = _mm(f"w_ple{l}", s['p'], _unshard_cols(wp['w_ple']), "nn")
        h = _ple_fwd(f"ple_fwd{l}", h, s['gl'], s['pe'], g2('ple_norm', l))
        saved.append(s)

    dh, loss_part = _loss("loss", h, target2)
    loss = lax.psum(loss_part[0, 0], AXES)

    gsmall = {n: [None] * L for n in SMALL}
    rs, where = {}, {}

    def rs_start(key, l, named):
        grads = [g for _, g in named]
        lands = [lax.dynamic_update_slice(lax.empty(g.shape, g.dtype),
                                          lax.dynamic_index_in_dim(g, _my_index(), 0, keepdims=True),
                                          (_my_index(),) + (0,) * (g.ndim - 1)) for g in grads]
        rs[key] = _xchg_start("rs_start_" + key, "scatter", grads, lands, [])
        where.update({(n, l): (key, i) for i, (n, _) in enumerate(named)})
        return [rs[key]['token']]

    def ffn_backward(tag, l, dh, xn, h_in, h1, h3, wts, norm_name, deps):
        pre = 'ffn_' + tag
        dxn, dhb, dh1, dh3, act = _ffn_bwd(f"{pre}_bwd{l}", dh, h1, h3, wts[pre + '_w1'], wts[pre + '_w3'],
                                           wts[pre + '_w2'], deps)
        g1 = _mm_tn_batch(f"{pre}_dw1_{l}", dh1, xn[None])
        deps = rs_start(f"{pre}_w1_{l}", l, [(pre + '_w1', g1)])
        g3 = _mm_tn_batch(f"{pre}_dw3_{l}", dh3, xn[None], deps=deps)
        deps = rs_start(f"{pre}_w3_{l}", l, [(pre + '_w3', g3)])
        g2_ = _mm_tn_batch(f"{pre}_dw2_{l}", act, dhb[None], alpha=0.5, deps=deps)
        deps = rs_start(f"{pre}_w2_{l}", l, [(pre + '_w2', g2_)])
        return _rms_bwd(f"rms_{tag}_bwd{l}", h_in, g2(norm_name, l), dxn, dh, deps)

    deps = []
    for l in reversed(range(L)):
        s = saved[l]
        d_gl, d_pe, gsmall['ple_norm'][l] = _ple_bwd(f"ple_bwd{l}", s['gl'], s['pe'], g2('ple_norm', l), dh, deps)
        g_ple = _mm(f"dw_ple{l}", s['p'], d_pe, "tn", out_dtype=BF16)
        g_gate = _mm(f"dw_gate{l}", s['xn_g'], d_gl, "tn", out_dtype=BF16)
        d_xng = _mm(f"d_xng{l}", d_gl, s['w_gate'], "nt")
        dh, gsmall['ple_gate_norm'][l] = _rms_bwd(f"rms_g_bwd{l}", s['h3'], g2('ple_gate_norm', l), d_xng, dh)
        deps = rs_start(f"ple_{l}", l, [('w_ple_gate', _shard_rows(g_gate)), ('w_ple', _shard_cols(g_ple))])
        dh, gsmall['ffn_b_norm'][l] = ffn_backward('b', l, dh, s['xn_b'], s['h2'], s['b_h1'], s['b_h3'], s['wb'],
                                                   'ffn_b_norm', deps)
        g_out = _mm(f"dw_out{l}", s['mixed'], dh, "tn", out_dtype=BF16)
        d_mixed = _mm(f"d_mixed{l}", dh, s['w_out'], "nt")
        mix = _mix_bwd(f"mix_bwd{l}", s['a_out'], s['z'], s['mix_small'], (o_u, o_v), d_mixed)
        d_a_out, d_u, d_v = mix[:3]
        gsmall['attn_out_norm'][l], gsmall['gm_out_norm'][l], gsmall['gm_v_norm'][l], gsmall['gm_ws'][l] = mix[3:7]
        gsmall['gm_bs'][l] = mix[7].T
        dq, dk, dv = _attn_bwd(f"attn_bwd{l}", s['q'], s['k'], s['v'], s['a_out'], s['lse'], d_a_out)
        mla = _mla_prep_bwd(f"mla_prep_bwd{l}", s['z'], tabs, s['mla_small'], s['w_uq'], s['w_ukv'], (o_cq, o_ckv, o_kr),
                            dq, dk, dv)
        d_cq, d_ckv, d_kr = mla[:3]
        gsmall['q_a_norm'][l], gsmall['kv_a_norm'][l], gsmall['q_norm'][l], gsmall['k_norm'][l] = mla[3:7]
        dz = jnp.concatenate([d_u, d_v, d_cq, d_ckv, d_kr, jnp.zeros((T, LANE), BF16)], axis=-1)
        g_in = _mm(f"dw_in{l}", dz, s['n'], "tn", out_dtype=BF16)
        g_in = jnp.concatenate([g_in[o_cq:o_cq + q_rank], g_in[o_ckv:o_ckv + kv_rank], g_in[o_kr:o_kr + ROPE],
                                g_in[o_u:o_u + gw], g_in[o_v:o_v + gw]], axis=0)
        d_n = _mm(f"d_n{l}", dz, s['w_in'], "nt")
        dh, gsmall['mix_norm'][l] = _rms_bwd(f"rms_mix_bwd{l}", s['h1'], g2('mix_norm', l), d_n, dh)
        deps = rs_start(f"mix_{l}", l, [('w_in', _shard_rows(g_in)), ('w_uq', mla[7].transpose(0, 2, 1).astype(BF16)),
                                        ('w_ukv', mla[8].astype(BF16)), ('w_out', _shard_rows(g_out))])
        dh, gsmall['ffn_a_norm'][l] = ffn_backward('a', l, dh, s['xn_a'], s['h0'], s['a_h1'], s['a_h3'], s['wa'],
                                                   'ffn_a_norm', deps)
        deps = []
    grad_x = dh[None]

    out = {}
    sizes = [w[n].size for n in SMALL]
    total = sum(sizes)
    padded = -(-total // (512 * LANE)) * (512 * LANE)

    def pack(d):
        flat = jnp.concatenate([d[n].reshape(-1) for n in SMALL] + [jnp.zeros((padded - total,), F32)])
        return flat.reshape(1, padded // LANE, LANE)

    gs = pack({n: jnp.stack([gsmall[n][l].reshape(w[n].shape[1:]) for l in range(L)]) for n in SMALL})
    small = _xchg_start("small_start", "gather", [gs[0]], [_landing(gs[0])], [])

    after = [dh, small['token']]
    landed = {}

    def partials(n, l):
        key, i = where[(n, l)]
        if key not in landed:
            landed[key] = _xchg_wait("rs_wait_" + key, rs[key], after)
        return landed[key][i]

    swap = lambda a: a.transpose(0, 2, 1)
    for stage in (3, 2, 1, 0):
        for n in STAGES[stage]:
            parts = [partials(n, l) for l in reversed(range(L))][::-1]
            if n in TRANSPOSED:
                out[n] = [swap(r) for r in _adamw_sum("adamw_" + n, parts, swap(w[n]), swap(m[n]), swap(v[n]))]
            else:
                out[n] = _adamw_sum("adamw_" + n, parts, w[n], m[n], v[n])
            after = [out[n][0]]

    res = _adamw_sum("adamw_small", _xchg_wait("small_wait", small, after), pack(w), pack(m), pack(v))
    off = 0
    for n, sz in zip(SMALL, sizes):
        out[n] = [r.reshape(-1)[off:off + sz].reshape(w[n].shape) for r in res]
        off += sz

    return (loss, grad_x, *[out[n][0] for n in WEIGHTS], *[out[n][1] for n in WEIGHTS],
            *[out[n][2] for n in WEIGHTS], *[out[n][3] for n in WEIGHTS])


def kernel(x, p, positions, ffn_a_norm, ffn_a_w1, ffn_a_w3, ffn_a_w2, mix_norm, w_in, q_a_norm, w_uq, kv_a_norm, w_ukv, q_norm, k_norm, gm_v_norm, gm_ws, gm_bs, attn_out_norm, gm_out_norm, w_out, ffn_b_norm, ffn_b_w1, ffn_b_w3, ffn_b_w2, ple_gate_norm, w_ple_gate, w_ple, ple_norm, loss_target, m_ffn_a_norm, m_ffn_a_w1, m_ffn_a_w3, m_ffn_a_w2, m_mix_norm, m_w_in, m_q_a_norm, m_w_uq, m_kv_a_norm, m_w_ukv, m_q_norm, m_k_norm, m_gm_v_norm, m_gm_ws, m_gm_bs, m_attn_out_norm, m_gm_out_norm, m_w_out, m_ffn_b_norm, m_ffn_b_w1, m_ffn_b_w3, m_ffn_b_w2, m_ple_gate_norm, m_w_ple_gate, m_w_ple, m_ple_norm, v_ffn_a_norm, v_ffn_a_w1, v_ffn_a_w3, v_ffn_a_w2, v_mix_norm, v_w_in, v_q_a_norm, v_w_uq, v_kv_a_norm, v_w_ukv, v_q_norm, v_k_norm, v_gm_v_norm, v_gm_ws, v_gm_bs, v_attn_out_norm, v_gm_out_norm, v_w_out, v_ffn_b_norm, v_ffn_b_w1, v_ffn_b_w3, v_ffn_b_w2, v_ple_gate_norm, v_w_ple_gate, v_w_ple, v_ple_norm):
    args = locals()
    w = {n: args[n] for n in WEIGHTS}
    m = {n: args["m_" + n] for n in WEIGHTS}
    v = {n: args["v_" + n] for n in WEIGHTS}
    return _step(x, p, positions, loss_target, w, m, v)
```

```python
import functools

import jax
import jax.numpy as jnp
from jax import lax
from jax.experimental import pallas as pl
from jax.experimental.pallas import tpu as pltpu

F32, BF16 = jnp.float32, jnp.bfloat16
EPS = 1e-6
N_DEV = 8
HEADS = 8
NOPE, ROPE, QK, VD = 128, 64, 192, 128
CHUNK = 128
GROUPS = 8
LANE = 128
ROPE_BASE = 10000.0
ADAM_LR, ADAM_B1, ADAM_B2, ADAM_EPS, ADAM_WD, ADAM_STEP = 0.001, 0.9, 0.999, 1e-08, 0.01, 10
AXES = ("x", "y", "c")
MESH = pl.DeviceIdType.MESH
ANY = pl.BlockSpec(memory_space=pl.ANY)

WEIGHTS = ['ffn_a_norm', 'ffn_a_w1', 'ffn_a_w3', 'ffn_a_w2', 'mix_norm', 'w_in', 'q_a_norm', 'w_uq', 'kv_a_norm',
           'w_ukv', 'q_norm', 'k_norm', 'gm_v_norm', 'gm_ws', 'gm_bs', 'attn_out_norm', 'gm_out_norm', 'w_out',
           'ffn_b_norm', 'ffn_b_w1', 'ffn_b_w3', 'ffn_b_w2', 'ple_gate_norm', 'w_ple_gate', 'w_ple', 'ple_norm']
BIG = ['ffn_a_w1', 'ffn_a_w3', 'ffn_a_w2', 'w_in', 'w_uq', 'w_ukv', 'w_out', 'ffn_b_w1', 'ffn_b_w3', 'ffn_b_w2',
       'w_ple_gate', 'w_ple']
SMALL = [n for n in WEIGHTS if n not in BIG]


def _pcall(body, **kw):
    return pl.pallas_call(body, **kw)


def _pick(n, cands):
    for c in cands:
        if n % c == 0:
            return c
    return n


def _rms(x, g):
    return x * lax.rsqrt(jnp.mean(x * x, axis=-1, keepdims=True) + EPS) * g


@jax.custom_vjp
def _bdot(x, w):
    return jnp.dot(x.astype(BF16), w.astype(BF16), preferred_element_type=F32)


def _bdot_fwd(x, w):
    return _bdot(x, w), (x, w)


def _bdot_bwd(res, dy):
    x, w = res
    dyb = dy.astype(BF16)
    dx = lax.dot_general(dyb, w.astype(BF16), (((1,), (1,)), ((), ())), preferred_element_type=F32)
    dw = lax.dot_general(x.astype(BF16), dyb, (((0,), (0,)), ((), ())), preferred_element_type=F32)
    return dx.astype(x.dtype), dw.astype(w.dtype)


_bdot.defvjp(_bdot_fwd, _bdot_bwd)


def _split_dot(x, p, dims):
    hi = x.astype(BF16)
    lo = (x - hi.astype(F32)).astype(BF16)
    return (lax.dot_general(hi, p, dims, preferred_element_type=F32)
            + lax.dot_general(lo, p, dims, preferred_element_type=F32))


@jax.custom_vjp
def _permute(x, p):
    return _split_dot(x, p, _DIMS["nn"])


def _permute_fwd(x, p):
    return _permute(x, p), p


def _permute_bwd(p, ct):
    return _split_dot(ct, p, _DIMS["nt"]), jnp.zeros_like(p)


_permute.defvjp(_permute_fwd, _permute_bwd)


def _flip(v, bit):
    return 1 - v if bit else v


HBM = pl.BlockSpec(memory_space=pltpu.HBM)
SEM = pl.BlockSpec(memory_space=pltpu.SEMAPHORE)
EFFECT = pltpu.SideEffectType.DATAFLOW_SIDE_EFFECTING
PEERS = N_DEV - 1


def _my_index():
    return 4 * lax.axis_index("x") + 2 * lax.axis_index("y") + lax.axis_index("c")


def _landing(own):
    zone = lax.empty((N_DEV,) + own.shape, own.dtype)
    return lax.dynamic_update_slice(zone, own[None], (_my_index(),) + (0,) * own.ndim)


COPIES = {"gather": PEERS, "scatter": PEERS, "chips": 4, "forward": 3}


def _copy_plan(kind, src_refs, land_refs, send_sems, recv_sems):
    cx, cy, cc = lax.axis_index("x"), lax.axis_index("y"), lax.axis_index("c")
    me = 4 * cx + 2 * cy + cc
    per = COPIES[kind]
    out = []
    for t, land in enumerate(land_refs):
        def pair(i, src, to_slot, from_slot, dev):
            kw = dict(send_sem=send_sems.at[per * t + i], recv_sem=recv_sems.at[per * t + i], device_id=dev,
                      device_id_type=MESH)
            out.append((pltpu.make_async_remote_copy(src_ref=src, dst_ref=land.at[to_slot], **kw),
                        pltpu.make_async_remote_copy(src_ref=src, dst_ref=land.at[from_slot], **kw)))

        if kind in ("gather", "scatter"):
            for k in range(1, N_DEV):
                px, py, pc = _flip(cx, k & 4), _flip(cy, k & 2), _flip(cc, k & 1)
                peer = 4 * px + 2 * py + pc
                pair(k - 1, src_refs[t].at[peer] if kind == "scatter" else src_refs[t], me, peer, (px, py, pc))
        elif kind == "chips":
            pair(0, src_refs[t], me, me + 1 - 2 * cc, (cx, cy, 1 - cc))
            for j in range(1, 4):
                px, py = _flip(cx, j & 2), _flip(cy, j & 1)
                pair(j, src_refs[t], me, 4 * px + 2 * py + cc, (px, py, cc))
        else:
            for j in range(1, 4):
                px, py = _flip(cx, j & 2), _flip(cy, j & 1)
                mine, theirs = 4 * px + 2 * py + cc, 4 * px + 2 * py + 1 - cc
                pair(j - 1, land.at[mine], mine, theirs, (cx, cy, 1 - cc))
    return out


def _xchg_start(name, kind, srcs, lands, after):
    ns, nb, na = len(srcs), len(srcs) + len(lands), len(after)
    n_sems = COPIES[kind] * len(lands)

    def body(*refs):
        send_sems, recv_sems = refs[nb + na], refs[nb + na + 1]
        for send, _ in _copy_plan(kind, refs[:ns], refs[ns:nb], send_sems, recv_sems):
            send.start()
        refs[-1][...] = jnp.zeros_like(refs[-1])

    bufs = list(srcs) + list(lands)
    res = _pcall(
        body, name=name,
        out_shape=(pltpu.SemaphoreType.DMA((n_sems,)), pltpu.SemaphoreType.DMA((n_sems,)),
                   *[pltpu.HBM(a.shape, a.dtype) for a in bufs], jax.ShapeDtypeStruct((8, LANE), F32)),
        in_specs=[HBM] * nb + [ANY] * na,
        out_specs=(SEM, SEM, *([HBM] * nb), pl.BlockSpec(memory_space=pltpu.VMEM)),
        input_output_aliases={i: 2 + i for i in range(nb)},
        compiler_params=pltpu.CompilerParams(has_side_effects=EFFECT),
    )(*[pltpu.with_memory_space_constraint(a, pltpu.HBM) for a in bufs], *after)
    return dict(kind=kind, send=res[0], recv=res[1], srcs=list(res[2:2 + ns]), lands=list(res[2 + ns:2 + nb]),
                token=res[-1])


def _xchg_wait(name, st, after):
    ns, nb = len(st['srcs']), len(st['srcs']) + len(st['lands'])

    def body(*refs):
        for _, back in _copy_plan(st['kind'], refs[:ns], refs[ns:nb], refs[nb], refs[nb + 1]):
            back.wait_send()
            back.wait_recv()

    bufs = st['srcs'] + st['lands']
    res = _pcall(
        body, name=name, out_shape=tuple(pltpu.HBM(a.shape, a.dtype) for a in bufs),
        in_specs=[HBM] * nb + [SEM, SEM] + [ANY] * len(after), out_specs=tuple([HBM] * nb),
        input_output_aliases={i: i for i in range(nb)},
        compiler_params=pltpu.CompilerParams(has_side_effects=EFFECT),
    )(*bufs, st['send'], st['recv'], *after)
    return list(res[ns:])


_DIMS = {"nn": (((1,), (0,)), ((), ())), "nt": (((1,), (1,)), ((), ())), "tn": (((0,), (0,)), ((), ()))}


MM_WHOLE_K = 2048


def _mm(name, a, b, mode, out_dtype=F32, res=None, alpha=1.0, deps=()):
    if mode == "tn":
        K, M = a.shape
        N = b.shape[1]
    else:
        M, K = a.shape
        N = b.shape[0] if mode == "nt" else b.shape[1]
    tn = _pick(N, (1024, 512, 256))
    tm = _pick(M, (1024, 512, 256, 128) if tn <= 1024 else (512, 256, 128))
    tk = K if K <= MM_WHOLE_K else _pick(K, (1024, 512, 256, 128))
    nk = K // tk
    a_spec = pl.BlockSpec((tk, tm), lambda i, j, k: (k, i)) if mode == "tn" else pl.BlockSpec((tm, tk), lambda i, j, k: (i, k))
    b_spec = pl.BlockSpec((tn, tk), lambda i, j, k: (j, k)) if mode == "nt" else pl.BlockSpec((tk, tn), lambda i, j, k: (k, j))
    o_spec = pl.BlockSpec((tm, tn), lambda i, j, k: (i, j))
    dims = _DIMS[mode]

    def body(*refs):
        a_ref, b_ref, r_ref = refs[0], refs[1], refs[2]
        part = lax.dot_general(a_ref[...].astype(BF16), b_ref[...].astype(BF16), dims, preferred_element_type=F32)

        def finish(o_ref, r):
            r = r * alpha if alpha != 1.0 else r
            if res is not None:
                r = r_ref[...] + r
            o_ref[...] = r.astype(o_ref.dtype)

        if nk == 1:
            finish(refs[-1], part)
            return
        o_ref, acc = refs[-2], refs[-1]
        k = pl.program_id(2)

        @pl.when(k == 0)
        def _():
            acc[...] = part

        @pl.when(k > 0)
        def _():
            acc[...] += part

        @pl.when(k == nk - 1)
        def _():
            finish(o_ref, acc[...])

    ins = [a, b] + ([] if res is None else [res]) + list(deps)
    specs = [a_spec, b_spec] + ([] if res is None else [o_spec]) + [ANY] * len(deps)
    return _pcall(body, name=name, grid=(M // tm, N // tn, nk), in_specs=specs, out_specs=o_spec,
                  out_shape=jax.ShapeDtypeStruct((M, N), out_dtype),
                  scratch_shapes=[] if nk == 1 else [pltpu.VMEM((tm, tn), F32)])(*ins)


def _mm_tn_batch(name, a3, b3, alpha=1.0, deps=()):
    ga, T, M = a3.shape
    gb, _, N = b3.shape
    G = max(ga, gb)
    tm = _pick(M, (1024, 512)) if N <= 1024 else M
    tk = _pick(T, (1024, 512, 256, 128))
    nk = T // tk
    a_spec = pl.BlockSpec((None, tk, tm), (lambda g, i, k: (g, k, i)) if ga > 1 else (lambda g, i, k: (0, k, i)))
    b_spec = pl.BlockSpec((None, tk, N), (lambda g, i, k: (g, k, 0)) if gb > 1 else (lambda g, i, k: (0, k, 0)))
    o_spec = pl.BlockSpec((None, tm, N), lambda g, i, k: (g, i, 0))

    def body(*refs):
        a_ref, b_ref, o_ref, acc = refs[0], refs[1], refs[-2], refs[-1]
        k = pl.program_id(2)
        part = lax.dot_general(a_ref[...].astype(BF16), b_ref[...].astype(BF16), _DIMS["tn"], preferred_element_type=F32)

        @pl.when(k == 0)
        def _():
            acc[...] = part

        @pl.when(k > 0)
        def _():
            acc[...] += part

        @pl.when(k == nk - 1)
        def _():
            o_ref[...] = (acc[...] * alpha if alpha != 1.0 else acc[...]).astype(o_ref.dtype)

    return _pcall(body, name=name, grid=(G, M // tm, nk), in_specs=[a_spec, b_spec] + [ANY] * len(deps),
                  out_specs=o_spec, out_shape=jax.ShapeDtypeStruct((G, M, N), BF16),
                  scratch_shapes=[pltpu.VMEM((tm, N), F32)])(a3, b3, *deps)


def _rowwise(name, fn, ins, outs, T, tm, deps=()):
    in_specs = []
    for arr, spec in ins:
        if spec == "row":
            in_specs.append(pl.BlockSpec((tm, arr.shape[1]), lambda i: (i, 0)))
        elif spec == "full":
            in_specs.append(pl.BlockSpec(arr.shape, lambda i, _n=arr.ndim: (0,) * _n))
        else:
            _, off, width = spec
            in_specs.append(pl.BlockSpec((tm, width), lambda i, _b=off // width: (i, _b)))
    in_specs += [ANY] * len(deps)
    out_specs, out_shapes = [], []
    for shape, dtype, spec in outs:
        out_shapes.append(jax.ShapeDtypeStruct(shape, dtype))
        if spec == "row":
            out_specs.append(pl.BlockSpec((tm, shape[1]), lambda i: (i, 0)))
        else:
            out_specs.append(pl.BlockSpec(shape, lambda i, _n=len(shape): (0,) * _n))
    n_in = len(ins)

    def body(*refs):
        res = fn(*[r[...] for r in refs[:n_in]])
        i = pl.program_id(0)
        for r, (_, _, spec), val in zip(refs[n_in + len(deps):], outs, res):
            if spec == "acc":
                @pl.when(i == 0)
                def _(r=r):
                    r[...] = jnp.zeros_like(r)
                r[...] += val.astype(r.dtype)
            else:
                r[...] = val.astype(r.dtype)

    return _pcall(body, name=name, grid=(T // tm,), in_specs=in_specs, out_specs=out_specs, out_shape=out_shapes)(
        *[a for a, _ in ins], *deps)


def _rms_fwd(name, h, g, deps=()):
    T, D = h.shape
    return _rowwise(name, lambda hv, gv: (_rms(hv, gv),), [(h, "row"), (g, "full")], [((T, D), BF16, "row")], T,
                    _pick(T, (512, 256, 128)), deps)[0]


def _rms_bwd(name, h, g, dxn, dh_in, deps=()):
    T, D = h.shape

    def fn(hv, gv, dv, dh0):
        _, vjp = jax.vjp(_rms, hv, gv)
        dh, dg = vjp(dv.astype(F32))
        return dh0 + dh, dg

    return _rowwise(name, fn, [(h, "row"), (g, "full"), (dxn, "row"), (dh_in, "row")],
                    [((T, D), F32, "row"), ((1, D), F32, "acc")], T, _pick(T, (256, 128)), deps)


def _ffn_fwd(name, xn, h, w1, w3, w2):
    T, D = xn.shape
    F8 = w1.shape[-1]
    tm = _pick(T, (512, 256, 128))
    wspec = lambda r, c: pl.BlockSpec((None, r, c), lambda i, d: (d, 0, 0))
    row = pl.BlockSpec((tm, D), lambda i, d: (i, 0))
    hid = pl.BlockSpec((None, tm, F8), lambda i, d: (d, i, 0))

    def body(xn_ref, h_ref, w1_ref, w3_ref, w2_ref, out_ref, h1_ref, h3_ref, acc):
        d = pl.program_id(1)

        @pl.when(d == 0)
        def _():
            acc[...] = jnp.zeros_like(acc)

        x = xn_ref[...]
        h1 = jnp.dot(x, w1_ref[...], preferred_element_type=F32)
        h3 = jnp.dot(x, w3_ref[...], preferred_element_type=F32)
        h1_ref[...] = h1.astype(BF16)
        h3_ref[...] = h3.astype(BF16)
        act = (h1 * jax.nn.sigmoid(h1) * h3).astype(BF16)
        acc[...] += jnp.dot(act, w2_ref[...], preferred_element_type=F32)

        @pl.when(d == N_DEV - 1)
        def _():
            out_ref[...] = h_ref[...] + 0.5 * acc[...]

    return _pcall(body, name=name, grid=(T // tm, N_DEV),
                  in_specs=[row, row, wspec(D, F8), wspec(D, F8), wspec(F8, D)],
                  out_specs=[row, hid, hid],
                  out_shape=[jax.ShapeDtypeStruct((T, D), F32), jax.ShapeDtypeStruct((N_DEV, T, F8), BF16),
                             jax.ShapeDtypeStruct((N_DEV, T, F8), BF16)],
                  scratch_shapes=[pltpu.VMEM((tm, D), F32)])(xn, h, w1, w3, w2)


FFN_ROW_GROUPS = 2


def _ffn_bwd(name, dy, h1, h3, w1, w3, w2, deps=()):
    T, D = dy.shape
    F8 = w1.shape[-1]
    tm = _pick(T, (512, 256, 128))
    wspec = lambda r, c: pl.BlockSpec((None, r, c), lambda i, d: (d, 0, 0))
    row = pl.BlockSpec((tm, D), lambda i, d: (i, 0))
    hid = pl.BlockSpec((None, tm, F8), lambda i, d: (d, i, 0))

    def body(*refs):
        dy_ref, h1_ref, h3_ref, w1_ref, w3_ref, w2_ref = refs[:6]
        dxn_ref, dyb, dh1_ref, dh3_ref, act_ref, acc = refs[6 + len(deps):]
        d = pl.program_id(1)

        @pl.when(d == 0)
        def _():
            acc[...] = jnp.zeros_like(acc)
            dyb[...] = dy_ref[...].astype(BF16)

        for r in range(FFN_ROW_GROUPS):
            rows = pl.ds(r * (tm // FFN_ROW_GROUPS), tm // FFN_ROW_GROUPS)
            dact = 0.5 * lax.dot_general(dyb[rows, :], w2_ref[...], _DIMS["nt"], preferred_element_type=F32)
            h1 = h1_ref[rows, :].astype(F32)
            h3 = h3_ref[rows, :].astype(F32)
            sig = jax.nn.sigmoid(h1)
            silu = h1 * sig
            dh1 = (dact * h3 * (sig * (1.0 + h1 * (1.0 - sig)))).astype(BF16)
            dh3 = (dact * silu).astype(BF16)
            dh1_ref[rows, :] = dh1
            dh3_ref[rows, :] = dh3
            act_ref[rows, :] = (silu * h3).astype(BF16)
            acc[rows, :] += (lax.dot_general(dh1, w1_ref[...], _DIMS["nt"], preferred_element_type=F32)
                             + lax.dot_general(dh3, w3_ref[...], _DIMS["nt"], preferred_element_type=F32))

        @pl.when(d == N_DEV - 1)
        def _():
            dxn_ref[...] = acc[...]

    hshape = jax.ShapeDtypeStruct((N_DEV, T, F8), BF16)
    return _pcall(body, name=name, grid=(T // tm, N_DEV),
                  in_specs=[row, hid, hid, wspec(D, F8), wspec(D, F8), wspec(F8, D)] + [ANY] * len(deps),
                  out_specs=[row, row, hid, hid, hid],
                  out_shape=[jax.ShapeDtypeStruct((T, D), F32), jax.ShapeDtypeStruct((T, D), BF16), hshape, hshape, hshape],
                  scratch_shapes=[pltpu.VMEM((tm, D), F32)])(dy, h1, h3, w1, w3, w2, *deps)


def _rot_matrix():
    i = jnp.arange(QK)[:, None]
    j = jnp.arange(QK)[None, :]
    half = ROPE // 2
    first = (j >= NOPE) & (j < NOPE + half) & (i == j + half)
    second = (j >= NOPE + half) & (i == j - half)
    return jnp.where(first, -1.0, jnp.where(second, 1.0, 0.0)).astype(F32)


def _mla_fn(cq, ckv, kr128, cos, sin, rot, qa_g, kva_g, qn_g, kn_g, w_uq, w_ukv):
    cqn = _rms(cq, qa_g)
    ckvn = _rms(ckv, kva_g)
    kr = kr128[:, :ROPE]
    qs, ks, vs = [], [], []
    for h in range(HEADS):
        qh = _rms(_bdot(cqn, w_uq[h]), qn_g)
        qs.append(qh * cos + _permute(qh, rot) * sin)
        kvh = _bdot(ckvn, w_ukv[h])
        kh = _rms(jnp.concatenate([kvh[:, :NOPE], kr], axis=-1), kn_g)
        ks.append(kh * cos + _permute(kh, rot) * sin)
        vs.append(kvh[:, NOPE:])
    return qs, ks, vs


def _mla_specs(z, tabs, small, w_uq, w_ukv, tm, offs):
    o_cq, o_ckv, o_kr = offs
    row = lambda w: pl.BlockSpec((tm, w), lambda i: (i, 0))
    col = lambda off, w: pl.BlockSpec((tm, w), lambda i: (i, off // w))
    full2 = lambda a: pl.BlockSpec(a.shape, lambda i: (0, 0))
    wsp = lambda a: pl.BlockSpec(a.shape, lambda i: (0, 0, 0))
    cq_w, ckv_w = w_uq.shape[1], w_ukv.shape[1]
    ins = [z, z, z, tabs[0], tabs[1], tabs[2]] + list(small) + [w_uq, w_ukv]
    specs = ([col(o_cq, cq_w), col(o_ckv, ckv_w), col(o_kr, LANE), row(QK), row(QK), full2(tabs[2])]
             + [full2(s) for s in small] + [wsp(w_uq), wsp(w_ukv)])
    return ins, specs


def _mla_prep_fwd(name, z, tabs, small, w_uq, w_ukv, offs):
    T = z.shape[0]
    tm = _pick(T, (256, 128))
    ins, specs = _mla_specs(z, tabs, small, w_uq, w_ukv, tm, offs)
    head = lambda w: pl.BlockSpec((HEADS, tm, w), lambda i: (0, i, 0))

    def body(*refs):
        vals = [r[...] for r in refs[:12]]
        q_ref, k_ref, v_ref = refs[12:]
        qs, ks, vs = _mla_fn(*vals)
        for h in range(HEADS):
            q_ref[h] = qs[h].astype(BF16)
            k_ref[h] = ks[h].astype(BF16)
            v_ref[h] = vs[h].astype(BF16)

    return _pcall(body, name=name, grid=(T // tm,), in_specs=specs, out_specs=[head(QK), head(QK), head(VD)],
                  out_shape=[jax.ShapeDtypeStruct((HEADS, T, QK), BF16), jax.ShapeDtypeStruct((HEADS, T, QK), BF16),
                             jax.ShapeDtypeStruct((HEADS, T, VD), BF16)])(*ins)


def _mla_prep_bwd(name, z, tabs, small, w_uq, w_ukv, offs, dq, dk, dv):
    T = z.shape[0]
    tm = _pick(T, (256, 128))
    ins, specs = _mla_specs(z, tabs, small, w_uq, w_ukv, tm, offs)
    head = lambda w: pl.BlockSpec((HEADS, tm, w), lambda i: (0, i, 0))
    ins += [dq, dk, dv]
    specs += [head(QK), head(QK), head(VD)]
    cq_w, ckv_w = w_uq.shape[1], w_ukv.shape[1]
    acc_shapes = [s.shape for s in small] + [w_uq.shape, w_ukv.shape]
    row_shapes = [(T, cq_w), (T, ckv_w), (T, LANE)]
    out_shape = [jax.ShapeDtypeStruct(s, BF16) for s in row_shapes] + [jax.ShapeDtypeStruct(s, F32) for s in acc_shapes]
    out_specs = ([pl.BlockSpec((tm, s[1]), lambda i: (i, 0)) for s in row_shapes]
                 + [pl.BlockSpec(s, lambda i, _n=len(s): (0,) * _n) for s in acc_shapes])

    def body(*refs):
        cq, ckv, kr128, cos, sin, rot, qa_g, kva_g, qn_g, kn_g, w_uq_v, w_ukv_v = [r[...] for r in refs[:12]]
        dq_ref, dk_ref, dv_ref = refs[12:15]
        outs = refs[15:]
        f = lambda a, b, c, g1, g2, g3, g4, wq, wkv: _mla_fn(a, b, c, cos, sin, rot, g1, g2, g3, g4, wq, wkv)
        _, vjp = jax.vjp(f, cq, ckv, kr128, qa_g, kva_g, qn_g, kn_g, w_uq_v.astype(F32), w_ukv_v.astype(F32))
        cts = ([dq_ref[h] for h in range(HEADS)], [dk_ref[h] for h in range(HEADS)], [dv_ref[h] for h in range(HEADS)])
        grads = vjp(cts)
        i = pl.program_id(0)
        for n, (r, gval) in enumerate(zip(outs, grads)):
            if n < 3:
                r[...] = gval.astype(r.dtype)
            else:
                @pl.when(i == 0)
                def _(r=r):
                    r[...] = jnp.zeros_like(r)
                r[...] += gval

    return _pcall(body, name=name, grid=(T // tm,), in_specs=specs, out_specs=out_specs, out_shape=out_shape)(*ins)


NEG = -1e30


ATTN_ROW_GROUPS = 2


def _tri_block(rows, cols, row0):
    return lax.broadcasted_iota(jnp.int32, (rows, cols), 1) <= lax.broadcasted_iota(jnp.int32, (rows, cols), 0) + row0


def _attn_tiles(T):
    t = _pick(T, (1024, 512, 256, 128))
    return t, T // t


def _attn_fwd(name, q, k, v):
    H, T, _ = q.shape
    t, n = _attn_tiles(T)
    scale = QK ** -0.5

    def body(q_ref, k_ref, v_ref, o_ref, lse_ref, m_s, l_s, acc):
        qi, ki = pl.program_id(1), pl.program_id(2)

        @pl.when(ki == 0)
        def _():
            m_s[...] = jnp.full_like(m_s, NEG)
            l_s[...] = jnp.zeros_like(l_s)
            acc[...] = jnp.zeros_like(acc)

        def tile(diagonal):
            s = lax.dot_general(q_ref[...], k_ref[...], _DIMS["nt"], preferred_element_type=F32) * scale
            if diagonal:
                s = jnp.where(_tri_block(t, t, 0), s, NEG)
            m_new = jnp.maximum(m_s[...], jnp.max(s, axis=-1, keepdims=True))
            alpha = jnp.exp(m_s[...] - m_new)
            p = jnp.exp(s - m_new)
            l_s[...] = alpha * l_s[...] + jnp.sum(p, axis=-1, keepdims=True)
            acc[...] = alpha * acc[...] + jnp.dot(p.astype(BF16), v_ref[...], preferred_element_type=F32)
            m_s[...] = m_new

        @pl.when(ki < qi)
        def _():
            tile(False)

        @pl.when(ki == qi)
        def _():
            tile(True)
            o_ref[...] = acc[...] / l_s[...]
            lse_ref[...] = m_s[...] + jnp.log(l_s[...])

    kv = lambda w: pl.BlockSpec((None, t, w), lambda h, qi, ki: (h, jnp.minimum(ki, qi), 0))
    return _pcall(body, name=name, grid=(H, n, n),
                  in_specs=[pl.BlockSpec((None, t, QK), lambda h, qi, ki: (h, qi, 0)), kv(QK), kv(VD)],
                  out_specs=[pl.BlockSpec((t, VD), lambda h, qi, ki: (qi, h)),
                             pl.BlockSpec((None, t, 1), lambda h, qi, ki: (h, qi, 0))],
                  out_shape=[jax.ShapeDtypeStruct((T, H * VD), F32), jax.ShapeDtypeStruct((H, T, 1), F32)],
                  scratch_shapes=[pltpu.VMEM((t, 1), F32), pltpu.VMEM((t, 1), F32), pltpu.VMEM((t, VD), F32)])(q, k, v)


def _attn_bwd(name, q, k, v, o, lse, do):
    H, T, _ = q.shape
    t, n = _attn_tiles(T)
    scale = QK ** -0.5

    def body(q_ref, k_ref, v_ref, o_ref, lse_ref, do_ref, dq_ref, dk_ref, dv_ref, dk_acc, dv_acc):
        ki, qi = pl.program_id(1), pl.program_id(2)

        @pl.when((ki == 0) & (qi == 0))
        def _():
            dq_ref[...] = jnp.zeros_like(dq_ref)

        @pl.when(qi == 0)
        def _():
            dk_acc[...] = jnp.zeros_like(dk_acc)
            dv_acc[...] = jnp.zeros_like(dv_acc)

        def tile(diagonal):
            g = t // ATTN_ROW_GROUPS
            for r in range(ATTN_ROW_GROUPS):
                rows = pl.ds(r * g, g)
                cols = (r + 1) * g if diagonal else t
                qv, kv_, dov = q_ref[rows, :], k_ref[:cols, :], do_ref[rows, :]
                s = lax.dot_general(qv, kv_, _DIMS["nt"], preferred_element_type=F32) * scale
                p = jnp.exp(s - lse_ref[rows, :])
                if diagonal:
                    p = jnp.where(_tri_block(g, cols, r * g), p, 0.0)
                dob = dov.astype(BF16)
                delta = jnp.sum(o_ref[rows, :] * dov, axis=-1, keepdims=True)
                dv_acc[:cols, :] += lax.dot_general(p.astype(BF16), dob, _DIMS["tn"], preferred_element_type=F32)
                dp = lax.dot_general(dob, v_ref[:cols, :], _DIMS["nt"], preferred_element_type=F32)
                ds = (p * (dp - delta) * scale).astype(BF16)
                dq_rows = pl.ds(pl.multiple_of(qi * t + r * g, g), g)
                dq_ref[dq_rows, :] += jnp.dot(ds, kv_, preferred_element_type=F32)
                dk_acc[:cols, :] += lax.dot_general(ds, qv, _DIMS["tn"], preferred_element_type=F32)

        @pl.when(qi > ki)
        def _():
            tile(False)

        @pl.when(qi == ki)
        def _():
            tile(True)

        @pl.when(qi == n - 1)
        def _():
            dk_ref[...] = dk_acc[...]
            dv_ref[...] = dv_acc[...]

    qrow = lambda w: pl.BlockSpec((None, t, w), lambda h, ki, qi: (h, jnp.maximum(qi, ki), 0))
    krow = lambda w: pl.BlockSpec((None, t, w), lambda h, ki, qi: (h, ki, 0))
    wide = pl.BlockSpec((t, VD), lambda h, ki, qi: (jnp.maximum(qi, ki), h))
    return _pcall(body, name=name, grid=(H, n, n),
                  in_specs=[qrow(QK), krow(QK), krow(VD), wide, qrow(1), wide],
                  out_specs=[pl.BlockSpec((None, T, QK), lambda h, ki, qi: (h, 0, 0)), krow(QK), krow(VD)],
                  out_shape=[jax.ShapeDtypeStruct((H, T, QK), F32), jax.ShapeDtypeStruct((H, T, QK), F32),
                             jax.ShapeDtypeStruct((H, T, VD), F32)],
                  scratch_shapes=[pltpu.VMEM((t, QK), F32), pltpu.VMEM((t, VD), F32)])(q, k, v, o, lse, do)


def _tril():
    return lax.broadcasted_iota(jnp.int32, (CHUNK, CHUNK), 1) <= lax.broadcasted_iota(jnp.int32, (CHUNK, CHUNK), 0)


@jax.custom_vjp
def _gm_gate(v, ws, b_t):
    wc = jnp.where(_tril()[None], ws, 0.0).astype(BF16)
    vb = v.astype(BF16)
    rows = []
    for c in range(v.shape[0] // CHUNK):
        cols = []
        for g in range(GROUPS):
            vc = vb[c * CHUNK:(c + 1) * CHUNK, g * LANE:(g + 1) * LANE]
            cols.append(jnp.dot(wc[g], vc, preferred_element_type=F32) + jnp.broadcast_to(b_t[:, g:g + 1], (CHUNK, LANE)))
        rows.append(jnp.concatenate(cols, axis=-1))
    return jnp.concatenate(rows, axis=0)


def _gm_gate_fwd(v, ws, b_t):
    return _gm_gate(v, ws, b_t), (v, ws)


def _gm_gate_bwd(res, dgate):
    v, ws = res
    tril = _tril()
    wc = jnp.where(tril[None], ws, 0.0).astype(BF16)
    vb = v.astype(BF16)
    dgb = dgate.astype(BF16)
    dws = [jnp.zeros((CHUNK, CHUNK), F32) for _ in range(GROUPS)]
    db = jnp.zeros((CHUNK, GROUPS), F32)
    lane_g = lax.broadcasted_iota(jnp.int32, (1, GROUPS), 1)
    rows = []
    for c in range(v.shape[0] // CHUNK):
        cols = []
        for g in range(GROUPS):
            sl = (slice(c * CHUNK, (c + 1) * CHUNK), slice(g * LANE, (g + 1) * LANE))
            cols.append(lax.dot_general(wc[g], dgb[sl], _DIMS["tn"], preferred_element_type=F32))
            dws[g] = dws[g] + lax.dot_general(dgb[sl], vb[sl], _DIMS["nt"], preferred_element_type=F32)
            db = db + jnp.sum(dgate[sl], axis=1, keepdims=True) * (lane_g == g).astype(F32)
        rows.append(jnp.concatenate(cols, axis=-1))
    dws = jnp.stack([jnp.where(tril, d, 0.0) for d in dws])
    return jnp.concatenate(rows, axis=0), dws, db


_gm_gate.defvjp(_gm_gate_fwd, _gm_gate_bwd)


def _mix_fn(a_out, zu, zv, aon_g, gon_g, vn_g, ws, b_t):
    u = jax.nn.gelu(zu)
    vv = _rms(jax.nn.gelu(zv), vn_g)
    g_out = u * _gm_gate(vv, ws, b_t)
    return jnp.concatenate([_rms(a_out, aon_g), _rms(g_out, gon_g)], axis=-1)


def _mix_ins(a_out, z, small, offs):
    gw = a_out.shape[1]
    return [(a_out, "row"), (z, ("cols", offs[0], gw)), (z, ("cols", offs[1], gw))] + [(s, "full") for s in small]


def _mix_fwd(name, a_out, z, small, offs):
    T, gw = a_out.shape
    return _rowwise(name, lambda *a: (_mix_fn(*a),), _mix_ins(a_out, z, small, offs), [((T, 2 * gw), BF16, "row")],
                    T, _pick(T, (256, 128)))[0]


def _mix_bwd(name, a_out, z, small, offs, dmixed):
    T, gw = a_out.shape

    def fn(*a):
        _, vjp = jax.vjp(_mix_fn, *a[:-1])
        return vjp(a[-1].astype(F32))

    outs = [((T, gw), F32, "row"), ((T, gw), BF16, "row"), ((T, gw), BF16, "row")] + [(s.shape, F32, "acc") for s in small]
    return _rowwise(name, fn, _mix_ins(a_out, z, small, offs) + [(dmixed, "row")], outs, T, _pick(T, (256, 128)))


def _ple_fn(gl, pe, g):
    return jax.nn.sigmoid(gl) * _rms(pe, g)


def _ple_fwd(name, h, gl, pe, g):
    T, D = h.shape
    return _rowwise(name, lambda hv, a, b, c: (hv + _ple_fn(a, b, c),),
                    [(h, "row"), (gl, "row"), (pe, "row"), (g, "full")], [((T, D), F32, "row")], T, _pick(T, (256, 128)))[0]


def _ple_bwd(name, gl, pe, g, dh, deps=()):
    T, D = gl.shape

    def fn(a, b, c, d):
        _, vjp = jax.vjp(_ple_fn, a, b, c)
        return vjp(d)

    return _rowwise(name, fn, [(gl, "row"), (pe, "row"), (g, "full"), (dh, "row")],
                    [((T, D), BF16, "row"), ((T, D), BF16, "row"), ((1, D), F32, "acc")], T, _pick(T, (256, 128)), deps)


def _loss(name, y, target):
    T, D = y.shape

    def fn(yv, tv):
        err = yv - tv
        part = 0.5 * jnp.sum(jnp.mean(err * err, axis=-1, keepdims=True), axis=0, keepdims=True)
        return err * (1.0 / D), jnp.broadcast_to(part, (8, LANE))

    return _rowwise(name, fn, [(y, "row"), (target, "row")], [((T, D), F32, "row"), ((8, LANE), F32, "acc")], T,
                    _pick(T, (512, 256, 128)))


ADAMW_BLOCK_ELEMS = 256 * 1024


def _adamw_sum(name, parts, w, m, v):
    L, R, C = w.shape
    tiles = [(r, c) for r in (R, 512, 256, 128, 64, 32, 16) for c in (C, 1024, 512, 256, 128)
             if R % r == 0 and C % c == 0 and r * c <= ADAMW_BLOCK_ELEMS]
    tr, tc = max(tiles, key=lambda rc: (rc[0] * rc[1], rc[1]))
    nr, nc = R // tr, C // tc
    c1 = 1.0 - ADAM_B1 ** ADAM_STEP
    c2 = 1.0 - ADAM_B2 ** ADAM_STEP

    def body(*refs):
        p_refs = refs[:L]
        w_ref, m_ref, v_ref, g_out, d_out, m_out, v_out = refs[L:]
        layer = pl.program_id(0)

        def part(s):
            val = p_refs[0][s].astype(F32)
            for j in range(1, L):
                val = jnp.where(layer == j, p_refs[j][s].astype(F32), val)
            return val

        g = part(0)
        for s in range(1, N_DEV):
            g = g + part(s)
        m2 = ADAM_B1 * m_ref[...] + (1.0 - ADAM_B1) * g
        v2 = ADAM_B2 * v_ref[...] + (1.0 - ADAM_B2) * (g * g)
        g_out[...] = g
        m_out[...] = m2
        v_out[...] = v2
        d_out[...] = -ADAM_LR * ((m2 / c1) / (jnp.sqrt(v2 / c2) + ADAM_EPS) + ADAM_WD * w_ref[...])

    def part_spec(j):
        def index(l, i, k):
            before, mine = l < j, l == j
            return (0, jnp.where(mine, i, jnp.where(before, 0, nr - 1)), jnp.where(mine, k, jnp.where(before, 0, nc - 1)))
        return pl.BlockSpec((N_DEV, tr, tc), index)

    blk = pl.BlockSpec((None, tr, tc), lambda l, i, k: (l, i, k))
    sd = jax.ShapeDtypeStruct((L, R, C), F32)
    return _pcall(body, name=name, grid=(L, nr, nc),
                  in_specs=[part_spec(j) for j in range(L)] + [blk, blk, blk],
                  out_specs=[blk, blk, blk, blk], out_shape=[sd, sd, sd, sd])(*parts, w, m, v)


def _unshard_cols(g):
    _, K, n = g.shape
    return g.transpose(1, 0, 2).reshape(K, N_DEV * n)


def _shard_cols(full):
    K, N = full.shape
    return full.reshape(K, N_DEV, N // N_DEV).transpose(1, 0, 2)


def _unshard_rows(g):
    _, k, N = g.shape
    return g.reshape(N_DEV * k, N)


def _shard_rows(full):
    K, N = full.shape
    return full.reshape(N_DEV, K // N_DEV, N)


STAGES = (('ffn_a_w1', 'ffn_a_w3', 'ffn_a_w2'), ('w_in', 'w_uq', 'w_ukv', 'w_out'),
          ('ffn_b_w1', 'ffn_b_w3', 'ffn_b_w2'), ('w_ple_gate', 'w_ple'))
TRANSPOSED = ('ffn_a_w1', 'ffn_a_w3', 'ffn_b_w1', 'ffn_b_w3', 'w_in', 'w_uq')


def _step(x, p, positions, target, w, m, v):
    T, D = x.shape[1], x.shape[2]
    L = p.shape[0]
    x2, target2 = x[0], target[0]
    q_rank, kv_rank = w['w_uq'].shape[1], w['w_ukv'].shape[1]
    gw = w['gm_v_norm'].shape[1]

    inv_freq = ROPE_BASE ** (-jnp.arange(0, ROPE, 2, dtype=F32) / ROPE)
    ang = positions[0].astype(F32)[:, None] * inv_freq
    cos = jnp.concatenate([jnp.ones((T, NOPE), F32), jnp.cos(ang), jnp.cos(ang)], axis=-1)
    sin = jnp.concatenate([jnp.zeros((T, NOPE), F32), jnp.sin(ang), jnp.sin(ang)], axis=-1)
    tabs = (cos, sin, _rot_matrix().astype(BF16))

    groups = [(l, names) for l in range(L) for names in STAGES]

    def ag_start(k, after):
        l, names = groups[k]
        shards = [w[n][l].astype(BF16) for n in names]
        return _xchg_start(f"ag_chips{k}", "chips", shards, [_landing(s) for s in shards], after)

    ag = {0: ag_start(0, [])}
    ag[1] = ag_start(1, [ag[0]['token']])

    def fetch(k, after):
        lands = _xchg_wait(f"ag_landed{k}", ag[k], after)
        fw = _xchg_start(f"ag_forward{k}", "forward", [], lands, [])
        deps = [fw['token']]
        for nxt in {0: (), 1: (2, 3)}.get(k, (k + 2,)):
            if nxt < len(groups):
                ag[nxt] = ag_start(nxt, deps)
                deps = [ag[nxt]['token']]
        if k == 0:
            deps = deps + [ag[1]['token']]
        return fw, deps

    def gathered(k, fw, after):
        return dict(zip(groups[k][1], _xchg_wait(f"ag_wait{k}", fw, after)))

    s0, s1, s2, s3 = q_rank, q_rank + kv_rank, q_rank + kv_rank + ROPE, q_rank + kv_rank + ROPE + gw
    o_u, o_v, o_cq, o_ckv, o_kr = 0, gw, 2 * gw, 2 * gw + q_rank, 2 * gw + q_rank + kv_rank
    kr_pad = 2 * LANE - ROPE

    def g2(name, l):
        return w[name][l][None, :]

    gm_bt = [w['gm_bs'][l].T for l in range(L)]

    saved = []
    h = x2
    for l in range(L):
        s = {}
        fw, deps = fetch(4 * l, [h])
        s['h0'] = h
        s['xn_a'] = _rms_fwd(f"rms_a{l}", h, g2('ffn_a_norm', l), deps)
        wa = s['wa'] = gathered(4 * l, fw, [s['xn_a']])
        h, s['a_h1'], s['a_h3'] = _ffn_fwd(f"ffn_a_fwd{l}", s['xn_a'], h, wa['ffn_a_w1'], wa['ffn_a_w3'], wa['ffn_a_w2'])
        fw, deps = fetch(4 * l + 1, [h])
        s['h1'] = h
        s['n'] = _rms_fwd(f"rms_mix{l}", h, g2('mix_norm', l), deps)
        wm = gathered(4 * l + 1, fw, [s['n']])
        w_in_full = _unshard_cols(wm['w_in'])
        s['w_in'] = jnp.concatenate([w_in_full[:, s2:s3], w_in_full[:, s3:], w_in_full[:, :s0], w_in_full[:, s0:s1],
                                     w_in_full[:, s1:s2], jnp.zeros((D, kr_pad), BF16)], axis=-1)
        s['w_out'] = _unshard_rows(wm['w_out'])
        s['w_uq'], s['w_ukv'] = wm['w_uq'], wm['w_ukv']
        s['z'] = _mm(f"w_in{l}", s['n'], s['w_in'], "nn")
        s['mla_small'] = [g2('q_a_norm', l), g2('kv_a_norm', l), g2('q_norm', l), g2('k_norm', l)]
        s['q'], s['k'], s['v'] = _mla_prep_fwd(f"mla_prep{l}", s['z'], tabs, s['mla_small'], s['w_uq'], s['w_ukv'],
                                               (o_cq, o_ckv, o_kr))
        s['a_out'], s['lse'] = _attn_fwd(f"attn_fwd{l}", s['q'], s['k'], s['v'])
        s['mix_small'] = [g2('attn_out_norm', l), g2('gm_out_norm', l), g2('gm_v_norm', l), w['gm_ws'][l], gm_bt[l]]
        s['mixed'] = _mix_fwd(f"mix_fwd{l}", s['a_out'], s['z'], s['mix_small'], (o_u, o_v))
        h = _mm(f"w_out{l}", s['mixed'], s['w_out'], "nn", res=h)
        fw, deps = fetch(4 * l + 2, [h])
        s['h2'] = h
        s['xn_b'] = _rms_fwd(f"rms_b{l}", h, g2('ffn_b_norm', l), deps)
        wb = s['wb'] = gathered(4 * l + 2, fw, [s['xn_b']])
        h, s['b_h1'], s['b_h3'] = _ffn_fwd(f"ffn_b_fwd{l}", s['xn_b'], h, wb['ffn_b_w1'], wb['ffn_b_w3'], wb['ffn_b_w2'])
        fw, deps = fetch(4 * l + 3, [h])
        s['h3'] = h
        s['xn_g'] = _rms_fwd(f"rms_g{l}", h, g2('ple_gate_norm', l), deps)
        wp = gathered(4 * l + 3, fw, [s['xn_g']])
        s['w_gate'] = _unshard_rows(wp['w_ple_gate'])
        s['gl'] = _mm(f"w_gate{l}", s['xn_g'], s['w_gate'], "nn")
        s['p'] = p[l, 0]
        s['pe'] = _mm(f"w_ple{l}", s['p'], _unshard_cols(wp['w_ple']), "nn")
        h = _ple_fwd(f"ple_fwd{l}", h, s['gl'], s['pe'], g2('ple_norm', l))
        saved.append(s)

    dh, loss_part = _loss("loss", h, target2)
    loss = lax.psum(loss_part[0, 0], AXES)

    gsmall = {n: [None] * L for n in SMALL}
    rs, where = {}, {}

    def rs_start(key, l, named):
        grads = [g for _, g in named]
        lands = [lax.dynamic_update_slice(lax.empty(g.shape, g.dtype),
                                          lax.dynamic_index_in_dim(g, _my_index(), 0, keepdims=True),
                                          (_my_index(),) + (0,) * (g.ndim - 1)) for g in grads]
        rs[key] = _xchg_start("rs_start_" + key, "scatter", grads, lands, [])
        where.update({(n, l): (key, i) for i, (n, _) in enumerate(named)})
        return [rs[key]['token']]

    def ffn_backward(tag, l, dh, xn, h_in, h1, h3, wts, norm_name, deps):
        pre = 'ffn_' + tag
        dxn, dhb, dh1, dh3, act = _ffn_bwd(f"{pre}_bwd{l}", dh, h1, h3, wts[pre + '_w1'], wts[pre + '_w3'],
                                           wts[pre + '_w2'], deps)
        g1 = _mm_tn_batch(f"{pre}_dw1_{l}", dh1, xn[None])
        deps = rs_start(f"{pre}_w1_{l}", l, [(pre + '_w1', g1)])
        g3 = _mm_tn_batch(f"{pre}_dw3_{l}", dh3, xn[None], deps=deps)
        deps = rs_start(f"{pre}_w3_{l}", l, [(pre + '_w3', g3)])
        g2_ = _mm_tn_batch(f"{pre}_dw2_{l}", act, dhb[None], alpha=0.5, deps=deps)
        deps = rs_start(f"{pre}_w2_{l}", l, [(pre + '_w2', g2_)])
        return _rms_bwd(f"rms_{tag}_bwd{l}", h_in, g2(norm_name, l), dxn, dh, deps)

    deps = []
    for l in reversed(range(L)):
        s = saved[l]
        d_gl, d_pe, gsmall['ple_norm'][l] = _ple_bwd(f"ple_bwd{l}", s['gl'], s['pe'], g2('ple_norm', l), dh, deps)
        g_ple = _mm(f"dw_ple{l}", s['p'], d_pe, "tn", out_dtype=BF16)
        g_gate = _mm(f"dw_gate{l}", s['xn_g'], d_gl, "tn", out_dtype=BF16)
        d_xng = _mm(f"d_xng{l}", d_gl, s['w_gate'], "nt")
        dh, gsmall['ple_gate_norm'][l] = _rms_bwd(f"rms_g_bwd{l}", s['h3'], g2('ple_gate_norm', l), d_xng, dh)
        deps = rs_start(f"ple_{l}", l, [('w_ple_gate', _shard_rows(g_gate)), ('w_ple', _shard_cols(g_ple))])
        dh, gsmall['ffn_b_norm'][l] = ffn_backward('b', l, dh, s['xn_b'], s['h2'], s['b_h1'], s['b_h3'], s['wb'],
                                                   'ffn_b_norm', deps)
        g_out = _mm(f"dw_out{l}", s['mixed'], dh, "tn", out_dtype=BF16)
        d_mixed = _mm(f"d_mixed{l}", dh, s['w_out'], "nt")
        mix = _mix_bwd(f"mix_bwd{l}", s['a_out'], s['z'], s['mix_small'], (o_u, o_v), d_mixed)
        d_a_out, d_u, d_v = mix[:3]
        gsmall['attn_out_norm'][l], gsmall['gm_out_norm'][l], gsmall['gm_v_norm'][l], gsmall['gm_ws'][l] = mix[3:7]
        gsmall['gm_bs'][l] = mix[7].T
        dq, dk, dv = _attn_bwd(f"attn_bwd{l}", s['q'], s['k'], s['v'], s['a_out'], s['lse'], d_a_out)
        mla = _mla_prep_bwd(f"mla_prep_bwd{l}", s['z'], tabs, s['mla_small'], s['w_uq'], s['w_ukv'], (o_cq, o_ckv, o_kr),
                            dq, dk, dv)
        d_cq, d_ckv, d_kr = mla[:3]
        gsmall['q_a_norm'][l], gsmall['kv_a_norm'][l], gsmall['q_norm'][l], gsmall['k_norm'][l] = mla[3:7]
        dz = jnp.concatenate([d_u, d_v, d_cq, d_ckv, d_kr, jnp.zeros((T, LANE), BF16)], axis=-1)
        g_in = _mm(f"dw_in{l}", dz, s['n'], "tn", out_dtype=BF16)
        g_in = jnp.concatenate([g_in[o_cq:o_cq + q_rank], g_in[o_ckv:o_ckv + kv_rank], g_in[o_kr:o_kr + ROPE],
                                g_in[o_u:o_u + gw], g_in[o_v:o_v + gw]], axis=0)
        d_n = _mm(f"d_n{l}", dz, s['w_in'], "nt")
        dh, gsmall['mix_norm'][l] = _rms_bwd(f"rms_mix_bwd{l}", s['h1'], g2('mix_norm', l), d_n, dh)
        deps = rs_start(f"mix_{l}", l, [('w_in', _shard_rows(g_in)), ('w_uq', mla[7].transpose(0, 2, 1).astype(BF16)),
                                        ('w_ukv', mla[8].astype(BF16)), ('w_out', _shard_rows(g_out))])
        dh, gsmall['ffn_a_norm'][l] = ffn_backward('a', l, dh, s['xn_a'], s['h0'], s['a_h1'], s['a_h3'], s['wa'],
                                                   'ffn_a_norm', deps)
        deps = []
    grad_x = dh[None]

    out = {}
    sizes = [w[n].size for n in SMALL]
    total = sum(sizes)
    padded = -(-total // (512 * LANE)) * (512 * LANE)

    def pack(d):
        flat = jnp.concatenate([d[n].reshape(-1) for n in SMALL] + [jnp.zeros((padded - total,), F32)])
        return flat.reshape(1, padded // LANE, LANE)

    gs = pack({n: jnp.stack([gsmall[n][l].reshape(w[n].shape[1:]) for l in range(L)]) for n in SMALL})
    small = _xchg_start("small_start", "gather", [gs[0]], [_landing(gs[0])], [])

    after = [dh, small['token']]
    landed = {}

    def partials(n, l):
        key, i = where[(n, l)]
        if key not in landed:
            landed[key] = _xchg_wait("rs_wait_" + key, rs[key], after)
        return landed[key][i]

    swap = lambda a: a.transpose(0, 2, 1)
    for stage in (3, 2, 1, 0):
        for n in STAGES[stage]:
            parts = [partials(n, l) for l in reversed(range(L))][::-1]
            if n in TRANSPOSED:
                out[n] = [swap(r) for r in _adamw_sum("adamw_" + n, parts, swap(w[n]), swap(m[n]), swap(v[n]))]
            else:
                out[n] = _adamw_sum("adamw_" + n, parts, w[n], m[n], v[n])
            after = [out[n][0]]

    res = _adamw_sum("adamw_small", _xchg_wait("small_wait", small, after), pack(w), pack(m), pack(v))
    off = 0
    for n, sz in zip(SMALL, sizes):
        out[n] = [r.reshape(-1)[off:off + sz].reshape(w[n].shape) for r in res]
        off += sz

    return (loss, grad_x, *[out[n][0] for n in WEIGHTS], *[out[n][1] for n in WEIGHTS],
            *[out[n][2] for n in WEIGHTS], *[out[n][3] for n in WEIGHTS])


def kernel(x, p, positions, ffn_a_norm, ffn_a_w1, ffn_a_w3, ffn_a_w2, mix_norm, w_in, q_a_norm, w_uq, kv_a_norm, w_ukv, q_norm, k_norm, gm_v_norm, gm_ws, gm_bs, attn_out_norm, gm_out_norm, w_out, ffn_b_norm, ffn_b_w1, ffn_b_w3, ffn_b_w2, ple_gate_norm, w_ple_gate, w_ple, ple_norm, loss_target, m_ffn_a_norm, m_ffn_a_w1, m_ffn_a_w3, m_ffn_a_w2, m_mix_norm, m_w_in, m_q_a_norm, m_w_uq, m_kv_a_norm, m_w_ukv, m_q_norm, m_k_norm, m_gm_v_norm, m_gm_ws, m_gm_bs, m_attn_out_norm, m_gm_out_norm, m_w_out, m_ffn_b_norm, m_ffn_b_w1, m_ffn_b_w3, m_ffn_b_w2, m_ple_gate_norm, m_w_ple_gate, m_w_ple, m_ple_norm, v_ffn_a_norm, v_ffn_a_w1, v_ffn_a_w3, v_ffn_a_w2, v_mix_norm, v_w_in, v_q_a_norm, v_w_uq, v_kv_a_norm, v_w_ukv, v_q_norm, v_k_norm, v_gm_v_norm, v_gm_ws, v_gm_bs, v_attn_out_norm, v_gm_out_norm, v_w_out, v_ffn_b_norm, v_ffn_b_w1, v_ffn_b_w3, v_ffn_b_w2, v_ple_gate_norm, v_w_ple_gate, v_w_ple, v_ple_norm):
    args = locals()
    w = {n: args[n] for n in WEIGHTS}
    m = {n: args["m_" + n] for n in WEIGHTS}
    v = {n: args["v_" + n] for n in WEIGHTS}
    return _step(x, p, positions, loss_target, w, m, v)
```

```python
import functools

import jax
import jax.numpy as jnp
from jax import lax
from jax.experimental import pallas as pl
from jax.experimental.pallas import tpu as pltpu

F32, BF16 = jnp.float32, jnp.bfloat16
EPS = 1e-6
N_DEV = 8
HEADS = 8
NOPE, ROPE, QK, VD = 128, 64, 192, 128
CHUNK = 128
GROUPS = 8
LANE = 128
ROPE_BASE = 10000.0
ADAM_LR, ADAM_B1, ADAM_B2, ADAM_EPS, ADAM_WD, ADAM_STEP = 0.001, 0.9, 0.999, 1e-08, 0.01, 10
AXES = ("x", "y", "c")
MESH = pl.DeviceIdType.MESH
ANY = pl.BlockSpec(memory_space=pl.ANY)

WEIGHTS = ['ffn_a_norm', 'ffn_a_w1', 'ffn_a_w3', 'ffn_a_w2', 'mix_norm', 'w_in', 'q_a_norm', 'w_uq', 'kv_a_norm',
           'w_ukv', 'q_norm', 'k_norm', 'gm_v_norm', 'gm_ws', 'gm_bs', 'attn_out_norm', 'gm_out_norm', 'w_out',
           'ffn_b_norm', 'ffn_b_w1', 'ffn_b_w3', 'ffn_b_w2', 'ple_gate_norm', 'w_ple_gate', 'w_ple', 'ple_norm']
BIG = ['ffn_a_w1', 'ffn_a_w3', 'ffn_a_w2', 'w_in', 'w_uq', 'w_ukv', 'w_out', 'ffn_b_w1', 'ffn_b_w3', 'ffn_b_w2',
       'w_ple_gate', 'w_ple']
SMALL = [n for n in WEIGHTS if n not in BIG]


def _pcall(body, **kw):
    return pl.pallas_call(body, **kw)


def _pick(n, cands):
    for c in cands:
        if n % c == 0:
            return c
    return n


def _rms(x, g):
    return x * lax.rsqrt(jnp.mean(x * x, axis=-1, keepdims=True) + EPS) * g


@jax.custom_vjp
def _bdot(x, w):
    return jnp.dot(x.astype(BF16), w.astype(BF16), preferred_element_type=F32)


def _bdot_fwd(x, w):
    return _bdot(x, w), (x, w)


def _bdot_bwd(res, dy):
    x, w = res
    dyb = dy.astype(BF16)
    dx = lax.dot_general(dyb, w.astype(BF16), (((1,), (1,)), ((), ())), preferred_element_type=F32)
    dw = lax.dot_general(x.astype(BF16), dyb, (((0,), (0,)), ((), ())), preferred_element_type=F32)
    return dx.astype(x.dtype), dw.astype(w.dtype)


_bdot.defvjp(_bdot_fwd, _bdot_bwd)


def _split_dot(x, p, dims):
    hi = x.astype(BF16)
    lo = (x - hi.astype(F32)).astype(BF16)
    return (lax.dot_general(hi, p, dims, preferred_element_type=F32)
            + lax.dot_general(lo, p, dims, preferred_element_type=F32))


@jax.custom_vjp
def _permute(x, p):
    return _split_dot(x, p, _DIMS["nn"])


def _permute_fwd(x, p):
    return _permute(x, p), p


def _permute_bwd(p, ct):
    return _split_dot(ct, p, _DIMS["nt"]), jnp.zeros_like(p)


_permute.defvjp(_permute_fwd, _permute_bwd)


def _flip(v, bit):
    return 1 - v if bit else v


HBM = pl.BlockSpec(memory_space=pltpu.HBM)
SEM = pl.BlockSpec(memory_space=pltpu.SEMAPHORE)
EFFECT = pltpu.SideEffectType.DATAFLOW_SIDE_EFFECTING
PEERS = N_DEV - 1


def _my_index():
    return 4 * lax.axis_index("x") + 2 * lax.axis_index("y") + lax.axis_index("c")


def _landing(own):
    zone = lax.empty((N_DEV,) + own.shape, own.dtype)
    return lax.dynamic_update_slice(zone, own[None], (_my_index(),) + (0,) * own.ndim)


COPIES = {"gather": PEERS, "scatter": PEERS, "chips": 4, "forward": 3}


def _copy_plan(kind, src_refs, land_refs, send_sems, recv_sems):
    cx, cy, cc = lax.axis_index("x"), lax.axis_index("y"), lax.axis_index("c")
    me = 4 * cx + 2 * cy + cc
    per = COPIES[kind]
    out = []
    for t, land in enumerate(land_refs):
        def pair(i, src, to_slot, from_slot, dev):
            kw = dict(send_sem=send_sems.at[per * t + i], recv_sem=recv_sems.at[per * t + i], device_id=dev,
                      device_id_type=MESH)
            out.append((pltpu.make_async_remote_copy(src_ref=src, dst_ref=land.at[to_slot], **kw),
                        pltpu.make_async_remote_copy(src_ref=src, dst_ref=land.at[from_slot], **kw)))

        if kind in ("gather", "scatter"):
            for k in range(1, N_DEV):
                px, py, pc = _flip(cx, k & 4), _flip(cy, k & 2), _flip(cc, k & 1)
                peer = 4 * px + 2 * py + pc
                pair(k - 1, src_refs[t].at[peer] if kind == "scatter" else src_refs[t], me, peer, (px, py, pc))
        elif kind == "chips":
            pair(0, src_refs[t], me, me + 1 - 2 * cc, (cx, cy, 1 - cc))
            for j in range(1, 4):
                px, py = _flip(cx, j & 2), _flip(cy, j & 1)
                pair(j, src_refs[t], me, 4 * px + 2 * py + cc, (px, py, cc))
        else:
            for j in range(1, 4):
                px, py = _flip(cx, j & 2), _flip(cy, j & 1)
                mine, theirs = 4 * px + 2 * py + cc, 4 * px + 2 * py + 1 - cc
                pair(j - 1, land.at[mine], mine, theirs, (cx, cy, 1 - cc))
    return out


def _xchg_start(name, kind, srcs, lands, after):
    ns, nb, na = len(srcs), len(srcs) + len(lands), len(after)
    n_sems = COPIES[kind] * len(lands)

    def body(*refs):
        send_sems, recv_sems = refs[nb + na], refs[nb + na + 1]
        for send, _ in _copy_plan(kind, refs[:ns], refs[ns:nb], send_sems, recv_sems):
            send.start()
        refs[-1][...] = jnp.zeros_like(refs[-1])

    bufs = list(srcs) + list(lands)
    res = _pcall(
        body, name=name,
        out_shape=(pltpu.SemaphoreType.DMA((n_sems,)), pltpu.SemaphoreType.DMA((n_sems,)),
                   *[pltpu.HBM(a.shape, a.dtype) for a in bufs], jax.ShapeDtypeStruct((8, LANE), F32)),
        in_specs=[HBM] * nb + [ANY] * na,
        out_specs=(SEM, SEM, *([HBM] * nb), pl.BlockSpec(memory_space=pltpu.VMEM)),
        input_output_aliases={i: 2 + i for i in range(nb)},
        compiler_params=pltpu.CompilerParams(has_side_effects=EFFECT),
    )(*[pltpu.with_memory_space_constraint(a, pltpu.HBM) for a in bufs], *after)
    return dict(kind=kind, send=res[0], recv=res[1], srcs=list(res[2:2 + ns]), lands=list(res[2 + ns:2 + nb]),
                token=res[-1])


def _xchg_wait(name, st, after):
    ns, nb = len(st['srcs']), len(st['srcs']) + len(st['lands'])

    def body(*refs):
        for _, back in _copy_plan(st['kind'], refs[:ns], refs[ns:nb], refs[nb], refs[nb + 1]):
            back.wait_send()
            back.wait_recv()

    bufs = st['srcs'] + st['lands']
    res = _pcall(
        body, name=name, out_shape=tuple(pltpu.HBM(a.shape, a.dtype) for a in bufs),
        in_specs=[HBM] * nb + [SEM, SEM] + [ANY] * len(after), out_specs=tuple([HBM] * nb),
        input_output_aliases={i: i for i in range(nb)},
        compiler_params=pltpu.CompilerParams(has_side_effects=EFFECT),
    )(*bufs, st['send'], st['recv'], *after)
    return list(res[ns:])


_DIMS = {"nn": (((1,), (0,)), ((), ())), "nt": (((1,), (1,)), ((), ())), "tn": (((0,), (0,)), ((), ()))}


MM_OPERAND_BYTES = 16 * 1024 * 1024


def _contraction_tile(K, bytes_per_k):
    fits = [c for c in (K, 2048, 1024, 512, 256, 128) if K % c == 0 and c * bytes_per_k <= MM_OPERAND_BYTES]
    return fits[0] if fits else _pick(K, (128,))


def _mm(name, a, b, mode, out_dtype=F32, res=None, alpha=1.0, deps=()):
    if mode == "tn":
        K, M = a.shape
        N = b.shape[1]
    else:
        M, K = a.shape
        N = b.shape[0] if mode == "nt" else b.shape[1]
    tn = _pick(N, (1024, 512, 256))
    tm = _pick(M, (1024, 512, 256, 128) if tn <= 1024 else (512, 256, 128))
    tk = _contraction_tile(K, tm * a.dtype.itemsize + tn * b.dtype.itemsize)
    nk = K // tk
    a_spec = pl.BlockSpec((tk, tm), lambda i, j, k: (k, i)) if mode == "tn" else pl.BlockSpec((tm, tk), lambda i, j, k: (i, k))
    b_spec = pl.BlockSpec((tn, tk), lambda i, j, k: (j, k)) if mode == "nt" else pl.BlockSpec((tk, tn), lambda i, j, k: (k, j))
    o_spec = pl.BlockSpec((tm, tn), lambda i, j, k: (i, j))
    dims = _DIMS[mode]

    def body(*refs):
        a_ref, b_ref, r_ref = refs[0], refs[1], refs[2]
        part = lax.dot_general(a_ref[...].astype(BF16), b_ref[...].astype(BF16), dims, preferred_element_type=F32)

        def finish(o_ref, r):
            r = r * alpha if alpha != 1.0 else r
            if res is not None:
                r = r_ref[...] + r
            o_ref[...] = r.astype(o_ref.dtype)

        if nk == 1:
            finish(refs[-1], part)
            return
        o_ref, acc = refs[-2], refs[-1]
        k = pl.program_id(2)

        @pl.when(k == 0)
        def _():
            acc[...] = part

        @pl.when(k > 0)
        def _():
            acc[...] += part

        @pl.when(k == nk - 1)
        def _():
            finish(o_ref, acc[...])

    ins = [a, b] + ([] if res is None else [res]) + list(deps)
    specs = [a_spec, b_spec] + ([] if res is None else [o_spec]) + [ANY] * len(deps)
    return _pcall(body, name=name, grid=(M // tm, N // tn, nk), in_specs=specs, out_specs=o_spec,
                  out_shape=jax.ShapeDtypeStruct((M, N), out_dtype),
                  scratch_shapes=[] if nk == 1 else [pltpu.VMEM((tm, tn), F32)])(*ins)


def _mm_tn_batch(name, a3, b3, alpha=1.0, deps=()):
    ga, T, M = a3.shape
    gb, _, N = b3.shape
    G = max(ga, gb)
    tm = _pick(M, (1024, 512)) if N <= 1024 else M
    tk = _contraction_tile(T, tm * a3.dtype.itemsize + N * b3.dtype.itemsize)
    nk = T // tk
    a_spec = pl.BlockSpec((None, tk, tm), (lambda g, i, k: (g, k, i)) if ga > 1 else (lambda g, i, k: (0, k, i)))
    b_spec = pl.BlockSpec((None, tk, N), (lambda g, i, k: (g, k, 0)) if gb > 1 else (lambda g, i, k: (0, k, 0)))
    o_spec = pl.BlockSpec((None, tm, N), lambda g, i, k: (g, i, 0))

    def body(*refs):
        a_ref, b_ref, o_ref, acc = refs[0], refs[1], refs[-2], refs[-1]
        k = pl.program_id(2)
        part = lax.dot_general(a_ref[...].astype(BF16), b_ref[...].astype(BF16), _DIMS["tn"], preferred_element_type=F32)

        @pl.when(k == 0)
        def _():
            acc[...] = part

        @pl.when(k > 0)
        def _():
            acc[...] += part

        @pl.when(k == nk - 1)
        def _():
            o_ref[...] = (acc[...] * alpha if alpha != 1.0 else acc[...]).astype(o_ref.dtype)

    return _pcall(body, name=name, grid=(G, M // tm, nk), in_specs=[a_spec, b_spec] + [ANY] * len(deps),
                  out_specs=o_spec, out_shape=jax.ShapeDtypeStruct((G, M, N), BF16),
                  scratch_shapes=[pltpu.VMEM((tm, N), F32)])(a3, b3, *deps)


def _rowwise(name, fn, ins, outs, T, tm, deps=()):
    in_specs = []
    for arr, spec in ins:
        if spec == "row":
            in_specs.append(pl.BlockSpec((tm, arr.shape[1]), lambda i: (i, 0)))
        elif spec == "full":
            in_specs.append(pl.BlockSpec(arr.shape, lambda i, _n=arr.ndim: (0,) * _n))
        else:
            _, off, width = spec
            in_specs.append(pl.BlockSpec((tm, width), lambda i, _b=off // width: (i, _b)))
    in_specs += [ANY] * len(deps)
    out_specs, out_shapes = [], []
    for shape, dtype, spec in outs:
        out_shapes.append(jax.ShapeDtypeStruct(shape, dtype))
        if spec == "row":
            out_specs.append(pl.BlockSpec((tm, shape[1]), lambda i: (i, 0)))
        else:
            out_specs.append(pl.BlockSpec(shape, lambda i, _n=len(shape): (0,) * _n))
    n_in = len(ins)

    def body(*refs):
        res = fn(*[r[...] for r in refs[:n_in]])
        i = pl.program_id(0)
        for r, (_, _, spec), val in zip(refs[n_in + len(deps):], outs, res):
            if spec == "acc":
                @pl.when(i == 0)
                def _(r=r):
                    r[...] = jnp.zeros_like(r)
                r[...] += val.astype(r.dtype)
            else:
                r[...] = val.astype(r.dtype)

    return _pcall(body, name=name, grid=(T // tm,), in_specs=in_specs, out_specs=out_specs, out_shape=out_shapes)(
        *[a for a, _ in ins], *deps)


def _rms_fwd(name, h, g, deps=()):
    T, D = h.shape
    return _rowwise(name, lambda hv, gv: (_rms(hv, gv),), [(h, "row"), (g, "full")], [((T, D), BF16, "row")], T,
                    _pick(T, (512, 256, 128)), deps)[0]


def _rms_bwd(name, h, g, dxn, dh_in, deps=()):
    T, D = h.shape

    def fn(hv, gv, dv, dh0):
        _, vjp = jax.vjp(_rms, hv, gv)
        dh, dg = vjp(dv.astype(F32))
        return dh0 + dh, dg

    return _rowwise(name, fn, [(h, "row"), (g, "full"), (dxn, "row"), (dh_in, "row")],
                    [((T, D), F32, "row"), ((1, D), F32, "acc")], T, _pick(T, (256, 128)), deps)


def _ffn_fwd(name, xn, h, w1, w3, w2):
    T, D = xn.shape
    F8 = w1.shape[-1]
    tm = _pick(T, (512, 256, 128))
    wspec = lambda r, c: pl.BlockSpec((None, r, c), lambda i, d: (d, 0, 0))
    row = pl.BlockSpec((tm, D), lambda i, d: (i, 0))
    hid = pl.BlockSpec((None, tm, F8), lambda i, d: (d, i, 0))

    def body(xn_ref, h_ref, w1_ref, w3_ref, w2_ref, out_ref, h1_ref, h3_ref, acc):
        d = pl.program_id(1)

        @pl.when(d == 0)
        def _():
            acc[...] = jnp.zeros_like(acc)

        x = xn_ref[...]
        h1 = jnp.dot(x, w1_ref[...], preferred_element_type=F32)
        h3 = jnp.dot(x, w3_ref[...], preferred_element_type=F32)
        h1_ref[...] = h1.astype(BF16)
        h3_ref[...] = h3.astype(BF16)
        act = (h1 * jax.nn.sigmoid(h1) * h3).astype(BF16)
        acc[...] += jnp.dot(act, w2_ref[...], preferred_element_type=F32)

        @pl.when(d == N_DEV - 1)
        def _():
            out_ref[...] = h_ref[...] + 0.5 * acc[...]

    return _pcall(body, name=name, grid=(T // tm, N_DEV),
                  in_specs=[row, row, wspec(D, F8), wspec(D, F8), wspec(F8, D)],
                  out_specs=[row, hid, hid],
                  out_shape=[jax.ShapeDtypeStruct((T, D), F32), jax.ShapeDtypeStruct((N_DEV, T, F8), BF16),
                             jax.ShapeDtypeStruct((N_DEV, T, F8), BF16)],
                  scratch_shapes=[pltpu.VMEM((tm, D), F32)])(xn, h, w1, w3, w2)


FFN_ROW_GROUPS = 2


def _ffn_bwd(name, dy, h1, h3, w1, w3, w2, deps=()):
    T, D = dy.shape
    F8 = w1.shape[-1]
    tm = _pick(T, (512, 256, 128))
    wspec = lambda r, c: pl.BlockSpec((None, r, c), lambda i, d: (d, 0, 0))
    row = pl.BlockSpec((tm, D), lambda i, d: (i, 0))
    hid = pl.BlockSpec((None, tm, F8), lambda i, d: (d, i, 0))

    def body(*refs):
        dy_ref, h1_ref, h3_ref, w1_ref, w3_ref, w2_ref = refs[:6]
        dxn_ref, dyb, dh1_ref, dh3_ref, act_ref, acc = refs[6 + len(deps):]
        d = pl.program_id(1)

        @pl.when(d == 0)
        def _():
            acc[...] = jnp.zeros_like(acc)
            dyb[...] = dy_ref[...].astype(BF16)

        for r in range(FFN_ROW_GROUPS):
            rows = pl.ds(r * (tm // FFN_ROW_GROUPS), tm // FFN_ROW_GROUPS)
            dact = 0.5 * lax.dot_general(dyb[rows, :], w2_ref[...], _DIMS["nt"], preferred_element_type=F32)
            h1 = h1_ref[rows, :].astype(F32)
            h3 = h3_ref[rows, :].astype(F32)
            sig = jax.nn.sigmoid(h1)
            silu = h1 * sig
            dh1 = (dact * h3 * (sig * (1.0 + h1 * (1.0 - sig)))).astype(BF16)
            dh3 = (dact * silu).astype(BF16)
            dh1_ref[rows, :] = dh1
            dh3_ref[rows, :] = dh3
            act_ref[rows, :] = (silu * h3).astype(BF16)
            acc[rows, :] += (lax.dot_general(dh1, w1_ref[...], _DIMS["nt"], preferred_element_type=F32)
                             + lax.dot_general(dh3, w3_ref[...], _DIMS["nt"], preferred_element_type=F32))

        @pl.when(d == N_DEV - 1)
        def _():
            dxn_ref[...] = acc[...]

    hshape = jax.ShapeDtypeStruct((N_DEV, T, F8), BF16)
    return _pcall(body, name=name, grid=(T // tm, N_DEV),
                  in_specs=[row, hid, hid, wspec(D, F8), wspec(D, F8), wspec(F8, D)] + [ANY] * len(deps),
                  out_specs=[row, row, hid, hid, hid],
                  out_shape=[jax.ShapeDtypeStruct((T, D), F32), jax.ShapeDtypeStruct((T, D), BF16), hshape, hshape, hshape],
                  scratch_shapes=[pltpu.VMEM((tm, D), F32)])(dy, h1, h3, w1, w3, w2, *deps)


def _rot_matrix():
    i = jnp.arange(QK)[:, None]
    j = jnp.arange(QK)[None, :]
    half = ROPE // 2
    first = (j >= NOPE) & (j < NOPE + half) & (i == j + half)
    second = (j >= NOPE + half) & (i == j - half)
    return jnp.where(first, -1.0, jnp.where(second, 1.0, 0.0)).astype(F32)


def _mla_fn(cq, ckv, kr128, cos, sin, rot, qa_g, kva_g, qn_g, kn_g, w_uq, w_ukv):
    cqn = _rms(cq, qa_g)
    ckvn = _rms(ckv, kva_g)
    kr = kr128[:, :ROPE]
    qs, ks, vs = [], [], []
    for h in range(HEADS):
        qh = _rms(_bdot(cqn, w_uq[h]), qn_g)
        qs.append(qh * cos + _permute(qh, rot) * sin)
        kvh = _bdot(ckvn, w_ukv[h])
        kh = _rms(jnp.concatenate([kvh[:, :NOPE], kr], axis=-1), kn_g)
        ks.append(kh * cos + _permute(kh, rot) * sin)
        vs.append(kvh[:, NOPE:])
    return qs, ks, vs


def _mla_specs(z, tabs, small, w_uq, w_ukv, tm, offs):
    o_cq, o_ckv, o_kr = offs
    row = lambda w: pl.BlockSpec((tm, w), lambda i: (i, 0))
    col = lambda off, w: pl.BlockSpec((tm, w), lambda i: (i, off // w))
    full2 = lambda a: pl.BlockSpec(a.shape, lambda i: (0, 0))
    wsp = lambda a: pl.BlockSpec(a.shape, lambda i: (0, 0, 0))
    cq_w, ckv_w = w_uq.shape[1], w_ukv.shape[1]
    ins = [z, z, z, tabs[0], tabs[1], tabs[2]] + list(small) + [w_uq, w_ukv]
    specs = ([col(o_cq, cq_w), col(o_ckv, ckv_w), col(o_kr, LANE), row(QK), row(QK), full2(tabs[2])]
             + [full2(s) for s in small] + [wsp(w_uq), wsp(w_ukv)])
    return ins, specs


def _mla_prep_fwd(name, z, tabs, small, w_uq, w_ukv, offs):
    T = z.shape[0]
    tm = _pick(T, (256, 128))
    ins, specs = _mla_specs(z, tabs, small, w_uq, w_ukv, tm, offs)
    head = lambda w: pl.BlockSpec((HEADS, tm, w), lambda i: (0, i, 0))

    def body(*refs):
        vals = [r[...] for r in refs[:12]]
        q_ref, k_ref, v_ref = refs[12:]
        qs, ks, vs = _mla_fn(*vals)
        for h in range(HEADS):
            q_ref[h] = qs[h].astype(BF16)
            k_ref[h] = ks[h].astype(BF16)
            v_ref[h] = vs[h].astype(BF16)

    return _pcall(body, name=name, grid=(T // tm,), in_specs=specs, out_specs=[head(QK), head(QK), head(VD)],
                  out_shape=[jax.ShapeDtypeStruct((HEADS, T, QK), BF16), jax.ShapeDtypeStruct((HEADS, T, QK), BF16),
                             jax.ShapeDtypeStruct((HEADS, T, VD), BF16)])(*ins)


def _mla_prep_bwd(name, z, tabs, small, w_uq, w_ukv, offs, dq, dk, dv):
    T = z.shape[0]
    tm = _pick(T, (256, 128))
    ins, specs = _mla_specs(z, tabs, small, w_uq, w_ukv, tm, offs)
    head = lambda w: pl.BlockSpec((HEADS, tm, w), lambda i: (0, i, 0))
    ins += [dq, dk, dv]
    specs += [head(QK), head(QK), head(VD)]
    cq_w, ckv_w = w_uq.shape[1], w_ukv.shape[1]
    acc_shapes = [s.shape for s in small] + [w_uq.shape, w_ukv.shape]
    row_shapes = [(T, cq_w), (T, ckv_w), (T, LANE)]
    out_shape = [jax.ShapeDtypeStruct(s, BF16) for s in row_shapes] + [jax.ShapeDtypeStruct(s, F32) for s in acc_shapes]
    out_specs = ([pl.BlockSpec((tm, s[1]), lambda i: (i, 0)) for s in row_shapes]
                 + [pl.BlockSpec(s, lambda i, _n=len(s): (0,) * _n) for s in acc_shapes])

    def body(*refs):
        cq, ckv, kr128, cos, sin, rot, qa_g, kva_g, qn_g, kn_g, w_uq_v, w_ukv_v = [r[...] for r in refs[:12]]
        dq_ref, dk_ref, dv_ref = refs[12:15]
        outs = refs[15:]
        f = lambda a, b, c, g1, g2, g3, g4, wq, wkv: _mla_fn(a, b, c, cos, sin, rot, g1, g2, g3, g4, wq, wkv)
        _, vjp = jax.vjp(f, cq, ckv, kr128, qa_g, kva_g, qn_g, kn_g, w_uq_v.astype(F32), w_ukv_v.astype(F32))
        cts = ([dq_ref[h] for h in range(HEADS)], [dk_ref[h] for h in range(HEADS)], [dv_ref[h] for h in range(HEADS)])
        grads = vjp(cts)
        i = pl.program_id(0)
        for n, (r, gval) in enumerate(zip(outs, grads)):
            if n < 3:
                r[...] = gval.astype(r.dtype)
            else:
                @pl.when(i == 0)
                def _(r=r):
                    r[...] = jnp.zeros_like(r)
                r[...] += gval

    return _pcall(body, name=name, grid=(T // tm,), in_specs=specs, out_specs=out_specs, out_shape=out_shape)(*ins)


NEG = -1e30


ATTN_ROW_GROUPS = 2


def _tri_block(rows, cols, row0):
    return lax.broadcasted_iota(jnp.int32, (rows, cols), 1) <= lax.broadcasted_iota(jnp.int32, (rows, cols), 0) + row0


def _attn_tiles(T):
    t = _pick(T, (1024, 512, 256, 128))
    return t, T // t


def _attn_fwd(name, q, k, v):
    H, T, _ = q.shape
    t, n = _attn_tiles(T)
    scale = QK ** -0.5

    def body(q_ref, k_ref, v_ref, o_ref, lse_ref, m_s, l_s, acc):
        qi, ki = pl.program_id(1), pl.program_id(2)

        @pl.when(ki == 0)
        def _():
            m_s[...] = jnp.full_like(m_s, NEG)
            l_s[...] = jnp.zeros_like(l_s)
            acc[...] = jnp.zeros_like(acc)

        def tile(diagonal):
            s = lax.dot_general(q_ref[...], k_ref[...], _DIMS["nt"], preferred_element_type=F32) * scale
            if diagonal:
                s = jnp.where(_tri_block(t, t, 0), s, NEG)
            m_new = jnp.maximum(m_s[...], jnp.max(s, axis=-1, keepdims=True))
            alpha = jnp.exp(m_s[...] - m_new)
            p = jnp.exp(s - m_new)
            l_s[...] = alpha * l_s[...] + jnp.sum(p, axis=-1, keepdims=True)
            acc[...] = alpha * acc[...] + jnp.dot(p.astype(BF16), v_ref[...], preferred_element_type=F32)
            m_s[...] = m_new

        @pl.when(ki < qi)
        def _():
            tile(False)

        @pl.when(ki == qi)
        def _():
            tile(True)
            o_ref[...] = acc[...] / l_s[...]
            lse_ref[...] = m_s[...] + jnp.log(l_s[...])

    kv = lambda w: pl.BlockSpec((None, t, w), lambda h, qi, ki: (h, jnp.minimum(ki, qi), 0))
    return _pcall(body, name=name, grid=(H, n, n),
                  in_specs=[pl.BlockSpec((None, t, QK), lambda h, qi, ki: (h, qi, 0)), kv(QK), kv(VD)],
                  out_specs=[pl.BlockSpec((t, VD), lambda h, qi, ki: (qi, h)),
                             pl.BlockSpec((None, t, 1), lambda h, qi, ki: (h, qi, 0))],
                  out_shape=[jax.ShapeDtypeStruct((T, H * VD), F32), jax.ShapeDtypeStruct((H, T, 1), F32)],
                  scratch_shapes=[pltpu.VMEM((t, 1), F32), pltpu.VMEM((t, 1), F32), pltpu.VMEM((t, VD), F32)])(q, k, v)


def _attn_bwd(name, q, k, v, o, lse, do):
    H, T, _ = q.shape
    t, n = _attn_tiles(T)
    scale = QK ** -0.5

    def body(q_ref, k_ref, v_ref, o_ref, lse_ref, do_ref, dq_ref, dk_ref, dv_ref, dk_acc, dv_acc):
        ki, qi = pl.program_id(1), pl.program_id(2)

        @pl.when((ki == 0) & (qi == 0))
        def _():
            dq_ref[...] = jnp.zeros_like(dq_ref)

        @pl.when(qi == 0)
        def _():
            dk_acc[...] = jnp.zeros_like(dk_acc)
            dv_acc[...] = jnp.zeros_like(dv_acc)

        def tile(diagonal):
            g = t // ATTN_ROW_GROUPS
            for r in range(ATTN_ROW_GROUPS):
                rows = pl.ds(r * g, g)
                cols = (r + 1) * g if diagonal else t
                qv, kv_, dov = q_ref[rows, :], k_ref[:cols, :], do_ref[rows, :]
                s = lax.dot_general(qv, kv_, _DIMS["nt"], preferred_element_type=F32) * scale
                p = jnp.exp(s - lse_ref[rows, :])
                if diagonal:
                    p = jnp.where(_tri_block(g, cols, r * g), p, 0.0)
                dob = dov.astype(BF16)
                delta = jnp.sum(o_ref[rows, :] * dov, axis=-1, keepdims=True)
                dv_acc[:cols, :] += lax.dot_general(p.astype(BF16), dob, _DIMS["tn"], preferred_element_type=F32)
                dp = lax.dot_general(dob, v_ref[:cols, :], _DIMS["nt"], preferred_element_type=F32)
                ds = (p * (dp - delta) * scale).astype(BF16)
                dq_rows = pl.ds(pl.multiple_of(qi * t + r * g, g), g)
                dq_ref[dq_rows, :] += jnp.dot(ds, kv_, preferred_element_type=F32)
                dk_acc[:cols, :] += lax.dot_general(ds, qv, _DIMS["tn"], preferred_element_type=F32)

        @pl.when(qi > ki)
        def _():
            tile(False)

        @pl.when(qi == ki)
        def _():
            tile(True)

        @pl.when(qi == n - 1)
        def _():
            dk_ref[...] = dk_acc[...]
            dv_ref[...] = dv_acc[...]

    qrow = lambda w: pl.BlockSpec((None, t, w), lambda h, ki, qi: (h, jnp.maximum(qi, ki), 0))
    krow = lambda w: pl.BlockSpec((None, t, w), lambda h, ki, qi: (h, ki, 0))
    wide = pl.BlockSpec((t, VD), lambda h, ki, qi: (jnp.maximum(qi, ki), h))
    return _pcall(body, name=name, grid=(H, n, n),
                  in_specs=[qrow(QK), krow(QK), krow(VD), wide, qrow(1), wide],
                  out_specs=[pl.BlockSpec((None, T, QK), lambda h, ki, qi: (h, 0, 0)), krow(QK), krow(VD)],
                  out_shape=[jax.ShapeDtypeStruct((H, T, QK), F32), jax.ShapeDtypeStruct((H, T, QK), F32),
                             jax.ShapeDtypeStruct((H, T, VD), F32)],
                  scratch_shapes=[pltpu.VMEM((t, QK), F32), pltpu.VMEM((t, VD), F32)])(q, k, v, o, lse, do)


def _tril():
    return lax.broadcasted_iota(jnp.int32, (CHUNK, CHUNK), 1) <= lax.broadcasted_iota(jnp.int32, (CHUNK, CHUNK), 0)


@jax.custom_vjp
def _gm_gate(v, ws, b_t):
    wc = jnp.where(_tril()[None], ws, 0.0).astype(BF16)
    vb = v.astype(BF16)
    rows = []
    for c in range(v.shape[0] // CHUNK):
        cols = []
        for g in range(GROUPS):
            vc = vb[c * CHUNK:(c + 1) * CHUNK, g * LANE:(g + 1) * LANE]
            cols.append(jnp.dot(wc[g], vc, preferred_element_type=F32) + jnp.broadcast_to(b_t[:, g:g + 1], (CHUNK, LANE)))
        rows.append(jnp.concatenate(cols, axis=-1))
    return jnp.concatenate(rows, axis=0)


def _gm_gate_fwd(v, ws, b_t):
    return _gm_gate(v, ws, b_t), (v, ws)


def _gm_gate_bwd(res, dgate):
    v, ws = res
    tril = _tril()
    wc = jnp.where(tril[None], ws, 0.0).astype(BF16)
    vb = v.astype(BF16)
    dgb = dgate.astype(BF16)
    dws = [jnp.zeros((CHUNK, CHUNK), F32) for _ in range(GROUPS)]
    db = jnp.zeros((CHUNK, GROUPS), F32)
    lane_g = lax.broadcasted_iota(jnp.int32, (1, GROUPS), 1)
    rows = []
    for c in range(v.shape[0] // CHUNK):
        cols = []
        for g in range(GROUPS):
            sl = (slice(c * CHUNK, (c + 1) * CHUNK), slice(g * LANE, (g + 1) * LANE))
            cols.append(lax.dot_general(wc[g], dgb[sl], _DIMS["tn"], preferred_element_type=F32))
            dws[g] = dws[g] + lax.dot_general(dgb[sl], vb[sl], _DIMS["nt"], preferred_element_type=F32)
            db = db + jnp.sum(dgate[sl], axis=1, keepdims=True) * (lane_g == g).astype(F32)
        rows.append(jnp.concatenate(cols, axis=-1))
    dws = jnp.stack([jnp.where(tril, d, 0.0) for d in dws])
    return jnp.concatenate(rows, axis=0), dws, db


_gm_gate.defvjp(_gm_gate_fwd, _gm_gate_bwd)


def _mix_fn(a_out, zu, zv, aon_g, gon_g, vn_g, ws, b_t):
    u = jax.nn.gelu(zu)
    vv = _rms(jax.nn.gelu(zv), vn_g)
    g_out = u * _gm_gate(vv, ws, b_t)
    return jnp.concatenate([_rms(a_out, aon_g), _rms(g_out, gon_g)], axis=-1)


def _mix_ins(a_out, z, small, offs):
    gw = a_out.shape[1]
    return [(a_out, "row"), (z, ("cols", offs[0], gw)), (z, ("cols", offs[1], gw))] + [(s, "full") for s in small]


def _mix_fwd(name, a_out, z, small, offs):
    T, gw = a_out.shape
    return _rowwise(name, lambda *a: (_mix_fn(*a),), _mix_ins(a_out, z, small, offs), [((T, 2 * gw), BF16, "row")],
                    T, _pick(T, (256, 128)))[0]


def _mix_bwd(name, a_out, z, small, offs, dmixed):
    T, gw = a_out.shape

    def fn(*a):
        _, vjp = jax.vjp(_mix_fn, *a[:-1])
        return vjp(a[-1].astype(F32))

    outs = [((T, gw), F32, "row"), ((T, gw), BF16, "row"), ((T, gw), BF16, "row")] + [(s.shape, F32, "acc") for s in small]
    return _rowwise(name, fn, _mix_ins(a_out, z, small, offs) + [(dmixed, "row")], outs, T, _pick(T, (256, 128)))


def _ple_fn(gl, pe, g):
    return jax.nn.sigmoid(gl) * _rms(pe, g)


def _ple_fwd(name, h, gl, pe, g):
    T, D = h.shape
    return _rowwise(name, lambda hv, a, b, c: (hv + _ple_fn(a, b, c),),
                    [(h, "row"), (gl, "row"), (pe, "row"), (g, "full")], [((T, D), F32, "row")], T, _pick(T, (256, 128)))[0]


def _ple_bwd(name, gl, pe, g, dh, deps=()):
    T, D = gl.shape

    def fn(a, b, c, d):
        _, vjp = jax.vjp(_ple_fn, a, b, c)
        return vjp(d)

    return _rowwise(name, fn, [(gl, "row"), (pe, "row"), (g, "full"), (dh, "row")],
                    [((T, D), BF16, "row"), ((T, D), BF16, "row"), ((1, D), F32, "acc")], T, _pick(T, (256, 128)), deps)


def _loss(name, y, target):
    T, D = y.shape

    def fn(yv, tv):
        err = yv - tv
        part = 0.5 * jnp.sum(jnp.mean(err * err, axis=-1, keepdims=True), axis=0, keepdims=True)
        return err * (1.0 / D), jnp.broadcast_to(part, (8, LANE))

    return _rowwise(name, fn, [(y, "row"), (target, "row")], [((T, D), F32, "row"), ((8, LANE), F32, "acc")], T,
                    _pick(T, (512, 256, 128)))


ADAMW_BLOCK_ELEMS = 256 * 1024


def _adamw_sum(name, parts, w, m, v):
    L, R, C = w.shape
    tiles = [(r, c) for r in (R, 512, 256, 128, 64, 32, 16) for c in (C, 1024, 512, 256, 128)
             if R % r == 0 and C % c == 0 and r * c <= ADAMW_BLOCK_ELEMS]
    tr, tc = max(tiles, key=lambda rc: (rc[0] * rc[1], rc[1]))
    nr, nc = R // tr, C // tc
    c1 = 1.0 - ADAM_B1 ** ADAM_STEP
    c2 = 1.0 - ADAM_B2 ** ADAM_STEP

    def body(*refs):
        p_refs = refs[:L]
        w_ref, m_ref, v_ref, g_out, d_out, m_out, v_out = refs[L:]
        layer = pl.program_id(0)

        def part(s):
            val = p_refs[0][s].astype(F32)
            for j in range(1, L):
                val = jnp.where(layer == j, p_refs[j][s].astype(F32), val)
            return val

        g = part(0)
        for s in range(1, N_DEV):
            g = g + part(s)
        m2 = ADAM_B1 * m_ref[...] + (1.0 - ADAM_B1) * g
        v2 = ADAM_B2 * v_ref[...] + (1.0 - ADAM_B2) * (g * g)
        g_out[...] = g
        m_out[...] = m2
        v_out[...] = v2
        d_out[...] = -ADAM_LR * ((m2 / c1) / (jnp.sqrt(v2 / c2) + ADAM_EPS) + ADAM_WD * w_ref[...])

    def part_spec(j):
        def index(l, i, k):
            before, mine = l < j, l == j
            return (0, jnp.where(mine, i, jnp.where(before, 0, nr - 1)), jnp.where(mine, k, jnp.where(before, 0, nc - 1)))
        return pl.BlockSpec((N_DEV, tr, tc), index)

    blk = pl.BlockSpec((None, tr, tc), lambda l, i, k: (l, i, k))
    sd = jax.ShapeDtypeStruct((L, R, C), F32)
    return _pcall(body, name=name, grid=(L, nr, nc),
                  in_specs=[part_spec(j) for j in range(L)] + [blk, blk, blk],
                  out_specs=[blk, blk, blk, blk], out_shape=[sd, sd, sd, sd])(*parts, w, m, v)


def _unshard_cols(g):
    _, K, n = g.shape
    return g.transpose(1, 0, 2).reshape(K, N_DEV * n)


def _shard_cols(full):
    K, N = full.shape
    return full.reshape(K, N_DEV, N // N_DEV).transpose(1, 0, 2)


def _unshard_rows(g):
    _, k, N = g.shape
    return g.reshape(N_DEV * k, N)


def _shard_rows(full):
    K, N = full.shape
    return full.reshape(N_DEV, K // N_DEV, N)


STAGES = (('ffn_a_w1', 'ffn_a_w3', 'ffn_a_w2'), ('w_in', 'w_uq', 'w_ukv', 'w_out'),
          ('ffn_b_w1', 'ffn_b_w3', 'ffn_b_w2'), ('w_ple_gate', 'w_ple'))
TRANSPOSED = ('ffn_a_w1', 'ffn_a_w3', 'ffn_b_w1', 'ffn_b_w3', 'w_in', 'w_uq')


def _step(x, p, positions, target, w, m, v):
    T, D = x.shape[1], x.shape[2]
    L = p.shape[0]
    x2, target2 = x[0], target[0]
    q_rank, kv_rank = w['w_uq'].shape[1], w['w_ukv'].shape[1]
    gw = w['gm_v_norm'].shape[1]

    inv_freq = ROPE_BASE ** (-jnp.arange(0, ROPE, 2, dtype=F32) / ROPE)
    ang = positions[0].astype(F32)[:, None] * inv_freq
    cos = jnp.concatenate([jnp.ones((T, NOPE), F32), jnp.cos(ang), jnp.cos(ang)], axis=-1)
    sin = jnp.concatenate([jnp.zeros((T, NOPE), F32), jnp.sin(ang), jnp.sin(ang)], axis=-1)
    tabs = (cos, sin, _rot_matrix().astype(BF16))

    groups = [(l, names) for l in range(L) for names in STAGES]

    def ag_start(k, after):
        l, names = groups[k]
        shards = [w[n][l].astype(BF16) for n in names]
        return _xchg_start(f"ag_chips{k}", "chips", shards, [_landing(s) for s in shards], after)

    ag = {0: ag_start(0, [])}
    ag[1] = ag_start(1, [ag[0]['token']])

    def fetch(k, after):
        lands = _xchg_wait(f"ag_landed{k}", ag[k], after)
        fw = _xchg_start(f"ag_forward{k}", "forward", [], lands, [])
        deps = [fw['token']]
        for nxt in {0: (), 1: (2, 3)}.get(k, (k + 2,)):
            if nxt < len(groups):
                ag[nxt] = ag_start(nxt, deps)
                deps = [ag[nxt]['token']]
        if k == 0:
            deps = deps + [ag[1]['token']]
        return fw, deps

    def gathered(k, fw, after):
        return dict(zip(groups[k][1], _xchg_wait(f"ag_wait{k}", fw, after)))

    s0, s1, s2, s3 = q_rank, q_rank + kv_rank, q_rank + kv_rank + ROPE, q_rank + kv_rank + ROPE + gw
    o_u, o_v, o_cq, o_ckv, o_kr = 0, gw, 2 * gw, 2 * gw + q_rank, 2 * gw + q_rank + kv_rank
    kr_pad = 2 * LANE - ROPE

    def g2(name, l):
        return w[name][l][None, :]

    gm_bt = [w['gm_bs'][l].T for l in range(L)]

    saved = []
    h = x2
    for l in range(L):
        s = {}
        fw, deps = fetch(4 * l, [h])
        s['h0'] = h
        s['xn_a'] = _rms_fwd(f"rms_a{l}", h, g2('ffn_a_norm', l), deps)
        wa = s['wa'] = gathered(4 * l, fw, [s['xn_a']])
        h, s['a_h1'], s['a_h3'] = _ffn_fwd(f"ffn_a_fwd{l}", s['xn_a'], h, wa['ffn_a_w1'], wa['ffn_a_w3'], wa['ffn_a_w2'])
        fw, deps = fetch(4 * l + 1, [h])
        s['h1'] = h
        s['n'] = _rms_fwd(f"rms_mix{l}", h, g2('mix_norm', l), deps)
        wm = gathered(4 * l + 1, fw, [s['n']])
        w_in_full = _unshard_cols(wm['w_in'])
        s['w_in'] = jnp.concatenate([w_in_full[:, s2:s3], w_in_full[:, s3:], w_in_full[:, :s0], w_in_full[:, s0:s1],
                                     w_in_full[:, s1:s2], jnp.zeros((D, kr_pad), BF16)], axis=-1)
        s['w_out'] = _unshard_rows(wm['w_out'])
        s['w_uq'], s['w_ukv'] = wm['w_uq'], wm['w_ukv']
        s['z'] = _mm(f"w_in{l}", s['n'], s['w_in'], "nn")
        s['mla_small'] = [g2('q_a_norm', l), g2('kv_a_norm', l), g2('q_norm', l), g2('k_norm', l)]
        s['q'], s['k'], s['v'] = _mla_prep_fwd(f"mla_prep{l}", s['z'], tabs, s['mla_small'], s['w_uq'], s['w_ukv'],
                                               (o_cq, o_ckv, o_kr))
        s['a_out'], s['lse'] = _attn_fwd(f"attn_fwd{l}", s['q'], s['k'], s['v'])
        s['mix_small'] = [g2('attn_out_norm', l), g2('gm_out_norm', l), g2('gm_v_norm', l), w['gm_ws'][l], gm_bt[l]]
        s['mixed'] = _mix_fwd(f"mix_fwd{l}", s['a_out'], s['z'], s['mix_small'], (o_u, o_v))
        h = _mm(f"w_out{l}", s['mixed'], s['w_out'], "nn", res=h)
        fw, deps = fetch(4 * l + 2, [h])
        s['h2'] = h
        s['xn_b'] = _rms_fwd(f"rms_b{l}", h, g2('ffn_b_norm', l), deps)
        wb = s['wb'] = gathered(4 * l + 2, fw, [s['xn_b']])
        h, s['b_h1'], s['b_h3'] = _ffn_fwd(f"ffn_b_fwd{l}", s['xn_b'], h, wb['ffn_b_w1'], wb['ffn_b_w3'], wb['ffn_b_w2'])
        fw, deps = fetch(4 * l + 3, [h])
        s['h3'] = h
        s['xn_g'] = _rms_fwd(f"rms_g{l}", h, g2('ple_gate_norm', l), deps)
        wp = gathered(4 * l + 3, fw, [s['xn_g']])
        s['w_gate'] = _unshard_rows(wp['w_ple_gate'])
        s['gl'] = _mm(f"w_gate{l}", s['xn_g'], s['w_gate'], "nn")
        s['p'] = p[l, 0]
        s['pe'] = _mm(f"w_ple{l}", s['p'], _unshard_cols(wp['w_ple']), "nn")
        h = _ple_fwd(f"ple_fwd{l}", h, s['gl'], s['pe'], g2('ple_norm', l))
        saved.append(s)

    dh, loss_part = _loss("loss", h, target2)
    loss = lax.psum(loss_part[0, 0], AXES)

    gsmall = {n: [None] * L for n in SMALL}
    rs, where = {}, {}

    def rs_start(key, l, named):
        grads = [g for _, g in named]
        lands = [lax.dynamic_update_slice(lax.empty(g.shape, g.dtype),
                                          lax.dynamic_index_in_dim(g, _my_index(), 0, keepdims=True),
                                          (_my_index(),) + (0,) * (g.ndim - 1)) for g in grads]
        rs[key] = _xchg_start("rs_start_" + key, "scatter", grads, lands, [])
        where.update({(n, l): (key, i) for i, (n, _) in enumerate(named)})
        return [rs[key]['token']]

    def ffn_backward(tag, l, dh, xn, h_in, h1, h3, wts, norm_name, deps):
        pre = 'ffn_' + tag
        dxn, dhb, dh1, dh3, act = _ffn_bwd(f"{pre}_bwd{l}", dh, h1, h3, wts[pre + '_w1'], wts[pre + '_w3'],
                                           wts[pre + '_w2'], deps)
        g1 = _mm_tn_batch(f"{pre}_dw1_{l}", dh1, xn[None])
        deps = rs_start(f"{pre}_w1_{l}", l, [(pre + '_w1', g1)])
        g3 = _mm_tn_batch(f"{pre}_dw3_{l}", dh3, xn[None], deps=deps)
        deps = rs_start(f"{pre}_w3_{l}", l, [(pre + '_w3', g3)])
        g2_ = _mm_tn_batch(f"{pre}_dw2_{l}", act, dhb[None], alpha=0.5, deps=deps)
        deps = rs_start(f"{pre}_w2_{l}", l, [(pre + '_w2', g2_)])
        return _rms_bwd(f"rms_{tag}_bwd{l}", h_in, g2(norm_name, l), dxn, dh, deps)

    deps = []
    for l in reversed(range(L)):
        s = saved[l]
        d_gl, d_pe, gsmall['ple_norm'][l] = _ple_bwd(f"ple_bwd{l}", s['gl'], s['pe'], g2('ple_norm', l), dh, deps)
        g_ple = _mm(f"dw_ple{l}", s['p'], d_pe, "tn", out_dtype=BF16)
        g_gate = _mm(f"dw_gate{l}", s['xn_g'], d_gl, "tn", out_dtype=BF16)
        d_xng = _mm(f"d_xng{l}", d_gl, s['w_gate'], "nt")
        dh, gsmall['ple_gate_norm'][l] = _rms_bwd(f"rms_g_bwd{l}", s['h3'], g2('ple_gate_norm', l), d_xng, dh)
        deps = rs_start(f"ple_{l}", l, [('w_ple_gate', _shard_rows(g_gate)), ('w_ple', _shard_cols(g_ple))])
        dh, gsmall['ffn_b_norm'][l] = ffn_backward('b', l, dh, s['xn_b'], s['h2'], s['b_h1'], s['b_h3'], s['wb'],
                                                   'ffn_b_norm', deps)
        g_out = _mm(f"dw_out{l}", s['mixed'], dh, "tn", out_dtype=BF16)
        d_mixed = _mm(f"d_mixed{l}", dh, s['w_out'], "nt")
        mix = _mix_bwd(f"mix_bwd{l}", s['a_out'], s['z'], s['mix_small'], (o_u, o_v), d_mixed)
        d_a_out, d_u, d_v = mix[:3]
        gsmall['attn_out_norm'][l], gsmall['gm_out_norm'][l], gsmall['gm_v_norm'][l], gsmall['gm_ws'][l] = mix[3:7]
        gsmall['gm_bs'][l] = mix[7].T
        dq, dk, dv = _attn_bwd(f"attn_bwd{l}", s['q'], s['k'], s['v'], s['a_out'], s['lse'], d_a_out)
        mla = _mla_prep_bwd(f"mla_prep_bwd{l}", s['z'], tabs, s['mla_small'], s['w_uq'], s['w_ukv'], (o_cq, o_ckv, o_kr),
                            dq, dk, dv)
        d_cq, d_ckv, d_kr = mla[:3]
        gsmall['q_a_norm'][l], gsmall['kv_a_norm'][l], gsmall['q_norm'][l], gsmall['k_norm'][l] = mla[3:7]
        dz = jnp.concatenate([d_u, d_v, d_cq, d_ckv, d_kr, jnp.zeros((T, LANE), BF16)], axis=-1)
        g_in = _mm(f"dw_in{l}", dz, s['n'], "tn", out_dtype=BF16)
        g_in = jnp.concatenate([g_in[o_cq:o_cq + q_rank], g_in[o_ckv:o_ckv + kv_rank], g_in[o_kr:o_kr + ROPE],
                                g_in[o_u:o_u + gw], g_in[o_v:o_v + gw]], axis=0)
        d_n = _mm(f"d_n{l}", dz, s['w_in'], "nt")
        dh, gsmall['mix_norm'][l] = _rms_bwd(f"rms_mix_bwd{l}", s['h1'], g2('mix_norm', l), d_n, dh)
        deps = rs_start(f"mix_{l}", l, [('w_in', _shard_rows(g_in)), ('w_uq', mla[7].transpose(0, 2, 1).astype(BF16)),
                                        ('w_ukv', mla[8].astype(BF16)), ('w_out', _shard_rows(g_out))])
        dh, gsmall['ffn_a_norm'][l] = ffn_backward('a', l, dh, s['xn_a'], s['h0'], s['a_h1'], s['a_h3'], s['wa'],
                                                   'ffn_a_norm', deps)
        deps = []
    grad_x = dh[None]

    out = {}
    sizes = [w[n].size for n in SMALL]
    total = sum(sizes)
    padded = -(-total // (512 * LANE)) * (512 * LANE)

    def pack(d):
        flat = jnp.concatenate([d[n].reshape(-1) for n in SMALL] + [jnp.zeros((padded - total,), F32)])
        return flat.reshape(1, padded // LANE, LANE)

    gs = pack({n: jnp.stack([gsmall[n][l].reshape(w[n].shape[1:]) for l in range(L)]) for n in SMALL})
    small = _xchg_start("small_start", "gather", [gs[0]], [_landing(gs[0])], [])

    after = [dh, small['token']]
    landed = {}

    def partials(n, l):
        key, i = where[(n, l)]
        if key not in landed:
            landed[key] = _xchg_wait("rs_wait_" + key, rs[key], after)
        return landed[key][i]

    swap = lambda a: a.transpose(0, 2, 1)
    for stage in (3, 2, 1, 0):
        for n in STAGES[stage]:
            parts = [partials(n, l) for l in reversed(range(L))][::-1]
            if n in TRANSPOSED:
                out[n] = [swap(r) for r in _adamw_sum("adamw_" + n, parts, swap(w[n]), swap(m[n]), swap(v[n]))]
            else:
                out[n] = _adamw_sum("adamw_" + n, parts, w[n], m[n], v[n])
            after = [out[n][0]]

    res = _adamw_sum("adamw_small", _xchg_wait("small_wait", small, after), pack(w), pack(m), pack(v))
    off = 0
    for n, sz in zip(SMALL, sizes):
        out[n] = [r.reshape(-1)[off:off + sz].reshape(w[n].shape) for r in res]
        off += sz

    return (loss, grad_x, *[out[n][0] for n in WEIGHTS], *[out[n][1] for n in WEIGHTS],
            *[out[n][2] for n in WEIGHTS], *[out[n][3] for n in WEIGHTS])


def kernel(x, p, positions, ffn_a_norm, ffn_a_w1, ffn_a_w3, ffn_a_w2, mix_norm, w_in, q_a_norm, w_uq, kv_a_norm, w_ukv, q_norm, k_norm, gm_v_norm, gm_ws, gm_bs, attn_out_norm, gm_out_norm, w_out, ffn_b_norm, ffn_b_w1, ffn_b_w3, ffn_b_w2, ple_gate_norm, w_ple_gate, w_ple, ple_norm, loss_target, m_ffn_a_norm, m_ffn_a_w1, m_ffn_a_w3, m_ffn_a_w2, m_mix_norm, m_w_in, m_q_a_norm, m_w_uq, m_kv_a_norm, m_w_ukv, m_q_norm, m_k_norm, m_gm_v_norm, m_gm_ws, m_gm_bs, m_attn_out_norm, m_gm_out_norm, m_w_out, m_ffn_b_norm, m_ffn_b_w1, m_ffn_b_w3, m_ffn_b_w2, m_ple_gate_norm, m_w_ple_gate, m_w_ple, m_ple_norm, v_ffn_a_norm, v_ffn_a_w1, v_ffn_a_w3, v_ffn_a_w2, v_mix_norm, v_w_in, v_q_a_norm, v_w_uq, v_kv_a_norm, v_w_ukv, v_q_norm, v_k_norm, v_gm_v_norm, v_gm_ws, v_gm_bs, v_attn_out_norm, v_gm_out_norm, v_w_out, v_ffn_b_norm, v_ffn_b_w1, v_ffn_b_w3, v_ffn_b_w2, v_ple_gate_norm, v_w_ple_gate, v_w_ple, v_ple_norm):
    args = locals()
    w = {n: args[n] for n in WEIGHTS}
    m = {n: args["m_" + n] for n in WEIGHTS}
    v = {n: args["v_" + n] for n in WEIGHTS}
    return _step(x, p, positions, loss_target, w, m, v)
```

```python
import functools

import jax
import jax.numpy as jnp
from jax import lax
from jax.experimental import pallas as pl
from jax.experimental.pallas import tpu as pltpu

F32, BF16 = jnp.float32, jnp.bfloat16
EPS = 1e-6
N_DEV = 8
HEADS = 8
NOPE, ROPE, QK, VD = 128, 64, 192, 128
SCORE_SCALE = QK ** -0.5
CHUNK = 128
GROUPS = 8
LANE = 128
ROPE_BASE = 10000.0
ADAM_LR, ADAM_B1, ADAM_B2, ADAM_EPS, ADAM_WD, ADAM_STEP = 0.001, 0.9, 0.999, 1e-08, 0.01, 10
AXES = ("x", "y", "c")
MESH = pl.DeviceIdType.MESH
ANY = pl.BlockSpec(memory_space=pl.ANY)

WEIGHTS = ['ffn_a_norm', 'ffn_a_w1', 'ffn_a_w3', 'ffn_a_w2', 'mix_norm', 'w_in', 'q_a_norm', 'w_uq', 'kv_a_norm',
           'w_ukv', 'q_norm', 'k_norm', 'gm_v_norm', 'gm_ws', 'gm_bs', 'attn_out_norm', 'gm_out_norm', 'w_out',
           'ffn_b_norm', 'ffn_b_w1', 'ffn_b_w3', 'ffn_b_w2', 'ple_gate_norm', 'w_ple_gate', 'w_ple', 'ple_norm']
BIG = ['ffn_a_w1', 'ffn_a_w3', 'ffn_a_w2', 'w_in', 'w_uq', 'w_ukv', 'w_out', 'ffn_b_w1', 'ffn_b_w3', 'ffn_b_w2',
       'w_ple_gate', 'w_ple']
SMALL = [n for n in WEIGHTS if n not in BIG]


def _pcall(body, **kw):
    return pl.pallas_call(body, **kw)


def _pick(n, cands):
    for c in cands:
        if n % c == 0:
            return c
    return n


def _rms(x, g):
    return x * lax.rsqrt(jnp.mean(x * x, axis=-1, keepdims=True) + EPS) * g


@jax.custom_vjp
def _bdot(x, w):
    return jnp.dot(x.astype(BF16), w.astype(BF16), preferred_element_type=F32)


def _bdot_fwd(x, w):
    return _bdot(x, w), (x, w)


def _bdot_bwd(res, dy):
    x, w = res
    dyb = dy.astype(BF16)
    dx = lax.dot_general(dyb, w.astype(BF16), (((1,), (1,)), ((), ())), preferred_element_type=F32)
    dw = lax.dot_general(x.astype(BF16), dyb, (((0,), (0,)), ((), ())), preferred_element_type=F32)
    return dx.astype(x.dtype), dw.astype(w.dtype)


_bdot.defvjp(_bdot_fwd, _bdot_bwd)


def _split_dot(x, p, dims):
    hi = x.astype(BF16)
    lo = (x - hi.astype(F32)).astype(BF16)
    return (lax.dot_general(hi, p, dims, preferred_element_type=F32)
            + lax.dot_general(lo, p, dims, preferred_element_type=F32))


@jax.custom_vjp
def _permute(x, p):
    return _split_dot(x, p, _DIMS["nn"])


def _permute_fwd(x, p):
    return _permute(x, p), p


def _permute_bwd(p, ct):
    return _split_dot(ct, p, _DIMS["nt"]), jnp.zeros_like(p)


_permute.defvjp(_permute_fwd, _permute_bwd)


def _flip(v, bit):
    return 1 - v if bit else v


HBM = pl.BlockSpec(memory_space=pltpu.HBM)
SEM = pl.BlockSpec(memory_space=pltpu.SEMAPHORE)
EFFECT = pltpu.SideEffectType.DATAFLOW_SIDE_EFFECTING
PEERS = N_DEV - 1


def _my_index():
    return 4 * lax.axis_index("x") + 2 * lax.axis_index("y") + lax.axis_index("c")


def _landing(own):
    zone = lax.empty((N_DEV,) + own.shape, own.dtype)
    return lax.dynamic_update_slice(zone, own[None], (_my_index(),) + (0,) * own.ndim)


COPIES = {"gather": PEERS, "scatter": PEERS, "chips": 4, "forward": 3}


def _copy_plan(kind, src_refs, land_refs, send_sems, recv_sems):
    cx, cy, cc = lax.axis_index("x"), lax.axis_index("y"), lax.axis_index("c")
    me = 4 * cx + 2 * cy + cc
    per = COPIES[kind]
    out = []
    for t, land in enumerate(land_refs):
        def pair(i, src, to_slot, from_slot, dev):
            kw = dict(send_sem=send_sems.at[per * t + i], recv_sem=recv_sems.at[per * t + i], device_id=dev,
                      device_id_type=MESH)
            out.append((pltpu.make_async_remote_copy(src_ref=src, dst_ref=land.at[to_slot], **kw),
                        pltpu.make_async_remote_copy(src_ref=src, dst_ref=land.at[from_slot], **kw)))

        if kind in ("gather", "scatter"):
            for k in range(1, N_DEV):
                px, py, pc = _flip(cx, k & 4), _flip(cy, k & 2), _flip(cc, k & 1)
                peer = 4 * px + 2 * py + pc
                pair(k - 1, src_refs[t].at[peer] if kind == "scatter" else src_refs[t], me, peer, (px, py, pc))
        elif kind == "chips":
            pair(0, src_refs[t], me, me + 1 - 2 * cc, (cx, cy, 1 - cc))
            for j in range(1, 4):
                px, py = _flip(cx, j & 2), _flip(cy, j & 1)
                pair(j, src_refs[t], me, 4 * px + 2 * py + cc, (px, py, cc))
        else:
            for j in range(1, 4):
                px, py = _flip(cx, j & 2), _flip(cy, j & 1)
                mine, theirs = 4 * px + 2 * py + cc, 4 * px + 2 * py + 1 - cc
                pair(j - 1, land.at[mine], mine, theirs, (cx, cy, 1 - cc))
    return out


def _xchg_start(name, kind, srcs, lands, after):
    ns, nb, na = len(srcs), len(srcs) + len(lands), len(after)
    n_sems = COPIES[kind] * len(lands)

    def body(*refs):
        send_sems, recv_sems = refs[nb + na], refs[nb + na + 1]
        for send, _ in _copy_plan(kind, refs[:ns], refs[ns:nb], send_sems, recv_sems):
            send.start()
        refs[-1][...] = jnp.zeros_like(refs[-1])

    bufs = list(srcs) + list(lands)
    res = _pcall(
        body, name=name,
        out_shape=(pltpu.SemaphoreType.DMA((n_sems,)), pltpu.SemaphoreType.DMA((n_sems,)),
                   *[pltpu.HBM(a.shape, a.dtype) for a in bufs], jax.ShapeDtypeStruct((8, LANE), F32)),
        in_specs=[HBM] * nb + [ANY] * na,
        out_specs=(SEM, SEM, *([HBM] * nb), pl.BlockSpec(memory_space=pltpu.VMEM)),
        input_output_aliases={i: 2 + i for i in range(nb)},
        compiler_params=pltpu.CompilerParams(has_side_effects=EFFECT),
    )(*[pltpu.with_memory_space_constraint(a, pltpu.HBM) for a in bufs], *after)
    return dict(kind=kind, send=res[0], recv=res[1], srcs=list(res[2:2 + ns]), lands=list(res[2 + ns:2 + nb]),
                token=res[-1])


def _xchg_wait(name, st, after):
    ns, nb = len(st['srcs']), len(st['srcs']) + len(st['lands'])

    def body(*refs):
        for _, back in _copy_plan(st['kind'], refs[:ns], refs[ns:nb], refs[nb], refs[nb + 1]):
            back.wait_send()
            back.wait_recv()

    bufs = st['srcs'] + st['lands']
    res = _pcall(
        body, name=name, out_shape=tuple(pltpu.HBM(a.shape, a.dtype) for a in bufs),
        in_specs=[HBM] * nb + [SEM, SEM] + [ANY] * len(after), out_specs=tuple([HBM] * nb),
        input_output_aliases={i: i for i in range(nb)},
        compiler_params=pltpu.CompilerParams(has_side_effects=EFFECT),
    )(*bufs, st['send'], st['recv'], *after)
    return list(res[ns:])


_DIMS = {"nn": (((1,), (0,)), ((), ())), "nt": (((1,), (1,)), ((), ())), "tn": (((0,), (0,)), ((), ()))}


MM_OPERAND_BYTES = 16 * 1024 * 1024


def _contraction_tile(K, bytes_per_k):
    fits = [c for c in (K, 2048, 1024, 512, 256, 128) if K % c == 0 and c * bytes_per_k <= MM_OPERAND_BYTES]
    return fits[0] if fits else _pick(K, (128,))


def _mm(name, a, b, mode, out_dtype=F32, res=None, alpha=1.0, deps=()):
    if mode == "tn":
        K, M = a.shape
        N = b.shape[1]
    else:
        M, K = a.shape
        N = b.shape[0] if mode == "nt" else b.shape[1]
    tn = _pick(N, (1024, 512, 256))
    tm = _pick(M, (1024, 512, 256, 128) if tn <= 1024 else (512, 256, 128))
    tk = _contraction_tile(K, tm * a.dtype.itemsize + tn * b.dtype.itemsize)
    nk = K // tk
    a_spec = pl.BlockSpec((tk, tm), lambda i, j, k: (k, i)) if mode == "tn" else pl.BlockSpec((tm, tk), lambda i, j, k: (i, k))
    b_spec = pl.BlockSpec((tn, tk), lambda i, j, k: (j, k)) if mode == "nt" else pl.BlockSpec((tk, tn), lambda i, j, k: (k, j))
    o_spec = pl.BlockSpec((tm, tn), lambda i, j, k: (i, j))
    dims = _DIMS[mode]

    def body(*refs):
        a_ref, b_ref, r_ref = refs[0], refs[1], refs[2]
        part = lax.dot_general(a_ref[...].astype(BF16), b_ref[...].astype(BF16), dims, preferred_element_type=F32)

        def finish(o_ref, r):
            r = r * alpha if alpha != 1.0 else r
            if res is not None:
                r = r_ref[...] + r
            o_ref[...] = r.astype(o_ref.dtype)

        if nk == 1:
            finish(refs[-1], part)
            return
        o_ref, acc = refs[-2], refs[-1]
        k = pl.program_id(2)

        @pl.when(k == 0)
        def _():
            acc[...] = part

        @pl.when(k > 0)
        def _():
            acc[...] += part

        @pl.when(k == nk - 1)
        def _():
            finish(o_ref, acc[...])

    ins = [a, b] + ([] if res is None else [res]) + list(deps)
    specs = [a_spec, b_spec] + ([] if res is None else [o_spec]) + [ANY] * len(deps)
    return _pcall(body, name=name, grid=(M // tm, N // tn, nk), in_specs=specs, out_specs=o_spec,
                  out_shape=jax.ShapeDtypeStruct((M, N), out_dtype),
                  scratch_shapes=[] if nk == 1 else [pltpu.VMEM((tm, tn), F32)])(*ins)


def _mm_tn_batch(name, a3, b3, alpha=1.0, deps=()):
    ga, T, M = a3.shape
    gb, _, N = b3.shape
    G = max(ga, gb)
    tm = _pick(M, (1024, 512)) if N <= 1024 else M
    tk = _contraction_tile(T, tm * a3.dtype.itemsize + N * b3.dtype.itemsize)
    nk = T // tk
    a_spec = pl.BlockSpec((None, tk, tm), (lambda g, i, k: (g, k, i)) if ga > 1 else (lambda g, i, k: (0, k, i)))
    b_spec = pl.BlockSpec((None, tk, N), (lambda g, i, k: (g, k, 0)) if gb > 1 else (lambda g, i, k: (0, k, 0)))
    o_spec = pl.BlockSpec((None, tm, N), lambda g, i, k: (g, i, 0))

    def body(*refs):
        a_ref, b_ref, o_ref, acc = refs[0], refs[1], refs[-2], refs[-1]
        k = pl.program_id(2)
        part = lax.dot_general(a_ref[...].astype(BF16), b_ref[...].astype(BF16), _DIMS["tn"], preferred_element_type=F32)

        @pl.when(k == 0)
        def _():
            acc[...] = part

        @pl.when(k > 0)
        def _():
            acc[...] += part

        @pl.when(k == nk - 1)
        def _():
            o_ref[...] = (acc[...] * alpha if alpha != 1.0 else acc[...]).astype(o_ref.dtype)

    return _pcall(body, name=name, grid=(G, M // tm, nk), in_specs=[a_spec, b_spec] + [ANY] * len(deps),
                  out_specs=o_spec, out_shape=jax.ShapeDtypeStruct((G, M, N), BF16),
                  scratch_shapes=[pltpu.VMEM((tm, N), F32)])(a3, b3, *deps)


def _rowwise(name, fn, ins, outs, T, tm, deps=()):
    in_specs = []
    for arr, spec in ins:
        if spec == "row":
            in_specs.append(pl.BlockSpec((tm, arr.shape[1]), lambda i: (i, 0)))
        elif spec == "full":
            in_specs.append(pl.BlockSpec(arr.shape, lambda i, _n=arr.ndim: (0,) * _n))
        else:
            _, off, width = spec
            in_specs.append(pl.BlockSpec((tm, width), lambda i, _b=off // width: (i, _b)))
    in_specs += [ANY] * len(deps)
    out_specs, out_shapes = [], []
    for shape, dtype, spec in outs:
        out_shapes.append(jax.ShapeDtypeStruct(shape, dtype))
        if spec == "row":
            out_specs.append(pl.BlockSpec((tm, shape[1]), lambda i: (i, 0)))
        else:
            out_specs.append(pl.BlockSpec(shape, lambda i, _n=len(shape): (0,) * _n))
    n_in = len(ins)

    def body(*refs):
        res = fn(*[r[...] for r in refs[:n_in]])
        i = pl.program_id(0)
        for r, (_, _, spec), val in zip(refs[n_in + len(deps):], outs, res):
            if spec == "acc":
                @pl.when(i == 0)
                def _(r=r):
                    r[...] = jnp.zeros_like(r)
                r[...] += val.astype(r.dtype)
            else:
                r[...] = val.astype(r.dtype)

    return _pcall(body, name=name, grid=(T // tm,), in_specs=in_specs, out_specs=out_specs, out_shape=out_shapes)(
        *[a for a, _ in ins], *deps)


def _rms_fwd(name, h, g, deps=()):
    T, D = h.shape
    return _rowwise(name, lambda hv, gv: (_rms(hv, gv),), [(h, "row"), (g, "full")], [((T, D), BF16, "row")], T,
                    _pick(T, (512, 256, 128)), deps)[0]


def _rms_bwd(name, h, g, dxn, dh_in, deps=()):
    T, D = h.shape

    def fn(hv, gv, dv, dh0):
        _, vjp = jax.vjp(_rms, hv, gv)
        dh, dg = vjp(dv.astype(F32))
        return dh0 + dh, dg

    return _rowwise(name, fn, [(h, "row"), (g, "full"), (dxn, "row"), (dh_in, "row")],
                    [((T, D), F32, "row"), ((1, D), F32, "acc")], T, _pick(T, (256, 128)), deps)


def _ffn_fwd(name, xn, h, w1, w3, w2):
    T, D = xn.shape
    F8 = w1.shape[-1]
    tm = _pick(T, (512, 256, 128))
    wspec = lambda r, c: pl.BlockSpec((None, r, c), lambda i, d: (d, 0, 0))
    row = pl.BlockSpec((tm, D), lambda i, d: (i, 0))
    hid = pl.BlockSpec((None, tm, F8), lambda i, d: (d, i, 0))

    def body(xn_ref, h_ref, w1_ref, w3_ref, w2_ref, out_ref, h1_ref, h3_ref, acc):
        d = pl.program_id(1)

        @pl.when(d == 0)
        def _():
            acc[...] = jnp.zeros_like(acc)

        x = xn_ref[...]
        h1 = jnp.dot(x, w1_ref[...], preferred_element_type=F32)
        h3 = jnp.dot(x, w3_ref[...], preferred_element_type=F32)
        h1_ref[...] = h1.astype(BF16)
        h3_ref[...] = h3.astype(BF16)
        act = (h1 * jax.nn.sigmoid(h1) * h3).astype(BF16)
        acc[...] += jnp.dot(act, w2_ref[...], preferred_element_type=F32)

        @pl.when(d == N_DEV - 1)
        def _():
            out_ref[...] = h_ref[...] + 0.5 * acc[...]

    return _pcall(body, name=name, grid=(T // tm, N_DEV),
                  in_specs=[row, row, wspec(D, F8), wspec(D, F8), wspec(F8, D)],
                  out_specs=[row, hid, hid],
                  out_shape=[jax.ShapeDtypeStruct((T, D), F32), jax.ShapeDtypeStruct((N_DEV, T, F8), BF16),
                             jax.ShapeDtypeStruct((N_DEV, T, F8), BF16)],
                  scratch_shapes=[pltpu.VMEM((tm, D), F32)])(xn, h, w1, w3, w2)


FFN_ROW_GROUPS = 2


def _ffn_bwd(name, dy, h1, h3, w1, w3, w2, deps=()):
    T, D = dy.shape
    F8 = w1.shape[-1]
    tm = _pick(T, (512, 256, 128))
    wspec = lambda r, c: pl.BlockSpec((None, r, c), lambda i, d: (d, 0, 0))
    row = pl.BlockSpec((tm, D), lambda i, d: (i, 0))
    hid = pl.BlockSpec((None, tm, F8), lambda i, d: (d, i, 0))

    def body(*refs):
        dy_ref, h1_ref, h3_ref, w1_ref, w3_ref, w2_ref = refs[:6]
        dxn_ref, dyb, dh1_ref, dh3_ref, act_ref, acc = refs[6 + len(deps):]
        d = pl.program_id(1)

        @pl.when(d == 0)
        def _():
            acc[...] = jnp.zeros_like(acc)
            dyb[...] = dy_ref[...].astype(BF16)

        for r in range(FFN_ROW_GROUPS):
            rows = pl.ds(r * (tm // FFN_ROW_GROUPS), tm // FFN_ROW_GROUPS)
            dact = 0.5 * lax.dot_general(dyb[rows, :], w2_ref[...], _DIMS["nt"], preferred_element_type=F32)
            h1 = h1_ref[rows, :].astype(F32)
            h3 = h3_ref[rows, :].astype(F32)
            sig = jax.nn.sigmoid(h1)
            silu = h1 * sig
            dh1 = (dact * h3 * (sig * (1.0 + h1 * (1.0 - sig)))).astype(BF16)
            dh3 = (dact * silu).astype(BF16)
            dh1_ref[rows, :] = dh1
            dh3_ref[rows, :] = dh3
            act_ref[rows, :] = (silu * h3).astype(BF16)
            acc[rows, :] += (lax.dot_general(dh1, w1_ref[...], _DIMS["nt"], preferred_element_type=F32)
                             + lax.dot_general(dh3, w3_ref[...], _DIMS["nt"], preferred_element_type=F32))

        @pl.when(d == N_DEV - 1)
        def _():
            dxn_ref[...] = acc[...]

    hshape = jax.ShapeDtypeStruct((N_DEV, T, F8), BF16)
    return _pcall(body, name=name, grid=(T // tm, N_DEV),
                  in_specs=[row, hid, hid, wspec(D, F8), wspec(D, F8), wspec(F8, D)] + [ANY] * len(deps),
                  out_specs=[row, row, hid, hid, hid],
                  out_shape=[jax.ShapeDtypeStruct((T, D), F32), jax.ShapeDtypeStruct((T, D), BF16), hshape, hshape, hshape],
                  scratch_shapes=[pltpu.VMEM((tm, D), F32)])(dy, h1, h3, w1, w3, w2, *deps)


def _rot_matrix():
    i = jnp.arange(QK)[:, None]
    j = jnp.arange(QK)[None, :]
    half = ROPE // 2
    first = (j >= NOPE) & (j < NOPE + half) & (i == j + half)
    second = (j >= NOPE + half) & (i == j - half)
    return jnp.where(first, -1.0, jnp.where(second, 1.0, 0.0)).astype(F32)


def _mla_fn(cq, ckv, kr128, cos, sin, rot, qa_g, kva_g, qn_g, kn_g, w_uq, w_ukv):
    cqn = _rms(cq, qa_g)
    ckvn = _rms(ckv, kva_g)
    kr = kr128[:, :ROPE]
    qs, ks, vs = [], [], []
    for h in range(HEADS):
        qh = _rms(_bdot(cqn, w_uq[h]), qn_g)
        qs.append((qh * cos + _permute(qh, rot) * sin) * SCORE_SCALE)
        kvh = _bdot(ckvn, w_ukv[h])
        kh = _rms(jnp.concatenate([kvh[:, :NOPE], kr], axis=-1), kn_g)
        ks.append(kh * cos + _permute(kh, rot) * sin)
        vs.append(kvh[:, NOPE:])
    return qs, ks, vs


def _mla_specs(z, tabs, small, w_uq, w_ukv, tm, offs):
    o_cq, o_ckv, o_kr = offs
    row = lambda w: pl.BlockSpec((tm, w), lambda i: (i, 0))
    col = lambda off, w: pl.BlockSpec((tm, w), lambda i: (i, off // w))
    full2 = lambda a: pl.BlockSpec(a.shape, lambda i: (0, 0))
    wsp = lambda a: pl.BlockSpec(a.shape, lambda i: (0, 0, 0))
    cq_w, ckv_w = w_uq.shape[1], w_ukv.shape[1]
    ins = [z, z, z, tabs[0], tabs[1], tabs[2]] + list(small) + [w_uq, w_ukv]
    specs = ([col(o_cq, cq_w), col(o_ckv, ckv_w), col(o_kr, LANE), row(QK), row(QK), full2(tabs[2])]
             + [full2(s) for s in small] + [wsp(w_uq), wsp(w_ukv)])
    return ins, specs


def _mla_prep_fwd(name, z, tabs, small, w_uq, w_ukv, offs):
    T = z.shape[0]
    tm = _pick(T, (512, 256, 128))
    ins, specs = _mla_specs(z, tabs, small, w_uq, w_ukv, tm, offs)
    head = lambda w: pl.BlockSpec((HEADS, tm, w), lambda i: (0, i, 0))

    def body(*refs):
        vals = [r[...] for r in refs[:12]]
        q_ref, k_ref, v_ref = refs[12:]
        qs, ks, vs = _mla_fn(*vals)
        for h in range(HEADS):
            q_ref[h] = qs[h].astype(BF16)
            k_ref[h] = ks[h].astype(BF16)
            v_ref[h] = vs[h].astype(BF16)

    return _pcall(body, name=name, grid=(T // tm,), in_specs=specs, out_specs=[head(QK), head(QK), head(VD)],
                  out_shape=[jax.ShapeDtypeStruct((HEADS, T, QK), BF16), jax.ShapeDtypeStruct((HEADS, T, QK), BF16),
                             jax.ShapeDtypeStruct((HEADS, T, VD), BF16)])(*ins)


def _mla_prep_bwd(name, z, tabs, small, w_uq, w_ukv, offs, dq, dk, dv):
    T = z.shape[0]
    tm = _pick(T, (512, 256, 128))
    ins, specs = _mla_specs(z, tabs, small, w_uq, w_ukv, tm, offs)
    head = lambda w: pl.BlockSpec((HEADS, tm, w), lambda i: (0, i, 0))
    ins += [dq, dk, dv]
    specs += [head(QK), head(QK), head(VD)]
    cq_w, ckv_w = w_uq.shape[1], w_ukv.shape[1]
    acc_shapes = [s.shape for s in small] + [w_uq.shape, w_ukv.shape]
    row_shapes = [(T, cq_w), (T, ckv_w), (T, LANE)]
    out_shape = [jax.ShapeDtypeStruct(s, BF16) for s in row_shapes] + [jax.ShapeDtypeStruct(s, F32) for s in acc_shapes]
    out_specs = ([pl.BlockSpec((tm, s[1]), lambda i: (i, 0)) for s in row_shapes]
                 + [pl.BlockSpec(s, lambda i, _n=len(s): (0,) * _n) for s in acc_shapes])

    def body(*refs):
        cq, ckv, kr128, cos, sin, rot, qa_g, kva_g, qn_g, kn_g, w_uq_v, w_ukv_v = [r[...] for r in refs[:12]]
        dq_ref, dk_ref, dv_ref = refs[12:15]
        outs = refs[15:]
        f = lambda a, b, c, g1, g2, g3, g4, wq, wkv: _mla_fn(a, b, c, cos, sin, rot, g1, g2, g3, g4, wq, wkv)
        _, vjp = jax.vjp(f, cq, ckv, kr128, qa_g, kva_g, qn_g, kn_g, w_uq_v.astype(F32), w_ukv_v.astype(F32))
        cts = ([dq_ref[h] for h in range(HEADS)], [dk_ref[h] for h in range(HEADS)], [dv_ref[h] for h in range(HEADS)])
        grads = vjp(cts)
        i = pl.program_id(0)
        for n, (r, gval) in enumerate(zip(outs, grads)):
            if n < 3:
                r[...] = gval.astype(r.dtype)
            else:
                @pl.when(i == 0)
                def _(r=r):
                    r[...] = jnp.zeros_like(r)
                r[...] += gval

    return _pcall(body, name=name, grid=(T // tm,), in_specs=specs, out_specs=out_specs, out_shape=out_shape)(*ins)


NEG = -1e30


ATTN_ROW_GROUPS = 2


def _tri_block(rows, cols, row0):
    return lax.broadcasted_iota(jnp.int32, (rows, cols), 1) <= lax.broadcasted_iota(jnp.int32, (rows, cols), 0) + row0


def _attn_tiles(T):
    t = _pick(T, (1024, 512, 256, 128))
    return t, T // t


def _attn_fwd(name, q, k, v):
    H, T, _ = q.shape
    t, n = _attn_tiles(T)

    def body(q_ref, k_ref, v_ref, o_ref, lse_ref, m_s, l_s, acc):
        qi, ki = pl.program_id(1), pl.program_id(2)

        @pl.when(ki == 0)
        def _():
            m_s[...] = jnp.full_like(m_s, NEG)
            l_s[...] = jnp.zeros_like(l_s)
            acc[...] = jnp.zeros_like(acc)

        def tile(diagonal):
            s = lax.dot_general(q_ref[...], k_ref[...], _DIMS["nt"], preferred_element_type=F32)
            if diagonal:
                s = jnp.where(_tri_block(t, t, 0), s, NEG)
            m_new = jnp.maximum(m_s[...], jnp.max(s, axis=-1, keepdims=True))
            alpha = jnp.exp(m_s[...] - m_new)
            p = jnp.exp(s - m_new)
            l_s[...] = alpha * l_s[...] + jnp.sum(p, axis=-1, keepdims=True)
            acc[...] = alpha * acc[...] + jnp.dot(p.astype(BF16), v_ref[...], preferred_element_type=F32)
            m_s[...] = m_new

        @pl.when(ki < qi)
        def _():
            tile(False)

        @pl.when(ki == qi)
        def _():
            tile(True)
            o_ref[...] = acc[...] / l_s[...]
            lse_ref[...] = m_s[...] + jnp.log(l_s[...])

    kv = lambda w: pl.BlockSpec((None, t, w), lambda h, qi, ki: (h, jnp.minimum(ki, qi), 0))
    return _pcall(body, name=name, grid=(H, n, n),
                  in_specs=[pl.BlockSpec((None, t, QK), lambda h, qi, ki: (h, qi, 0)), kv(QK), kv(VD)],
                  out_specs=[pl.BlockSpec((t, VD), lambda h, qi, ki: (qi, h)),
                             pl.BlockSpec((None, t, 1), lambda h, qi, ki: (h, qi, 0))],
                  out_shape=[jax.ShapeDtypeStruct((T, H * VD), F32), jax.ShapeDtypeStruct((H, T, 1), F32)],
                  scratch_shapes=[pltpu.VMEM((t, 1), F32), pltpu.VMEM((t, 1), F32), pltpu.VMEM((t, VD), F32)])(q, k, v)


def _attn_bwd(name, q, k, v, o, lse, do):
    H, T, _ = q.shape
    t, n = _attn_tiles(T)

    def body(q_ref, k_ref, v_ref, o_ref, lse_ref, do_ref, dq_ref, dk_ref, dv_ref, dk_acc, dv_acc):
        ki, qi = pl.program_id(1), pl.program_id(2)

        @pl.when((ki == 0) & (qi == 0))
        def _():
            dq_ref[...] = jnp.zeros_like(dq_ref)

        @pl.when(qi == 0)
        def _():
            dk_acc[...] = jnp.zeros_like(dk_acc)
            dv_acc[...] = jnp.zeros_like(dv_acc)

        def tile(diagonal):
            g = t // ATTN_ROW_GROUPS
            for r in range(ATTN_ROW_GROUPS):
                rows = pl.ds(r * g, g)
                cols = (r + 1) * g if diagonal else t
                qv, kv_, dov = q_ref[rows, :], k_ref[:cols, :], do_ref[rows, :]
                s = lax.dot_general(qv, kv_, _DIMS["nt"], preferred_element_type=F32)
                p = jnp.exp(s - lse_ref[rows, :])
                if diagonal:
                    p = jnp.where(_tri_block(g, cols, r * g), p, 0.0)
                dob = dov.astype(BF16)
                delta = jnp.sum(o_ref[rows, :] * dov, axis=-1, keepdims=True)
                dv_acc[:cols, :] += lax.dot_general(p.astype(BF16), dob, _DIMS["tn"], preferred_element_type=F32)
                dp = lax.dot_general(dob, v_ref[:cols, :], _DIMS["nt"], preferred_element_type=F32)
                ds = (p * (dp - delta)).astype(BF16)
                dq_rows = pl.ds(pl.multiple_of(qi * t + r * g, g), g)
                dq_ref[dq_rows, :] += jnp.dot(ds, kv_, preferred_element_type=F32)
                dk_acc[:cols, :] += lax.dot_general(ds, qv, _DIMS["tn"], preferred_element_type=F32)

        @pl.when(qi > ki)
        def _():
            tile(False)

        @pl.when(qi == ki)
        def _():
            tile(True)

        @pl.when(qi == n - 1)
        def _():
            dk_ref[...] = dk_acc[...]
            dv_ref[...] = dv_acc[...]

    qrow = lambda w: pl.BlockSpec((None, t, w), lambda h, ki, qi: (h, jnp.maximum(qi, ki), 0))
    krow = lambda w: pl.BlockSpec((None, t, w), lambda h, ki, qi: (h, ki, 0))
    wide = pl.BlockSpec((t, VD), lambda h, ki, qi: (jnp.maximum(qi, ki), h))
    return _pcall(body, name=name, grid=(H, n, n),
                  in_specs=[qrow(QK), krow(QK), krow(VD), wide, qrow(1), wide],
                  out_specs=[pl.BlockSpec((None, T, QK), lambda h, ki, qi: (h, 0, 0)), krow(QK), krow(VD)],
                  out_shape=[jax.ShapeDtypeStruct((H, T, QK), F32), jax.ShapeDtypeStruct((H, T, QK), F32),
                             jax.ShapeDtypeStruct((H, T, VD), F32)],
                  scratch_shapes=[pltpu.VMEM((t, QK), F32), pltpu.VMEM((t, VD), F32)])(q, k, v, o, lse, do)


def _tril():
    return lax.broadcasted_iota(jnp.int32, (CHUNK, CHUNK), 1) <= lax.broadcasted_iota(jnp.int32, (CHUNK, CHUNK), 0)


@jax.custom_vjp
def _gm_gate(v, ws, b_t):
    wc = jnp.where(_tril()[None], ws, 0.0).astype(BF16)
    vb = v.astype(BF16)
    rows = []
    for c in range(v.shape[0] // CHUNK):
        cols = []
        for g in range(GROUPS):
            vc = vb[c * CHUNK:(c + 1) * CHUNK, g * LANE:(g + 1) * LANE]
            cols.append(jnp.dot(wc[g], vc, preferred_element_type=F32) + jnp.broadcast_to(b_t[:, g:g + 1], (CHUNK, LANE)))
        rows.append(jnp.concatenate(cols, axis=-1))
    return jnp.concatenate(rows, axis=0)


def _gm_gate_fwd(v, ws, b_t):
    return _gm_gate(v, ws, b_t), (v, ws)


def _gm_gate_bwd(res, dgate):
    v, ws = res
    tril = _tril()
    wc = jnp.where(tril[None], ws, 0.0).astype(BF16)
    vb = v.astype(BF16)
    dgb = dgate.astype(BF16)
    dws = [jnp.zeros((CHUNK, CHUNK), F32) for _ in range(GROUPS)]
    db = jnp.zeros((CHUNK, GROUPS), F32)
    lane_g = lax.broadcasted_iota(jnp.int32, (1, GROUPS), 1)
    rows = []
    for c in range(v.shape[0] // CHUNK):
        cols = []
        for g in range(GROUPS):
            sl = (slice(c * CHUNK, (c + 1) * CHUNK), slice(g * LANE, (g + 1) * LANE))
            cols.append(lax.dot_general(wc[g], dgb[sl], _DIMS["tn"], preferred_element_type=F32))
            dws[g] = dws[g] + lax.dot_general(dgb[sl], vb[sl], _DIMS["nt"], preferred_element_type=F32)
            db = db + jnp.sum(dgate[sl], axis=1, keepdims=True) * (lane_g == g).astype(F32)
        rows.append(jnp.concatenate(cols, axis=-1))
    dws = jnp.stack([jnp.where(tril, d, 0.0) for d in dws])
    return jnp.concatenate(rows, axis=0), dws, db


_gm_gate.defvjp(_gm_gate_fwd, _gm_gate_bwd)


def _mix_fn(a_out, zu, zv, aon_g, gon_g, vn_g, ws, b_t):
    u = jax.nn.gelu(zu)
    vv = _rms(jax.nn.gelu(zv), vn_g)
    g_out = u * _gm_gate(vv, ws, b_t)
    return jnp.concatenate([_rms(a_out, aon_g), _rms(g_out, gon_g)], axis=-1)


def _mix_ins(a_out, z, small, offs):
    gw = a_out.shape[1]
    return [(a_out, "row"), (z, ("cols", offs[0], gw)), (z, ("cols", offs[1], gw))] + [(s, "full") for s in small]


def _mix_fwd(name, a_out, z, small, offs):
    T, gw = a_out.shape
    return _rowwise(name, lambda *a: (_mix_fn(*a),), _mix_ins(a_out, z, small, offs), [((T, 2 * gw), BF16, "row")],
                    T, _pick(T, (256, 128)))[0]


def _mix_bwd(name, a_out, z, small, offs, dmixed):
    T, gw = a_out.shape

    def fn(*a):
        _, vjp = jax.vjp(_mix_fn, *a[:-1])
        return vjp(a[-1].astype(F32))

    outs = [((T, gw), F32, "row"), ((T, gw), BF16, "row"), ((T, gw), BF16, "row")] + [(s.shape, F32, "acc") for s in small]
    return _rowwise(name, fn, _mix_ins(a_out, z, small, offs) + [(dmixed, "row")], outs, T, _pick(T, (256, 128)))


def _ple_fn(gl, pe, g):
    return jax.nn.sigmoid(gl) * _rms(pe, g)


def _ple_fwd(name, h, gl, pe, g):
    T, D = h.shape
    return _rowwise(name, lambda hv, a, b, c: (hv + _ple_fn(a, b, c),),
                    [(h, "row"), (gl, "row"), (pe, "row"), (g, "full")], [((T, D), F32, "row")], T, _pick(T, (256, 128)))[0]


def _ple_bwd(name, gl, pe, g, dh, deps=()):
    T, D = gl.shape

    def fn(a, b, c, d):
        _, vjp = jax.vjp(_ple_fn, a, b, c)
        return vjp(d)

    return _rowwise(name, fn, [(gl, "row"), (pe, "row"), (g, "full"), (dh, "row")],
                    [((T, D), BF16, "row"), ((T, D), BF16, "row"), ((1, D), F32, "acc")], T, _pick(T, (256, 128)), deps)


def _loss(name, y, target):
    T, D = y.shape

    def fn(yv, tv):
        err = yv - tv
        part = 0.5 * jnp.sum(jnp.mean(err * err, axis=-1, keepdims=True), axis=0, keepdims=True)
        return err * (1.0 / D), jnp.broadcast_to(part, (8, LANE))

    return _rowwise(name, fn, [(y, "row"), (target, "row")], [((T, D), F32, "row"), ((8, LANE), F32, "acc")], T,
                    _pick(T, (512, 256, 128)))


ADAMW_BLOCK_ELEMS = 256 * 1024


def _adamw_sum(name, parts, w, m, v):
    L, R, C = w.shape
    tiles = [(r, c) for r in (R, 512, 256, 128, 64, 32, 16) for c in (C, 1024, 512, 256, 128)
             if R % r == 0 and C % c == 0 and r * c <= ADAMW_BLOCK_ELEMS]
    tr, tc = max(tiles, key=lambda rc: (rc[0] * rc[1], rc[1]))
    nr, nc = R // tr, C // tc
    c1 = 1.0 - ADAM_B1 ** ADAM_STEP
    c2 = 1.0 - ADAM_B2 ** ADAM_STEP

    def body(*refs):
        p_refs = refs[:L]
        w_ref, m_ref, v_ref, g_out, d_out, m_out, v_out = refs[L:]
        layer = pl.program_id(0)

        def part(s):
            val = p_refs[0][s].astype(F32)
            for j in range(1, L):
                val = jnp.where(layer == j, p_refs[j][s].astype(F32), val)
            return val

        g = part(0)
        for s in range(1, N_DEV):
            g = g + part(s)
        m2 = ADAM_B1 * m_ref[...] + (1.0 - ADAM_B1) * g
        v2 = ADAM_B2 * v_ref[...] + (1.0 - ADAM_B2) * (g * g)
        g_out[...] = g
        m_out[...] = m2
        v_out[...] = v2
        d_out[...] = -ADAM_LR * ((m2 / c1) / (jnp.sqrt(v2 / c2) + ADAM_EPS) + ADAM_WD * w_ref[...])

    def part_spec(j):
        def index(l, i, k):
            before, mine = l < j, l == j
            return (0, jnp.where(mine, i, jnp.where(before, 0, nr - 1)), jnp.where(mine, k, jnp.where(before, 0, nc - 1)))
        return pl.BlockSpec((N_DEV, tr, tc), index)

    blk = pl.BlockSpec((None, tr, tc), lambda l, i, k: (l, i, k))
    sd = jax.ShapeDtypeStruct((L, R, C), F32)
    return _pcall(body, name=name, grid=(L, nr, nc),
                  in_specs=[part_spec(j) for j in range(L)] + [blk, blk, blk],
                  out_specs=[blk, blk, blk, blk], out_shape=[sd, sd, sd, sd])(*parts, w, m, v)


def _unshard_cols(g):
    _, K, n = g.shape
    return g.transpose(1, 0, 2).reshape(K, N_DEV * n)


def _shard_cols(full):
    K, N = full.shape
    return full.reshape(K, N_DEV, N // N_DEV).transpose(1, 0, 2)


def _unshard_rows(g):
    _, k, N = g.shape
    return g.reshape(N_DEV * k, N)


def _shard_rows(full):
    K, N = full.shape
    return full.reshape(N_DEV, K // N_DEV, N)


STAGES = (('ffn_a_w1', 'ffn_a_w3', 'ffn_a_w2'), ('w_in', 'w_uq', 'w_ukv', 'w_out'),
          ('ffn_b_w1', 'ffn_b_w3', 'ffn_b_w2'), ('w_ple_gate', 'w_ple'))
TRANSPOSED = ('ffn_a_w1', 'ffn_a_w3', 'ffn_b_w1', 'ffn_b_w3', 'w_in', 'w_uq')


def _step(x, p, positions, target, w, m, v):
    T, D = x.shape[1], x.shape[2]
    L = p.shape[0]
    x2, target2 = x[0], target[0]
    q_rank, kv_rank = w['w_uq'].shape[1], w['w_ukv'].shape[1]
    gw = w['gm_v_norm'].shape[1]

    inv_freq = ROPE_BASE ** (-jnp.arange(0, ROPE, 2, dtype=F32) / ROPE)
    ang = positions[0].astype(F32)[:, None] * inv_freq
    cos = jnp.concatenate([jnp.ones((T, NOPE), F32), jnp.cos(ang), jnp.cos(ang)], axis=-1)
    sin = jnp.concatenate([jnp.zeros((T, NOPE), F32), jnp.sin(ang), jnp.sin(ang)], axis=-1)
    tabs = (cos, sin, _rot_matrix().astype(BF16))

    groups = [(l, names) for l in range(L) for names in STAGES]

    def ag_start(k, after):
        l, names = groups[k]
        shards = [w[n][l].astype(BF16) for n in names]
        return _xchg_start(f"ag_chips{k}", "chips", shards, [_landing(s) for s in shards], after)

    ag = {0: ag_start(0, [])}
    ag[1] = ag_start(1, [ag[0]['token']])

    def fetch(k, after):
        lands = _xchg_wait(f"ag_landed{k}", ag[k], after)
        fw = _xchg_start(f"ag_forward{k}", "forward", [], lands, [])
        deps = [fw['token']]
        for nxt in {0: (), 1: (2, 3)}.get(k, (k + 2,)):
            if nxt < len(groups):
                ag[nxt] = ag_start(nxt, deps)
                deps = [ag[nxt]['token']]
        if k == 0:
            deps = deps + [ag[1]['token']]
        return fw, deps

    def gathered(k, fw, after):
        return dict(zip(groups[k][1], _xchg_wait(f"ag_wait{k}", fw, after)))

    s0, s1, s2, s3 = q_rank, q_rank + kv_rank, q_rank + kv_rank + ROPE, q_rank + kv_rank + ROPE + gw
    o_u, o_v, o_cq, o_ckv, o_kr = 0, gw, 2 * gw, 2 * gw + q_rank, 2 * gw + q_rank + kv_rank
    kr_pad = 2 * LANE - ROPE

    def g2(name, l):
        return w[name][l][None, :]

    gm_bt = [w['gm_bs'][l].T for l in range(L)]

    saved = []
    h = x2
    for l in range(L):
        s = {}
        fw, deps = fetch(4 * l, [h])
        s['h0'] = h
        s['xn_a'] = _rms_fwd(f"rms_a{l}", h, g2('ffn_a_norm', l), deps)
        wa = s['wa'] = gathered(4 * l, fw, [s['xn_a']])
        h, s['a_h1'], s['a_h3'] = _ffn_fwd(f"ffn_a_fwd{l}", s['xn_a'], h, wa['ffn_a_w1'], wa['ffn_a_w3'], wa['ffn_a_w2'])
        fw, deps = fetch(4 * l + 1, [h])
        s['h1'] = h
        s['n'] = _rms_fwd(f"rms_mix{l}", h, g2('mix_norm', l), deps)
        wm = gathered(4 * l + 1, fw, [s['n']])
        w_in_full = _unshard_cols(wm['w_in'])
        s['w_in'] = jnp.concatenate([w_in_full[:, s2:s3], w_in_full[:, s3:], w_in_full[:, :s0], w_in_full[:, s0:s1],
                                     w_in_full[:, s1:s2], jnp.zeros((D, kr_pad), BF16)], axis=-1)
        s['w_out'] = _unshard_rows(wm['w_out'])
        s['w_uq'], s['w_ukv'] = wm['w_uq'], wm['w_ukv']
        s['z'] = _mm(f"w_in{l}", s['n'], s['w_in'], "nn")
        s['mla_small'] = [g2('q_a_norm', l), g2('kv_a_norm', l), g2('q_norm', l), g2('k_norm', l)]
        s['q'], s['k'], s['v'] = _mla_prep_fwd(f"mla_prep{l}", s['z'], tabs, s['mla_small'], s['w_uq'], s['w_ukv'],
                                               (o_cq, o_ckv, o_kr))
        s['a_out'], s['lse'] = _attn_fwd(f"attn_fwd{l}", s['q'], s['k'], s['v'])
        s['mix_small'] = [g2('attn_out_norm', l), g2('gm_out_norm', l), g2('gm_v_norm', l), w['gm_ws'][l], gm_bt[l]]
        s['mixed'] = _mix_fwd(f"mix_fwd{l}", s['a_out'], s['z'], s['mix_small'], (o_u, o_v))
        h = _mm(f"w_out{l}", s['mixed'], s['w_out'], "nn", res=h)
        fw, deps = fetch(4 * l + 2, [h])
        s['h2'] = h
        s['xn_b'] = _rms_fwd(f"rms_b{l}", h, g2('ffn_b_norm', l), deps)
        wb = s['wb'] = gathered(4 * l + 2, fw, [s['xn_b']])
        h, s['b_h1'], s['b_h3'] = _ffn_fwd(f"ffn_b_fwd{l}", s['xn_b'], h, wb['ffn_b_w1'], wb['ffn_b_w3'], wb['ffn_b_w2'])
        fw, deps = fetch(4 * l + 3, [h])
        s['h3'] = h
        s['xn_g'] = _rms_fwd(f"rms_g{l}", h, g2('ple_gate_norm', l), deps)
        wp = gathered(4 * l + 3, fw, [s['xn_g']])
        s['w_gate'] = _unshard_rows(wp['w_ple_gate'])
        s['gl'] = _mm(f"w_gate{l}", s['xn_g'], s['w_gate'], "nn")
        s['p'] = p[l, 0]
        s['pe'] = _mm(f"w_ple{l}", s['p'], _unshard_cols(wp['w_ple']), "nn")
        h = _ple_fwd(f"ple_fwd{l}", h, s['gl'], s['pe'], g2('ple_norm', l))
        saved.append(s)

    dh, loss_part = _loss("loss", h, target2)
    loss = lax.psum(loss_part[0, 0], AXES)

    gsmall = {n: [None] * L for n in SMALL}
    rs, where = {}, {}

    def rs_start(key, l, named):
        grads = [g for _, g in named]
        lands = [lax.dynamic_update_slice(lax.empty(g.shape, g.dtype),
                                          lax.dynamic_index_in_dim(g, _my_index(), 0, keepdims=True),
                                          (_my_index(),) + (0,) * (g.ndim - 1)) for g in grads]
        rs[key] = _xchg_start("rs_start_" + key, "scatter", grads, lands, [])
        where.update({(n, l): (key, i) for i, (n, _) in enumerate(named)})
        return [rs[key]['token']]

    def ffn_backward(tag, l, dh, xn, h_in, h1, h3, wts, norm_name, deps):
        pre = 'ffn_' + tag
        dxn, dhb, dh1, dh3, act = _ffn_bwd(f"{pre}_bwd{l}", dh, h1, h3, wts[pre + '_w1'], wts[pre + '_w3'],
                                           wts[pre + '_w2'], deps)
        g1 = _mm_tn_batch(f"{pre}_dw1_{l}", dh1, xn[None])
        deps = rs_start(f"{pre}_w1_{l}", l, [(pre + '_w1', g1)])
        g3 = _mm_tn_batch(f"{pre}_dw3_{l}", dh3, xn[None], deps=deps)
        deps = rs_start(f"{pre}_w3_{l}", l, [(pre + '_w3', g3)])
        g2_ = _mm_tn_batch(f"{pre}_dw2_{l}", act, dhb[None], alpha=0.5, deps=deps)
        deps = rs_start(f"{pre}_w2_{l}", l, [(pre + '_w2', g2_)])
        return _rms_bwd(f"rms_{tag}_bwd{l}", h_in, g2(norm_name, l), dxn, dh, deps)

    deps = []
    for l in reversed(range(L)):
        s = saved[l]
        d_gl, d_pe, gsmall['ple_norm'][l] = _ple_bwd(f"ple_bwd{l}", s['gl'], s['pe'], g2('ple_norm', l), dh, deps)
        g_ple = _mm(f"dw_ple{l}", s['p'], d_pe, "tn", out_dtype=BF16)
        g_gate = _mm(f"dw_gate{l}", s['xn_g'], d_gl, "tn", out_dtype=BF16)
        d_xng = _mm(f"d_xng{l}", d_gl, s['w_gate'], "nt")
        dh, gsmall['ple_gate_norm'][l] = _rms_bwd(f"rms_g_bwd{l}", s['h3'], g2('ple_gate_norm', l), d_xng, dh)
        deps = rs_start(f"ple_{l}", l, [('w_ple_gate', _shard_rows(g_gate)), ('w_ple', _shard_cols(g_ple))])
        dh, gsmall['ffn_b_norm'][l] = ffn_backward('b', l, dh, s['xn_b'], s['h2'], s['b_h1'], s['b_h3'], s['wb'],
                                                   'ffn_b_norm', deps)
        g_out = _mm(f"dw_out{l}", s['mixed'], dh, "tn", out_dtype=BF16)
        d_mixed = _mm(f"d_mixed{l}", dh, s['w_out'], "nt")
        mix = _mix_bwd(f"mix_bwd{l}", s['a_out'], s['z'], s['mix_small'], (o_u, o_v), d_mixed)
        d_a_out, d_u, d_v = mix[:3]
        gsmall['attn_out_norm'][l], gsmall['gm_out_norm'][l], gsmall['gm_v_norm'][l], gsmall['gm_ws'][l] = mix[3:7]
        gsmall['gm_bs'][l] = mix[7].T
        dq, dk, dv = _attn_bwd(f"attn_bwd{l}", s['q'], s['k'], s['v'], s['a_out'], s['lse'], d_a_out)
        mla = _mla_prep_bwd(f"mla_prep_bwd{l}", s['z'], tabs, s['mla_small'], s['w_uq'], s['w_ukv'], (o_cq, o_ckv, o_kr),
                            dq, dk, dv)
        d_cq, d_ckv, d_kr = mla[:3]
        gsmall['q_a_norm'][l], gsmall['kv_a_norm'][l], gsmall['q_norm'][l], gsmall['k_norm'][l] = mla[3:7]
        dz = jnp.concatenate([d_u, d_v, d_cq, d_ckv, d_kr, jnp.zeros((T, LANE), BF16)], axis=-1)
        g_in = _mm(f"dw_in{l}", dz, s['n'], "tn", out_dtype=BF16)
        g_in = jnp.concatenate([g_in[o_cq:o_cq + q_rank], g_in[o_ckv:o_ckv + kv_rank], g_in[o_kr:o_kr + ROPE],
                                g_in[o_u:o_u + gw], g_in[o_v:o_v + gw]], axis=0)
        d_n = _mm(f"d_n{l}", dz, s['w_in'], "nt")
        dh, gsmall['mix_norm'][l] = _rms_bwd(f"rms_mix_bwd{l}", s['h1'], g2('mix_norm', l), d_n, dh)
        deps = rs_start(f"mix_{l}", l, [('w_in', _shard_rows(g_in)), ('w_uq', mla[7].transpose(0, 2, 1).astype(BF16)),
                                        ('w_ukv', mla[8].astype(BF16)), ('w_out', _shard_rows(g_out))])
        dh, gsmall['ffn_a_norm'][l] = ffn_backward('a', l, dh, s['xn_a'], s['h0'], s['a_h1'], s['a_h3'], s['wa'],
                                                   'ffn_a_norm', deps)
        deps = []
    grad_x = dh[None]

    out = {}
    sizes = [w[n].size for n in SMALL]
    total = sum(sizes)
    padded = -(-total // (512 * LANE)) * (512 * LANE)

    def pack(d):
        flat = jnp.concatenate([d[n].reshape(-1) for n in SMALL] + [jnp.zeros((padded - total,), F32)])
        return flat.reshape(1, padded // LANE, LANE)

    gs = pack({n: jnp.stack([gsmall[n][l].reshape(w[n].shape[1:]) for l in range(L)]) for n in SMALL})
    small = _xchg_start("small_start", "gather", [gs[0]], [_landing(gs[0])], [])

    after = [dh, small['token']]
    landed = {}

    def partials(n, l):
        key, i = where[(n, l)]
        if key not in landed:
            landed[key] = _xchg_wait("rs_wait_" + key, rs[key], after)
        return landed[key][i]

    swap = lambda a: a.transpose(0, 2, 1)
    for stage in (3, 2, 1, 0):
        for n in STAGES[stage]:
            parts = [partials(n, l) for l in reversed(range(L))][::-1]
            if n in TRANSPOSED:
                out[n] = [swap(r) for r in _adamw_sum("adamw_" + n, parts, swap(w[n]), swap(m[n]), swap(v[n]))]
            else:
                out[n] = _adamw_sum("adamw_" + n, parts, w[n], m[n], v[n])
            after = [out[n][0]]

    res = _adamw_sum("adamw_small", _xchg_wait("small_wait", small, after), pack(w), pack(m), pack(v))
    off = 0
    for n, sz in zip(SMALL, sizes):
        out[n] = [r.reshape(-1)[off:off + sz].reshape(w[n].shape) for r in res]
        off += sz

    return (loss, grad_x, *[out[n][0] for n in WEIGHTS], *[out[n][1] for n in WEIGHTS],
            *[out[n][2] for n in WEIGHTS], *[out[n][3] for n in WEIGHTS])


def kernel(x, p, positions, ffn_a_norm, ffn_a_w1, ffn_a_w3, ffn_a_w2, mix_norm, w_in, q_a_norm, w_uq, kv_a_norm, w_ukv, q_norm, k_norm, gm_v_norm, gm_ws, gm_bs, attn_out_norm, gm_out_norm, w_out, ffn_b_norm, ffn_b_w1, ffn_b_w3, ffn_b_w2, ple_gate_norm, w_ple_gate, w_ple, ple_norm, loss_target, m_ffn_a_norm, m_ffn_a_w1, m_ffn_a_w3, m_ffn_a_w2, m_mix_norm, m_w_in, m_q_a_norm, m_w_uq, m_kv_a_norm, m_w_ukv, m_q_norm, m_k_norm, m_gm_v_norm, m_gm_ws, m_gm_bs, m_attn_out_norm, m_gm_out_norm, m_w_out, m_ffn_b_norm, m_ffn_b_w1, m_ffn_b_w3, m_ffn_b_w2, m_ple_gate_norm, m_w_ple_gate, m_w_ple, m_ple_norm, v_ffn_a_norm, v_ffn_a_w1, v_ffn_a_w3, v_ffn_a_w2, v_mix_norm, v_w_in, v_q_a_norm, v_w_uq, v_kv_a_norm, v_w_ukv, v_q_norm, v_k_norm, v_gm_v_norm, v_gm_ws, v_gm_bs, v_attn_out_norm, v_gm_out_norm, v_w_out, v_ffn_b_norm, v_ffn_b_w1, v_ffn_b_w3, v_ffn_b_w2, v_ple_gate_norm, v_w_ple_gate, v_w_ple, v_ple_norm):
    args = locals()
    w = {n: args[n] for n in WEIGHTS}
    m = {n: args["m_" + n] for n in WEIGHTS}
    v = {n: args["v_" + n] for n in WEIGHTS}
    return _step(x, p, positions, loss_target, w, m, v)
```

```python
import functools

import jax
import jax.numpy as jnp
from jax import lax
from jax.experimental import pallas as pl
from jax.experimental.pallas import tpu as pltpu

F32, BF16 = jnp.float32, jnp.bfloat16
EPS = 1e-6
N_DEV = 8
HEADS = 8
NOPE, ROPE, QK, VD = 128, 64, 192, 128
SCORE_SCALE = QK ** -0.5
CHUNK = 128
GROUPS = 8
LANE = 128
ROPE_BASE = 10000.0
ADAM_LR, ADAM_B1, ADAM_B2, ADAM_EPS, ADAM_WD, ADAM_STEP = 0.001, 0.9, 0.999, 1e-08, 0.01, 10
AXES = ("x", "y", "c")
MESH = pl.DeviceIdType.MESH
ANY = pl.BlockSpec(memory_space=pl.ANY)

WEIGHTS = ['ffn_a_norm', 'ffn_a_w1', 'ffn_a_w3', 'ffn_a_w2', 'mix_norm', 'w_in', 'q_a_norm', 'w_uq', 'kv_a_norm',
           'w_ukv', 'q_norm', 'k_norm', 'gm_v_norm', 'gm_ws', 'gm_bs', 'attn_out_norm', 'gm_out_norm', 'w_out',
           'ffn_b_norm', 'ffn_b_w1', 'ffn_b_w3', 'ffn_b_w2', 'ple_gate_norm', 'w_ple_gate', 'w_ple', 'ple_norm']
BIG = ['ffn_a_w1', 'ffn_a_w3', 'ffn_a_w2', 'w_in', 'w_uq', 'w_ukv', 'w_out', 'ffn_b_w1', 'ffn_b_w3', 'ffn_b_w2',
       'w_ple_gate', 'w_ple']
SMALL = [n for n in WEIGHTS if n not in BIG]


def _pcall(body, **kw):
    return pl.pallas_call(body, **kw)


def _pick(n, cands):
    for c in cands:
        if n % c == 0:
            return c
    return n


def _rms(x, g):
    return x * lax.rsqrt(jnp.mean(x * x, axis=-1, keepdims=True) + EPS) * g


@jax.custom_vjp
def _bdot(x, w):
    return jnp.dot(x.astype(BF16), w.astype(BF16), preferred_element_type=F32)


def _bdot_fwd(x, w):
    return _bdot(x, w), (x, w)


def _bdot_bwd(res, dy):
    x, w = res
    dyb = dy.astype(BF16)
    dx = lax.dot_general(dyb, w.astype(BF16), (((1,), (1,)), ((), ())), preferred_element_type=F32)
    dw = lax.dot_general(x.astype(BF16), dyb, (((0,), (0,)), ((), ())), preferred_element_type=F32)
    return dx.astype(x.dtype), dw.astype(w.dtype)


_bdot.defvjp(_bdot_fwd, _bdot_bwd)


def _split_dot(x, p, dims):
    hi = x.astype(BF16)
    lo = (x - hi.astype(F32)).astype(BF16)
    return (lax.dot_general(hi, p, dims, preferred_element_type=F32)
            + lax.dot_general(lo, p, dims, preferred_element_type=F32))


@jax.custom_vjp
def _permute(x, p):
    return _split_dot(x, p, _DIMS["nn"])


def _permute_fwd(x, p):
    return _permute(x, p), p


def _permute_bwd(p, ct):
    return _split_dot(ct, p, _DIMS["nt"]), jnp.zeros_like(p)


_permute.defvjp(_permute_fwd, _permute_bwd)


def _flip(v, bit):
    return 1 - v if bit else v


HBM = pl.BlockSpec(memory_space=pltpu.HBM)
SEM = pl.BlockSpec(memory_space=pltpu.SEMAPHORE)
EFFECT = pltpu.SideEffectType.DATAFLOW_SIDE_EFFECTING
PEERS = N_DEV - 1


def _my_index():
    return 4 * lax.axis_index("x") + 2 * lax.axis_index("y") + lax.axis_index("c")


def _landing(own):
    zone = lax.empty((N_DEV,) + own.shape, own.dtype)
    return lax.dynamic_update_slice(zone, own[None], (_my_index(),) + (0,) * own.ndim)


COPIES = {"gather": PEERS, "scatter": PEERS, "chips": 4, "forward": 3}


def _copy_plan(kind, src_refs, land_refs, send_sems, recv_sems):
    cx, cy, cc = lax.axis_index("x"), lax.axis_index("y"), lax.axis_index("c")
    me = 4 * cx + 2 * cy + cc
    per = COPIES[kind]
    out = []
    for t, land in enumerate(land_refs):
        def pair(i, src, to_slot, from_slot, dev):
            kw = dict(send_sem=send_sems.at[per * t + i], recv_sem=recv_sems.at[per * t + i], device_id=dev,
                      device_id_type=MESH)
            out.append((pltpu.make_async_remote_copy(src_ref=src, dst_ref=land.at[to_slot], **kw),
                        pltpu.make_async_remote_copy(src_ref=src, dst_ref=land.at[from_slot], **kw)))

        if kind in ("gather", "scatter"):
            for k in range(1, N_DEV):
                px, py, pc = _flip(cx, k & 4), _flip(cy, k & 2), _flip(cc, k & 1)
                peer = 4 * px + 2 * py + pc
                pair(k - 1, src_refs[t].at[peer] if kind == "scatter" else src_refs[t], me, peer, (px, py, pc))
        elif kind == "chips":
            pair(0, src_refs[t], me, me + 1 - 2 * cc, (cx, cy, 1 - cc))
            for j in range(1, 4):
                px, py = _flip(cx, j & 2), _flip(cy, j & 1)
                pair(j, src_refs[t], me, 4 * px + 2 * py + cc, (px, py, cc))
        else:
            for j in range(1, 4):
                px, py = _flip(cx, j & 2), _flip(cy, j & 1)
                mine, theirs = 4 * px + 2 * py + cc, 4 * px + 2 * py + 1 - cc
                pair(j - 1, land.at[mine], mine, theirs, (cx, cy, 1 - cc))
    return out


def _xchg_start(name, kind, srcs, lands, after):
    ns, nb, na = len(srcs), len(srcs) + len(lands), len(after)
    n_sems = COPIES[kind] * len(lands)

    def body(*refs):
        send_sems, recv_sems = refs[nb + na], refs[nb + na + 1]
        for send, _ in _copy_plan(kind, refs[:ns], refs[ns:nb], send_sems, recv_sems):
            send.start()
        refs[-1][...] = jnp.zeros_like(refs[-1])

    bufs = list(srcs) + list(lands)
    res = _pcall(
        body, name=name,
        out_shape=(pltpu.SemaphoreType.DMA((n_sems,)), pltpu.SemaphoreType.DMA((n_sems,)),
                   *[pltpu.HBM(a.shape, a.dtype) for a in bufs], jax.ShapeDtypeStruct((8, LANE), F32)),
        in_specs=[HBM] * nb + [ANY] * na,
        out_specs=(SEM, SEM, *([HBM] * nb), pl.BlockSpec(memory_space=pltpu.VMEM)),
        input_output_aliases={i: 2 + i for i in range(nb)},
        compiler_params=pltpu.CompilerParams(has_side_effects=EFFECT),
    )(*[pltpu.with_memory_space_constraint(a, pltpu.HBM) for a in bufs], *after)
    return dict(kind=kind, send=res[0], recv=res[1], srcs=list(res[2:2 + ns]), lands=list(res[2 + ns:2 + nb]),
                token=res[-1])


def _xchg_wait(name, st, after):
    ns, nb = len(st['srcs']), len(st['srcs']) + len(st['lands'])

    def body(*refs):
        for _, back in _copy_plan(st['kind'], refs[:ns], refs[ns:nb], refs[nb], refs[nb + 1]):
            back.wait_send()
            back.wait_recv()

    bufs = st['srcs'] + st['lands']
    res = _pcall(
        body, name=name, out_shape=tuple(pltpu.HBM(a.shape, a.dtype) for a in bufs),
        in_specs=[HBM] * nb + [SEM, SEM] + [ANY] * len(after), out_specs=tuple([HBM] * nb),
        input_output_aliases={i: i for i in range(nb)},
        compiler_params=pltpu.CompilerParams(has_side_effects=EFFECT),
    )(*bufs, st['send'], st['recv'], *after)
    return list(res[ns:])


_DIMS = {"nn": (((1,), (0,)), ((), ())), "nt": (((1,), (1,)), ((), ())), "tn": (((0,), (0,)), ((), ()))}


MM_OPERAND_BYTES = 16 * 1024 * 1024


def _contraction_tile(K, bytes_per_k):
    fits = [c for c in (K, 2048, 1024, 512, 256, 128) if K % c == 0 and c * bytes_per_k <= MM_OPERAND_BYTES]
    return fits[0] if fits else _pick(K, (128,))


def _mm(name, a, b, mode, out_dtype=F32, res=None, alpha=1.0, deps=()):
    if mode == "tn":
        K, M = a.shape
        N = b.shape[1]
    else:
        M, K = a.shape
        N = b.shape[0] if mode == "nt" else b.shape[1]
    tn = _pick(N, (1024, 512, 256))
    tm = _pick(M, (1024, 512, 256, 128) if tn <= 1024 else (512, 256, 128))
    tk = _contraction_tile(K, tm * a.dtype.itemsize + tn * b.dtype.itemsize)
    nk = K // tk
    a_spec = pl.BlockSpec((tk, tm), lambda i, j, k: (k, i)) if mode == "tn" else pl.BlockSpec((tm, tk), lambda i, j, k: (i, k))
    b_spec = pl.BlockSpec((tn, tk), lambda i, j, k: (j, k)) if mode == "nt" else pl.BlockSpec((tk, tn), lambda i, j, k: (k, j))
    o_spec = pl.BlockSpec((tm, tn), lambda i, j, k: (i, j))
    dims = _DIMS[mode]

    def body(*refs):
        a_ref, b_ref, r_ref = refs[0], refs[1], refs[2]
        part = lax.dot_general(a_ref[...].astype(BF16), b_ref[...].astype(BF16), dims, preferred_element_type=F32)

        def finish(o_ref, r):
            r = r * alpha if alpha != 1.0 else r
            if res is not None:
                r = r_ref[...] + r
            o_ref[...] = r.astype(o_ref.dtype)

        if nk == 1:
            finish(refs[-1], part)
            return
        o_ref, acc = refs[-2], refs[-1]
        k = pl.program_id(2)

        @pl.when(k == 0)
        def _():
            acc[...] = part

        @pl.when(k > 0)
        def _():
            acc[...] += part

        @pl.when(k == nk - 1)
        def _():
            finish(o_ref, acc[...])

    ins = [a, b] + ([] if res is None else [res]) + list(deps)
    specs = [a_spec, b_spec] + ([] if res is None else [o_spec]) + [ANY] * len(deps)
    return _pcall(body, name=name, grid=(M // tm, N // tn, nk), in_specs=specs, out_specs=o_spec,
                  out_shape=jax.ShapeDtypeStruct((M, N), out_dtype),
                  scratch_shapes=[] if nk == 1 else [pltpu.VMEM((tm, tn), F32)])(*ins)


def _mm_tn_batch(name, a3, b3, alpha=1.0, deps=()):
    ga, T, M = a3.shape
    gb, _, N = b3.shape
    G = max(ga, gb)
    tm = _pick(M, (1024, 512)) if N <= 1024 else M
    tk = _contraction_tile(T, tm * a3.dtype.itemsize + N * b3.dtype.itemsize)
    nk = T // tk
    a_spec = pl.BlockSpec((None, tk, tm), (lambda g, i, k: (g, k, i)) if ga > 1 else (lambda g, i, k: (0, k, i)))
    b_spec = pl.BlockSpec((None, tk, N), (lambda g, i, k: (g, k, 0)) if gb > 1 else (lambda g, i, k: (0, k, 0)))
    o_spec = pl.BlockSpec((None, tm, N), lambda g, i, k: (g, i, 0))

    def body(*refs):
        a_ref, b_ref, o_ref, acc = refs[0], refs[1], refs[-2], refs[-1]
        k = pl.program_id(2)
        part = lax.dot_general(a_ref[...].astype(BF16), b_ref[...].astype(BF16), _DIMS["tn"], preferred_element_type=F32)

        @pl.when(k == 0)
        def _():
            acc[...] = part

        @pl.when(k > 0)
        def _():
            acc[...] += part

        @pl.when(k == nk - 1)
        def _():
            o_ref[...] = (acc[...] * alpha if alpha != 1.0 else acc[...]).astype(o_ref.dtype)

    return _pcall(body, name=name, grid=(G, M // tm, nk), in_specs=[a_spec, b_spec] + [ANY] * len(deps),
                  out_specs=o_spec, out_shape=jax.ShapeDtypeStruct((G, M, N), BF16),
                  scratch_shapes=[pltpu.VMEM((tm, N), F32)])(a3, b3, *deps)


def _rowwise(name, fn, ins, outs, T, tm, deps=()):
    in_specs = []
    for arr, spec in ins:
        if spec == "row":
            in_specs.append(pl.BlockSpec((tm, arr.shape[1]), lambda i: (i, 0)))
        elif spec == "full":
            in_specs.append(pl.BlockSpec(arr.shape, lambda i, _n=arr.ndim: (0,) * _n))
        else:
            _, off, width = spec
            in_specs.append(pl.BlockSpec((tm, width), lambda i, _b=off // width: (i, _b)))
    in_specs += [ANY] * len(deps)
    out_specs, out_shapes = [], []
    for shape, dtype, spec in outs:
        out_shapes.append(jax.ShapeDtypeStruct(shape, dtype))
        if spec == "row":
            out_specs.append(pl.BlockSpec((tm, shape[1]), lambda i: (i, 0)))
        else:
            out_specs.append(pl.BlockSpec(shape, lambda i, _n=len(shape): (0,) * _n))
    n_in = len(ins)

    def body(*refs):
        res = fn(*[r[...] for r in refs[:n_in]])
        i = pl.program_id(0)
        for r, (_, _, spec), val in zip(refs[n_in + len(deps):], outs, res):
            if spec == "acc":
                @pl.when(i == 0)
                def _(r=r):
                    r[...] = jnp.zeros_like(r)
                r[...] += val.astype(r.dtype)
            else:
                r[...] = val.astype(r.dtype)

    return _pcall(body, name=name, grid=(T // tm,), in_specs=in_specs, out_specs=out_specs, out_shape=out_shapes)(
        *[a for a, _ in ins], *deps)


def _rms_fwd(name, h, g, deps=()):
    T, D = h.shape
    return _rowwise(name, lambda hv, gv: (_rms(hv, gv),), [(h, "row"), (g, "full")], [((T, D), BF16, "row")], T,
                    _pick(T, (512, 256, 128)), deps)[0]


def _rms_bwd(name, h, g, dxn, dh_in, deps=()):
    T, D = h.shape

    def fn(hv, gv, dv, dh0):
        _, vjp = jax.vjp(_rms, hv, gv)
        dh, dg = vjp(dv.astype(F32))
        return dh0 + dh, dg

    return _rowwise(name, fn, [(h, "row"), (g, "full"), (dxn, "row"), (dh_in, "row")],
                    [((T, D), F32, "row"), ((1, D), F32, "acc")], T, _pick(T, (256, 128)), deps)


def _ffn_fwd(name, xn, h, w1, w3, w2):
    T, D = xn.shape
    F8 = w1.shape[-1]
    tm = _pick(T, (512, 256, 128))
    wspec = lambda r, c: pl.BlockSpec((None, r, c), lambda i, d: (d, 0, 0))
    row = pl.BlockSpec((tm, D), lambda i, d: (i, 0))
    hid = pl.BlockSpec((None, tm, F8), lambda i, d: (d, i, 0))

    def body(xn_ref, h_ref, w1_ref, w3_ref, w2_ref, out_ref, h1_ref, h3_ref, acc):
        d = pl.program_id(1)

        @pl.when(d == 0)
        def _():
            acc[...] = jnp.zeros_like(acc)

        x = xn_ref[...]
        h1 = jnp.dot(x, w1_ref[...], preferred_element_type=F32)
        h3 = jnp.dot(x, w3_ref[...], preferred_element_type=F32)
        h1_ref[...] = h1.astype(BF16)
        h3_ref[...] = h3.astype(BF16)
        act = (h1 * jax.nn.sigmoid(h1) * h3).astype(BF16)
        acc[...] += jnp.dot(act, w2_ref[...], preferred_element_type=F32)

        @pl.when(d == N_DEV - 1)
        def _():
            out_ref[...] = h_ref[...] + 0.5 * acc[...]

    return _pcall(body, name=name, grid=(T // tm, N_DEV),
                  in_specs=[row, row, wspec(D, F8), wspec(D, F8), wspec(F8, D)],
                  out_specs=[row, hid, hid],
                  out_shape=[jax.ShapeDtypeStruct((T, D), F32), jax.ShapeDtypeStruct((N_DEV, T, F8), BF16),
                             jax.ShapeDtypeStruct((N_DEV, T, F8), BF16)],
                  scratch_shapes=[pltpu.VMEM((tm, D), F32)])(xn, h, w1, w3, w2)


FFN_ROW_GROUPS = 2


def _ffn_bwd(name, dy, h1, h3, w1, w3, w2, deps=()):
    T, D = dy.shape
    F8 = w1.shape[-1]
    tm = _pick(T, (512, 256, 128))
    wspec = lambda r, c: pl.BlockSpec((None, r, c), lambda i, d: (d, 0, 0))
    row = pl.BlockSpec((tm, D), lambda i, d: (i, 0))
    hid = pl.BlockSpec((None, tm, F8), lambda i, d: (d, i, 0))

    def body(*refs):
        dy_ref, h1_ref, h3_ref, w1_ref, w3_ref, w2_ref = refs[:6]
        dxn_ref, dyb, dh1_ref, dh3_ref, act_ref, acc = refs[6 + len(deps):]
        d = pl.program_id(1)

        @pl.when(d == 0)
        def _():
            acc[...] = jnp.zeros_like(acc)
            dyb[...] = dy_ref[...].astype(BF16)

        groups = [pl.ds(r * (tm // FFN_ROW_GROUPS), tm // FFN_ROW_GROUPS) for r in range(FFN_ROW_GROUPS)]
        dacts = [0.5 * lax.dot_general(dyb[rows, :], w2_ref[...], _DIMS["nt"], preferred_element_type=F32)
                 for rows in groups]
        for rows, dact in zip(groups, dacts):
            h1 = h1_ref[rows, :].astype(F32)
            h3 = h3_ref[rows, :].astype(F32)
            sig = jax.nn.sigmoid(h1)
            silu = h1 * sig
            dh1 = (dact * h3 * (sig * (1.0 + h1 * (1.0 - sig)))).astype(BF16)
            dh3 = (dact * silu).astype(BF16)
            dh1_ref[rows, :] = dh1
            dh3_ref[rows, :] = dh3
            act_ref[rows, :] = (silu * h3).astype(BF16)
            acc[rows, :] += (lax.dot_general(dh1, w1_ref[...], _DIMS["nt"], preferred_element_type=F32)
                             + lax.dot_general(dh3, w3_ref[...], _DIMS["nt"], preferred_element_type=F32))

        @pl.when(d == N_DEV - 1)
        def _():
            dxn_ref[...] = acc[...]

    hshape = jax.ShapeDtypeStruct((N_DEV, T, F8), BF16)
    return _pcall(body, name=name, grid=(T // tm, N_DEV),
                  in_specs=[row, hid, hid, wspec(D, F8), wspec(D, F8), wspec(F8, D)] + [ANY] * len(deps),
                  out_specs=[row, row, hid, hid, hid],
                  out_shape=[jax.ShapeDtypeStruct((T, D), F32), jax.ShapeDtypeStruct((T, D), BF16), hshape, hshape, hshape],
                  scratch_shapes=[pltpu.VMEM((tm, D), F32)])(dy, h1, h3, w1, w3, w2, *deps)


def _rot_matrix():
    i = jnp.arange(QK)[:, None]
    j = jnp.arange(QK)[None, :]
    half = ROPE // 2
    first = (j >= NOPE) & (j < NOPE + half) & (i == j + half)
    second = (j >= NOPE + half) & (i == j - half)
    return jnp.where(first, -1.0, jnp.where(second, 1.0, 0.0)).astype(F32)


def _mla_fn(cq, ckv, kr128, cos, sin, rot, qa_g, kva_g, qn_g, kn_g, w_uq, w_ukv):
    cqn = _rms(cq, qa_g)
    ckvn = _rms(ckv, kva_g)
    kr = kr128[:, :ROPE]
    qs, ks, vs = [], [], []
    for h in range(HEADS):
        qh = _rms(_bdot(cqn, w_uq[h]), qn_g)
        qs.append((qh * cos + _permute(qh, rot) * sin) * SCORE_SCALE)
        kvh = _bdot(ckvn, w_ukv[h])
        kh = _rms(jnp.concatenate([kvh[:, :NOPE], kr], axis=-1), kn_g)
        ks.append(kh * cos + _permute(kh, rot) * sin)
        vs.append(kvh[:, NOPE:])
    return qs, ks, vs


def _mla_specs(z, tabs, small, w_uq, w_ukv, tm, offs):
    o_cq, o_ckv, o_kr = offs
    row = lambda w: pl.BlockSpec((tm, w), lambda i: (i, 0))
    col = lambda off, w: pl.BlockSpec((tm, w), lambda i: (i, off // w))
    full2 = lambda a: pl.BlockSpec(a.shape, lambda i: (0, 0))
    wsp = lambda a: pl.BlockSpec(a.shape, lambda i: (0, 0, 0))
    cq_w, ckv_w = w_uq.shape[1], w_ukv.shape[1]
    ins = [z, z, z, tabs[0], tabs[1], tabs[2]] + list(small) + [w_uq, w_ukv]
    specs = ([col(o_cq, cq_w), col(o_ckv, ckv_w), col(o_kr, LANE), row(QK), row(QK), full2(tabs[2])]
             + [full2(s) for s in small] + [wsp(w_uq), wsp(w_ukv)])
    return ins, specs


def _mla_prep_fwd(name, z, tabs, small, w_uq, w_ukv, offs):
    T = z.shape[0]
    tm = _pick(T, (512, 256, 128))
    ins, specs = _mla_specs(z, tabs, small, w_uq, w_ukv, tm, offs)
    head = lambda w: pl.BlockSpec((HEADS, tm, w), lambda i: (0, i, 0))

    def body(*refs):
        vals = [r[...] for r in refs[:12]]
        q_ref, k_ref, v_ref = refs[12:]
        qs, ks, vs = _mla_fn(*vals)
        for h in range(HEADS):
            q_ref[h] = qs[h].astype(BF16)
            k_ref[h] = ks[h].astype(BF16)
            v_ref[h] = vs[h].astype(BF16)

    return _pcall(body, name=name, grid=(T // tm,), in_specs=specs, out_specs=[head(QK), head(QK), head(VD)],
                  out_shape=[jax.ShapeDtypeStruct((HEADS, T, QK), BF16), jax.ShapeDtypeStruct((HEADS, T, QK), BF16),
                             jax.ShapeDtypeStruct((HEADS, T, VD), BF16)])(*ins)


def _mla_prep_bwd(name, z, tabs, small, w_uq, w_ukv, offs, dq, dk, dv):
    T = z.shape[0]
    tm = _pick(T, (512, 256, 128))
    ins, specs = _mla_specs(z, tabs, small, w_uq, w_ukv, tm, offs)
    head = lambda w: pl.BlockSpec((HEADS, tm, w), lambda i: (0, i, 0))
    ins += [dq, dk, dv]
    specs += [head(QK), head(QK), head(VD)]
    cq_w, ckv_w = w_uq.shape[1], w_ukv.shape[1]
    acc_shapes = [s.shape for s in small] + [w_uq.shape, w_ukv.shape]
    row_shapes = [(T, cq_w), (T, ckv_w), (T, LANE)]
    out_shape = [jax.ShapeDtypeStruct(s, BF16) for s in row_shapes] + [jax.ShapeDtypeStruct(s, F32) for s in acc_shapes]
    out_specs = ([pl.BlockSpec((tm, s[1]), lambda i: (i, 0)) for s in row_shapes]
                 + [pl.BlockSpec(s, lambda i, _n=len(s): (0,) * _n) for s in acc_shapes])

    def body(*refs):
        cq, ckv, kr128, cos, sin, rot, qa_g, kva_g, qn_g, kn_g, w_uq_v, w_ukv_v = [r[...] for r in refs[:12]]
        dq_ref, dk_ref, dv_ref = refs[12:15]
        outs = refs[15:]
        f = lambda a, b, c, g1, g2, g3, g4, wq, wkv: _mla_fn(a, b, c, cos, sin, rot, g1, g2, g3, g4, wq, wkv)
        _, vjp = jax.vjp(f, cq, ckv, kr128, qa_g, kva_g, qn_g, kn_g, w_uq_v.astype(F32), w_ukv_v.astype(F32))
        cts = ([dq_ref[h] for h in range(HEADS)], [dk_ref[h] for h in range(HEADS)], [dv_ref[h] for h in range(HEADS)])
        grads = vjp(cts)
        i = pl.program_id(0)
        for n, (r, gval) in enumerate(zip(outs, grads)):
            if n < 3:
                r[...] = gval.astype(r.dtype)
            else:
                @pl.when(i == 0)
                def _(r=r):
                    r[...] = jnp.zeros_like(r)
                r[...] += gval

    return _pcall(body, name=name, grid=(T // tm,), in_specs=specs, out_specs=out_specs, out_shape=out_shape)(*ins)


NEG = -1e30


ATTN_ROW_GROUPS = 2


def _tri_block(rows, cols, row0):
    return lax.broadcasted_iota(jnp.int32, (rows, cols), 1) <= lax.broadcasted_iota(jnp.int32, (rows, cols), 0) + row0


def _attn_tiles(T):
    t = _pick(T, (1024, 512, 256, 128))
    return t, T // t


def _attn_fwd(name, q, k, v):
    H, T, _ = q.shape
    t, n = _attn_tiles(T)

    def body(q_ref, k_ref, v_ref, o_ref, lse_ref, m_s, l_s, acc):
        qi, ki = pl.program_id(1), pl.program_id(2)

        @pl.when(ki == 0)
        def _():
            m_s[...] = jnp.full_like(m_s, NEG)
            l_s[...] = jnp.zeros_like(l_s)
            acc[...] = jnp.zeros_like(acc)

        def tile(diagonal):
            s = lax.dot_general(q_ref[...], k_ref[...], _DIMS["nt"], preferred_element_type=F32)
            if diagonal:
                s = jnp.where(_tri_block(t, t, 0), s, NEG)
            m_new = jnp.maximum(m_s[...], jnp.max(s, axis=-1, keepdims=True))
            alpha = jnp.exp(m_s[...] - m_new)
            p = jnp.exp(s - m_new)
            l_s[...] = alpha * l_s[...] + jnp.sum(p, axis=-1, keepdims=True)
            acc[...] = alpha * acc[...] + jnp.dot(p.astype(BF16), v_ref[...], preferred_element_type=F32)
            m_s[...] = m_new

        @pl.when(ki < qi)
        def _():
            tile(False)

        @pl.when(ki == qi)
        def _():
            tile(True)
            o_ref[...] = acc[...] / l_s[...]
            lse_ref[...] = m_s[...] + jnp.log(l_s[...])

    kv = lambda w: pl.BlockSpec((None, t, w), lambda h, qi, ki: (h, jnp.minimum(ki, qi), 0))
    return _pcall(body, name=name, grid=(H, n, n),
                  in_specs=[pl.BlockSpec((None, t, QK), lambda h, qi, ki: (h, qi, 0)), kv(QK), kv(VD)],
                  out_specs=[pl.BlockSpec((t, VD), lambda h, qi, ki: (qi, h)),
                             pl.BlockSpec((None, t, 1), lambda h, qi, ki: (h, qi, 0))],
                  out_shape=[jax.ShapeDtypeStruct((T, H * VD), F32), jax.ShapeDtypeStruct((H, T, 1), F32)],
                  scratch_shapes=[pltpu.VMEM((t, 1), F32), pltpu.VMEM((t, 1), F32), pltpu.VMEM((t, VD), F32)])(q, k, v)


def _attn_bwd(name, q, k, v, o, lse, do):
    H, T, _ = q.shape
    t, n = _attn_tiles(T)

    def body(q_ref, k_ref, v_ref, o_ref, lse_ref, do_ref, dq_ref, dk_ref, dv_ref, dk_acc, dv_acc):
        ki, qi = pl.program_id(1), pl.program_id(2)

        @pl.when((ki == 0) & (qi == 0))
        def _():
            dq_ref[...] = jnp.zeros_like(dq_ref)

        @pl.when(qi == 0)
        def _():
            dk_acc[...] = jnp.zeros_like(dk_acc)
            dv_acc[...] = jnp.zeros_like(dv_acc)

        def tile(diagonal):
            g = t // ATTN_ROW_GROUPS
            for r in range(ATTN_ROW_GROUPS):
                rows = pl.ds(r * g, g)
                cols = (r + 1) * g if diagonal else t
                qv, kv_, dov = q_ref[rows, :], k_ref[:cols, :], do_ref[rows, :]
                s = lax.dot_general(qv, kv_, _DIMS["nt"], preferred_element_type=F32)
                p = jnp.exp(s - lse_ref[rows, :])
                if diagonal:
                    p = jnp.where(_tri_block(g, cols, r * g), p, 0.0)
                dob = dov.astype(BF16)
                delta = jnp.sum(o_ref[rows, :] * dov, axis=-1, keepdims=True)
                dv_acc[:cols, :] += lax.dot_general(p.astype(BF16), dob, _DIMS["tn"], preferred_element_type=F32)
                dp = lax.dot_general(dob, v_ref[:cols, :], _DIMS["nt"], preferred_element_type=F32)
                ds = (p * (dp - delta)).astype(BF16)
                dq_rows = pl.ds(pl.multiple_of(qi * t + r * g, g), g)
                dq_ref[dq_rows, :] += jnp.dot(ds, kv_, preferred_element_type=F32)
                dk_acc[:cols, :] += lax.dot_general(ds, qv, _DIMS["tn"], preferred_element_type=F32)

        @pl.when(qi > ki)
        def _():
            tile(False)

        @pl.when(qi == ki)
        def _():
            tile(True)

        @pl.when(qi == n - 1)
        def _():
            dk_ref[...] = dk_acc[...]
            dv_ref[...] = dv_acc[...]

    qrow = lambda w: pl.BlockSpec((None, t, w), lambda h, ki, qi: (h, jnp.maximum(qi, ki), 0))
    krow = lambda w: pl.BlockSpec((None, t, w), lambda h, ki, qi: (h, ki, 0))
    wide = pl.BlockSpec((t, VD), lambda h, ki, qi: (jnp.maximum(qi, ki), h))
    return _pcall(body, name=name, grid=(H, n, n),
                  in_specs=[qrow(QK), krow(QK), krow(VD), wide, qrow(1), wide],
                  out_specs=[pl.BlockSpec((None, T, QK), lambda h, ki, qi: (h, 0, 0)), krow(QK), krow(VD)],
                  out_shape=[jax.ShapeDtypeStruct((H, T, QK), F32), jax.ShapeDtypeStruct((H, T, QK), F32),
                             jax.ShapeDtypeStruct((H, T, VD), F32)],
                  scratch_shapes=[pltpu.VMEM((t, QK), F32), pltpu.VMEM((t, VD), F32)])(q, k, v, o, lse, do)


def _tril():
    return lax.broadcasted_iota(jnp.int32, (CHUNK, CHUNK), 1) <= lax.broadcasted_iota(jnp.int32, (CHUNK, CHUNK), 0)


@jax.custom_vjp
def _gm_gate(v, ws, b_t):
    wc = jnp.where(_tril()[None], ws, 0.0).astype(BF16)
    vb = v.astype(BF16)
    rows = []
    for c in range(v.shape[0] // CHUNK):
        cols = []
        for g in range(GROUPS):
            vc = vb[c * CHUNK:(c + 1) * CHUNK, g * LANE:(g + 1) * LANE]
            cols.append(jnp.dot(wc[g], vc, preferred_element_type=F32) + jnp.broadcast_to(b_t[:, g:g + 1], (CHUNK, LANE)))
        rows.append(jnp.concatenate(cols, axis=-1))
    return jnp.concatenate(rows, axis=0)


def _gm_gate_fwd(v, ws, b_t):
    return _gm_gate(v, ws, b_t), (v, ws)


def _gm_gate_bwd(res, dgate):
    v, ws = res
    tril = _tril()
    wc = jnp.where(tril[None], ws, 0.0).astype(BF16)
    vb = v.astype(BF16)
    dgb = dgate.astype(BF16)
    dws = [jnp.zeros((CHUNK, CHUNK), F32) for _ in range(GROUPS)]
    db = jnp.zeros((CHUNK, GROUPS), F32)
    lane_g = lax.broadcasted_iota(jnp.int32, (1, GROUPS), 1)
    rows = []
    for c in range(v.shape[0] // CHUNK):
        cols = []
        for g in range(GROUPS):
            sl = (slice(c * CHUNK, (c + 1) * CHUNK), slice(g * LANE, (g + 1) * LANE))
            cols.append(lax.dot_general(wc[g], dgb[sl], _DIMS["tn"], preferred_element_type=F32))
            dws[g] = dws[g] + lax.dot_general(dgb[sl], vb[sl], _DIMS["nt"], preferred_element_type=F32)
            db = db + jnp.sum(dgate[sl], axis=1, keepdims=True) * (lane_g == g).astype(F32)
        rows.append(jnp.concatenate(cols, axis=-1))
    dws = jnp.stack([jnp.where(tril, d, 0.0) for d in dws])
    return jnp.concatenate(rows, axis=0), dws, db


_gm_gate.defvjp(_gm_gate_fwd, _gm_gate_bwd)


def _mix_fn(a_out, zu, zv, aon_g, gon_g, vn_g, ws, b_t):
    u = jax.nn.gelu(zu)
    vv = _rms(jax.nn.gelu(zv), vn_g)
    g_out = u * _gm_gate(vv, ws, b_t)
    return jnp.concatenate([_rms(a_out, aon_g), _rms(g_out, gon_g)], axis=-1)


def _mix_ins(a_out, z, small, offs):
    gw = a_out.shape[1]
    return [(a_out, "row"), (z, ("cols", offs[0], gw)), (z, ("cols", offs[1], gw))] + [(s, "full") for s in small]


def _mix_fwd(name, a_out, z, small, offs):
    T, gw = a_out.shape
    return _rowwise(name, lambda *a: (_mix_fn(*a),), _mix_ins(a_out, z, small, offs), [((T, 2 * gw), BF16, "row")],
                    T, _pick(T, (256, 128)))[0]


def _mix_bwd(name, a_out, z, small, offs, dmixed):
    T, gw = a_out.shape

    def fn(*a):
        _, vjp = jax.vjp(_mix_fn, *a[:-1])
        return vjp(a[-1].astype(F32))

    outs = [((T, gw), F32, "row"), ((T, gw), BF16, "row"), ((T, gw), BF16, "row")] + [(s.shape, F32, "acc") for s in small]
    return _rowwise(name, fn, _mix_ins(a_out, z, small, offs) + [(dmixed, "row")], outs, T, _pick(T, (256, 128)))


def _ple_fn(gl, pe, g):
    return jax.nn.sigmoid(gl) * _rms(pe, g)


def _ple_fwd(name, h, gl, pe, g):
    T, D = h.shape
    return _rowwise(name, lambda hv, a, b, c: (hv + _ple_fn(a, b, c),),
                    [(h, "row"), (gl, "row"), (pe, "row"), (g, "full")], [((T, D), F32, "row")], T, _pick(T, (256, 128)))[0]


def _ple_bwd(name, gl, pe, g, dh, deps=()):
    T, D = gl.shape

    def fn(a, b, c, d):
        _, vjp = jax.vjp(_ple_fn, a, b, c)
        return vjp(d)

    return _rowwise(name, fn, [(gl, "row"), (pe, "row"), (g, "full"), (dh, "row")],
                    [((T, D), BF16, "row"), ((T, D), BF16, "row"), ((1, D), F32, "acc")], T, _pick(T, (256, 128)), deps)


def _loss(name, y, target):
    T, D = y.shape

    def fn(yv, tv):
        err = yv - tv
        part = 0.5 * jnp.sum(jnp.mean(err * err, axis=-1, keepdims=True), axis=0, keepdims=True)
        return err * (1.0 / D), jnp.broadcast_to(part, (8, LANE))

    return _rowwise(name, fn, [(y, "row"), (target, "row")], [((T, D), F32, "row"), ((8, LANE), F32, "acc")], T,
                    _pick(T, (512, 256, 128)))


ADAMW_BLOCK_ELEMS = 256 * 1024


def _adamw_sum(name, parts, w, m, v):
    L, R, C = w.shape
    tiles = [(r, c) for r in (R, 512, 256, 128, 64, 32, 16) for c in (C, 1024, 512, 256, 128)
             if R % r == 0 and C % c == 0 and r * c <= ADAMW_BLOCK_ELEMS]
    tr, tc = max(tiles, key=lambda rc: (rc[0] * rc[1], rc[1]))
    nr, nc = R // tr, C // tc
    c1 = 1.0 - ADAM_B1 ** ADAM_STEP
    c2 = 1.0 - ADAM_B2 ** ADAM_STEP

    def body(*refs):
        p_refs = refs[:L]
        w_ref, m_ref, v_ref, g_out, d_out, m_out, v_out = refs[L:]
        layer = pl.program_id(0)

        def part(s):
            val = p_refs[0][s].astype(F32)
            for j in range(1, L):
                val = jnp.where(layer == j, p_refs[j][s].astype(F32), val)
            return val

        g = part(0)
        for s in range(1, N_DEV):
            g = g + part(s)
        m2 = ADAM_B1 * m_ref[...] + (1.0 - ADAM_B1) * g
        v2 = ADAM_B2 * v_ref[...] + (1.0 - ADAM_B2) * (g * g)
        g_out[...] = g
        m_out[...] = m2
        v_out[...] = v2
        d_out[...] = -ADAM_LR * ((m2 / c1) / (jnp.sqrt(v2 / c2) + ADAM_EPS) + ADAM_WD * w_ref[...])

    def part_spec(j):
        def index(l, i, k):
            before, mine = l < j, l == j
            return (0, jnp.where(mine, i, jnp.where(before, 0, nr - 1)), jnp.where(mine, k, jnp.where(before, 0, nc - 1)))
        return pl.BlockSpec((N_DEV, tr, tc), index)

    blk = pl.BlockSpec((None, tr, tc), lambda l, i, k: (l, i, k))
    sd = jax.ShapeDtypeStruct((L, R, C), F32)
    return _pcall(body, name=name, grid=(L, nr, nc),
                  in_specs=[part_spec(j) for j in range(L)] + [blk, blk, blk],
                  out_specs=[blk, blk, blk, blk], out_shape=[sd, sd, sd, sd])(*parts, w, m, v)


def _unshard_cols(g):
    _, K, n = g.shape
    return g.transpose(1, 0, 2).reshape(K, N_DEV * n)


def _shard_cols(full):
    K, N = full.shape
    return full.reshape(K, N_DEV, N // N_DEV).transpose(1, 0, 2)


def _unshard_rows(g):
    _, k, N = g.shape
    return g.reshape(N_DEV * k, N)


def _shard_rows(full):
    K, N = full.shape
    return full.reshape(N_DEV, K // N_DEV, N)


STAGES = (('ffn_a_w1', 'ffn_a_w3', 'ffn_a_w2'), ('w_in', 'w_uq', 'w_ukv', 'w_out'),
          ('ffn_b_w1', 'ffn_b_w3', 'ffn_b_w2'), ('w_ple_gate', 'w_ple'))
TRANSPOSED = ('ffn_a_w1', 'ffn_a_w3', 'ffn_b_w1', 'ffn_b_w3', 'w_in', 'w_uq')


def _step(x, p, positions, target, w, m, v):
    T, D = x.shape[1], x.shape[2]
    L = p.shape[0]
    x2, target2 = x[0], target[0]
    q_rank, kv_rank = w['w_uq'].shape[1], w['w_ukv'].shape[1]
    gw = w['gm_v_norm'].shape[1]

    inv_freq = ROPE_BASE ** (-jnp.arange(0, ROPE, 2, dtype=F32) / ROPE)
    ang = positions[0].astype(F32)[:, None] * inv_freq
    cos = jnp.concatenate([jnp.ones((T, NOPE), F32), jnp.cos(ang), jnp.cos(ang)], axis=-1)
    sin = jnp.concatenate([jnp.zeros((T, NOPE), F32), jnp.sin(ang), jnp.sin(ang)], axis=-1)
    tabs = (cos, sin, _rot_matrix().astype(BF16))

    groups = [(l, names) for l in range(L) for names in STAGES]

    def ag_start(k, after):
        l, names = groups[k]
        shards = [w[n][l].astype(BF16) for n in names]
        return _xchg_start(f"ag_chips{k}", "chips", shards, [_landing(s) for s in shards], after)

    ag = {0: ag_start(0, [])}
    ag[1] = ag_start(1, [ag[0]['token']])

    def fetch(k, after):
        lands = _xchg_wait(f"ag_landed{k}", ag[k], after)
        fw = _xchg_start(f"ag_forward{k}", "forward", [], lands, [])
        deps = [fw['token']]
        for nxt in {0: (), 1: (2, 3)}.get(k, (k + 2,)):
            if nxt < len(groups):
                ag[nxt] = ag_start(nxt, deps)
                deps = [ag[nxt]['token']]
        if k == 0:
            deps = deps + [ag[1]['token']]
        return fw, deps

    def gathered(k, fw, after):
        return dict(zip(groups[k][1], _xchg_wait(f"ag_wait{k}", fw, after)))

    s0, s1, s2, s3 = q_rank, q_rank + kv_rank, q_rank + kv_rank + ROPE, q_rank + kv_rank + ROPE + gw
    o_u, o_v, o_cq, o_ckv, o_kr = 0, gw, 2 * gw, 2 * gw + q_rank, 2 * gw + q_rank + kv_rank
    kr_pad = 2 * LANE - ROPE

    def g2(name, l):
        return w[name][l][None, :]

    gm_bt = [w['gm_bs'][l].T for l in range(L)]

    saved = []
    h = x2
    for l in range(L):
        s = {}
        fw, deps = fetch(4 * l, [h])
        s['h0'] = h
        s['xn_a'] = _rms_fwd(f"rms_a{l}", h, g2('ffn_a_norm', l), deps)
        wa = s['wa'] = gathered(4 * l, fw, [s['xn_a']])
        h, s['a_h1'], s['a_h3'] = _ffn_fwd(f"ffn_a_fwd{l}", s['xn_a'], h, wa['ffn_a_w1'], wa['ffn_a_w3'], wa['ffn_a_w2'])
        fw, deps = fetch(4 * l + 1, [h])
        s['h1'] = h
        s['n'] = _rms_fwd(f"rms_mix{l}", h, g2('mix_norm', l), deps)
        wm = gathered(4 * l + 1, fw, [s['n']])
        w_in_full = _unshard_cols(wm['w_in'])
        s['w_in'] = jnp.concatenate([w_in_full[:, s2:s3], w_in_full[:, s3:], w_in_full[:, :s0], w_in_full[:, s0:s1],
                                     w_in_full[:, s1:s2], jnp.zeros((D, kr_pad), BF16)], axis=-1)
        s['w_out'] = _unshard_rows(wm['w_out'])
        s['w_uq'], s['w_ukv'] = wm['w_uq'], wm['w_ukv']
        s['z'] = _mm(f"w_in{l}", s['n'], s['w_in'], "nn")
        s['mla_small'] = [g2('q_a_norm', l), g2('kv_a_norm', l), g2('q_norm', l), g2('k_norm', l)]
        s['q'], s['k'], s['v'] = _mla_prep_fwd(f"mla_prep{l}", s['z'], tabs, s['mla_small'], s['w_uq'], s['w_ukv'],
                                               (o_cq, o_ckv, o_kr))
        s['a_out'], s['lse'] = _attn_fwd(f"attn_fwd{l}", s['q'], s['k'], s['v'])
        s['mix_small'] = [g2('attn_out_norm', l), g2('gm_out_norm', l), g2('gm_v_norm', l), w['gm_ws'][l], gm_bt[l]]
        s['mixed'] = _mix_fwd(f"mix_fwd{l}", s['a_out'], s['z'], s['mix_small'], (o_u, o_v))
        h = _mm(f"w_out{l}", s['mixed'], s['w_out'], "nn", res=h)
        fw, deps = fetch(4 * l + 2, [h])
        s['h2'] = h
        s['xn_b'] = _rms_fwd(f"rms_b{l}", h, g2('ffn_b_norm', l), deps)
        wb = s['wb'] = gathered(4 * l + 2, fw, [s['xn_b']])
        h, s['b_h1'], s['b_h3'] = _ffn_fwd(f"ffn_b_fwd{l}", s['xn_b'], h, wb['ffn_b_w1'], wb['ffn_b_w3'], wb['ffn_b_w2'])
        fw, deps = fetch(4 * l + 3, [h])
        s['h3'] = h
        s['xn_g'] = _rms_fwd(f"rms_g{l}", h, g2('ple_gate_norm', l), deps)
        wp = gathered(4 * l + 3, fw, [s['xn_g']])
        s['w_gate'] = _unshard_rows(wp['w_ple_gate'])
        s['gl'] = _mm(f"w_gate{l}", s['xn_g'], s['w_gate'], "nn")
        s['p'] = p[l, 0]
        s['pe'] = _mm(f"w_ple{l}", s['p'], _unshard_cols(wp['w_ple']), "nn")
        h = _ple_fwd(f"ple_fwd{l}", h, s['gl'], s['pe'], g2('ple_norm', l))
        saved.append(s)

    dh, loss_part = _loss("loss", h, target2)
    loss = lax.psum(loss_part[0, 0], AXES)

    gsmall = {n: [None] * L for n in SMALL}
    rs, where = {}, {}

    def rs_start(key, l, named):
        grads = [g for _, g in named]
        lands = [lax.dynamic_update_slice(lax.empty(g.shape, g.dtype),
                                          lax.dynamic_index_in_dim(g, _my_index(), 0, keepdims=True),
                                          (_my_index(),) + (0,) * (g.ndim - 1)) for g in grads]
        rs[key] = _xchg_start("rs_start_" + key, "scatter", grads, lands, [])
        where.update({(n, l): (key, i) for i, (n, _) in enumerate(named)})
        return [rs[key]['token']]

    def ffn_backward(tag, l, dh, xn, h_in, h1, h3, wts, norm_name, deps):
        pre = 'ffn_' + tag
        dxn, dhb, dh1, dh3, act = _ffn_bwd(f"{pre}_bwd{l}", dh, h1, h3, wts[pre + '_w1'], wts[pre + '_w3'],
                                           wts[pre + '_w2'], deps)
        g1 = _mm_tn_batch(f"{pre}_dw1_{l}", dh1, xn[None])
        deps = rs_start(f"{pre}_w1_{l}", l, [(pre + '_w1', g1)])
        g3 = _mm_tn_batch(f"{pre}_dw3_{l}", dh3, xn[None], deps=deps)
        deps = rs_start(f"{pre}_w3_{l}", l, [(pre + '_w3', g3)])
        g2_ = _mm_tn_batch(f"{pre}_dw2_{l}", act, dhb[None], alpha=0.5, deps=deps)
        deps = rs_start(f"{pre}_w2_{l}", l, [(pre + '_w2', g2_)])
        return _rms_bwd(f"rms_{tag}_bwd{l}", h_in, g2(norm_name, l), dxn, dh, deps)

    deps = []
    for l in reversed(range(L)):
        s = saved[l]
        d_gl, d_pe, gsmall['ple_norm'][l] = _ple_bwd(f"ple_bwd{l}", s['gl'], s['pe'], g2('ple_norm', l), dh, deps)
        g_ple = _mm(f"dw_ple{l}", s['p'], d_pe, "tn", out_dtype=BF16)
        g_gate = _mm(f"dw_gate{l}", s['xn_g'], d_gl, "tn", out_dtype=BF16)
        d_xng = _mm(f"d_xng{l}", d_gl, s['w_gate'], "nt")
        dh, gsmall['ple_gate_norm'][l] = _rms_bwd(f"rms_g_bwd{l}", s['h3'], g2('ple_gate_norm', l), d_xng, dh)
        deps = rs_start(f"ple_{l}", l, [('w_ple_gate', _shard_rows(g_gate)), ('w_ple', _shard_cols(g_ple))])
        dh, gsmall['ffn_b_norm'][l] = ffn_backward('b', l, dh, s['xn_b'], s['h2'], s['b_h1'], s['b_h3'], s['wb'],
                                                   'ffn_b_norm', deps)
        g_out = _mm(f"dw_out{l}", s['mixed'], dh, "tn", out_dtype=BF16)
        d_mixed = _mm(f"d_mixed{l}", dh, s['w_out'], "nt")
        mix = _mix_bwd(f"mix_bwd{l}", s['a_out'], s['z'], s['mix_small'], (o_u, o_v), d_mixed)
        d_a_out, d_u, d_v = mix[:3]
        gsmall['attn_out_norm'][l], gsmall['gm_out_norm'][l], gsmall['gm_v_norm'][l], gsmall['gm_ws'][l] = mix[3:7]
        gsmall['gm_bs'][l] = mix[7].T
        dq, dk, dv = _attn_bwd(f"attn_bwd{l}", s['q'], s['k'], s['v'], s['a_out'], s['lse'], d_a_out)
        mla = _mla_prep_bwd(f"mla_prep_bwd{l}", s['z'], tabs, s['mla_small'], s['w_uq'], s['w_ukv'], (o_cq, o_ckv, o_kr),
                            dq, dk, dv)
        d_cq, d_ckv, d_kr = mla[:3]
        gsmall['q_a_norm'][l], gsmall['kv_a_norm'][l], gsmall['q_norm'][l], gsmall['k_norm'][l] = mla[3:7]
        dz = jnp.concatenate([d_u, d_v, d_cq, d_ckv, d_kr, jnp.zeros((T, LANE), BF16)], axis=-1)
        g_in = _mm(f"dw_in{l}", dz, s['n'], "tn", out_dtype=BF16)
        g_in = jnp.concatenate([g_in[o_cq:o_cq + q_rank], g_in[o_ckv:o_ckv + kv_rank], g_in[o_kr:o_kr + ROPE],
                                g_in[o_u:o_u + gw], g_in[o_v:o_v + gw]], axis=0)
        d_n = _mm(f"d_n{l}", dz, s['w_in'], "nt")
        dh, gsmall['mix_norm'][l] = _rms_bwd(f"rms_mix_bwd{l}", s['h1'], g2('mix_norm', l), d_n, dh)
        deps = rs_start(f"mix_{l}", l, [('w_in', _shard_rows(g_in)), ('w_uq', mla[7].transpose(0, 2, 1).astype(BF16)),
                                        ('w_ukv', mla[8].astype(BF16)), ('w_out', _shard_rows(g_out))])
        dh, gsmall['ffn_a_norm'][l] = ffn_backward('a', l, dh, s['xn_a'], s['h0'], s['a_h1'], s['a_h3'], s['wa'],
                                                   'ffn_a_norm', deps)
        deps = []
    grad_x = dh[None]

    out = {}
    sizes = [w[n].size for n in SMALL]
    total = sum(sizes)
    padded = -(-total // (512 * LANE)) * (512 * LANE)

    def pack(d):
        flat = jnp.concatenate([d[n].reshape(-1) for n in SMALL] + [jnp.zeros((padded - total,), F32)])
        return flat.reshape(1, padded // LANE, LANE)

    gs = pack({n: jnp.stack([gsmall[n][l].reshape(w[n].shape[1:]) for l in range(L)]) for n in SMALL})
    small = _xchg_start("small_start", "gather", [gs[0]], [_landing(gs[0])], [])

    after = [dh, small['token']]
    landed = {}

    def partials(n, l):
        key, i = where[(n, l)]
        if key not in landed:
            landed[key] = _xchg_wait("rs_wait_" + key, rs[key], after)
        return landed[key][i]

    swap = lambda a: a.transpose(0, 2, 1)
    for stage in (3, 2, 1, 0):
        for n in STAGES[stage]:
            parts = [partials(n, l) for l in reversed(range(L))][::-1]
            if n in TRANSPOSED:
                out[n] = [swap(r) for r in _adamw_sum("adamw_" + n, parts, swap(w[n]), swap(m[n]), swap(v[n]))]
            else:
                out[n] = _adamw_sum("adamw_" + n, parts, w[n], m[n], v[n])
            after = [out[n][0]]

    res = _adamw_sum("adamw_small", _xchg_wait("small_wait", small, after), pack(w), pack(m), pack(v))
    off = 0
    for n, sz in zip(SMALL, sizes):
        out[n] = [r.reshape(-1)[off:off + sz].reshape(w[n].shape) for r in res]
        off += sz

    return (loss, grad_x, *[out[n][0] for n in WEIGHTS], *[out[n][1] for n in WEIGHTS],
            *[out[n][2] for n in WEIGHTS], *[out[n][3] for n in WEIGHTS])


def kernel(x, p, positions, ffn_a_norm, ffn_a_w1, ffn_a_w3, ffn_a_w2, mix_norm, w_in, q_a_norm, w_uq, kv_a_norm, w_ukv, q_norm, k_norm, gm_v_norm, gm_ws, gm_bs, attn_out_norm, gm_out_norm, w_out, ffn_b_norm, ffn_b_w1, ffn_b_w3, ffn_b_w2, ple_gate_norm, w_ple_gate, w_ple, ple_norm, loss_target, m_ffn_a_norm, m_ffn_a_w1, m_ffn_a_w3, m_ffn_a_w2, m_mix_norm, m_w_in, m_q_a_norm, m_w_uq, m_kv_a_norm, m_w_ukv, m_q_norm, m_k_norm, m_gm_v_norm, m_gm_ws, m_gm_bs, m_attn_out_norm, m_gm_out_norm, m_w_out, m_ffn_b_norm, m_ffn_b_w1, m_ffn_b_w3, m_ffn_b_w2, m_ple_gate_norm, m_w_ple_gate, m_w_ple, m_ple_norm, v_ffn_a_norm, v_ffn_a_w1, v_ffn_a_w3, v_ffn_a_w2, v_mix_norm, v_w_in, v_q_a_norm, v_w_uq, v_kv_a_norm, v_w_ukv, v_q_norm, v_k_norm, v_gm_v_norm, v_gm_ws, v_gm_bs, v_attn_out_norm, v_gm_out_norm, v_w_out, v_ffn_b_norm, v_ffn_b_w1, v_ffn_b_w3, v_ffn_b_w2, v_ple_gate_norm, v_w_ple_gate, v_w_ple, v_ple_norm):
    args = locals()
    w = {n: args[n] for n in WEIGHTS}
    m = {n: args["m_" + n] for n in WEIGHTS}
    v = {n: args["v_" + n] for n in WEIGHTS}
    return _step(x, p, positions, loss_target, w, m, v)
```

```python
import functools

import jax
import jax.numpy as jnp
from jax import lax
from jax.experimental import pallas as pl
from jax.experimental.pallas import tpu as pltpu

F32, BF16 = jnp.float32, jnp.bfloat16
EPS = 1e-6
N_DEV = 8
HEADS = 8
NOPE, ROPE, QK, VD = 128, 64, 192, 128
SCORE_SCALE = QK ** -0.5
CHUNK = 128
GROUPS = 8
LANE = 128
ROPE_BASE = 10000.0
ADAM_LR, ADAM_B1, ADAM_B2, ADAM_EPS, ADAM_WD, ADAM_STEP = 0.001, 0.9, 0.999, 1e-08, 0.01, 10
AXES = ("x", "y", "c")
MESH = pl.DeviceIdType.MESH
ANY = pl.BlockSpec(memory_space=pl.ANY)

WEIGHTS = ['ffn_a_norm', 'ffn_a_w1', 'ffn_a_w3', 'ffn_a_w2', 'mix_norm', 'w_in', 'q_a_norm', 'w_uq', 'kv_a_norm',
           'w_ukv', 'q_norm', 'k_norm', 'gm_v_norm', 'gm_ws', 'gm_bs', 'attn_out_norm', 'gm_out_norm', 'w_out',
           'ffn_b_norm', 'ffn_b_w1', 'ffn_b_w3', 'ffn_b_w2', 'ple_gate_norm', 'w_ple_gate', 'w_ple', 'ple_norm']
BIG = ['ffn_a_w1', 'ffn_a_w3', 'ffn_a_w2', 'w_in', 'w_uq', 'w_ukv', 'w_out', 'ffn_b_w1', 'ffn_b_w3', 'ffn_b_w2',
       'w_ple_gate', 'w_ple']
SMALL = [n for n in WEIGHTS if n not in BIG]


def _pcall(body, **kw):
    return pl.pallas_call(body, **kw)


def _pick(n, cands):
    for c in cands:
        if n % c == 0:
            return c
    return n


def _rms(x, g):
    return x * lax.rsqrt(jnp.mean(x * x, axis=-1, keepdims=True) + EPS) * g


@jax.custom_vjp
def _bdot(x, w):
    return jnp.dot(x.astype(BF16), w.astype(BF16), preferred_element_type=F32)


def _bdot_fwd(x, w):
    return _bdot(x, w), (x, w)


def _bdot_bwd(res, dy):
    x, w = res
    dyb = dy.astype(BF16)
    dx = lax.dot_general(dyb, w.astype(BF16), (((1,), (1,)), ((), ())), preferred_element_type=F32)
    dw = lax.dot_general(x.astype(BF16), dyb, (((0,), (0,)), ((), ())), preferred_element_type=F32)
    return dx.astype(x.dtype), dw.astype(w.dtype)


_bdot.defvjp(_bdot_fwd, _bdot_bwd)


def _split_dot(x, p, dims):
    hi = x.astype(BF16)
    lo = (x - hi.astype(F32)).astype(BF16)
    return (lax.dot_general(hi, p, dims, preferred_element_type=F32)
            + lax.dot_general(lo, p, dims, preferred_element_type=F32))


@jax.custom_vjp
def _permute(x, p):
    return _split_dot(x, p, _DIMS["nn"])


def _permute_fwd(x, p):
    return _permute(x, p), p


def _permute_bwd(p, ct):
    return _split_dot(ct, p, _DIMS["nt"]), jnp.zeros_like(p)


_permute.defvjp(_permute_fwd, _permute_bwd)


def _flip(v, bit):
    return 1 - v if bit else v


HBM = pl.BlockSpec(memory_space=pltpu.HBM)
SEM = pl.BlockSpec(memory_space=pltpu.SEMAPHORE)
EFFECT = pltpu.SideEffectType.DATAFLOW_SIDE_EFFECTING
PEERS = N_DEV - 1


def _my_index():
    return 4 * lax.axis_index("x") + 2 * lax.axis_index("y") + lax.axis_index("c")


def _landing(own):
    zone = lax.empty((N_DEV,) + own.shape, own.dtype)
    return lax.dynamic_update_slice(zone, own[None], (_my_index(),) + (0,) * own.ndim)


COPIES = {"gather": PEERS, "scatter": PEERS, "chips": 4, "forward": 3}


def _copy_plan(kind, src_refs, land_refs, send_sems, recv_sems):
    cx, cy, cc = lax.axis_index("x"), lax.axis_index("y"), lax.axis_index("c")
    me = 4 * cx + 2 * cy + cc
    per = COPIES[kind]
    out = []
    for t, land in enumerate(land_refs):
        def pair(i, src, to_slot, from_slot, dev):
            kw = dict(send_sem=send_sems.at[per * t + i], recv_sem=recv_sems.at[per * t + i], device_id=dev,
                      device_id_type=MESH)
            out.append((pltpu.make_async_remote_copy(src_ref=src, dst_ref=land.at[to_slot], **kw),
                        pltpu.make_async_remote_copy(src_ref=src, dst_ref=land.at[from_slot], **kw)))

        if kind in ("gather", "scatter"):
            for k in range(1, N_DEV):
                px, py, pc = _flip(cx, k & 4), _flip(cy, k & 2), _flip(cc, k & 1)
                peer = 4 * px + 2 * py + pc
                pair(k - 1, src_refs[t].at[peer] if kind == "scatter" else src_refs[t], me, peer, (px, py, pc))
        elif kind == "chips":
            pair(0, src_refs[t], me, me + 1 - 2 * cc, (cx, cy, 1 - cc))
            for j in range(1, 4):
                px, py = _flip(cx, j & 2), _flip(cy, j & 1)
                pair(j, src_refs[t], me, 4 * px + 2 * py + cc, (px, py, cc))
        else:
            for j in range(1, 4):
                px, py = _flip(cx, j & 2), _flip(cy, j & 1)
                mine, theirs = 4 * px + 2 * py + cc, 4 * px + 2 * py + 1 - cc
                pair(j - 1, land.at[mine], mine, theirs, (cx, cy, 1 - cc))
    return out


def _xchg_start(name, kind, srcs, lands, after):
    ns, nb, na = len(srcs), len(srcs) + len(lands), len(after)
    n_sems = COPIES[kind] * len(lands)

    def body(*refs):
        send_sems, recv_sems = refs[nb + na], refs[nb + na + 1]
        for send, _ in _copy_plan(kind, refs[:ns], refs[ns:nb], send_sems, recv_sems):
            send.start()
        refs[-1][...] = jnp.zeros_like(refs[-1])

    bufs = list(srcs) + list(lands)
    res = _pcall(
        body, name=name,
        out_shape=(pltpu.SemaphoreType.DMA((n_sems,)), pltpu.SemaphoreType.DMA((n_sems,)),
                   *[pltpu.HBM(a.shape, a.dtype) for a in bufs], jax.ShapeDtypeStruct((8, LANE), F32)),
        in_specs=[HBM] * nb + [ANY] * na,
        out_specs=(SEM, SEM, *([HBM] * nb), pl.BlockSpec(memory_space=pltpu.VMEM)),
        input_output_aliases={i: 2 + i for i in range(nb)},
        compiler_params=pltpu.CompilerParams(has_side_effects=EFFECT),
    )(*[pltpu.with_memory_space_constraint(a, pltpu.HBM) for a in bufs], *after)
    return dict(kind=kind, send=res[0], recv=res[1], srcs=list(res[2:2 + ns]), lands=list(res[2 + ns:2 + nb]),
                token=res[-1])


def _xchg_wait(name, st, after):
    ns, nb = len(st['srcs']), len(st['srcs']) + len(st['lands'])

    def body(*refs):
        for _, back in _copy_plan(st['kind'], refs[:ns], refs[ns:nb], refs[nb], refs[nb + 1]):
            back.wait_send()
            back.wait_recv()

    bufs = st['srcs'] + st['lands']
    res = _pcall(
        body, name=name, out_shape=tuple(pltpu.HBM(a.shape, a.dtype) for a in bufs),
        in_specs=[HBM] * nb + [SEM, SEM] + [ANY] * len(after), out_specs=tuple([HBM] * nb),
        input_output_aliases={i: i for i in range(nb)},
        compiler_params=pltpu.CompilerParams(has_side_effects=EFFECT),
    )(*bufs, st['send'], st['recv'], *after)
    return list(res[ns:])


_DIMS = {"nn": (((1,), (0,)), ((), ())), "nt": (((1,), (1,)), ((), ())), "tn": (((0,), (0,)), ((), ()))}


MM_OPERAND_BYTES = 16 * 1024 * 1024


def _contraction_tile(K, bytes_per_k):
    fits = [c for c in (K, 2048, 1024, 512, 256, 128) if K % c == 0 and c * bytes_per_k <= MM_OPERAND_BYTES]
    return fits[0] if fits else _pick(K, (128,))


def _mm(name, a, b, mode, out_dtype=F32, res=None, alpha=1.0, deps=()):
    if mode == "tn":
        K, M = a.shape
        N = b.shape[1]
    else:
        M, K = a.shape
        N = b.shape[0] if mode == "nt" else b.shape[1]
    tn = _pick(N, (1024, 512, 256))
    tm = _pick(M, (1024, 512, 256, 128) if tn <= 1024 else (512, 256, 128))
    tk = _contraction_tile(K, tm * a.dtype.itemsize + tn * b.dtype.itemsize)
    nk = K // tk
    a_spec = pl.BlockSpec((tk, tm), lambda i, j, k: (k, i)) if mode == "tn" else pl.BlockSpec((tm, tk), lambda i, j, k: (i, k))
    b_spec = pl.BlockSpec((tn, tk), lambda i, j, k: (j, k)) if mode == "nt" else pl.BlockSpec((tk, tn), lambda i, j, k: (k, j))
    o_spec = pl.BlockSpec((tm, tn), lambda i, j, k: (i, j))
    dims = _DIMS[mode]

    def body(*refs):
        a_ref, b_ref, r_ref = refs[0], refs[1], refs[2]
        part = lax.dot_general(a_ref[...].astype(BF16), b_ref[...].astype(BF16), dims, preferred_element_type=F32)

        def finish(o_ref, r):
            r = r * alpha if alpha != 1.0 else r
            if res is not None:
                r = r_ref[...] + r
            o_ref[...] = r.astype(o_ref.dtype)

        if nk == 1:
            finish(refs[-1], part)
            return
        o_ref, acc = refs[-2], refs[-1]
        k = pl.program_id(2)

        @pl.when(k == 0)
        def _():
            acc[...] = part

        @pl.when(k > 0)
        def _():
            acc[...] += part

        @pl.when(k == nk - 1)
        def _():
            finish(o_ref, acc[...])

    ins = [a, b] + ([] if res is None else [res]) + list(deps)
    specs = [a_spec, b_spec] + ([] if res is None else [o_spec]) + [ANY] * len(deps)
    return _pcall(body, name=name, grid=(M // tm, N // tn, nk), in_specs=specs, out_specs=o_spec,
                  out_shape=jax.ShapeDtypeStruct((M, N), out_dtype),
                  scratch_shapes=[] if nk == 1 else [pltpu.VMEM((tm, tn), F32)])(*ins)


def _mm_tn_batch(name, a3, b3, alpha=1.0, deps=()):
    ga, T, M = a3.shape
    gb, _, N = b3.shape
    G = max(ga, gb)
    tm = _pick(M, (1024, 512)) if N <= 1024 else M
    tk = _contraction_tile(T, tm * a3.dtype.itemsize + N * b3.dtype.itemsize)
    nk = T // tk
    a_spec = pl.BlockSpec((None, tk, tm), (lambda g, i, k: (g, k, i)) if ga > 1 else (lambda g, i, k: (0, k, i)))
    b_spec = pl.BlockSpec((None, tk, N), (lambda g, i, k: (g, k, 0)) if gb > 1 else (lambda g, i, k: (0, k, 0)))
    o_spec = pl.BlockSpec((None, tm, N), lambda g, i, k: (g, i, 0))

    def body(*refs):
        a_ref, b_ref, o_ref, acc = refs[0], refs[1], refs[-2], refs[-1]
        k = pl.program_id(2)
        part = lax.dot_general(a_ref[...].astype(BF16), b_ref[...].astype(BF16), _DIMS["tn"], preferred_element_type=F32)

        @pl.when(k == 0)
        def _():
            acc[...] = part

        @pl.when(k > 0)
        def _():
            acc[...] += part

        @pl.when(k == nk - 1)
        def _():
            o_ref[...] = (acc[...] * alpha if alpha != 1.0 else acc[...]).astype(o_ref.dtype)

    return _pcall(body, name=name, grid=(G, M // tm, nk), in_specs=[a_spec, b_spec] + [ANY] * len(deps),
                  out_specs=o_spec, out_shape=jax.ShapeDtypeStruct((G, M, N), BF16),
                  scratch_shapes=[pltpu.VMEM((tm, N), F32)])(a3, b3, *deps)


def _rowwise(name, fn, ins, outs, T, tm, deps=()):
    in_specs = []
    for arr, spec in ins:
        if spec == "row":
            in_specs.append(pl.BlockSpec((tm, arr.shape[1]), lambda i: (i, 0)))
        elif spec == "full":
            in_specs.append(pl.BlockSpec(arr.shape, lambda i, _n=arr.ndim: (0,) * _n))
        else:
            _, off, width = spec
            in_specs.append(pl.BlockSpec((tm, width), lambda i, _b=off // width: (i, _b)))
    in_specs += [ANY] * len(deps)
    out_specs, out_shapes = [], []
    for shape, dtype, spec in outs:
        out_shapes.append(jax.ShapeDtypeStruct(shape, dtype))
        if spec == "row":
            out_specs.append(pl.BlockSpec((tm, shape[1]), lambda i: (i, 0)))
        else:
            out_specs.append(pl.BlockSpec(shape, lambda i, _n=len(shape): (0,) * _n))
    n_in = len(ins)

    def body(*refs):
        res = fn(*[r[...] for r in refs[:n_in]])
        i = pl.program_id(0)
        for r, (_, _, spec), val in zip(refs[n_in + len(deps):], outs, res):
            if spec == "acc":
                @pl.when(i == 0)
                def _(r=r):
                    r[...] = jnp.zeros_like(r)
                r[...] += val.astype(r.dtype)
            else:
                r[...] = val.astype(r.dtype)

    return _pcall(body, name=name, grid=(T // tm,), in_specs=in_specs, out_specs=out_specs, out_shape=out_shapes)(
        *[a for a, _ in ins], *deps)


def _rms_fwd(name, h, g, deps=()):
    T, D = h.shape
    return _rowwise(name, lambda hv, gv: (_rms(hv, gv),), [(h, "row"), (g, "full")], [((T, D), BF16, "row")], T,
                    _pick(T, (512, 256, 128)), deps)[0]


def _rms_bwd(name, h, g, dxn, dh_in, deps=()):
    T, D = h.shape

    def fn(hv, gv, dv, dh0):
        _, vjp = jax.vjp(_rms, hv, gv)
        dh, dg = vjp(dv.astype(F32))
        return dh0 + dh, dg

    return _rowwise(name, fn, [(h, "row"), (g, "full"), (dxn, "row"), (dh_in, "row")],
                    [((T, D), F32, "row"), ((1, D), F32, "acc")], T, _pick(T, (512, 256, 128)), deps)


def _ffn_fwd(name, xn, h, w1, w3, w2):
    T, D = xn.shape
    F8 = w1.shape[-1]
    tm = _pick(T, (512, 256, 128))
    wspec = lambda r, c: pl.BlockSpec((None, r, c), lambda i, d: (d, 0, 0))
    row = pl.BlockSpec((tm, D), lambda i, d: (i, 0))
    hid = pl.BlockSpec((None, tm, F8), lambda i, d: (d, i, 0))

    def body(xn_ref, h_ref, w1_ref, w3_ref, w2_ref, out_ref, h1_ref, h3_ref, acc):
        d = pl.program_id(1)

        @pl.when(d == 0)
        def _():
            acc[...] = jnp.zeros_like(acc)

        x = xn_ref[...]
        h1 = jnp.dot(x, w1_ref[...], preferred_element_type=F32)
        h3 = jnp.dot(x, w3_ref[...], preferred_element_type=F32)
        h1_ref[...] = h1.astype(BF16)
        h3_ref[...] = h3.astype(BF16)
        act = (h1 * jax.nn.sigmoid(h1) * h3).astype(BF16)
        acc[...] += jnp.dot(act, w2_ref[...], preferred_element_type=F32)

        @pl.when(d == N_DEV - 1)
        def _():
            out_ref[...] = h_ref[...] + 0.5 * acc[...]

    return _pcall(body, name=name, grid=(T // tm, N_DEV),
                  in_specs=[row, row, wspec(D, F8), wspec(D, F8), wspec(F8, D)],
                  out_specs=[row, hid, hid],
                  out_shape=[jax.ShapeDtypeStruct((T, D), F32), jax.ShapeDtypeStruct((N_DEV, T, F8), BF16),
                             jax.ShapeDtypeStruct((N_DEV, T, F8), BF16)],
                  scratch_shapes=[pltpu.VMEM((tm, D), F32)])(xn, h, w1, w3, w2)


FFN_ROW_GROUPS = 2


def _ffn_bwd(name, dy, h1, h3, w1, w3, w2, deps=()):
    T, D = dy.shape
    F8 = w1.shape[-1]
    tm = _pick(T, (512, 256, 128))
    wspec = lambda r, c: pl.BlockSpec((None, r, c), lambda i, d: (d, 0, 0))
    row = pl.BlockSpec((tm, D), lambda i, d: (i, 0))
    hid = pl.BlockSpec((None, tm, F8), lambda i, d: (d, i, 0))

    def body(*refs):
        dy_ref, h1_ref, h3_ref, w1_ref, w3_ref, w2_ref = refs[:6]
        dxn_ref, dyb, dh1_ref, dh3_ref, act_ref, acc = refs[6 + len(deps):]
        d = pl.program_id(1)

        @pl.when(d == 0)
        def _():
            acc[...] = jnp.zeros_like(acc)
            dyb[...] = dy_ref[...].astype(BF16)

        groups = [pl.ds(r * (tm // FFN_ROW_GROUPS), tm // FFN_ROW_GROUPS) for r in range(FFN_ROW_GROUPS)]
        dacts = [0.5 * lax.dot_general(dyb[rows, :], w2_ref[...], _DIMS["nt"], preferred_element_type=F32)
                 for rows in groups]
        for rows, dact in zip(groups, dacts):
            h1 = h1_ref[rows, :].astype(F32)
            h3 = h3_ref[rows, :].astype(F32)
            sig = jax.nn.sigmoid(h1)
            silu = h1 * sig
            dh1 = (dact * h3 * (sig * (1.0 + h1 * (1.0 - sig)))).astype(BF16)
            dh3 = (dact * silu).astype(BF16)
            dh1_ref[rows, :] = dh1
            dh3_ref[rows, :] = dh3
            act_ref[rows, :] = (silu * h3).astype(BF16)
            acc[rows, :] += (lax.dot_general(dh1, w1_ref[...], _DIMS["nt"], preferred_element_type=F32)
                             + lax.dot_general(dh3, w3_ref[...], _DIMS["nt"], preferred_element_type=F32))

        @pl.when(d == N_DEV - 1)
        def _():
            dxn_ref[...] = acc[...]

    hshape = jax.ShapeDtypeStruct((N_DEV, T, F8), BF16)
    return _pcall(body, name=name, grid=(T // tm, N_DEV),
                  in_specs=[row, hid, hid, wspec(D, F8), wspec(D, F8), wspec(F8, D)] + [ANY] * len(deps),
                  out_specs=[row, row, hid, hid, hid],
                  out_shape=[jax.ShapeDtypeStruct((T, D), F32), jax.ShapeDtypeStruct((T, D), BF16), hshape, hshape, hshape],
                  scratch_shapes=[pltpu.VMEM((tm, D), F32)])(dy, h1, h3, w1, w3, w2, *deps)


def _rot_matrix():
    i = jnp.arange(QK)[:, None]
    j = jnp.arange(QK)[None, :]
    half = ROPE // 2
    first = (j >= NOPE) & (j < NOPE + half) & (i == j + half)
    second = (j >= NOPE + half) & (i == j - half)
    return jnp.where(first, -1.0, jnp.where(second, 1.0, 0.0)).astype(F32)


def _mla_fn(cq, ckv, kr128, cos, sin, rot, qa_g, kva_g, qn_g, kn_g, w_uq, w_ukv):
    cqn = _rms(cq, qa_g)
    ckvn = _rms(ckv, kva_g)
    kr = kr128[:, :ROPE]
    qs, ks, vs = [], [], []
    for h in range(HEADS):
        qh = _rms(_bdot(cqn, w_uq[h]), qn_g)
        qs.append((qh * cos + _permute(qh, rot) * sin) * SCORE_SCALE)
        kvh = _bdot(ckvn, w_ukv[h])
        kh = _rms(jnp.concatenate([kvh[:, :NOPE], kr], axis=-1), kn_g)
        ks.append(kh * cos + _permute(kh, rot) * sin)
        vs.append(kvh[:, NOPE:])
    return qs, ks, vs


def _mla_specs(z, tabs, small, w_uq, w_ukv, tm, offs):
    o_cq, o_ckv, o_kr = offs
    row = lambda w: pl.BlockSpec((tm, w), lambda i: (i, 0))
    col = lambda off, w: pl.BlockSpec((tm, w), lambda i: (i, off // w))
    full2 = lambda a: pl.BlockSpec(a.shape, lambda i: (0, 0))
    wsp = lambda a: pl.BlockSpec(a.shape, lambda i: (0, 0, 0))
    cq_w, ckv_w = w_uq.shape[1], w_ukv.shape[1]
    ins = [z, z, z, tabs[0], tabs[1], tabs[2]] + list(small) + [w_uq, w_ukv]
    specs = ([col(o_cq, cq_w), col(o_ckv, ckv_w), col(o_kr, LANE), row(QK), row(QK), full2(tabs[2])]
             + [full2(s) for s in small] + [wsp(w_uq), wsp(w_ukv)])
    return ins, specs


def _mla_prep_fwd(name, z, tabs, small, w_uq, w_ukv, offs):
    T = z.shape[0]
    tm = _pick(T, (512, 256, 128))
    ins, specs = _mla_specs(z, tabs, small, w_uq, w_ukv, tm, offs)
    head = lambda w: pl.BlockSpec((HEADS, tm, w), lambda i: (0, i, 0))

    def body(*refs):
        vals = [r[...] for r in refs[:12]]
        q_ref, k_ref, v_ref = refs[12:]
        qs, ks, vs = _mla_fn(*vals)
        for h in range(HEADS):
            q_ref[h] = qs[h].astype(BF16)
            k_ref[h] = ks[h].astype(BF16)
            v_ref[h] = vs[h].astype(BF16)

    return _pcall(body, name=name, grid=(T // tm,), in_specs=specs, out_specs=[head(QK), head(QK), head(VD)],
                  out_shape=[jax.ShapeDtypeStruct((HEADS, T, QK), BF16), jax.ShapeDtypeStruct((HEADS, T, QK), BF16),
                             jax.ShapeDtypeStruct((HEADS, T, VD), BF16)])(*ins)


def _mla_prep_bwd(name, z, tabs, small, w_uq, w_ukv, offs, dq, dk, dv):
    T = z.shape[0]
    tm = _pick(T, (512, 256, 128))
    ins, specs = _mla_specs(z, tabs, small, w_uq, w_ukv, tm, offs)
    head = lambda w: pl.BlockSpec((HEADS, tm, w), lambda i: (0, i, 0))
    ins += [dq, dk, dv]
    specs += [head(QK), head(QK), head(VD)]
    cq_w, ckv_w = w_uq.shape[1], w_ukv.shape[1]
    acc_shapes = [s.shape for s in small] + [w_uq.shape, w_ukv.shape]
    row_shapes = [(T, cq_w), (T, ckv_w), (T, LANE)]
    out_shape = [jax.ShapeDtypeStruct(s, BF16) for s in row_shapes] + [jax.ShapeDtypeStruct(s, F32) for s in acc_shapes]
    out_specs = ([pl.BlockSpec((tm, s[1]), lambda i: (i, 0)) for s in row_shapes]
                 + [pl.BlockSpec(s, lambda i, _n=len(s): (0,) * _n) for s in acc_shapes])

    def body(*refs):
        cq, ckv, kr128, cos, sin, rot, qa_g, kva_g, qn_g, kn_g, w_uq_v, w_ukv_v = [r[...] for r in refs[:12]]
        dq_ref, dk_ref, dv_ref = refs[12:15]
        outs = refs[15:]
        f = lambda a, b, c, g1, g2, g3, g4, wq, wkv: _mla_fn(a, b, c, cos, sin, rot, g1, g2, g3, g4, wq, wkv)
        _, vjp = jax.vjp(f, cq, ckv, kr128, qa_g, kva_g, qn_g, kn_g, w_uq_v.astype(F32), w_ukv_v.astype(F32))
        cts = ([dq_ref[h] for h in range(HEADS)], [dk_ref[h] for h in range(HEADS)], [dv_ref[h] for h in range(HEADS)])
        grads = vjp(cts)
        i = pl.program_id(0)
        for n, (r, gval) in enumerate(zip(outs, grads)):
            if n < 3:
                r[...] = gval.astype(r.dtype)
            else:
                @pl.when(i == 0)
                def _(r=r):
                    r[...] = jnp.zeros_like(r)
                r[...] += gval

    return _pcall(body, name=name, grid=(T // tm,), in_specs=specs, out_specs=out_specs, out_shape=out_shape)(*ins)


NEG = -1e30


ATTN_ROW_GROUPS = 2


def _tri_block(rows, cols, row0):
    return lax.broadcasted_iota(jnp.int32, (rows, cols), 1) <= lax.broadcasted_iota(jnp.int32, (rows, cols), 0) + row0


def _attn_tiles(T):
    t = _pick(T, (1024, 512, 256, 128))
    return t, T // t


def _attn_fwd(name, q, k, v):
    H, T, _ = q.shape
    t, n = _attn_tiles(T)

    def body(q_ref, k_ref, v_ref, o_ref, lse_ref, m_s, l_s, acc):
        qi, ki = pl.program_id(1), pl.program_id(2)

        @pl.when(ki == 0)
        def _():
            m_s[...] = jnp.full_like(m_s, NEG)
            l_s[...] = jnp.zeros_like(l_s)
            acc[...] = jnp.zeros_like(acc)

        def tile(diagonal):
            s = lax.dot_general(q_ref[...], k_ref[...], _DIMS["nt"], preferred_element_type=F32)
            if diagonal:
                s = jnp.where(_tri_block(t, t, 0), s, NEG)
            m_new = jnp.maximum(m_s[...], jnp.max(s, axis=-1, keepdims=True))
            alpha = jnp.exp(m_s[...] - m_new)
            p = jnp.exp(s - m_new)
            l_s[...] = alpha * l_s[...] + jnp.sum(p, axis=-1, keepdims=True)
            acc[...] = alpha * acc[...] + jnp.dot(p.astype(BF16), v_ref[...], preferred_element_type=F32)
            m_s[...] = m_new

        @pl.when(ki < qi)
        def _():
            tile(False)

        @pl.when(ki == qi)
        def _():
            tile(True)
            o_ref[...] = acc[...] / l_s[...]
            lse_ref[...] = m_s[...] + jnp.log(l_s[...])

    kv = lambda w: pl.BlockSpec((None, t, w), lambda h, qi, ki: (h, jnp.minimum(ki, qi), 0))
    return _pcall(body, name=name, grid=(H, n, n),
                  in_specs=[pl.BlockSpec((None, t, QK), lambda h, qi, ki: (h, qi, 0)), kv(QK), kv(VD)],
                  out_specs=[pl.BlockSpec((t, VD), lambda h, qi, ki: (qi, h)),
                             pl.BlockSpec((None, t, 1), lambda h, qi, ki: (h, qi, 0))],
                  out_shape=[jax.ShapeDtypeStruct((T, H * VD), F32), jax.ShapeDtypeStruct((H, T, 1), F32)],
                  scratch_shapes=[pltpu.VMEM((t, 1), F32), pltpu.VMEM((t, 1), F32), pltpu.VMEM((t, VD), F32)])(q, k, v)


def _attn_bwd(name, q, k, v, o, lse, do):
    H, T, _ = q.shape
    t, n = _attn_tiles(T)

    def body(q_ref, k_ref, v_ref, o_ref, lse_ref, do_ref, dq_ref, dk_ref, dv_ref, dk_acc, dv_acc):
        ki, qi = pl.program_id(1), pl.program_id(2)

        @pl.when((ki == 0) & (qi == 0))
        def _():
            dq_ref[...] = jnp.zeros_like(dq_ref)

        @pl.when(qi == 0)
        def _():
            dk_acc[...] = jnp.zeros_like(dk_acc)
            dv_acc[...] = jnp.zeros_like(dv_acc)

        def tile(diagonal):
            g = t // ATTN_ROW_GROUPS
            for r in range(ATTN_ROW_GROUPS):
                rows = pl.ds(r * g, g)
                cols = (r + 1) * g if diagonal else t
                qv, kv_, dov = q_ref[rows, :], k_ref[:cols, :], do_ref[rows, :]
                s = lax.dot_general(qv, kv_, _DIMS["nt"], preferred_element_type=F32)
                p = jnp.exp(s - lse_ref[rows, :])
                if diagonal:
                    p = jnp.where(_tri_block(g, cols, r * g), p, 0.0)
                dob = dov.astype(BF16)
                delta = jnp.sum(o_ref[rows, :] * dov, axis=-1, keepdims=True)
                dv_acc[:cols, :] += lax.dot_general(p.astype(BF16), dob, _DIMS["tn"], preferred_element_type=F32)
                dp = lax.dot_general(dob, v_ref[:cols, :], _DIMS["nt"], preferred_element_type=F32)
                ds = (p * (dp - delta)).astype(BF16)
                dq_rows = pl.ds(pl.multiple_of(qi * t + r * g, g), g)
                dq_ref[dq_rows, :] += jnp.dot(ds, kv_, preferred_element_type=F32)
                dk_acc[:cols, :] += lax.dot_general(ds, qv, _DIMS["tn"], preferred_element_type=F32)

        @pl.when(qi > ki)
        def _():
            tile(False)

        @pl.when(qi == ki)
        def _():
            tile(True)

        @pl.when(qi == n - 1)
        def _():
            dk_ref[...] = dk_acc[...]
            dv_ref[...] = dv_acc[...]

    qrow = lambda w: pl.BlockSpec((None, t, w), lambda h, ki, qi: (h, jnp.maximum(qi, ki), 0))
    krow = lambda w: pl.BlockSpec((None, t, w), lambda h, ki, qi: (h, ki, 0))
    wide = pl.BlockSpec((t, VD), lambda h, ki, qi: (jnp.maximum(qi, ki), h))
    return _pcall(body, name=name, grid=(H, n, n),
                  in_specs=[qrow(QK), krow(QK), krow(VD), wide, qrow(1), wide],
                  out_specs=[pl.BlockSpec((None, T, QK), lambda h, ki, qi: (h, 0, 0)), krow(QK), krow(VD)],
                  out_shape=[jax.ShapeDtypeStruct((H, T, QK), F32), jax.ShapeDtypeStruct((H, T, QK), F32),
                             jax.ShapeDtypeStruct((H, T, VD), F32)],
                  scratch_shapes=[pltpu.VMEM((t, QK), F32), pltpu.VMEM((t, VD), F32)])(q, k, v, o, lse, do)


def _tril():
    return lax.broadcasted_iota(jnp.int32, (CHUNK, CHUNK), 1) <= lax.broadcasted_iota(jnp.int32, (CHUNK, CHUNK), 0)


@jax.custom_vjp
def _gm_gate(v, ws, b_t):
    wc = jnp.where(_tril()[None], ws, 0.0).astype(BF16)
    vb = v.astype(BF16)
    rows = []
    for c in range(v.shape[0] // CHUNK):
        cols = []
        for g in range(GROUPS):
            vc = vb[c * CHUNK:(c + 1) * CHUNK, g * LANE:(g + 1) * LANE]
            cols.append(jnp.dot(wc[g], vc, preferred_element_type=F32) + jnp.broadcast_to(b_t[:, g:g + 1], (CHUNK, LANE)))
        rows.append(jnp.concatenate(cols, axis=-1))
    return jnp.concatenate(rows, axis=0)


def _gm_gate_fwd(v, ws, b_t):
    return _gm_gate(v, ws, b_t), (v, ws)


def _gm_gate_bwd(res, dgate):
    v, ws = res
    tril = _tril()
    wc = jnp.where(tril[None], ws, 0.0).astype(BF16)
    vb = v.astype(BF16)
    dgb = dgate.astype(BF16)
    dws = [jnp.zeros((CHUNK, CHUNK), F32) for _ in range(GROUPS)]
    db = jnp.zeros((CHUNK, GROUPS), F32)
    lane_g = lax.broadcasted_iota(jnp.int32, (1, GROUPS), 1)
    rows = []
    for c in range(v.shape[0] // CHUNK):
        cols = []
        for g in range(GROUPS):
            sl = (slice(c * CHUNK, (c + 1) * CHUNK), slice(g * LANE, (g + 1) * LANE))
            cols.append(lax.dot_general(wc[g], dgb[sl], _DIMS["tn"], preferred_element_type=F32))
            dws[g] = dws[g] + lax.dot_general(dgb[sl], vb[sl], _DIMS["nt"], preferred_element_type=F32)
            db = db + jnp.sum(dgate[sl], axis=1, keepdims=True) * (lane_g == g).astype(F32)
        rows.append(jnp.concatenate(cols, axis=-1))
    dws = jnp.stack([jnp.where(tril, d, 0.0) for d in dws])
    return jnp.concatenate(rows, axis=0), dws, db


_gm_gate.defvjp(_gm_gate_fwd, _gm_gate_bwd)


def _mix_fn(a_out, zu, zv, aon_g, gon_g, vn_g, ws, b_t):
    u = jax.nn.gelu(zu)
    vv = _rms(jax.nn.gelu(zv), vn_g)
    g_out = u * _gm_gate(vv, ws, b_t)
    return jnp.concatenate([_rms(a_out, aon_g), _rms(g_out, gon_g)], axis=-1)


def _mix_ins(a_out, z, small, offs):
    gw = a_out.shape[1]
    return [(a_out, "row"), (z, ("cols", offs[0], gw)), (z, ("cols", offs[1], gw))] + [(s, "full") for s in small]


def _mix_fwd(name, a_out, z, small, offs):
    T, gw = a_out.shape
    return _rowwise(name, lambda *a: (_mix_fn(*a),), _mix_ins(a_out, z, small, offs), [((T, 2 * gw), BF16, "row")],
                    T, _pick(T, (512, 256, 128)))[0]


def _mix_bwd(name, a_out, z, small, offs, dmixed):
    T, gw = a_out.shape

    def fn(*a):
        _, vjp = jax.vjp(_mix_fn, *a[:-1])
        return vjp(a[-1].astype(F32))

    outs = [((T, gw), F32, "row"), ((T, gw), BF16, "row"), ((T, gw), BF16, "row")] + [(s.shape, F32, "acc") for s in small]
    return _rowwise(name, fn, _mix_ins(a_out, z, small, offs) + [(dmixed, "row")], outs, T, _pick(T, (256, 128)))


def _ple_fn(gl, pe, g):
    return jax.nn.sigmoid(gl) * _rms(pe, g)


def _ple_fwd(name, h, gl, pe, g):
    T, D = h.shape
    return _rowwise(name, lambda hv, a, b, c: (hv + _ple_fn(a, b, c),),
                    [(h, "row"), (gl, "row"), (pe, "row"), (g, "full")], [((T, D), F32, "row")], T,
                    _pick(T, (512, 256, 128)))[0]


def _ple_bwd(name, gl, pe, g, dh, deps=()):
    T, D = gl.shape

    def fn(a, b, c, d):
        _, vjp = jax.vjp(_ple_fn, a, b, c)
        return vjp(d)

    return _rowwise(name, fn, [(gl, "row"), (pe, "row"), (g, "full"), (dh, "row")],
                    [((T, D), BF16, "row"), ((T, D), BF16, "row"), ((1, D), F32, "acc")], T, _pick(T, (256, 128)), deps)


def _loss(name, y, target):
    T, D = y.shape

    def fn(yv, tv):
        err = yv - tv
        part = 0.5 * jnp.sum(jnp.mean(err * err, axis=-1, keepdims=True), axis=0, keepdims=True)
        return err * (1.0 / D), jnp.broadcast_to(part, (8, LANE))

    return _rowwise(name, fn, [(y, "row"), (target, "row")], [((T, D), F32, "row"), ((8, LANE), F32, "acc")], T,
                    _pick(T, (512, 256, 128)))


ADAMW_BLOCK_ELEMS = 256 * 1024


def _adamw_sum(name, parts, w, m, v):
    L, R, C = w.shape
    tiles = [(r, c) for r in (R, 512, 256, 128, 64, 32, 16) for c in (C, 1024, 512, 256, 128)
             if R % r == 0 and C % c == 0 and r * c <= ADAMW_BLOCK_ELEMS]
    tr, tc = max(tiles, key=lambda rc: (rc[0] * rc[1], rc[1]))
    nr, nc = R // tr, C // tc
    c1 = 1.0 - ADAM_B1 ** ADAM_STEP
    c2 = 1.0 - ADAM_B2 ** ADAM_STEP

    def body(*refs):
        p_refs = refs[:L]
        w_ref, m_ref, v_ref, g_out, d_out, m_out, v_out = refs[L:]
        layer = pl.program_id(0)

        def part(s):
            val = p_refs[0][s].astype(F32)
            for j in range(1, L):
                val = jnp.where(layer == j, p_refs[j][s].astype(F32), val)
            return val

        g = part(0)
        for s in range(1, N_DEV):
            g = g + part(s)
        m2 = ADAM_B1 * m_ref[...] + (1.0 - ADAM_B1) * g
        v2 = ADAM_B2 * v_ref[...] + (1.0 - ADAM_B2) * (g * g)
        g_out[...] = g
        m_out[...] = m2
        v_out[...] = v2
        d_out[...] = -ADAM_LR * ((m2 / c1) / (jnp.sqrt(v2 / c2) + ADAM_EPS) + ADAM_WD * w_ref[...])

    def part_spec(j):
        def index(l, i, k):
            before, mine = l < j, l == j
            return (0, jnp.where(mine, i, jnp.where(before, 0, nr - 1)), jnp.where(mine, k, jnp.where(before, 0, nc - 1)))
        return pl.BlockSpec((N_DEV, tr, tc), index)

    blk = pl.BlockSpec((None, tr, tc), lambda l, i, k: (l, i, k))
    sd = jax.ShapeDtypeStruct((L, R, C), F32)
    return _pcall(body, name=name, grid=(L, nr, nc),
                  in_specs=[part_spec(j) for j in range(L)] + [blk, blk, blk],
                  out_specs=[blk, blk, blk, blk], out_shape=[sd, sd, sd, sd])(*parts, w, m, v)


def _unshard_cols(g):
    _, K, n = g.shape
    return g.transpose(1, 0, 2).reshape(K, N_DEV * n)


def _shard_cols(full):
    K, N = full.shape
    return full.reshape(K, N_DEV, N // N_DEV).transpose(1, 0, 2)


def _unshard_rows(g):
    _, k, N = g.shape
    return g.reshape(N_DEV * k, N)


def _shard_rows(full):
    K, N = full.shape
    return full.reshape(N_DEV, K // N_DEV, N)


STAGES = (('ffn_a_w1', 'ffn_a_w3', 'ffn_a_w2'), ('w_in', 'w_uq', 'w_ukv', 'w_out'),
          ('ffn_b_w1', 'ffn_b_w3', 'ffn_b_w2'), ('w_ple_gate', 'w_ple'))
TRANSPOSED = ('ffn_a_w1', 'ffn_a_w3', 'ffn_b_w1', 'ffn_b_w3', 'w_in', 'w_uq')


def _step(x, p, positions, target, w, m, v):
    T, D = x.shape[1], x.shape[2]
    L = p.shape[0]
    x2, target2 = x[0], target[0]
    q_rank, kv_rank = w['w_uq'].shape[1], w['w_ukv'].shape[1]
    gw = w['gm_v_norm'].shape[1]

    inv_freq = ROPE_BASE ** (-jnp.arange(0, ROPE, 2, dtype=F32) / ROPE)
    ang = positions[0].astype(F32)[:, None] * inv_freq
    cos = jnp.concatenate([jnp.ones((T, NOPE), F32), jnp.cos(ang), jnp.cos(ang)], axis=-1)
    sin = jnp.concatenate([jnp.zeros((T, NOPE), F32), jnp.sin(ang), jnp.sin(ang)], axis=-1)
    tabs = (cos, sin, _rot_matrix().astype(BF16))

    groups = [(l, names) for l in range(L) for names in STAGES]

    def ag_start(k, after):
        l, names = groups[k]
        shards = [w[n][l].astype(BF16) for n in names]
        return _xchg_start(f"ag_chips{k}", "chips", shards, [_landing(s) for s in shards], after)

    ag = {0: ag_start(0, [])}
    ag[1] = ag_start(1, [ag[0]['token']])

    def fetch(k, after):
        lands = _xchg_wait(f"ag_landed{k}", ag[k], after)
        fw = _xchg_start(f"ag_forward{k}", "forward", [], lands, [])
        deps = [fw['token']]
        for nxt in {0: (), 1: (2, 3)}.get(k, (k + 2,)):
            if nxt < len(groups):
                ag[nxt] = ag_start(nxt, deps)
                deps = [ag[nxt]['token']]
        if k == 0:
            deps = deps + [ag[1]['token']]
        return fw, deps

    def gathered(k, fw, after):
        return dict(zip(groups[k][1], _xchg_wait(f"ag_wait{k}", fw, after)))

    s0, s1, s2, s3 = q_rank, q_rank + kv_rank, q_rank + kv_rank + ROPE, q_rank + kv_rank + ROPE + gw
    o_u, o_v, o_cq, o_ckv, o_kr = 0, gw, 2 * gw, 2 * gw + q_rank, 2 * gw + q_rank + kv_rank
    kr_pad = 2 * LANE - ROPE

    def g2(name, l):
        return w[name][l][None, :]

    gm_bt = [w['gm_bs'][l].T for l in range(L)]

    saved = []
    h = x2
    for l in range(L):
        s = {}
        fw, deps = fetch(4 * l, [h])
        s['h0'] = h
        s['xn_a'] = _rms_fwd(f"rms_a{l}", h, g2('ffn_a_norm', l), deps)
        wa = s['wa'] = gathered(4 * l, fw, [s['xn_a']])
        h, s['a_h1'], s['a_h3'] = _ffn_fwd(f"ffn_a_fwd{l}", s['xn_a'], h, wa['ffn_a_w1'], wa['ffn_a_w3'], wa['ffn_a_w2'])
        fw, deps = fetch(4 * l + 1, [h])
        s['h1'] = h
        s['n'] = _rms_fwd(f"rms_mix{l}", h, g2('mix_norm', l), deps)
        wm = gathered(4 * l + 1, fw, [s['n']])
        w_in_full = _unshard_cols(wm['w_in'])
        s['w_in'] = jnp.concatenate([w_in_full[:, s2:s3], w_in_full[:, s3:], w_in_full[:, :s0], w_in_full[:, s0:s1],
                                     w_in_full[:, s1:s2], jnp.zeros((D, kr_pad), BF16)], axis=-1)
        s['w_out'] = _unshard_rows(wm['w_out'])
        s['w_uq'], s['w_ukv'] = wm['w_uq'], wm['w_ukv']
        s['z'] = _mm(f"w_in{l}", s['n'], s['w_in'], "nn")
        s['mla_small'] = [g2('q_a_norm', l), g2('kv_a_norm', l), g2('q_norm', l), g2('k_norm', l)]
        s['q'], s['k'], s['v'] = _mla_prep_fwd(f"mla_prep{l}", s['z'], tabs, s['mla_small'], s['w_uq'], s['w_ukv'],
                                               (o_cq, o_ckv, o_kr))
        s['a_out'], s['lse'] = _attn_fwd(f"attn_fwd{l}", s['q'], s['k'], s['v'])
        s['mix_small'] = [g2('attn_out_norm', l), g2('gm_out_norm', l), g2('gm_v_norm', l), w['gm_ws'][l], gm_bt[l]]
        s['mixed'] = _mix_fwd(f"mix_fwd{l}", s['a_out'], s['z'], s['mix_small'], (o_u, o_v))
        h = _mm(f"w_out{l}", s['mixed'], s['w_out'], "nn", res=h)
        fw, deps = fetch(4 * l + 2, [h])
        s['h2'] = h
        s['xn_b'] = _rms_fwd(f"rms_b{l}", h, g2('ffn_b_norm', l), deps)
        wb = s['wb'] = gathered(4 * l + 2, fw, [s['xn_b']])
        h, s['b_h1'], s['b_h3'] = _ffn_fwd(f"ffn_b_fwd{l}", s['xn_b'], h, wb['ffn_b_w1'], wb['ffn_b_w3'], wb['ffn_b_w2'])
        fw, deps = fetch(4 * l + 3, [h])
        s['h3'] = h
        s['xn_g'] = _rms_fwd(f"rms_g{l}", h, g2('ple_gate_norm', l), deps)
        wp = gathered(4 * l + 3, fw, [s['xn_g']])
        s['w_gate'] = _unshard_rows(wp['w_ple_gate'])
        s['gl'] = _mm(f"w_gate{l}", s['xn_g'], s['w_gate'], "nn")
        s['p'] = p[l, 0]
        s['pe'] = _mm(f"w_ple{l}", s['p'], _unshard_cols(wp['w_ple']), "nn")
        h = _ple_fwd(f"ple_fwd{l}", h, s['gl'], s['pe'], g2('ple_norm', l))
        saved.append(s)

    dh, loss_part = _loss("loss", h, target2)
    loss = lax.psum(loss_part[0, 0], AXES)

    gsmall = {n: [None] * L for n in SMALL}
    rs, where = {}, {}

    def rs_start(key, l, named):
        grads = [g for _, g in named]
        lands = [lax.dynamic_update_slice(lax.empty(g.shape, g.dtype),
                                          lax.dynamic_index_in_dim(g, _my_index(), 0, keepdims=True),
                                          (_my_index(),) + (0,) * (g.ndim - 1)) for g in grads]
        rs[key] = _xchg_start("rs_start_" + key, "scatter", grads, lands, [])
        where.update({(n, l): (key, i) for i, (n, _) in enumerate(named)})
        return [rs[key]['token']]

    def ffn_backward(tag, l, dh, xn, h_in, h1, h3, wts, norm_name, deps):
        pre = 'ffn_' + tag
        dxn, dhb, dh1, dh3, act = _ffn_bwd(f"{pre}_bwd{l}", dh, h1, h3, wts[pre + '_w1'], wts[pre + '_w3'],
                                           wts[pre + '_w2'], deps)
        g1 = _mm_tn_batch(f"{pre}_dw1_{l}", dh1, xn[None])
        deps = rs_start(f"{pre}_w1_{l}", l, [(pre + '_w1', g1)])
        g3 = _mm_tn_batch(f"{pre}_dw3_{l}", dh3, xn[None], deps=deps)
        deps = rs_start(f"{pre}_w3_{l}", l, [(pre + '_w3', g3)])
        g2_ = _mm_tn_batch(f"{pre}_dw2_{l}", act, dhb[None], alpha=0.5, deps=deps)
        deps = rs_start(f"{pre}_w2_{l}", l, [(pre + '_w2', g2_)])
        return _rms_bwd(f"rms_{tag}_bwd{l}", h_in, g2(norm_name, l), dxn, dh, deps)

    deps = []
    for l in reversed(range(L)):
        s = saved[l]
        d_gl, d_pe, gsmall['ple_norm'][l] = _ple_bwd(f"ple_bwd{l}", s['gl'], s['pe'], g2('ple_norm', l), dh, deps)
        g_ple = _mm(f"dw_ple{l}", s['p'], d_pe, "tn", out_dtype=BF16)
        g_gate = _mm(f"dw_gate{l}", s['xn_g'], d_gl, "tn", out_dtype=BF16)
        d_xng = _mm(f"d_xng{l}", d_gl, s['w_gate'], "nt")
        dh, gsmall['ple_gate_norm'][l] = _rms_bwd(f"rms_g_bwd{l}", s['h3'], g2('ple_gate_norm', l), d_xng, dh)
        deps = rs_start(f"ple_{l}", l, [('w_ple_gate', _shard_rows(g_gate)), ('w_ple', _shard_cols(g_ple))])
        dh, gsmall['ffn_b_norm'][l] = ffn_backward('b', l, dh, s['xn_b'], s['h2'], s['b_h1'], s['b_h3'], s['wb'],
                                                   'ffn_b_norm', deps)
        g_out = _mm(f"dw_out{l}", s['mixed'], dh, "tn", out_dtype=BF16)
        d_mixed = _mm(f"d_mixed{l}", dh, s['w_out'], "nt")
        mix = _mix_bwd(f"mix_bwd{l}", s['a_out'], s['z'], s['mix_small'], (o_u, o_v), d_mixed)
        d_a_out, d_u, d_v = mix[:3]
        gsmall['attn_out_norm'][l], gsmall['gm_out_norm'][l], gsmall['gm_v_norm'][l], gsmall['gm_ws'][l] = mix[3:7]
        gsmall['gm_bs'][l] = mix[7].T
        dq, dk, dv = _attn_bwd(f"attn_bwd{l}", s['q'], s['k'], s['v'], s['a_out'], s['lse'], d_a_out)
        mla = _mla_prep_bwd(f"mla_prep_bwd{l}", s['z'], tabs, s['mla_small'], s['w_uq'], s['w_ukv'], (o_cq, o_ckv, o_kr),
                            dq, dk, dv)
        d_cq, d_ckv, d_kr = mla[:3]
        gsmall['q_a_norm'][l], gsmall['kv_a_norm'][l], gsmall['q_norm'][l], gsmall['k_norm'][l] = mla[3:7]
        dz = jnp.concatenate([d_u, d_v, d_cq, d_ckv, d_kr, jnp.zeros((T, LANE), BF16)], axis=-1)
        g_in = _mm(f"dw_in{l}", dz, s['n'], "tn", out_dtype=BF16)
        g_in = jnp.concatenate([g_in[o_cq:o_cq + q_rank], g_in[o_ckv:o_ckv + kv_rank], g_in[o_kr:o_kr + ROPE],
                                g_in[o_u:o_u + gw], g_in[o_v:o_v + gw]], axis=0)
        d_n = _mm(f"d_n{l}", dz, s['w_in'], "nt")
        dh, gsmall['mix_norm'][l] = _rms_bwd(f"rms_mix_bwd{l}", s['h1'], g2('mix_norm', l), d_n, dh)
        deps = rs_start(f"mix_{l}", l, [('w_in', _shard_rows(g_in)), ('w_uq', mla[7].transpose(0, 2, 1).astype(BF16)),
                                        ('w_ukv', mla[8].astype(BF16)), ('w_out', _shard_rows(g_out))])
        dh, gsmall['ffn_a_norm'][l] = ffn_backward('a', l, dh, s['xn_a'], s['h0'], s['a_h1'], s['a_h3'], s['wa'],
                                                   'ffn_a_norm', deps)
        deps = []
    grad_x = dh[None]

    out = {}
    sizes = [w[n].size for n in SMALL]
    total = sum(sizes)
    padded = -(-total // (512 * LANE)) * (512 * LANE)

    def pack(d):
        flat = jnp.concatenate([d[n].reshape(-1) for n in SMALL] + [jnp.zeros((padded - total,), F32)])
        return flat.reshape(1, padded // LANE, LANE)

    gs = pack({n: jnp.stack([gsmall[n][l].reshape(w[n].shape[1:]) for l in range(L)]) for n in SMALL})
    small = _xchg_start("small_start", "gather", [gs[0]], [_landing(gs[0])], [])

    after = [dh, small['token']]
    landed = {}

    def partials(n, l):
        key, i = where[(n, l)]
        if key not in landed:
            landed[key] = _xchg_wait("rs_wait_" + key, rs[key], after)
        return landed[key][i]

    swap = lambda a: a.transpose(0, 2, 1)
    for stage in (3, 2, 1, 0):
        for n in STAGES[stage]:
            parts = [partials(n, l) for l in reversed(range(L))][::-1]
            if n in TRANSPOSED:
                out[n] = [swap(r) for r in _adamw_sum("adamw_" + n, parts, swap(w[n]), swap(m[n]), swap(v[n]))]
            else:
                out[n] = _adamw_sum("adamw_" + n, parts, w[n], m[n], v[n])
            after = [out[n][0]]

    res = _adamw_sum("adamw_small", _xchg_wait("small_wait", small, after), pack(w), pack(m), pack(v))
    off = 0
    for n, sz in zip(SMALL, sizes):
        out[n] = [r.reshape(-1)[off:off + sz].reshape(w[n].shape) for r in res]
        off += sz

    return (loss, grad_x, *[out[n][0] for n in WEIGHTS], *[out[n][1] for n in WEIGHTS],
            *[out[n][2] for n in WEIGHTS], *[out[n][3] for n in WEIGHTS])


def kernel(x, p, positions, ffn_a_norm, ffn_a_w1, ffn_a_w3, ffn_a_w2, mix_norm, w_in, q_a_norm, w_uq, kv_a_norm, w_ukv, q_norm, k_norm, gm_v_norm, gm_ws, gm_bs, attn_out_norm, gm_out_norm, w_out, ffn_b_norm, ffn_b_w1, ffn_b_w3, ffn_b_w2, ple_gate_norm, w_ple_gate, w_ple, ple_norm, loss_target, m_ffn_a_norm, m_ffn_a_w1, m_ffn_a_w3, m_ffn_a_w2, m_mix_norm, m_w_in, m_q_a_norm, m_w_uq, m_kv_a_norm, m_w_ukv, m_q_norm, m_k_norm, m_gm_v_norm, m_gm_ws, m_gm_bs, m_attn_out_norm, m_gm_out_norm, m_w_out, m_ffn_b_norm, m_ffn_b_w1, m_ffn_b_w3, m_ffn_b_w2, m_ple_gate_norm, m_w_ple_gate, m_w_ple, m_ple_norm, v_ffn_a_norm, v_ffn_a_w1, v_ffn_a_w3, v_ffn_a_w2, v_mix_norm, v_w_in, v_q_a_norm, v_w_uq, v_kv_a_norm, v_w_ukv, v_q_norm, v_k_norm, v_gm_v_norm, v_gm_ws, v_gm_bs, v_attn_out_norm, v_gm_out_norm, v_w_out, v_ffn_b_norm, v_ffn_b_w1, v_ffn_b_w3, v_ffn_b_w2, v_ple_gate_norm, v_w_ple_gate, v_w_ple, v_ple_norm):
    args = locals()
    w = {n: args[n] for n in WEIGHTS}
    m = {n: args["m_" + n] for n in WEIGHTS}
    v = {n: args["v_" + n] for n in WEIGHTS}
    return _step(x, p, positions, loss_target, w, m, v)
```

```python
import functools

import jax
import jax.numpy as jnp
from jax import lax
from jax.experimental import pallas as pl
from jax.experimental.pallas import tpu as pltpu

F32, BF16 = jnp.float32, jnp.bfloat16
EPS = 1e-6
N_DEV = 8
HEADS = 8
NOPE, ROPE, QK, VD = 128, 64, 192, 128
SCORE_SCALE = QK ** -0.5
CHUNK = 128
GROUPS = 8
LANE = 128
ROPE_BASE = 10000.0
ADAM_LR, ADAM_B1, ADAM_B2, ADAM_EPS, ADAM_WD, ADAM_STEP = 0.001, 0.9, 0.999, 1e-08, 0.01, 10
AXES = ("x", "y", "c")
MESH = pl.DeviceIdType.MESH
ANY = pl.BlockSpec(memory_space=pl.ANY)

WEIGHTS = ['ffn_a_norm', 'ffn_a_w1', 'ffn_a_w3', 'ffn_a_w2', 'mix_norm', 'w_in', 'q_a_norm', 'w_uq', 'kv_a_norm',
           'w_ukv', 'q_norm', 'k_norm', 'gm_v_norm', 'gm_ws', 'gm_bs', 'attn_out_norm', 'gm_out_norm', 'w_out',
           'ffn_b_norm', 'ffn_b_w1', 'ffn_b_w3', 'ffn_b_w2', 'ple_gate_norm', 'w_ple_gate', 'w_ple', 'ple_norm']
BIG = ['ffn_a_w1', 'ffn_a_w3', 'ffn_a_w2', 'w_in', 'w_uq', 'w_ukv', 'w_out', 'ffn_b_w1', 'ffn_b_w3', 'ffn_b_w2',
       'w_ple_gate', 'w_ple']
SMALL = [n for n in WEIGHTS if n not in BIG]


def _pcall(body, **kw):
    return pl.pallas_call(body, **kw)


def _pick(n, cands):
    for c in cands:
        if n % c == 0:
            return c
    return n


def _rms(x, g):
    return x * lax.rsqrt(jnp.mean(x * x, axis=-1, keepdims=True) + EPS) * g


@jax.custom_vjp
def _bdot(x, w):
    return jnp.dot(x.astype(BF16), w.astype(BF16), preferred_element_type=F32)


def _bdot_fwd(x, w):
    return _bdot(x, w), (x, w)


def _bdot_bwd(res, dy):
    x, w = res
    dyb = dy.astype(BF16)
    dx = lax.dot_general(dyb, w.astype(BF16), (((1,), (1,)), ((), ())), preferred_element_type=F32)
    dw = lax.dot_general(x.astype(BF16), dyb, (((0,), (0,)), ((), ())), preferred_element_type=F32)
    return dx.astype(x.dtype), dw.astype(w.dtype)


_bdot.defvjp(_bdot_fwd, _bdot_bwd)


def _split_dot(x, p, dims):
    hi = x.astype(BF16)
    lo = (x - hi.astype(F32)).astype(BF16)
    return (lax.dot_general(hi, p, dims, preferred_element_type=F32)
            + lax.dot_general(lo, p, dims, preferred_element_type=F32))


@jax.custom_vjp
def _permute(x, p):
    return _split_dot(x, p, _DIMS["nn"])


def _permute_fwd(x, p):
    return _permute(x, p), p


def _permute_bwd(p, ct):
    return _split_dot(ct, p, _DIMS["nt"]), jnp.zeros_like(p)


_permute.defvjp(_permute_fwd, _permute_bwd)


def _flip(v, bit):
    return 1 - v if bit else v


HBM = pl.BlockSpec(memory_space=pltpu.HBM)
SEM = pl.BlockSpec(memory_space=pltpu.SEMAPHORE)
EFFECT = pltpu.SideEffectType.DATAFLOW_SIDE_EFFECTING
PEERS = N_DEV - 1


def _my_index():
    return 4 * lax.axis_index("x") + 2 * lax.axis_index("y") + lax.axis_index("c")


def _landing(own):
    zone = lax.empty((N_DEV,) + own.shape, own.dtype)
    return lax.dynamic_update_slice(zone, own[None], (_my_index(),) + (0,) * own.ndim)


COPIES = {"gather": PEERS, "scatter": PEERS, "chips": 4, "forward": 3}


def _copy_plan(kind, src_refs, land_refs, send_sems, recv_sems):
    cx, cy, cc = lax.axis_index("x"), lax.axis_index("y"), lax.axis_index("c")
    me = 4 * cx + 2 * cy + cc
    per = COPIES[kind]
    out = []
    for t, land in enumerate(land_refs):
        def pair(i, src, to_slot, from_slot, dev):
            kw = dict(send_sem=send_sems.at[per * t + i], recv_sem=recv_sems.at[per * t + i], device_id=dev,
                      device_id_type=MESH)
            out.append((pltpu.make_async_remote_copy(src_ref=src, dst_ref=land.at[to_slot], **kw),
                        pltpu.make_async_remote_copy(src_ref=src, dst_ref=land.at[from_slot], **kw)))

        if kind in ("gather", "scatter"):
            for k in range(1, N_DEV):
                px, py, pc = _flip(cx, k & 4), _flip(cy, k & 2), _flip(cc, k & 1)
                peer = 4 * px + 2 * py + pc
                pair(k - 1, src_refs[t].at[peer] if kind == "scatter" else src_refs[t], me, peer, (px, py, pc))
        elif kind == "chips":
            pair(0, src_refs[t], me, me + 1 - 2 * cc, (cx, cy, 1 - cc))
            for j in range(1, 4):
                px, py = _flip(cx, j & 2), _flip(cy, j & 1)
                pair(j, src_refs[t], me, 4 * px + 2 * py + cc, (px, py, cc))
        else:
            for j in range(1, 4):
                px, py = _flip(cx, j & 2), _flip(cy, j & 1)
                mine, theirs = 4 * px + 2 * py + cc, 4 * px + 2 * py + 1 - cc
                pair(j - 1, land.at[mine], mine, theirs, (cx, cy, 1 - cc))
    return out


def _xchg_start(name, kind, srcs, lands, after):
    ns, nb, na = len(srcs), len(srcs) + len(lands), len(after)
    n_sems = COPIES[kind] * len(lands)

    def body(*refs):
        send_sems, recv_sems = refs[nb + na], refs[nb + na + 1]
        for send, _ in _copy_plan(kind, refs[:ns], refs[ns:nb], send_sems, recv_sems):
            send.start()
        refs[-1][...] = jnp.zeros_like(refs[-1])

    bufs = list(srcs) + list(lands)
    res = _pcall(
        body, name=name,
        out_shape=(pltpu.SemaphoreType.DMA((n_sems,)), pltpu.SemaphoreType.DMA((n_sems,)),
                   *[pltpu.HBM(a.shape, a.dtype) for a in bufs], jax.ShapeDtypeStruct((8, LANE), F32)),
        in_specs=[HBM] * nb + [ANY] * na,
        out_specs=(SEM, SEM, *([HBM] * nb), pl.BlockSpec(memory_space=pltpu.VMEM)),
        input_output_aliases={i: 2 + i for i in range(nb)},
        compiler_params=pltpu.CompilerParams(has_side_effects=EFFECT),
    )(*[pltpu.with_memory_space_constraint(a, pltpu.HBM) for a in bufs], *after)
    return dict(kind=kind, send=res[0], recv=res[1], srcs=list(res[2:2 + ns]), lands=list(res[2 + ns:2 + nb]),
                token=res[-1])


def _xchg_wait(name, st, after):
    ns, nb = len(st['srcs']), len(st['srcs']) + len(st['lands'])

    def body(*refs):
        for _, back in _copy_plan(st['kind'], refs[:ns], refs[ns:nb], refs[nb], refs[nb + 1]):
            back.wait_send()
            back.wait_recv()

    bufs = st['srcs'] + st['lands']
    res = _pcall(
        body, name=name, out_shape=tuple(pltpu.HBM(a.shape, a.dtype) for a in bufs),
        in_specs=[HBM] * nb + [SEM, SEM] + [ANY] * len(after), out_specs=tuple([HBM] * nb),
        input_output_aliases={i: i for i in range(nb)},
        compiler_params=pltpu.CompilerParams(has_side_effects=EFFECT),
    )(*bufs, st['send'], st['recv'], *after)
    return list(res[ns:])


_DIMS = {"nn": (((1,), (0,)), ((), ())), "nt": (((1,), (1,)), ((), ())), "tn": (((0,), (0,)), ((), ()))}


MM_OPERAND_BYTES = 16 * 1024 * 1024


def _contraction_tile(K, bytes_per_k):
    fits = [c for c in (K, 2048, 1024, 512, 256, 128) if K % c == 0 and c * bytes_per_k <= MM_OPERAND_BYTES]
    return fits[0] if fits else _pick(K, (128,))


def _mm(name, a, b, mode, out_dtype=F32, res=None, alpha=1.0, deps=()):
    if mode == "tn":
        K, M = a.shape
        N = b.shape[1]
    else:
        M, K = a.shape
        N = b.shape[0] if mode == "nt" else b.shape[1]
    tn = _pick(N, (1024, 512, 256))
    tm = _pick(M, (1024, 512, 256, 128) if tn <= 1024 else (512, 256, 128))
    tk = _contraction_tile(K, tm * a.dtype.itemsize + tn * b.dtype.itemsize)
    nk = K // tk
    a_spec = pl.BlockSpec((tk, tm), lambda i, j, k: (k, i)) if mode == "tn" else pl.BlockSpec((tm, tk), lambda i, j, k: (i, k))
    b_spec = pl.BlockSpec((tn, tk), lambda i, j, k: (j, k)) if mode == "nt" else pl.BlockSpec((tk, tn), lambda i, j, k: (k, j))
    o_spec = pl.BlockSpec((tm, tn), lambda i, j, k: (i, j))
    dims = _DIMS[mode]

    def body(*refs):
        a_ref, b_ref, r_ref = refs[0], refs[1], refs[2]
        part = lax.dot_general(a_ref[...].astype(BF16), b_ref[...].astype(BF16), dims, preferred_element_type=F32)

        def finish(o_ref, r):
            r = r * alpha if alpha != 1.0 else r
            if res is not None:
                r = r_ref[...] + r
            o_ref[...] = r.astype(o_ref.dtype)

        if nk == 1:
            finish(refs[-1], part)
            return
        o_ref, acc = refs[-2], refs[-1]
        k = pl.program_id(2)

        @pl.when(k == 0)
        def _():
            acc[...] = part

        @pl.when(k > 0)
        def _():
            acc[...] += part

        @pl.when(k == nk - 1)
        def _():
            finish(o_ref, acc[...])

    ins = [a, b] + ([] if res is None else [res]) + list(deps)
    specs = [a_spec, b_spec] + ([] if res is None else [o_spec]) + [ANY] * len(deps)
    return _pcall(body, name=name, grid=(M // tm, N // tn, nk), in_specs=specs, out_specs=o_spec,
                  out_shape=jax.ShapeDtypeStruct((M, N), out_dtype),
                  scratch_shapes=[] if nk == 1 else [pltpu.VMEM((tm, tn), F32)])(*ins)


def _mm_tn_batch(name, a3, b3, alpha=1.0, deps=()):
    ga, T, M = a3.shape
    gb, _, N = b3.shape
    G = max(ga, gb)
    tm = _pick(M, (1024, 512)) if N <= 1024 else M
    tk = _contraction_tile(T, tm * a3.dtype.itemsize + N * b3.dtype.itemsize)
    nk = T // tk
    a_spec = pl.BlockSpec((None, tk, tm), (lambda g, i, k: (g, k, i)) if ga > 1 else (lambda g, i, k: (0, k, i)))
    b_spec = pl.BlockSpec((None, tk, N), (lambda g, i, k: (g, k, 0)) if gb > 1 else (lambda g, i, k: (0, k, 0)))
    o_spec = pl.BlockSpec((None, tm, N), lambda g, i, k: (g, i, 0))

    def body(*refs):
        a_ref, b_ref, o_ref, acc = refs[0], refs[1], refs[-2], refs[-1]
        k = pl.program_id(2)
        part = lax.dot_general(a_ref[...].astype(BF16), b_ref[...].astype(BF16), _DIMS["tn"], preferred_element_type=F32)

        @pl.when(k == 0)
        def _():
            acc[...] = part

        @pl.when(k > 0)
        def _():
            acc[...] += part

        @pl.when(k == nk - 1)
        def _():
            o_ref[...] = (acc[...] * alpha if alpha != 1.0 else acc[...]).astype(o_ref.dtype)

    return _pcall(body, name=name, grid=(G, M // tm, nk), in_specs=[a_spec, b_spec] + [ANY] * len(deps),
                  out_specs=o_spec, out_shape=jax.ShapeDtypeStruct((G, M, N), BF16),
                  scratch_shapes=[pltpu.VMEM((tm, N), F32)])(a3, b3, *deps)


def _rowwise(name, fn, ins, outs, T, tm, deps=()):
    in_specs = []
    for arr, spec in ins:
        if spec == "row":
            in_specs.append(pl.BlockSpec((tm, arr.shape[1]), lambda i: (i, 0)))
        elif spec == "full":
            in_specs.append(pl.BlockSpec(arr.shape, lambda i, _n=arr.ndim: (0,) * _n))
        else:
            _, off, width = spec
            in_specs.append(pl.BlockSpec((tm, width), lambda i, _b=off // width: (i, _b)))
    in_specs += [ANY] * len(deps)
    out_specs, out_shapes = [], []
    for shape, dtype, spec in outs:
        out_shapes.append(jax.ShapeDtypeStruct(shape, dtype))
        if spec == "row":
            out_specs.append(pl.BlockSpec((tm, shape[1]), lambda i: (i, 0)))
        else:
            out_specs.append(pl.BlockSpec(shape, lambda i, _n=len(shape): (0,) * _n))
    n_in = len(ins)

    def body(*refs):
        res = fn(*[r[...] for r in refs[:n_in]])
        i = pl.program_id(0)
        for r, (_, _, spec), val in zip(refs[n_in + len(deps):], outs, res):
            if spec == "acc":
                @pl.when(i == 0)
                def _(r=r):
                    r[...] = jnp.zeros_like(r)
                r[...] += val.astype(r.dtype)
            else:
                r[...] = val.astype(r.dtype)

    return _pcall(body, name=name, grid=(T // tm,), in_specs=in_specs, out_specs=out_specs, out_shape=out_shapes)(
        *[a for a, _ in ins], *deps)


def _rms_fwd(name, h, g, deps=()):
    T, D = h.shape
    return _rowwise(name, lambda hv, gv: (_rms(hv, gv),), [(h, "row"), (g, "full")], [((T, D), BF16, "row")], T,
                    _pick(T, (512, 256, 128)), deps)[0]


def _rms_bwd(name, h, g, dxn, dh_in, deps=()):
    T, D = h.shape

    def fn(hv, gv, dv, dh0):
        _, vjp = jax.vjp(_rms, hv, gv)
        dh, dg = vjp(dv.astype(F32))
        return dh0 + dh, dg

    return _rowwise(name, fn, [(h, "row"), (g, "full"), (dxn, "row"), (dh_in, "row")],
                    [((T, D), F32, "row"), ((1, D), F32, "acc")], T, _pick(T, (512, 256, 128)), deps)


def _ffn_fwd(name, xn, h, w1, w3, w2):
    T, D = xn.shape
    F8 = w1.shape[-1]
    tm = _pick(T, (512, 256, 128))
    wspec = lambda r, c: pl.BlockSpec((None, r, c), lambda i, d: (d, 0, 0))
    row = pl.BlockSpec((tm, D), lambda i, d: (i, 0))
    hid = pl.BlockSpec((None, tm, F8), lambda i, d: (d, i, 0))

    def body(xn_ref, h_ref, w1_ref, w3_ref, w2_ref, out_ref, h1_ref, h3_ref, acc):
        d = pl.program_id(1)

        @pl.when(d == 0)
        def _():
            acc[...] = jnp.zeros_like(acc)

        x = xn_ref[...]
        h1 = jnp.dot(x, w1_ref[...], preferred_element_type=F32)
        h3 = jnp.dot(x, w3_ref[...], preferred_element_type=F32)
        h1_ref[...] = h1.astype(BF16)
        h3_ref[...] = h3.astype(BF16)
        act = (h1 * jax.nn.sigmoid(h1) * h3).astype(BF16)
        acc[...] += jnp.dot(act, w2_ref[...], preferred_element_type=F32)

        @pl.when(d == N_DEV - 1)
        def _():
            out_ref[...] = h_ref[...] + 0.5 * acc[...]

    return _pcall(body, name=name, grid=(T // tm, N_DEV),
                  in_specs=[row, row, wspec(D, F8), wspec(D, F8), wspec(F8, D)],
                  out_specs=[row, hid, hid],
                  out_shape=[jax.ShapeDtypeStruct((T, D), F32), jax.ShapeDtypeStruct((N_DEV, T, F8), BF16),
                             jax.ShapeDtypeStruct((N_DEV, T, F8), BF16)],
                  scratch_shapes=[pltpu.VMEM((tm, D), F32)])(xn, h, w1, w3, w2)


FFN_ROW_GROUPS = 2


def _ffn_bwd(name, dy, h1, h3, w1, w3, w2, deps=()):
    T, D = dy.shape
    F8 = w1.shape[-1]
    tm = _pick(T, (512, 256, 128))
    wspec = lambda r, c: pl.BlockSpec((None, r, c), lambda i, d: (d, 0, 0))
    row = pl.BlockSpec((tm, D), lambda i, d: (i, 0))
    hid = pl.BlockSpec((None, tm, F8), lambda i, d: (d, i, 0))

    def body(*refs):
        dy_ref, h1_ref, h3_ref, w1_ref, w3_ref, w2_ref = refs[:6]
        dxn_ref, dyb, dh1_ref, dh3_ref, act_ref, acc = refs[6 + len(deps):]
        d = pl.program_id(1)

        @pl.when(d == 0)
        def _():
            acc[...] = jnp.zeros_like(acc)
            dyb[...] = dy_ref[...].astype(BF16)

        groups = [pl.ds(r * (tm // FFN_ROW_GROUPS), tm // FFN_ROW_GROUPS) for r in range(FFN_ROW_GROUPS)]
        dacts = [0.5 * lax.dot_general(dyb[rows, :], w2_ref[...], _DIMS["nt"], preferred_element_type=F32)
                 for rows in groups]
        for rows, dact in zip(groups, dacts):
            h1 = h1_ref[rows, :].astype(F32)
            h3 = h3_ref[rows, :].astype(F32)
            sig = jax.nn.sigmoid(h1)
            silu = h1 * sig
            dh1 = (dact * h3 * (sig * (1.0 + h1 * (1.0 - sig)))).astype(BF16)
            dh3 = (dact * silu).astype(BF16)
            dh1_ref[rows, :] = dh1
            dh3_ref[rows, :] = dh3
            act_ref[rows, :] = (silu * h3).astype(BF16)
            acc[rows, :] += (lax.dot_general(dh1, w1_ref[...], _DIMS["nt"], preferred_element_type=F32)
                             + lax.dot_general(dh3, w3_ref[...], _DIMS["nt"], preferred_element_type=F32))

        @pl.when(d == N_DEV - 1)
        def _():
            dxn_ref[...] = acc[...]

    hshape = jax.ShapeDtypeStruct((N_DEV, T, F8), BF16)
    return _pcall(body, name=name, grid=(T // tm, N_DEV),
                  in_specs=[row, hid, hid, wspec(D, F8), wspec(D, F8), wspec(F8, D)] + [ANY] * len(deps),
                  out_specs=[row, row, hid, hid, hid],
                  out_shape=[jax.ShapeDtypeStruct((T, D), F32), jax.ShapeDtypeStruct((T, D), BF16), hshape, hshape, hshape],
                  scratch_shapes=[pltpu.VMEM((tm, D), F32)])(dy, h1, h3, w1, w3, w2, *deps)


def _rot_matrix():
    i = jnp.arange(QK)[:, None]
    j = jnp.arange(QK)[None, :]
    half = ROPE // 2
    first = (j >= NOPE) & (j < NOPE + half) & (i == j + half)
    second = (j >= NOPE + half) & (i == j - half)
    return jnp.where(first, -1.0, jnp.where(second, 1.0, 0.0)).astype(F32)


def _mla_fn(cq, ckv, kr128, cos, sin, rot, qa_g, kva_g, qn_g, kn_g, w_uq, w_ukv):
    cqn = _rms(cq, qa_g)
    ckvn = _rms(ckv, kva_g)
    kr = kr128[:, :ROPE]
    qs, ks, vs = [], [], []
    for h in range(HEADS):
        qh = _rms(_bdot(cqn, w_uq[h]), qn_g)
        qs.append((qh * cos + _permute(qh, rot) * sin) * SCORE_SCALE)
        kvh = _bdot(ckvn, w_ukv[h])
        kh = _rms(jnp.concatenate([kvh[:, :NOPE], kr], axis=-1), kn_g)
        ks.append(kh * cos + _permute(kh, rot) * sin)
        vs.append(kvh[:, NOPE:])
    return qs, ks, vs


def _mla_specs(z, tabs, small, w_uq, w_ukv, tm, offs):
    o_cq, o_ckv, o_kr = offs
    row = lambda w: pl.BlockSpec((tm, w), lambda i: (i, 0))
    col = lambda off, w: pl.BlockSpec((tm, w), lambda i: (i, off // w))
    full2 = lambda a: pl.BlockSpec(a.shape, lambda i: (0, 0))
    wsp = lambda a: pl.BlockSpec(a.shape, lambda i: (0, 0, 0))
    cq_w, ckv_w = w_uq.shape[1], w_ukv.shape[1]
    ins = [z, z, z, tabs[0], tabs[1], tabs[2]] + list(small) + [w_uq, w_ukv]
    specs = ([col(o_cq, cq_w), col(o_ckv, ckv_w), col(o_kr, LANE), row(QK), row(QK), full2(tabs[2])]
             + [full2(s) for s in small] + [wsp(w_uq), wsp(w_ukv)])
    return ins, specs


def _mla_prep_fwd(name, z, tabs, small, w_uq, w_ukv, offs):
    T = z.shape[0]
    tm = _pick(T, (512, 256, 128))
    ins, specs = _mla_specs(z, tabs, small, w_uq, w_ukv, tm, offs)
    head = lambda w: pl.BlockSpec((HEADS, tm, w), lambda i: (0, i, 0))

    def body(*refs):
        vals = [r[...] for r in refs[:12]]
        q_ref, k_ref, v_ref = refs[12:]
        qs, ks, vs = _mla_fn(*vals)
        for h in range(HEADS):
            q_ref[h] = qs[h].astype(BF16)
            k_ref[h] = ks[h].astype(BF16)
            v_ref[h] = vs[h].astype(BF16)

    return _pcall(body, name=name, grid=(T // tm,), in_specs=specs, out_specs=[head(QK), head(QK), head(VD)],
                  out_shape=[jax.ShapeDtypeStruct((HEADS, T, QK), BF16), jax.ShapeDtypeStruct((HEADS, T, QK), BF16),
                             jax.ShapeDtypeStruct((HEADS, T, VD), BF16)])(*ins)


def _mla_prep_bwd(name, z, tabs, small, w_uq, w_ukv, offs, dq, dk, dv):
    T = z.shape[0]
    tm = _pick(T, (512, 256, 128))
    ins, specs = _mla_specs(z, tabs, small, w_uq, w_ukv, tm, offs)
    head = lambda w: pl.BlockSpec((HEADS, tm, w), lambda i: (0, i, 0))
    ins += [dq, dk, dv]
    specs += [head(QK), head(QK), head(VD)]
    cq_w, ckv_w = w_uq.shape[1], w_ukv.shape[1]
    acc_shapes = [s.shape for s in small] + [w_uq.shape, w_ukv.shape]
    row_shapes = [(T, cq_w), (T, ckv_w), (T, LANE)]
    out_shape = [jax.ShapeDtypeStruct(s, BF16) for s in row_shapes] + [jax.ShapeDtypeStruct(s, F32) for s in acc_shapes]
    out_specs = ([pl.BlockSpec((tm, s[1]), lambda i: (i, 0)) for s in row_shapes]
                 + [pl.BlockSpec(s, lambda i, _n=len(s): (0,) * _n) for s in acc_shapes])

    def body(*refs):
        cq, ckv, kr128, cos, sin, rot, qa_g, kva_g, qn_g, kn_g, w_uq_v, w_ukv_v = [r[...] for r in refs[:12]]
        dq_ref, dk_ref, dv_ref = refs[12:15]
        outs = refs[15:]
        f = lambda a, b, c, g1, g2, g3, g4, wq, wkv: _mla_fn(a, b, c, cos, sin, rot, g1, g2, g3, g4, wq, wkv)
        _, vjp = jax.vjp(f, cq, ckv, kr128, qa_g, kva_g, qn_g, kn_g, w_uq_v.astype(F32), w_ukv_v.astype(F32))
        cts = ([dq_ref[h] for h in range(HEADS)], [dk_ref[h] for h in range(HEADS)], [dv_ref[h] for h in range(HEADS)])
        grads = vjp(cts)
        i = pl.program_id(0)
        for n, (r, gval) in enumerate(zip(outs, grads)):
            if n < 3:
                r[...] = gval.astype(r.dtype)
            else:
                @pl.when(i == 0)
                def _(r=r):
                    r[...] = jnp.zeros_like(r)
                r[...] += gval

    return _pcall(body, name=name, grid=(T // tm,), in_specs=specs, out_specs=out_specs, out_shape=out_shape)(*ins)


NEG = -1e30


ATTN_ROW_GROUPS = 2


def _tri_block(rows, cols, row0):
    return lax.broadcasted_iota(jnp.int32, (rows, cols), 1) <= lax.broadcasted_iota(jnp.int32, (rows, cols), 0) + row0


def _attn_tiles(T):
    t = _pick(T, (1024, 512, 256, 128))
    return t, T // t


def _attn_fwd(name, q, k, v):
    H, T, _ = q.shape
    t, n = _attn_tiles(T)

    def body(q_ref, k_ref, v_ref, o_ref, lse_ref, m_s, l_s, acc):
        qi, ki = pl.program_id(1), pl.program_id(2)

        @pl.when(ki == 0)
        def _():
            m_s[...] = jnp.full_like(m_s, NEG)
            l_s[...] = jnp.zeros_like(l_s)
            acc[...] = jnp.zeros_like(acc)

        def tile(diagonal):
            s = lax.dot_general(q_ref[...], k_ref[...], _DIMS["nt"], preferred_element_type=F32)
            if diagonal:
                s = jnp.where(_tri_block(t, t, 0), s, NEG)
            m_new = jnp.maximum(m_s[...], jnp.max(s, axis=-1, keepdims=True))
            alpha = jnp.exp(m_s[...] - m_new)
            p = jnp.exp(s - m_new)
            l_s[...] = alpha * l_s[...] + jnp.sum(p, axis=-1, keepdims=True)
            acc[...] = alpha * acc[...] + jnp.dot(p.astype(BF16), v_ref[...], preferred_element_type=F32)
            m_s[...] = m_new

        @pl.when(ki < qi)
        def _():
            tile(False)

        @pl.when(ki == qi)
        def _():
            tile(True)
            o_ref[...] = acc[...] / l_s[...]
            lse_ref[...] = m_s[...] + jnp.log(l_s[...])

    kv = lambda w: pl.BlockSpec((None, t, w), lambda h, qi, ki: (h, jnp.minimum(ki, qi), 0))
    return _pcall(body, name=name, grid=(H, n, n),
                  in_specs=[pl.BlockSpec((None, t, QK), lambda h, qi, ki: (h, qi, 0)), kv(QK), kv(VD)],
                  out_specs=[pl.BlockSpec((t, VD), lambda h, qi, ki: (qi, h)),
                             pl.BlockSpec((None, t, 1), lambda h, qi, ki: (h, qi, 0))],
                  out_shape=[jax.ShapeDtypeStruct((T, H * VD), F32), jax.ShapeDtypeStruct((H, T, 1), F32)],
                  scratch_shapes=[pltpu.VMEM((t, 1), F32), pltpu.VMEM((t, 1), F32), pltpu.VMEM((t, VD), F32)])(q, k, v)


def _attn_bwd(name, q, k, v, o, lse, do):
    H, T, _ = q.shape
    t, n = _attn_tiles(T)

    def body(q_ref, k_ref, v_ref, o_ref, lse_ref, do_ref, dq_ref, dk_ref, dv_ref, dk_acc, dv_acc):
        ki, qi = pl.program_id(1), pl.program_id(2)

        @pl.when((ki == 0) & (qi == 0))
        def _():
            dq_ref[...] = jnp.zeros_like(dq_ref)

        @pl.when(qi == 0)
        def _():
            dk_acc[...] = jnp.zeros_like(dk_acc)
            dv_acc[...] = jnp.zeros_like(dv_acc)

        def tile(diagonal):
            g = t // ATTN_ROW_GROUPS
            for r in range(ATTN_ROW_GROUPS):
                rows = pl.ds(r * g, g)
                cols = (r + 1) * g if diagonal else t
                qv, kv_, dov = q_ref[rows, :], k_ref[:cols, :], do_ref[rows, :]
                s = lax.dot_general(qv, kv_, _DIMS["nt"], preferred_element_type=F32)
                p = jnp.exp(s - lse_ref[rows, :])
                if diagonal:
                    p = jnp.where(_tri_block(g, cols, r * g), p, 0.0)
                dob = dov.astype(BF16)
                delta = jnp.sum(o_ref[rows, :] * dov, axis=-1, keepdims=True)
                dv_acc[:cols, :] += lax.dot_general(p.astype(BF16), dob, _DIMS["tn"], preferred_element_type=F32)
                dp = lax.dot_general(dob, v_ref[:cols, :], _DIMS["nt"], preferred_element_type=F32)
                ds = (p * (dp - delta)).astype(BF16)
                dq_rows = pl.ds(pl.multiple_of(qi * t + r * g, g), g)
                dq_ref[dq_rows, :] += jnp.dot(ds, kv_, preferred_element_type=F32)
                dk_acc[:cols, :] += lax.dot_general(ds, qv, _DIMS["tn"], preferred_element_type=F32)

        @pl.when(qi > ki)
        def _():
            tile(False)

        @pl.when(qi == ki)
        def _():
            tile(True)

        @pl.when(qi == n - 1)
        def _():
            dk_ref[...] = dk_acc[...]
            dv_ref[...] = dv_acc[...]

    qrow = lambda w: pl.BlockSpec((None, t, w), lambda h, ki, qi: (h, jnp.maximum(qi, ki), 0))
    krow = lambda w: pl.BlockSpec((None, t, w), lambda h, ki, qi: (h, ki, 0))
    wide = pl.BlockSpec((t, VD), lambda h, ki, qi: (jnp.maximum(qi, ki), h))
    return _pcall(body, name=name, grid=(H, n, n),
                  in_specs=[qrow(QK), krow(QK), krow(VD), wide, qrow(1), wide],
                  out_specs=[pl.BlockSpec((None, T, QK), lambda h, ki, qi: (h, 0, 0)), krow(QK), krow(VD)],
                  out_shape=[jax.ShapeDtypeStruct((H, T, QK), F32), jax.ShapeDtypeStruct((H, T, QK), F32),
                             jax.ShapeDtypeStruct((H, T, VD), F32)],
                  scratch_shapes=[pltpu.VMEM((t, QK), F32), pltpu.VMEM((t, VD), F32)])(q, k, v, o, lse, do)


def _tril():
    return lax.broadcasted_iota(jnp.int32, (CHUNK, CHUNK), 1) <= lax.broadcasted_iota(jnp.int32, (CHUNK, CHUNK), 0)


@jax.custom_vjp
def _gm_gate(v, ws, b_t):
    wc = jnp.where(_tril()[None], ws, 0.0).astype(BF16)
    vb = v.astype(BF16)
    rows = []
    for c in range(v.shape[0] // CHUNK):
        cols = []
        for g in range(GROUPS):
            vc = vb[c * CHUNK:(c + 1) * CHUNK, g * LANE:(g + 1) * LANE]
            cols.append(jnp.dot(wc[g], vc, preferred_element_type=F32) + jnp.broadcast_to(b_t[:, g:g + 1], (CHUNK, LANE)))
        rows.append(jnp.concatenate(cols, axis=-1))
    return jnp.concatenate(rows, axis=0)


def _gm_gate_fwd(v, ws, b_t):
    return _gm_gate(v, ws, b_t), (v, ws)


def _gm_gate_bwd(res, dgate):
    v, ws = res
    tril = _tril()
    wc = jnp.where(tril[None], ws, 0.0).astype(BF16)
    vb = v.astype(BF16)
    dgb = dgate.astype(BF16)
    dws = [jnp.zeros((CHUNK, CHUNK), F32) for _ in range(GROUPS)]
    db = jnp.zeros((CHUNK, GROUPS), F32)
    lane_g = lax.broadcasted_iota(jnp.int32, (1, GROUPS), 1)
    rows = []
    for c in range(v.shape[0] // CHUNK):
        cols = []
        for g in range(GROUPS):
            sl = (slice(c * CHUNK, (c + 1) * CHUNK), slice(g * LANE, (g + 1) * LANE))
            cols.append(lax.dot_general(wc[g], dgb[sl], _DIMS["tn"], preferred_element_type=F32))
            dws[g] = dws[g] + lax.dot_general(dgb[sl], vb[sl], _DIMS["nt"], preferred_element_type=F32)
            db = db + jnp.sum(dgate[sl], axis=1, keepdims=True) * (lane_g == g).astype(F32)
        rows.append(jnp.concatenate(cols, axis=-1))
    dws = jnp.stack([jnp.where(tril, d, 0.0) for d in dws])
    return jnp.concatenate(rows, axis=0), dws, db


_gm_gate.defvjp(_gm_gate_fwd, _gm_gate_bwd)


def _mix_fn(a_out, zu, zv, aon_g, gon_g, vn_g, ws, b_t):
    u = jax.nn.gelu(zu)
    vv = _rms(jax.nn.gelu(zv), vn_g)
    g_out = u * _gm_gate(vv, ws, b_t)
    return jnp.concatenate([_rms(a_out, aon_g), _rms(g_out, gon_g)], axis=-1)


def _mix_ins(a_out, z, small, offs):
    gw = a_out.shape[1]
    return [(a_out, "row"), (z, ("cols", offs[0], gw)), (z, ("cols", offs[1], gw))] + [(s, "full") for s in small]


def _mix_fwd(name, a_out, z, small, offs):
    T, gw = a_out.shape
    return _rowwise(name, lambda *a: (_mix_fn(*a),), _mix_ins(a_out, z, small, offs), [((T, 2 * gw), BF16, "row")],
                    T, _pick(T, (512, 256, 128)))[0]


def _mix_bwd(name, a_out, z, small, offs, dmixed):
    T, gw = a_out.shape

    def fn(*a):
        _, vjp = jax.vjp(_mix_fn, *a[:-1])
        return vjp(a[-1].astype(F32))

    outs = [((T, gw), F32, "row"), ((T, gw), BF16, "row"), ((T, gw), BF16, "row")] + [(s.shape, F32, "acc") for s in small]
    return _rowwise(name, fn, _mix_ins(a_out, z, small, offs) + [(dmixed, "row")], outs, T, _pick(T, (256, 128)))


def _ple_fn(gl, pe, g):
    return jax.nn.sigmoid(gl) * _rms(pe, g)


def _ple_fwd(name, h, gl, pe, g):
    T, D = h.shape
    return _rowwise(name, lambda hv, a, b, c: (hv + _ple_fn(a, b, c),),
                    [(h, "row"), (gl, "row"), (pe, "row"), (g, "full")], [((T, D), F32, "row")], T,
                    _pick(T, (512, 256, 128)))[0]


def _ple_bwd(name, gl, pe, g, dh, deps=()):
    T, D = gl.shape

    def fn(a, b, c, d):
        _, vjp = jax.vjp(_ple_fn, a, b, c)
        return vjp(d)

    return _rowwise(name, fn, [(gl, "row"), (pe, "row"), (g, "full"), (dh, "row")],
                    [((T, D), BF16, "row"), ((T, D), BF16, "row"), ((1, D), F32, "acc")], T, _pick(T, (256, 128)), deps)


def _loss(name, y, target):
    T, D = y.shape

    def fn(yv, tv):
        err = yv - tv
        part = 0.5 * jnp.sum(jnp.mean(err * err, axis=-1, keepdims=True), axis=0, keepdims=True)
        return err * (1.0 / D), jnp.broadcast_to(part, (8, LANE))

    return _rowwise(name, fn, [(y, "row"), (target, "row")], [((T, D), F32, "row"), ((8, LANE), F32, "acc")], T,
                    _pick(T, (512, 256, 128)))


ADAMW_BLOCK_ELEMS = 256 * 1024


def _adamw_sum(name, parts, w, m, v):
    L, R, C = w.shape
    tiles = [(r, c) for r in (R, 512, 256, 128, 64, 32, 16) for c in (C, 1024, 512, 256, 128)
             if R % r == 0 and C % c == 0 and r * c <= ADAMW_BLOCK_ELEMS]
    tr, tc = max(tiles, key=lambda rc: (rc[0] * rc[1], rc[1]))
    nr, nc = R // tr, C // tc
    c1 = 1.0 - ADAM_B1 ** ADAM_STEP
    c2 = 1.0 - ADAM_B2 ** ADAM_STEP

    def body(*refs):
        p_refs = refs[:L]
        w_ref, m_ref, v_ref, g_out, d_out, m_out, v_out = refs[L:]
        layer = pl.program_id(0)

        def part(s):
            val = p_refs[0][s].astype(F32)
            for j in range(1, L):
                val = jnp.where(layer == j, p_refs[j][s].astype(F32), val)
            return val

        g = part(0)
        for s in range(1, N_DEV):
            g = g + part(s)
        m2 = ADAM_B1 * m_ref[...] + (1.0 - ADAM_B1) * g
        v2 = ADAM_B2 * v_ref[...] + (1.0 - ADAM_B2) * (g * g)
        g_out[...] = g
        m_out[...] = m2
        v_out[...] = v2
        d_out[...] = -ADAM_LR * ((m2 / c1) / (jnp.sqrt(v2 / c2) + ADAM_EPS) + ADAM_WD * w_ref[...])

    def part_spec(j):
        def index(l, i, k):
            before, mine = l < j, l == j
            return (0, jnp.where(mine, i, jnp.where(before, 0, nr - 1)), jnp.where(mine, k, jnp.where(before, 0, nc - 1)))
        return pl.BlockSpec((N_DEV, tr, tc), index)

    blk = pl.BlockSpec((None, tr, tc), lambda l, i, k: (l, i, k))
    sd = jax.ShapeDtypeStruct((L, R, C), F32)
    return _pcall(body, name=name, grid=(L, nr, nc),
                  in_specs=[part_spec(j) for j in range(L)] + [blk, blk, blk],
                  out_specs=[blk, blk, blk, blk], out_shape=[sd, sd, sd, sd])(*parts, w, m, v)


def _unshard_cols(g):
    _, K, n = g.shape
    return g.transpose(1, 0, 2).reshape(K, N_DEV * n)


def _shard_cols(full):
    K, N = full.shape
    return full.reshape(K, N_DEV, N // N_DEV).transpose(1, 0, 2)


def _unshard_rows(g):
    _, k, N = g.shape
    return g.reshape(N_DEV * k, N)


def _shard_rows(full):
    K, N = full.shape
    return full.reshape(N_DEV, K // N_DEV, N)


STAGES = (('ffn_a_w1', 'ffn_a_w3', 'ffn_a_w2'), ('w_in', 'w_uq', 'w_ukv', 'w_out'),
          ('ffn_b_w1', 'ffn_b_w3', 'ffn_b_w2'), ('w_ple_gate', 'w_ple'))
TRANSPOSED = ('ffn_a_w1', 'ffn_a_w3', 'ffn_b_w1', 'ffn_b_w3', 'w_in', 'w_uq')


def _step(x, p, positions, target, w, m, v):
    T, D = x.shape[1], x.shape[2]
    L = p.shape[0]
    x2, target2 = x[0], target[0]
    q_rank, kv_rank = w['w_uq'].shape[1], w['w_ukv'].shape[1]
    gw = w['gm_v_norm'].shape[1]

    inv_freq = ROPE_BASE ** (-jnp.arange(0, ROPE, 2, dtype=F32) / ROPE)
    ang = positions[0].astype(F32)[:, None] * inv_freq
    cos = jnp.concatenate([jnp.ones((T, NOPE), F32), jnp.cos(ang), jnp.cos(ang)], axis=-1)
    sin = jnp.concatenate([jnp.zeros((T, NOPE), F32), jnp.sin(ang), jnp.sin(ang)], axis=-1)
    tabs = (cos, sin, _rot_matrix().astype(BF16))

    groups = [(l, names) for l in range(L) for names in STAGES]

    def ag_start(k, after):
        l, names = groups[k]
        shards = [w[n][l].astype(BF16) for n in names]
        return _xchg_start(f"ag_chips{k}", "chips", shards, [_landing(s) for s in shards], after)

    ag = {0: ag_start(0, [])}
    ag[1] = ag_start(1, [ag[0]['token']])

    def fetch(k, after):
        lands = _xchg_wait(f"ag_landed{k}", ag[k], after)
        fw = _xchg_start(f"ag_forward{k}", "forward", [], lands, [])
        deps = [fw['token']]
        for nxt in {0: (), 1: (2, 3)}.get(k, (k + 2,)):
            if nxt < len(groups):
                ag[nxt] = ag_start(nxt, deps)
                deps = [ag[nxt]['token']]
        if k == 0:
            deps = deps + [ag[1]['token']]
        return fw, deps

    def gathered(k, fw, after):
        return dict(zip(groups[k][1], _xchg_wait(f"ag_wait{k}", fw, after)))

    s0, s1, s2, s3 = q_rank, q_rank + kv_rank, q_rank + kv_rank + ROPE, q_rank + kv_rank + ROPE + gw
    o_u, o_v, o_cq, o_ckv, o_kr = 0, gw, 2 * gw, 2 * gw + q_rank, 2 * gw + q_rank + kv_rank
    kr_pad = 2 * LANE - ROPE

    def g2(name, l):
        return w[name][l][None, :]

    gm_bt = [w['gm_bs'][l].T for l in range(L)]

    saved = []
    h = x2
    for l in range(L):
        s = {}
        fw, deps = fetch(4 * l, [h])
        s['h0'] = h
        s['xn_a'] = _rms_fwd(f"rms_a{l}", h, g2('ffn_a_norm', l), deps)
        wa = s['wa'] = gathered(4 * l, fw, [s['xn_a']])
        h, s['a_h1'], s['a_h3'] = _ffn_fwd(f"ffn_a_fwd{l}", s['xn_a'], h, wa['ffn_a_w1'], wa['ffn_a_w3'], wa['ffn_a_w2'])
        fw, deps = fetch(4 * l + 1, [h])
        s['h1'] = h
        s['n'] = _rms_fwd(f"rms_mix{l}", h, g2('mix_norm', l), deps)
        wm = gathered(4 * l + 1, fw, [s['n']])
        w_in_full = _unshard_cols(wm['w_in'])
        s['w_in'] = jnp.concatenate([w_in_full[:, s2:s3], w_in_full[:, s3:], w_in_full[:, :s0], w_in_full[:, s0:s1],
                                     w_in_full[:, s1:s2], jnp.zeros((D, kr_pad), BF16)], axis=-1)
        s['w_out'] = _unshard_rows(wm['w_out'])
        s['w_uq'], s['w_ukv'] = wm['w_uq'], wm['w_ukv']
        s['z'] = _mm(f"w_in{l}", s['n'], s['w_in'], "nn")
        s['mla_small'] = [g2('q_a_norm', l), g2('kv_a_norm', l), g2('q_norm', l), g2('k_norm', l)]
        s['q'], s['k'], s['v'] = _mla_prep_fwd(f"mla_prep{l}", s['z'], tabs, s['mla_small'], s['w_uq'], s['w_ukv'],
                                               (o_cq, o_ckv, o_kr))
        s['a_out'], s['lse'] = _attn_fwd(f"attn_fwd{l}", s['q'], s['k'], s['v'])
        s['mix_small'] = [g2('attn_out_norm', l), g2('gm_out_norm', l), g2('gm_v_norm', l), w['gm_ws'][l], gm_bt[l]]
        s['mixed'] = _mix_fwd(f"mix_fwd{l}", s['a_out'], s['z'], s['mix_small'], (o_u, o_v))
        h = _mm(f"w_out{l}", s['mixed'], s['w_out'], "nn", res=h)
        fw, deps = fetch(4 * l + 2, [h])
        s['h2'] = h
        s['xn_b'] = _rms_fwd(f"rms_b{l}", h, g2('ffn_b_norm', l), deps)
        wb = s['wb'] = gathered(4 * l + 2, fw, [s['xn_b']])
        h, s['b_h1'], s['b_h3'] = _ffn_fwd(f"ffn_b_fwd{l}", s['xn_b'], h, wb['ffn_b_w1'], wb['ffn_b_w3'], wb['ffn_b_w2'])
        fw, deps = fetch(4 * l + 3, [h])
        s['h3'] = h
        s['xn_g'] = _rms_fwd(f"rms_g{l}", h, g2('ple_gate_norm', l), deps)
        wp = gathered(4 * l + 3, fw, [s['xn_g']])
        s['w_gate'] = _unshard_rows(wp['w_ple_gate'])
        s['gl'] = _mm(f"w_gate{l}", s['xn_g'], s['w_gate'], "nn")
        s['p'] = p[l, 0]
        s['pe'] = _mm(f"w_ple{l}", s['p'], _unshard_cols(wp['w_ple']), "nn")
        h = _ple_fwd(f"ple_fwd{l}", h, s['gl'], s['pe'], g2('ple_norm', l))
        saved.append(s)

    dh, loss_part = _loss("loss", h, target2)
    loss = lax.psum(loss_part[0, 0], AXES)

    gsmall = {n: [None] * L for n in SMALL}
    rs, where = {}, {}
    sizes = [w[n].size for n in SMALL]
    total = sum(sizes)
    padded = -(-total // (512 * LANE)) * (512 * LANE)

    def pack(d):
        flat = jnp.concatenate([d[n].reshape(-1) for n in SMALL] + [jnp.zeros((padded - total,), F32)])
        return flat.reshape(1, padded // LANE, LANE)

    def rs_start(key, l, named):
        grads = [g for _, g in named]
        lands = [lax.dynamic_update_slice(lax.empty(g.shape, g.dtype),
                                          lax.dynamic_index_in_dim(g, _my_index(), 0, keepdims=True),
                                          (_my_index(),) + (0,) * (g.ndim - 1)) for g in grads]
        rs[key] = _xchg_start("rs_start_" + key, "scatter", grads, lands, [])
        where.update({(n, l): (key, i) for i, (n, _) in enumerate(named)})
        return [rs[key]['token']]

    def ffn_backward(tag, l, dh, xn, h_in, h1, h3, wts, norm_name, deps):
        pre = 'ffn_' + tag
        dxn, dhb, dh1, dh3, act = _ffn_bwd(f"{pre}_bwd{l}", dh, h1, h3, wts[pre + '_w1'], wts[pre + '_w3'],
                                           wts[pre + '_w2'], deps)
        g1 = _mm_tn_batch(f"{pre}_dw1_{l}", dh1, xn[None])
        deps = rs_start(f"{pre}_w1_{l}", l, [(pre + '_w1', g1)])
        g3 = _mm_tn_batch(f"{pre}_dw3_{l}", dh3, xn[None], deps=deps)
        deps = rs_start(f"{pre}_w3_{l}", l, [(pre + '_w3', g3)])
        g2_ = _mm_tn_batch(f"{pre}_dw2_{l}", act, dhb[None], alpha=0.5, deps=deps)
        deps = rs_start(f"{pre}_w2_{l}", l, [(pre + '_w2', g2_)])
        return _rms_bwd(f"rms_{tag}_bwd{l}", h_in, g2(norm_name, l), dxn, dh, deps)

    deps = []
    for l in reversed(range(L)):
        s = saved[l]
        d_gl, d_pe, gsmall['ple_norm'][l] = _ple_bwd(f"ple_bwd{l}", s['gl'], s['pe'], g2('ple_norm', l), dh, deps)
        g_ple = _mm(f"dw_ple{l}", s['p'], d_pe, "tn", out_dtype=BF16)
        g_gate = _mm(f"dw_gate{l}", s['xn_g'], d_gl, "tn", out_dtype=BF16)
        d_xng = _mm(f"d_xng{l}", d_gl, s['w_gate'], "nt")
        dh, gsmall['ple_gate_norm'][l] = _rms_bwd(f"rms_g_bwd{l}", s['h3'], g2('ple_gate_norm', l), d_xng, dh)
        deps = rs_start(f"ple_{l}", l, [('w_ple_gate', _shard_rows(g_gate)), ('w_ple', _shard_cols(g_ple))])
        dh, gsmall['ffn_b_norm'][l] = ffn_backward('b', l, dh, s['xn_b'], s['h2'], s['b_h1'], s['b_h3'], s['wb'],
                                                   'ffn_b_norm', deps)
        g_out = _mm(f"dw_out{l}", s['mixed'], dh, "tn", out_dtype=BF16)
        d_mixed = _mm(f"d_mixed{l}", dh, s['w_out'], "nt")
        mix = _mix_bwd(f"mix_bwd{l}", s['a_out'], s['z'], s['mix_small'], (o_u, o_v), d_mixed)
        d_a_out, d_u, d_v = mix[:3]
        gsmall['attn_out_norm'][l], gsmall['gm_out_norm'][l], gsmall['gm_v_norm'][l], gsmall['gm_ws'][l] = mix[3:7]
        gsmall['gm_bs'][l] = mix[7].T
        dq, dk, dv = _attn_bwd(f"attn_bwd{l}", s['q'], s['k'], s['v'], s['a_out'], s['lse'], d_a_out)
        mla = _mla_prep_bwd(f"mla_prep_bwd{l}", s['z'], tabs, s['mla_small'], s['w_uq'], s['w_ukv'], (o_cq, o_ckv, o_kr),
                            dq, dk, dv)
        d_cq, d_ckv, d_kr = mla[:3]
        gsmall['q_a_norm'][l], gsmall['kv_a_norm'][l], gsmall['q_norm'][l], gsmall['k_norm'][l] = mla[3:7]
        dz = jnp.concatenate([d_u, d_v, d_cq, d_ckv, d_kr, jnp.zeros((T, LANE), BF16)], axis=-1)
        g_in = _mm(f"dw_in{l}", dz, s['n'], "tn", out_dtype=BF16)
        g_in = jnp.concatenate([g_in[o_cq:o_cq + q_rank], g_in[o_ckv:o_ckv + kv_rank], g_in[o_kr:o_kr + ROPE],
                                g_in[o_u:o_u + gw], g_in[o_v:o_v + gw]], axis=0)
        d_n = _mm(f"d_n{l}", dz, s['w_in'], "nt")
        dh, gsmall['mix_norm'][l] = _rms_bwd(f"rms_mix_bwd{l}", s['h1'], g2('mix_norm', l), d_n, dh)
        deps = rs_start(f"mix_{l}", l, [('w_in', _shard_rows(g_in)), ('w_uq', mla[7].transpose(0, 2, 1).astype(BF16)),
                                        ('w_ukv', mla[8].astype(BF16)), ('w_out', _shard_rows(g_out))])
        if l == 0:
            gsmall['ffn_a_norm'][0] = jnp.zeros((1, D), F32)
            gs = pack({n: jnp.stack([gsmall[n][k].reshape(w[n].shape[1:]) for k in range(L)]) for n in SMALL})
            small = _xchg_start("small_start", "gather", [gs[0]], [_landing(gs[0])], [])
            deps = deps + [small['token']]
        dh, gsmall['ffn_a_norm'][l] = ffn_backward('a', l, dh, s['xn_a'], s['h0'], s['a_h1'], s['a_h3'], s['wa'],
                                                   'ffn_a_norm', deps)
        deps = []
    grad_x = dh[None]
    head = gsmall['ffn_a_norm'][0].reshape(D // LANE, LANE)
    small_head = _xchg_start("small_head_start", "gather", [head], [_landing(head)], [])

    out = {}
    after = [dh, small_head['token']]
    landed = {}

    def partials(n, l):
        key, i = where[(n, l)]
        if key not in landed:
            landed[key] = _xchg_wait("rs_wait_" + key, rs[key], after)
        return landed[key][i]

    swap = lambda a: a.transpose(0, 2, 1)
    for stage in (3, 2, 1, 0):
        for n in STAGES[stage]:
            parts = [partials(n, l) for l in reversed(range(L))][::-1]
            if n in TRANSPOSED:
                out[n] = [swap(r) for r in _adamw_sum("adamw_" + n, parts, swap(w[n]), swap(m[n]), swap(v[n]))]
            else:
                out[n] = _adamw_sum("adamw_" + n, parts, w[n], m[n], v[n])
            after = [out[n][0]]

    body_parts = _xchg_wait("small_wait", small, after)[0]
    head_parts = _xchg_wait("small_head_wait", small_head, after)[0]
    parts = lax.dynamic_update_slice(body_parts, head_parts, (0, 0, 0))
    res = _adamw_sum("adamw_small", [parts], pack(w), pack(m), pack(v))
    off = 0
    for n, sz in zip(SMALL, sizes):
        out[n] = [r.reshape(-1)[off:off + sz].reshape(w[n].shape) for r in res]
        off += sz

    return (loss, grad_x, *[out[n][0] for n in WEIGHTS], *[out[n][1] for n in WEIGHTS],
            *[out[n][2] for n in WEIGHTS], *[out[n][3] for n in WEIGHTS])


def kernel(x, p, positions, ffn_a_norm, ffn_a_w1, ffn_a_w3, ffn_a_w2, mix_norm, w_in, q_a_norm, w_uq, kv_a_norm, w_ukv, q_norm, k_norm, gm_v_norm, gm_ws, gm_bs, attn_out_norm, gm_out_norm, w_out, ffn_b_norm, ffn_b_w1, ffn_b_w3, ffn_b_w2, ple_gate_norm, w_ple_gate, w_ple, ple_norm, loss_target, m_ffn_a_norm, m_ffn_a_w1, m_ffn_a_w3, m_ffn_a_w2, m_mix_norm, m_w_in, m_q_a_norm, m_w_uq, m_kv_a_norm, m_w_ukv, m_q_norm, m_k_norm, m_gm_v_norm, m_gm_ws, m_gm_bs, m_attn_out_norm, m_gm_out_norm, m_w_out, m_ffn_b_norm, m_ffn_b_w1, m_ffn_b_w3, m_ffn_b_w2, m_ple_gate_norm, m_w_ple_gate, m_w_ple, m_ple_norm, v_ffn_a_norm, v_ffn_a_w1, v_ffn_a_w3, v_ffn_a_w2, v_mix_norm, v_w_in, v_q_a_norm, v_w_uq, v_kv_a_norm, v_w_ukv, v_q_norm, v_k_norm, v_gm_v_norm, v_gm_ws, v_gm_bs, v_attn_out_norm, v_gm_out_norm, v_w_out, v_ffn_b_norm, v_ffn_b_w1, v_ffn_b_w3, v_ffn_b_w2, v_ple_gate_norm, v_w_ple_gate, v_w_ple, v_ple_norm):
    args = locals()
    w = {n: args[n] for n in WEIGHTS}
    m = {n: args["m_" + n] for n in WEIGHTS}
    v = {n: args["v_" + n] for n in WEIGHTS}
    return _step(x, p, positions, loss_target, w, m, v)
```

```python
import functools

import jax
import jax.numpy as jnp
from jax import lax
from jax.experimental import pallas as pl
from jax.experimental.pallas import tpu as pltpu

F32, BF16 = jnp.float32, jnp.bfloat16
EPS = 1e-6
N_DEV = 8
HEADS = 8
NOPE, ROPE, QK, VD = 128, 64, 192, 128
SCORE_SCALE = QK ** -0.5
CHUNK = 128
GROUPS = 8
LANE = 128
ROPE_BASE = 10000.0
ADAM_LR, ADAM_B1, ADAM_B2, ADAM_EPS, ADAM_WD, ADAM_STEP = 0.001, 0.9, 0.999, 1e-08, 0.01, 10
AXES = ("x", "y", "c")
MESH = pl.DeviceIdType.MESH
ANY = pl.BlockSpec(memory_space=pl.ANY)

WEIGHTS = ['ffn_a_norm', 'ffn_a_w1', 'ffn_a_w3', 'ffn_a_w2', 'mix_norm', 'w_in', 'q_a_norm', 'w_uq', 'kv_a_norm',
           'w_ukv', 'q_norm', 'k_norm', 'gm_v_norm', 'gm_ws', 'gm_bs', 'attn_out_norm', 'gm_out_norm', 'w_out',
           'ffn_b_norm', 'ffn_b_w1', 'ffn_b_w3', 'ffn_b_w2', 'ple_gate_norm', 'w_ple_gate', 'w_ple', 'ple_norm']
BIG = ['ffn_a_w1', 'ffn_a_w3', 'ffn_a_w2', 'w_in', 'w_uq', 'w_ukv', 'w_out', 'ffn_b_w1', 'ffn_b_w3', 'ffn_b_w2',
       'w_ple_gate', 'w_ple']
SMALL = [n for n in WEIGHTS if n not in BIG]


def _pcall(body, **kw):
    return pl.pallas_call(body, **kw)


def _pick(n, cands):
    for c in cands:
        if n % c == 0:
            return c
    return n


def _rms(x, g):
    return x * lax.rsqrt(jnp.mean(x * x, axis=-1, keepdims=True) + EPS) * g


@jax.custom_vjp
def _bdot(x, w):
    return jnp.dot(x.astype(BF16), w.astype(BF16), preferred_element_type=F32)


def _bdot_fwd(x, w):
    return _bdot(x, w), (x, w)


def _bdot_bwd(res, dy):
    x, w = res
    dyb = dy.astype(BF16)
    dx = lax.dot_general(dyb, w.astype(BF16), (((1,), (1,)), ((), ())), preferred_element_type=F32)
    dw = lax.dot_general(x.astype(BF16), dyb, (((0,), (0,)), ((), ())), preferred_element_type=F32)
    return dx.astype(x.dtype), dw.astype(w.dtype)


_bdot.defvjp(_bdot_fwd, _bdot_bwd)


def _split_dot(x, p, dims):
    hi = x.astype(BF16)
    lo = (x - hi.astype(F32)).astype(BF16)
    return (lax.dot_general(hi, p, dims, preferred_element_type=F32)
            + lax.dot_general(lo, p, dims, preferred_element_type=F32))


@jax.custom_vjp
def _permute(x, p):
    return _split_dot(x, p, _DIMS["nn"])


def _permute_fwd(x, p):
    return _permute(x, p), p


def _permute_bwd(p, ct):
    return _split_dot(ct, p, _DIMS["nt"]), jnp.zeros_like(p)


_permute.defvjp(_permute_fwd, _permute_bwd)


def _flip(v, bit):
    return 1 - v if bit else v


HBM = pl.BlockSpec(memory_space=pltpu.HBM)
SEM = pl.BlockSpec(memory_space=pltpu.SEMAPHORE)
EFFECT = pltpu.SideEffectType.DATAFLOW_SIDE_EFFECTING
PEERS = N_DEV - 1


def _my_index():
    return 4 * lax.axis_index("x") + 2 * lax.axis_index("y") + lax.axis_index("c")


def _landing(own):
    zone = lax.empty((N_DEV,) + own.shape, own.dtype)
    return lax.dynamic_update_slice(zone, own[None], (_my_index(),) + (0,) * own.ndim)


COPIES = {"gather": PEERS, "scatter": PEERS, "chips": 4, "forward": 3}


def _copy_plan(kind, src_refs, land_refs, send_sems, recv_sems):
    cx, cy, cc = lax.axis_index("x"), lax.axis_index("y"), lax.axis_index("c")
    me = 4 * cx + 2 * cy + cc
    per = COPIES[kind]
    out = []
    for t, land in enumerate(land_refs):
        def pair(i, src, to_slot, from_slot, dev):
            kw = dict(send_sem=send_sems.at[per * t + i], recv_sem=recv_sems.at[per * t + i], device_id=dev,
                      device_id_type=MESH)
            out.append((pltpu.make_async_remote_copy(src_ref=src, dst_ref=land.at[to_slot], **kw),
                        pltpu.make_async_remote_copy(src_ref=src, dst_ref=land.at[from_slot], **kw)))

        if kind in ("gather", "scatter"):
            for k in range(1, N_DEV):
                px, py, pc = _flip(cx, k & 4), _flip(cy, k & 2), _flip(cc, k & 1)
                peer = 4 * px + 2 * py + pc
                pair(k - 1, src_refs[t].at[peer] if kind == "scatter" else src_refs[t], me, peer, (px, py, pc))
        elif kind == "chips":
            pair(0, src_refs[t], me, me + 1 - 2 * cc, (cx, cy, 1 - cc))
            for j in range(1, 4):
                px, py = _flip(cx, j & 2), _flip(cy, j & 1)
                pair(j, src_refs[t], me, 4 * px + 2 * py + cc, (px, py, cc))
        else:
            for j in range(1, 4):
                px, py = _flip(cx, j & 2), _flip(cy, j & 1)
                mine, theirs = 4 * px + 2 * py + cc, 4 * px + 2 * py + 1 - cc
                pair(j - 1, land.at[mine], mine, theirs, (cx, cy, 1 - cc))
    return out


def _xchg_start(name, kind, srcs, lands, after):
    ns, nb, na = len(srcs), len(srcs) + len(lands), len(after)
    n_sems = COPIES[kind] * len(lands)

    def body(*refs):
        send_sems, recv_sems = refs[nb + na], refs[nb + na + 1]
        for send, _ in _copy_plan(kind, refs[:ns], refs[ns:nb], send_sems, recv_sems):
            send.start()
        refs[-1][...] = jnp.zeros_like(refs[-1])

    bufs = list(srcs) + list(lands)
    res = _pcall(
        body, name=name,
        out_shape=(pltpu.SemaphoreType.DMA((n_sems,)), pltpu.SemaphoreType.DMA((n_sems,)),
                   *[pltpu.HBM(a.shape, a.dtype) for a in bufs], jax.ShapeDtypeStruct((8, LANE), F32)),
        in_specs=[HBM] * nb + [ANY] * na,
        out_specs=(SEM, SEM, *([HBM] * nb), pl.BlockSpec(memory_space=pltpu.VMEM)),
        input_output_aliases={i: 2 + i for i in range(nb)},
        compiler_params=pltpu.CompilerParams(has_side_effects=EFFECT),
    )(*[pltpu.with_memory_space_constraint(a, pltpu.HBM) for a in bufs], *after)
    return dict(kind=kind, send=res[0], recv=res[1], srcs=list(res[2:2 + ns]), lands=list(res[2 + ns:2 + nb]),
                token=res[-1])


def _xchg_wait(name, st, after):
    ns, nb = len(st['srcs']), len(st['srcs']) + len(st['lands'])

    def body(*refs):
        for _, back in _copy_plan(st['kind'], refs[:ns], refs[ns:nb], refs[nb], refs[nb + 1]):
            back.wait_send()
            back.wait_recv()

    bufs = st['srcs'] + st['lands']
    res = _pcall(
        body, name=name, out_shape=tuple(pltpu.HBM(a.shape, a.dtype) for a in bufs),
        in_specs=[HBM] * nb + [SEM, SEM] + [ANY] * len(after), out_specs=tuple([HBM] * nb),
        input_output_aliases={i: i for i in range(nb)},
        compiler_params=pltpu.CompilerParams(has_side_effects=EFFECT),
    )(*bufs, st['send'], st['recv'], *after)
    return list(res[ns:])


_DIMS = {"nn": (((1,), (0,)), ((), ())), "nt": (((1,), (1,)), ((), ())), "tn": (((0,), (0,)), ((), ()))}


MM_OPERAND_BYTES = 16 * 1024 * 1024


def _contraction_tile(K, bytes_per_k):
    fits = [c for c in (K, 2048, 1024, 512, 256, 128) if K % c == 0 and c * bytes_per_k <= MM_OPERAND_BYTES]
    return fits[0] if fits else _pick(K, (128,))


def _mm(name, a, b, mode, out_dtype=F32, res=None, alpha=1.0, deps=()):
    if mode == "tn":
        K, M = a.shape
        N = b.shape[1]
    else:
        M, K = a.shape
        N = b.shape[0] if mode == "nt" else b.shape[1]
    tn = _pick(N, (1024, 512, 256))
    tm = _pick(M, (1024, 512, 256, 128) if tn <= 1024 else (512, 256, 128))
    tk = _contraction_tile(K, tm * a.dtype.itemsize + tn * b.dtype.itemsize)
    nk = K // tk
    a_spec = pl.BlockSpec((tk, tm), lambda i, j, k: (k, i)) if mode == "tn" else pl.BlockSpec((tm, tk), lambda i, j, k: (i, k))
    b_spec = pl.BlockSpec((tn, tk), lambda i, j, k: (j, k)) if mode == "nt" else pl.BlockSpec((tk, tn), lambda i, j, k: (k, j))
    o_spec = pl.BlockSpec((tm, tn), lambda i, j, k: (i, j))
    dims = _DIMS[mode]

    def body(*refs):
        a_ref, b_ref, r_ref = refs[0], refs[1], refs[2]
        part = lax.dot_general(a_ref[...].astype(BF16), b_ref[...].astype(BF16), dims, preferred_element_type=F32)

        def finish(o_ref, r):
            r = r * alpha if alpha != 1.0 else r
            if res is not None:
                r = r_ref[...] + r
            o_ref[...] = r.astype(o_ref.dtype)

        if nk == 1:
            finish(refs[-1], part)
            return
        o_ref, acc = refs[-2], refs[-1]
        k = pl.program_id(2)

        @pl.when(k == 0)
        def _():
            acc[...] = part

        @pl.when(k > 0)
        def _():
            acc[...] += part

        @pl.when(k == nk - 1)
        def _():
            finish(o_ref, acc[...])

    ins = [a, b] + ([] if res is None else [res]) + list(deps)
    specs = [a_spec, b_spec] + ([] if res is None else [o_spec]) + [ANY] * len(deps)
    return _pcall(body, name=name, grid=(M // tm, N // tn, nk), in_specs=specs, out_specs=o_spec,
                  out_shape=jax.ShapeDtypeStruct((M, N), out_dtype),
                  scratch_shapes=[] if nk == 1 else [pltpu.VMEM((tm, tn), F32)])(*ins)


def _mm_tn_batch(name, a3, b3, alpha=1.0, deps=()):
    ga, T, M = a3.shape
    gb, _, N = b3.shape
    G = max(ga, gb)
    tm = _pick(M, (1024, 512)) if N <= 1024 else M
    tk = _contraction_tile(T, tm * a3.dtype.itemsize + N * b3.dtype.itemsize)
    nk = T // tk
    a_spec = pl.BlockSpec((None, tk, tm), (lambda g, i, k: (g, k, i)) if ga > 1 else (lambda g, i, k: (0, k, i)))
    b_spec = pl.BlockSpec((None, tk, N), (lambda g, i, k: (g, k, 0)) if gb > 1 else (lambda g, i, k: (0, k, 0)))
    o_spec = pl.BlockSpec((None, tm, N), lambda g, i, k: (g, i, 0))

    def body(*refs):
        a_ref, b_ref, o_ref, acc = refs[0], refs[1], refs[-2], refs[-1]
        k = pl.program_id(2)
        part = lax.dot_general(a_ref[...].astype(BF16), b_ref[...].astype(BF16), _DIMS["tn"], preferred_element_type=F32)

        @pl.when(k == 0)
        def _():
            acc[...] = part

        @pl.when(k > 0)
        def _():
            acc[...] += part

        @pl.when(k == nk - 1)
        def _():
            o_ref[...] = (acc[...] * alpha if alpha != 1.0 else acc[...]).astype(o_ref.dtype)

    return _pcall(body, name=name, grid=(G, M // tm, nk), in_specs=[a_spec, b_spec] + [ANY] * len(deps),
                  out_specs=o_spec, out_shape=jax.ShapeDtypeStruct((G, M, N), BF16),
                  scratch_shapes=[pltpu.VMEM((tm, N), F32)])(a3, b3, *deps)


def _rowwise(name, fn, ins, outs, T, tm, deps=()):
    in_specs = []
    for arr, spec in ins:
        if spec == "row":
            in_specs.append(pl.BlockSpec((tm, arr.shape[1]), lambda i: (i, 0)))
        elif spec == "full":
            in_specs.append(pl.BlockSpec(arr.shape, lambda i, _n=arr.ndim: (0,) * _n))
        else:
            _, off, width = spec
            in_specs.append(pl.BlockSpec((tm, width), lambda i, _b=off // width: (i, _b)))
    in_specs += [ANY] * len(deps)
    out_specs, out_shapes = [], []
    for shape, dtype, spec in outs:
        out_shapes.append(jax.ShapeDtypeStruct(shape, dtype))
        if spec == "row":
            out_specs.append(pl.BlockSpec((tm, shape[1]), lambda i: (i, 0)))
        else:
            out_specs.append(pl.BlockSpec(shape, lambda i, _n=len(shape): (0,) * _n))
    n_in = len(ins)

    def body(*refs):
        res = fn(*[r[...] for r in refs[:n_in]])
        i = pl.program_id(0)
        for r, (_, _, spec), val in zip(refs[n_in + len(deps):], outs, res):
            if spec == "acc":
                @pl.when(i == 0)
                def _(r=r):
                    r[...] = jnp.zeros_like(r)
                r[...] += val.astype(r.dtype)
            else:
                r[...] = val.astype(r.dtype)

    return _pcall(body, name=name, grid=(T // tm,), in_specs=in_specs, out_specs=out_specs, out_shape=out_shapes)(
        *[a for a, _ in ins], *deps)


def _rms_fwd(name, h, g, deps=()):
    T, D = h.shape
    return _rowwise(name, lambda hv, gv: (_rms(hv, gv),), [(h, "row"), (g, "full")], [((T, D), BF16, "row")], T,
                    _pick(T, (512, 256, 128)), deps)[0]


def _rms_bwd(name, h, g, dxn, dh_in, deps=()):
    T, D = h.shape

    def fn(hv, gv, dv, dh0):
        _, vjp = jax.vjp(_rms, hv, gv)
        dh, dg = vjp(dv.astype(F32))
        return dh0 + dh, dg

    return _rowwise(name, fn, [(h, "row"), (g, "full"), (dxn, "row"), (dh_in, "row")],
                    [((T, D), F32, "row"), ((1, D), F32, "acc")], T, _pick(T, (512, 256, 128)), deps)


def _ffn_fwd(name, xn, h, w1, w3, w2):
    T, D = xn.shape
    F8 = w1.shape[-1]
    tm = _pick(T, (512, 256, 128))
    wspec = lambda r, c: pl.BlockSpec((None, r, c), lambda i, d: (d, 0, 0))
    row = pl.BlockSpec((tm, D), lambda i, d: (i, 0))
    hid = pl.BlockSpec((None, tm, F8), lambda i, d: (d, i, 0))

    def body(xn_ref, h_ref, w1_ref, w3_ref, w2_ref, out_ref, h1_ref, h3_ref, acc):
        d = pl.program_id(1)

        @pl.when(d == 0)
        def _():
            acc[...] = jnp.zeros_like(acc)

        x = xn_ref[...]
        h1 = jnp.dot(x, w1_ref[...], preferred_element_type=F32)
        h3 = jnp.dot(x, w3_ref[...], preferred_element_type=F32)
        h1_ref[...] = h1.astype(BF16)
        h3_ref[...] = h3.astype(BF16)
        act = (h1 * jax.nn.sigmoid(h1) * h3).astype(BF16)
        acc[...] += jnp.dot(act, w2_ref[...], preferred_element_type=F32)

        @pl.when(d == N_DEV - 1)
        def _():
            out_ref[...] = h_ref[...] + 0.5 * acc[...]

    return _pcall(body, name=name, grid=(T // tm, N_DEV),
                  in_specs=[row, row, wspec(D, F8), wspec(D, F8), wspec(F8, D)],
                  out_specs=[row, hid, hid],
                  out_shape=[jax.ShapeDtypeStruct((T, D), F32), jax.ShapeDtypeStruct((N_DEV, T, F8), BF16),
                             jax.ShapeDtypeStruct((N_DEV, T, F8), BF16)],
                  scratch_shapes=[pltpu.VMEM((tm, D), F32)])(xn, h, w1, w3, w2)


FFN_ROW_GROUPS = 2


def _ffn_bwd(name, dy, h1, h3, w1, w3, w2, deps=()):
    T, D = dy.shape
    F8 = w1.shape[-1]
    tm = _pick(T, (512, 256, 128))
    wspec = lambda r, c: pl.BlockSpec((None, r, c), lambda i, d: (d, 0, 0))
    row = pl.BlockSpec((tm, D), lambda i, d: (i, 0))
    hid = pl.BlockSpec((None, tm, F8), lambda i, d: (d, i, 0))

    def body(*refs):
        dy_ref, h1_ref, h3_ref, w1_ref, w3_ref, w2_ref = refs[:6]
        dxn_ref, dyb, dh1_ref, dh3_ref, act_ref, acc = refs[6 + len(deps):]
        d = pl.program_id(1)

        @pl.when(d == 0)
        def _():
            acc[...] = jnp.zeros_like(acc)
            dyb[...] = dy_ref[...].astype(BF16)

        groups = [pl.ds(r * (tm // FFN_ROW_GROUPS), tm // FFN_ROW_GROUPS) for r in range(FFN_ROW_GROUPS)]
        dacts = [0.5 * lax.dot_general(dyb[rows, :], w2_ref[...], _DIMS["nt"], preferred_element_type=F32)
                 for rows in groups]
        for rows, dact in zip(groups, dacts):
            h1 = h1_ref[rows, :].astype(F32)
            h3 = h3_ref[rows, :].astype(F32)
            sig = jax.nn.sigmoid(h1)
            silu = h1 * sig
            dh1 = (dact * h3 * (sig * (1.0 + h1 * (1.0 - sig)))).astype(BF16)
            dh3 = (dact * silu).astype(BF16)
            dh1_ref[rows, :] = dh1
            dh3_ref[rows, :] = dh3
            act_ref[rows, :] = (silu * h3).astype(BF16)
            acc[rows, :] += (lax.dot_general(dh1, w1_ref[...], _DIMS["nt"], preferred_element_type=F32)
                             + lax.dot_general(dh3, w3_ref[...], _DIMS["nt"], preferred_element_type=F32))

        @pl.when(d == N_DEV - 1)
        def _():
            dxn_ref[...] = acc[...]

    hshape = jax.ShapeDtypeStruct((N_DEV, T, F8), BF16)
    return _pcall(body, name=name, grid=(T // tm, N_DEV),
                  in_specs=[row, hid, hid, wspec(D, F8), wspec(D, F8), wspec(F8, D)] + [ANY] * len(deps),
                  out_specs=[row, row, hid, hid, hid],
                  out_shape=[jax.ShapeDtypeStruct((T, D), F32), jax.ShapeDtypeStruct((T, D), BF16), hshape, hshape, hshape],
                  scratch_shapes=[pltpu.VMEM((tm, D), F32)])(dy, h1, h3, w1, w3, w2, *deps)


def _rot_matrix():
    i = jnp.arange(QK)[:, None]
    j = jnp.arange(QK)[None, :]
    half = ROPE // 2
    first = (j >= NOPE) & (j < NOPE + half) & (i == j + half)
    second = (j >= NOPE + half) & (i == j - half)
    return jnp.where(first, -1.0, jnp.where(second, 1.0, 0.0)).astype(F32)


def _mla_fn(cq, ckv, kr128, cos, sin, rot, qa_g, kva_g, qn_g, kn_g, w_uq, w_ukv):
    cqn = _rms(cq, qa_g)
    ckvn = _rms(ckv, kva_g)
    kr = kr128[:, :ROPE]
    qs, ks, vs = [], [], []
    for h in range(HEADS):
        qh = _rms(_bdot(cqn, w_uq[h]), qn_g)
        qs.append((qh * cos + _permute(qh, rot) * sin) * SCORE_SCALE)
        kvh = _bdot(ckvn, w_ukv[h])
        kh = _rms(jnp.concatenate([kvh[:, :NOPE], kr], axis=-1), kn_g)
        ks.append(kh * cos + _permute(kh, rot) * sin)
        vs.append(kvh[:, NOPE:])
    return qs, ks, vs


def _mla_specs(z, tabs, small, w_uq, w_ukv, tm, offs):
    o_cq, o_ckv, o_kr = offs
    row = lambda w: pl.BlockSpec((tm, w), lambda i: (i, 0))
    col = lambda off, w: pl.BlockSpec((tm, w), lambda i: (i, off // w))
    full2 = lambda a: pl.BlockSpec(a.shape, lambda i: (0, 0))
    wsp = lambda a: pl.BlockSpec(a.shape, lambda i: (0, 0, 0))
    cq_w, ckv_w = w_uq.shape[1], w_ukv.shape[1]
    ins = [z, z, z, tabs[0], tabs[1], tabs[2]] + list(small) + [w_uq, w_ukv]
    specs = ([col(o_cq, cq_w), col(o_ckv, ckv_w), col(o_kr, LANE), row(QK), row(QK), full2(tabs[2])]
             + [full2(s) for s in small] + [wsp(w_uq), wsp(w_ukv)])
    return ins, specs


def _mla_prep_fwd(name, z, tabs, small, w_uq, w_ukv, offs):
    T = z.shape[0]
    tm = _pick(T, (512, 256, 128))
    ins, specs = _mla_specs(z, tabs, small, w_uq, w_ukv, tm, offs)
    head = lambda w: pl.BlockSpec((HEADS, tm, w), lambda i: (0, i, 0))

    def body(*refs):
        vals = [r[...] for r in refs[:12]]
        q_ref, k_ref, v_ref = refs[12:]
        qs, ks, vs = _mla_fn(*vals)
        for h in range(HEADS):
            q_ref[h] = qs[h].astype(BF16)
            k_ref[h] = ks[h].astype(BF16)
            v_ref[h] = vs[h].astype(BF16)

    return _pcall(body, name=name, grid=(T // tm,), in_specs=specs, out_specs=[head(QK), head(QK), head(VD)],
                  out_shape=[jax.ShapeDtypeStruct((HEADS, T, QK), BF16), jax.ShapeDtypeStruct((HEADS, T, QK), BF16),
                             jax.ShapeDtypeStruct((HEADS, T, VD), BF16)])(*ins)


def _mla_prep_bwd(name, z, tabs, small, w_uq, w_ukv, offs, dq, dk, dv):
    T = z.shape[0]
    tm = _pick(T, (512, 256, 128))
    ins, specs = _mla_specs(z, tabs, small, w_uq, w_ukv, tm, offs)
    head = lambda w: pl.BlockSpec((HEADS, tm, w), lambda i: (0, i, 0))
    ins += [dq, dk, dv]
    specs += [head(QK), head(QK), head(VD)]
    cq_w, ckv_w = w_uq.shape[1], w_ukv.shape[1]
    acc_shapes = [s.shape for s in small] + [w_uq.shape, w_ukv.shape]
    row_shapes = [(T, cq_w), (T, ckv_w), (T, LANE)]
    out_shape = [jax.ShapeDtypeStruct(s, BF16) for s in row_shapes] + [jax.ShapeDtypeStruct(s, F32) for s in acc_shapes]
    out_specs = ([pl.BlockSpec((tm, s[1]), lambda i: (i, 0)) for s in row_shapes]
                 + [pl.BlockSpec(s, lambda i, _n=len(s): (0,) * _n) for s in acc_shapes])

    def body(*refs):
        cq, ckv, kr128, cos, sin, rot, qa_g, kva_g, qn_g, kn_g, w_uq_v, w_ukv_v = [r[...] for r in refs[:12]]
        dq_ref, dk_ref, dv_ref = refs[12:15]
        outs = refs[15:]
        f = lambda a, b, c, g1, g2, g3, g4, wq, wkv: _mla_fn(a, b, c, cos, sin, rot, g1, g2, g3, g4, wq, wkv)
        _, vjp = jax.vjp(f, cq, ckv, kr128, qa_g, kva_g, qn_g, kn_g, w_uq_v.astype(F32), w_ukv_v.astype(F32))
        cts = ([dq_ref[h] for h in range(HEADS)], [dk_ref[h] for h in range(HEADS)], [dv_ref[h] for h in range(HEADS)])
        grads = vjp(cts)
        i = pl.program_id(0)
        for n, (r, gval) in enumerate(zip(outs, grads)):
            if n < 3:
                r[...] = gval.astype(r.dtype)
            else:
                @pl.when(i == 0)
                def _(r=r):
                    r[...] = jnp.zeros_like(r)
                r[...] += gval

    return _pcall(body, name=name, grid=(T // tm,), in_specs=specs, out_specs=out_specs, out_shape=out_shape)(*ins)


NEG = -1e30


ATTN_ROW_GROUPS = 2


def _tri_block(rows, cols, row0):
    return lax.broadcasted_iota(jnp.int32, (rows, cols), 1) <= lax.broadcasted_iota(jnp.int32, (rows, cols), 0) + row0


def _attn_tiles(T):
    t = _pick(T, (1024, 512, 256, 128))
    return t, T // t


def _attn_fwd(name, q, k, v):
    H, T, _ = q.shape
    t, n = _attn_tiles(T)

    def body(q_ref, k_ref, v_ref, o_ref, lse_ref, m_s, l_s, acc):
        qi, ki = pl.program_id(1), pl.program_id(2)

        @pl.when(ki == 0)
        def _():
            m_s[...] = jnp.full_like(m_s, NEG)
            l_s[...] = jnp.zeros_like(l_s)
            acc[...] = jnp.zeros_like(acc)

        def tile(diagonal):
            s = lax.dot_general(q_ref[...], k_ref[...], _DIMS["nt"], preferred_element_type=F32)
            if diagonal:
                s = jnp.where(_tri_block(t, t, 0), s, NEG)
            m_new = jnp.maximum(m_s[...], jnp.max(s, axis=-1, keepdims=True))
            alpha = jnp.exp(m_s[...] - m_new)
            p = jnp.exp(s - m_new)
            l_s[...] = alpha * l_s[...] + jnp.sum(p, axis=-1, keepdims=True)
            acc[...] = alpha * acc[...] + jnp.dot(p.astype(BF16), v_ref[...], preferred_element_type=F32)
            m_s[...] = m_new

        @pl.when(ki < qi)
        def _():
            tile(False)

        @pl.when(ki == qi)
        def _():
            tile(True)
            o_ref[...] = acc[...] / l_s[...]
            lse_ref[...] = m_s[...] + jnp.log(l_s[...])

    kv = lambda w: pl.BlockSpec((None, t, w), lambda h, qi, ki: (h, jnp.minimum(ki, qi), 0))
    return _pcall(body, name=name, grid=(H, n, n),
                  in_specs=[pl.BlockSpec((None, t, QK), lambda h, qi, ki: (h, qi, 0)), kv(QK), kv(VD)],
                  out_specs=[pl.BlockSpec((t, VD), lambda h, qi, ki: (qi, h)),
                             pl.BlockSpec((None, t, 1), lambda h, qi, ki: (h, qi, 0))],
                  out_shape=[jax.ShapeDtypeStruct((T, H * VD), F32), jax.ShapeDtypeStruct((H, T, 1), F32)],
                  scratch_shapes=[pltpu.VMEM((t, 1), F32), pltpu.VMEM((t, 1), F32), pltpu.VMEM((t, VD), F32)])(q, k, v)


def _attn_bwd(name, q, k, v, o, lse, do):
    H, T, _ = q.shape
    t, n = _attn_tiles(T)

    def body(q_ref, k_ref, v_ref, o_ref, lse_ref, do_ref, dq_ref, dk_ref, dv_ref, dk_acc, dv_acc):
        ki, qi = pl.program_id(1), pl.program_id(2)

        @pl.when((ki == 0) & (qi == 0))
        def _():
            dq_ref[...] = jnp.zeros_like(dq_ref)

        @pl.when(qi == 0)
        def _():
            dk_acc[...] = jnp.zeros_like(dk_acc)
            dv_acc[...] = jnp.zeros_like(dv_acc)

        def tile(diagonal):
            g = t // ATTN_ROW_GROUPS
            for r in range(ATTN_ROW_GROUPS):
                rows = pl.ds(r * g, g)
                cols = (r + 1) * g if diagonal else t
                qv, kv_, dov = q_ref[rows, :], k_ref[:cols, :], do_ref[rows, :]
                s = lax.dot_general(qv, kv_, _DIMS["nt"], preferred_element_type=F32)
                p = jnp.exp(s - lse_ref[rows, :])
                if diagonal:
                    p = jnp.where(_tri_block(g, cols, r * g), p, 0.0)
                dob = dov.astype(BF16)
                delta = jnp.sum(o_ref[rows, :] * dov, axis=-1, keepdims=True)
                dv_acc[:cols, :] += lax.dot_general(p.astype(BF16), dob, _DIMS["tn"], preferred_element_type=F32)
                dp = lax.dot_general(dob, v_ref[:cols, :], _DIMS["nt"], preferred_element_type=F32)
                ds = (p * (dp - delta)).astype(BF16)
                dq_rows = pl.ds(pl.multiple_of(qi * t + r * g, g), g)
                dq_ref[dq_rows, :] += jnp.dot(ds, kv_, preferred_element_type=F32)
                dk_acc[:cols, :] += lax.dot_general(ds, qv, _DIMS["tn"], preferred_element_type=F32)

        @pl.when(qi > ki)
        def _():
            tile(False)

        @pl.when(qi == ki)
        def _():
            tile(True)

        @pl.when(qi == n - 1)
        def _():
            dk_ref[...] = dk_acc[...]
            dv_ref[...] = dv_acc[...]

    qrow = lambda w: pl.BlockSpec((None, t, w), lambda h, ki, qi: (h, jnp.maximum(qi, ki), 0))
    krow = lambda w: pl.BlockSpec((None, t, w), lambda h, ki, qi: (h, ki, 0))
    wide = pl.BlockSpec((t, VD), lambda h, ki, qi: (jnp.maximum(qi, ki), h))
    return _pcall(body, name=name, grid=(H, n, n),
                  in_specs=[qrow(QK), krow(QK), krow(VD), wide, qrow(1), wide],
                  out_specs=[pl.BlockSpec((None, T, QK), lambda h, ki, qi: (h, 0, 0)), krow(QK), krow(VD)],
                  out_shape=[jax.ShapeDtypeStruct((H, T, QK), F32), jax.ShapeDtypeStruct((H, T, QK), F32),
                             jax.ShapeDtypeStruct((H, T, VD), F32)],
                  scratch_shapes=[pltpu.VMEM((t, QK), F32), pltpu.VMEM((t, VD), F32)])(q, k, v, o, lse, do)


def _tril():
    return lax.broadcasted_iota(jnp.int32, (CHUNK, CHUNK), 1) <= lax.broadcasted_iota(jnp.int32, (CHUNK, CHUNK), 0)


@jax.custom_vjp
def _gm_gate(v, ws, b_t):
    wc = jnp.where(_tril()[None], ws, 0.0).astype(BF16)
    vb = v.astype(BF16)
    rows = []
    for c in range(v.shape[0] // CHUNK):
        cols = []
        for g in range(GROUPS):
            vc = vb[c * CHUNK:(c + 1) * CHUNK, g * LANE:(g + 1) * LANE]
            cols.append(jnp.dot(wc[g], vc, preferred_element_type=F32) + jnp.broadcast_to(b_t[:, g:g + 1], (CHUNK, LANE)))
        rows.append(jnp.concatenate(cols, axis=-1))
    return jnp.concatenate(rows, axis=0)


def _gm_gate_fwd(v, ws, b_t):
    return _gm_gate(v, ws, b_t), (v, ws)


def _gm_gate_bwd(res, dgate):
    v, ws = res
    tril = _tril()
    wc = jnp.where(tril[None], ws, 0.0).astype(BF16)
    vb = v.astype(BF16)
    dgb = dgate.astype(BF16)
    dws = [jnp.zeros((CHUNK, CHUNK), F32) for _ in range(GROUPS)]
    db = jnp.zeros((CHUNK, GROUPS), F32)
    lane_g = lax.broadcasted_iota(jnp.int32, (1, GROUPS), 1)
    rows = []
    for c in range(v.shape[0] // CHUNK):
        cols = []
        for g in range(GROUPS):
            sl = (slice(c * CHUNK, (c + 1) * CHUNK), slice(g * LANE, (g + 1) * LANE))
            cols.append(lax.dot_general(wc[g], dgb[sl], _DIMS["tn"], preferred_element_type=F32))
            dws[g] = dws[g] + lax.dot_general(dgb[sl], vb[sl], _DIMS["nt"], preferred_element_type=F32)
            db = db + jnp.sum(dgate[sl], axis=1, keepdims=True) * (lane_g == g).astype(F32)
        rows.append(jnp.concatenate(cols, axis=-1))
    dws = jnp.stack([jnp.where(tril, d, 0.0) for d in dws])
    return jnp.concatenate(rows, axis=0), dws, db


_gm_gate.defvjp(_gm_gate_fwd, _gm_gate_bwd)


def _mix_fn(a_out, zu, zv, aon_g, gon_g, vn_g, ws, b_t):
    u = jax.nn.gelu(zu)
    vv = _rms(jax.nn.gelu(zv), vn_g)
    g_out = u * _gm_gate(vv, ws, b_t)
    return jnp.concatenate([_rms(a_out, aon_g), _rms(g_out, gon_g)], axis=-1)


def _mix_ins(a_out, z, small, offs):
    gw = a_out.shape[1]
    return [(a_out, "row"), (z, ("cols", offs[0], gw)), (z, ("cols", offs[1], gw))] + [(s, "full") for s in small]


def _mix_fwd(name, a_out, z, small, offs):
    T, gw = a_out.shape
    return _rowwise(name, lambda *a: (_mix_fn(*a),), _mix_ins(a_out, z, small, offs), [((T, 2 * gw), BF16, "row")],
                    T, _pick(T, (512, 256, 128)))[0]


def _mix_bwd(name, a_out, z, small, offs, dmixed):
    T, gw = a_out.shape

    def fn(*a):
        _, vjp = jax.vjp(_mix_fn, *a[:-1])
        return vjp(a[-1].astype(F32))

    outs = [((T, gw), F32, "row"), ((T, gw), BF16, "row"), ((T, gw), BF16, "row")] + [(s.shape, F32, "acc") for s in small]
    return _rowwise(name, fn, _mix_ins(a_out, z, small, offs) + [(dmixed, "row")], outs, T, _pick(T, (256, 128)))


def _ple_fn(gl, pe, g):
    return jax.nn.sigmoid(gl) * _rms(pe, g)


def _ple_fwd(name, h, gl, pe, g):
    T, D = h.shape
    return _rowwise(name, lambda hv, a, b, c: (hv + _ple_fn(a, b, c),),
                    [(h, "row"), (gl, "row"), (pe, "row"), (g, "full")], [((T, D), F32, "row")], T,
                    _pick(T, (512, 256, 128)))[0]


def _ple_bwd(name, gl, pe, g, dh, deps=()):
    T, D = gl.shape

    def fn(a, b, c, d):
        _, vjp = jax.vjp(_ple_fn, a, b, c)
        return vjp(d)

    return _rowwise(name, fn, [(gl, "row"), (pe, "row"), (g, "full"), (dh, "row")],
                    [((T, D), BF16, "row"), ((T, D), BF16, "row"), ((1, D), F32, "acc")], T, _pick(T, (256, 128)), deps)


def _loss(name, y, target):
    T, D = y.shape

    def fn(yv, tv):
        err = yv - tv
        part = 0.5 * jnp.sum(jnp.mean(err * err, axis=-1, keepdims=True), axis=0, keepdims=True)
        return err * (1.0 / D), jnp.broadcast_to(part, (8, LANE))

    return _rowwise(name, fn, [(y, "row"), (target, "row")], [((T, D), F32, "row"), ((8, LANE), F32, "acc")], T,
                    _pick(T, (512, 256, 128)))


ADAMW_BLOCK_ELEMS = 256 * 1024


def _adamw_sum(name, parts, w, m, v):
    L, R, C = w.shape
    tiles = [(r, c) for r in (R, 512, 256, 128, 64, 32, 16) for c in (C, 1024, 512, 256, 128)
             if R % r == 0 and C % c == 0 and r * c <= ADAMW_BLOCK_ELEMS]
    tr, tc = max(tiles, key=lambda rc: (rc[0] * rc[1], rc[1]))
    nr, nc = R // tr, C // tc
    c1 = 1.0 - ADAM_B1 ** ADAM_STEP
    c2 = 1.0 - ADAM_B2 ** ADAM_STEP

    def body(*refs):
        p_refs = refs[:L]
        w_ref, m_ref, v_ref, g_out, d_out, m_out, v_out = refs[L:]
        layer = pl.program_id(0)

        def part(s):
            val = p_refs[0][s].astype(F32)
            for j in range(1, L):
                val = jnp.where(layer == j, p_refs[j][s].astype(F32), val)
            return val

        g = part(0)
        for s in range(1, N_DEV):
            g = g + part(s)
        m2 = ADAM_B1 * m_ref[...] + (1.0 - ADAM_B1) * g
        v2 = ADAM_B2 * v_ref[...] + (1.0 - ADAM_B2) * (g * g)
        g_out[...] = g
        m_out[...] = m2
        v_out[...] = v2
        d_out[...] = -ADAM_LR * ((m2 / c1) / (jnp.sqrt(v2 / c2) + ADAM_EPS) + ADAM_WD * w_ref[...])

    def part_spec(j):
        def index(l, i, k):
            before, mine = l < j, l == j
            return (0, jnp.where(mine, i, jnp.where(before, 0, nr - 1)), jnp.where(mine, k, jnp.where(before, 0, nc - 1)))
        return pl.BlockSpec((N_DEV, tr, tc), index)

    blk = pl.BlockSpec((None, tr, tc), lambda l, i, k: (l, i, k))
    sd = jax.ShapeDtypeStruct((L, R, C), F32)
    return _pcall(body, name=name, grid=(L, nr, nc),
                  in_specs=[part_spec(j) for j in range(L)] + [blk, blk, blk],
                  out_specs=[blk, blk, blk, blk], out_shape=[sd, sd, sd, sd])(*parts, w, m, v)


def _unshard_cols(g):
    _, K, n = g.shape
    return g.transpose(1, 0, 2).reshape(K, N_DEV * n)


def _shard_cols(full):
    K, N = full.shape
    return full.reshape(K, N_DEV, N // N_DEV).transpose(1, 0, 2)


def _unshard_rows(g):
    _, k, N = g.shape
    return g.reshape(N_DEV * k, N)


def _shard_rows(full):
    K, N = full.shape
    return full.reshape(N_DEV, K // N_DEV, N)


STAGES = (('ffn_a_w1', 'ffn_a_w3', 'ffn_a_w2'), ('w_in', 'w_uq', 'w_ukv', 'w_out'),
          ('ffn_b_w1', 'ffn_b_w3', 'ffn_b_w2'), ('w_ple_gate', 'w_ple'))
TRANSPOSED = ('ffn_a_w1', 'ffn_a_w3', 'ffn_b_w1', 'ffn_b_w3', 'w_in', 'w_uq')


def _step(x, p, positions, target, w, m, v):
    T, D = x.shape[1], x.shape[2]
    L = p.shape[0]
    x2, target2 = x[0], target[0]
    q_rank, kv_rank = w['w_uq'].shape[1], w['w_ukv'].shape[1]
    gw = w['gm_v_norm'].shape[1]

    groups = [(l, names) for l in range(L) for names in STAGES]

    def prepare(k, pin):
        l, names = groups[k]
        return [pin(w[n][l]).astype(BF16) for n in names]

    def ag_start(k, after):
        return _xchg_start(f"ag_chips{k}", "chips", shards[k], [_landing(s) for s in shards[k]], after)

    shards = {0: prepare(0, lambda a: a)}
    ag = {0: ag_start(0, [])}
    tok0 = ag[0]['token']
    behind = lambda a: lax.optimization_barrier((tok0, a))[1]
    shards = {k: prepare(k, behind) for k in range(1, len(groups))}

    inv_freq = ROPE_BASE ** (-jnp.arange(0, ROPE, 2, dtype=F32) / ROPE)
    ang = behind(positions)[0].astype(F32)[:, None] * inv_freq
    cos = jnp.concatenate([jnp.ones((T, NOPE), F32), jnp.cos(ang), jnp.cos(ang)], axis=-1)
    sin = jnp.concatenate([jnp.zeros((T, NOPE), F32), jnp.sin(ang), jnp.sin(ang)], axis=-1)
    (shards, cos, sin), ready = lax.optimization_barrier(((shards, cos, sin), tok0))
    tabs = (cos, sin, _rot_matrix().astype(BF16))
    ag[1] = ag_start(1, [tok0])

    def fetch(k, after):
        if k == 0:
            after = list(after) + [ready, ag[1]['token']]
        lands = _xchg_wait(f"ag_landed{k}", ag[k], after)
        fw = _xchg_start(f"ag_forward{k}", "forward", [], lands, [])
        deps = [fw['token']]
        for nxt in {0: (), 1: (2, 3)}.get(k, (k + 2,)):
            if nxt < len(groups):
                ag[nxt] = ag_start(nxt, deps)
                deps = [ag[nxt]['token']]
        if k == 0:
            deps = deps + [ag[1]['token']]
        return fw, deps

    def gathered(k, fw, after):
        return dict(zip(groups[k][1], _xchg_wait(f"ag_wait{k}", fw, after)))

    s0, s1, s2, s3 = q_rank, q_rank + kv_rank, q_rank + kv_rank + ROPE, q_rank + kv_rank + ROPE + gw
    o_u, o_v, o_cq, o_ckv, o_kr = 0, gw, 2 * gw, 2 * gw + q_rank, 2 * gw + q_rank + kv_rank
    kr_pad = 2 * LANE - ROPE

    def g2(name, l):
        return w[name][l][None, :]

    gm_bt = [w['gm_bs'][l].T for l in range(L)]

    saved = []
    h = x2
    for l in range(L):
        s = {}
        fw, deps = fetch(4 * l, [h])
        s['h0'] = h
        s['xn_a'] = _rms_fwd(f"rms_a{l}", h, g2('ffn_a_norm', l), deps)
        wa = s['wa'] = gathered(4 * l, fw, [s['xn_a']])
        h, s['a_h1'], s['a_h3'] = _ffn_fwd(f"ffn_a_fwd{l}", s['xn_a'], h, wa['ffn_a_w1'], wa['ffn_a_w3'], wa['ffn_a_w2'])
        fw, deps = fetch(4 * l + 1, [h])
        s['h1'] = h
        s['n'] = _rms_fwd(f"rms_mix{l}", h, g2('mix_norm', l), deps)
        wm = gathered(4 * l + 1, fw, [s['n']])
        w_in_full = _unshard_cols(wm['w_in'])
        s['w_in'] = jnp.concatenate([w_in_full[:, s2:s3], w_in_full[:, s3:], w_in_full[:, :s0], w_in_full[:, s0:s1],
                                     w_in_full[:, s1:s2], jnp.zeros((D, kr_pad), BF16)], axis=-1)
        s['w_out'] = _unshard_rows(wm['w_out'])
        s['w_uq'], s['w_ukv'] = wm['w_uq'], wm['w_ukv']
        s['z'] = _mm(f"w_in{l}", s['n'], s['w_in'], "nn")
        s['mla_small'] = [g2('q_a_norm', l), g2('kv_a_norm', l), g2('q_norm', l), g2('k_norm', l)]
        s['q'], s['k'], s['v'] = _mla_prep_fwd(f"mla_prep{l}", s['z'], tabs, s['mla_small'], s['w_uq'], s['w_ukv'],
                                               (o_cq, o_ckv, o_kr))
        s['a_out'], s['lse'] = _attn_fwd(f"attn_fwd{l}", s['q'], s['k'], s['v'])
        s['mix_small'] = [g2('attn_out_norm', l), g2('gm_out_norm', l), g2('gm_v_norm', l), w['gm_ws'][l], gm_bt[l]]
        s['mixed'] = _mix_fwd(f"mix_fwd{l}", s['a_out'], s['z'], s['mix_small'], (o_u, o_v))
        h = _mm(f"w_out{l}", s['mixed'], s['w_out'], "nn", res=h)
        fw, deps = fetch(4 * l + 2, [h])
        s['h2'] = h
        s['xn_b'] = _rms_fwd(f"rms_b{l}", h, g2('ffn_b_norm', l), deps)
        wb = s['wb'] = gathered(4 * l + 2, fw, [s['xn_b']])
        h, s['b_h1'], s['b_h3'] = _ffn_fwd(f"ffn_b_fwd{l}", s['xn_b'], h, wb['ffn_b_w1'], wb['ffn_b_w3'], wb['ffn_b_w2'])
        fw, deps = fetch(4 * l + 3, [h])
        s['h3'] = h
        s['xn_g'] = _rms_fwd(f"rms_g{l}", h, g2('ple_gate_norm', l), deps)
        wp = gathered(4 * l + 3, fw, [s['xn_g']])
        s['w_gate'] = _unshard_rows(wp['w_ple_gate'])
        s['gl'] = _mm(f"w_gate{l}", s['xn_g'], s['w_gate'], "nn")
        s['p'] = p[l, 0]
        s['pe'] = _mm(f"w_ple{l}", s['p'], _unshard_cols(wp['w_ple']), "nn")
        h = _ple_fwd(f"ple_fwd{l}", h, s['gl'], s['pe'], g2('ple_norm', l))
        saved.append(s)

    dh, loss_part = _loss("loss", h, target2)
    loss = lax.psum(loss_part[0, 0], AXES)

    gsmall = {n: [None] * L for n in SMALL}
    rs, where = {}, {}
    sizes = [w[n].size for n in SMALL]
    total = sum(sizes)
    padded = -(-total // (512 * LANE)) * (512 * LANE)

    def pack(d):
        flat = jnp.concatenate([d[n].reshape(-1) for n in SMALL] + [jnp.zeros((padded - total,), F32)])
        return flat.reshape(1, padded // LANE, LANE)

    def rs_start(key, l, named):
        grads = [g for _, g in named]
        lands = [lax.dynamic_update_slice(lax.empty(g.shape, g.dtype),
                                          lax.dynamic_index_in_dim(g, _my_index(), 0, keepdims=True),
                                          (_my_index(),) + (0,) * (g.ndim - 1)) for g in grads]
        rs[key] = _xchg_start("rs_start_" + key, "scatter", grads, lands, [])
        where.update({(n, l): (key, i) for i, (n, _) in enumerate(named)})
        return [rs[key]['token']]

    def ffn_backward(tag, l, dh, xn, h_in, h1, h3, wts, norm_name, deps):
        pre = 'ffn_' + tag
        dxn, dhb, dh1, dh3, act = _ffn_bwd(f"{pre}_bwd{l}", dh, h1, h3, wts[pre + '_w1'], wts[pre + '_w3'],
                                           wts[pre + '_w2'], deps)
        g1 = _mm_tn_batch(f"{pre}_dw1_{l}", dh1, xn[None])
        deps = rs_start(f"{pre}_w1_{l}", l, [(pre + '_w1', g1)])
        g3 = _mm_tn_batch(f"{pre}_dw3_{l}", dh3, xn[None], deps=deps)
        deps = rs_start(f"{pre}_w3_{l}", l, [(pre + '_w3', g3)])
        g2_ = _mm_tn_batch(f"{pre}_dw2_{l}", act, dhb[None], alpha=0.5, deps=deps)
        deps = rs_start(f"{pre}_w2_{l}", l, [(pre + '_w2', g2_)])
        return _rms_bwd(f"rms_{tag}_bwd{l}", h_in, g2(norm_name, l), dxn, dh, deps)

    deps = []
    for l in reversed(range(L)):
        s = saved[l]
        d_gl, d_pe, gsmall['ple_norm'][l] = _ple_bwd(f"ple_bwd{l}", s['gl'], s['pe'], g2('ple_norm', l), dh, deps)
        g_ple = _mm(f"dw_ple{l}", s['p'], d_pe, "tn", out_dtype=BF16)
        g_gate = _mm(f"dw_gate{l}", s['xn_g'], d_gl, "tn", out_dtype=BF16)
        d_xng = _mm(f"d_xng{l}", d_gl, s['w_gate'], "nt")
        dh, gsmall['ple_gate_norm'][l] = _rms_bwd(f"rms_g_bwd{l}", s['h3'], g2('ple_gate_norm', l), d_xng, dh)
        deps = rs_start(f"ple_{l}", l, [('w_ple_gate', _shard_rows(g_gate)), ('w_ple', _shard_cols(g_ple))])
        dh, gsmall['ffn_b_norm'][l] = ffn_backward('b', l, dh, s['xn_b'], s['h2'], s['b_h1'], s['b_h3'], s['wb'],
                                                   'ffn_b_norm', deps)
        g_out = _mm(f"dw_out{l}", s['mixed'], dh, "tn", out_dtype=BF16)
        d_mixed = _mm(f"d_mixed{l}", dh, s['w_out'], "nt")
        mix = _mix_bwd(f"mix_bwd{l}", s['a_out'], s['z'], s['mix_small'], (o_u, o_v), d_mixed)
        d_a_out, d_u, d_v = mix[:3]
        gsmall['attn_out_norm'][l], gsmall['gm_out_norm'][l], gsmall['gm_v_norm'][l], gsmall['gm_ws'][l] = mix[3:7]
        gsmall['gm_bs'][l] = mix[7].T
        dq, dk, dv = _attn_bwd(f"attn_bwd{l}", s['q'], s['k'], s['v'], s['a_out'], s['lse'], d_a_out)
        mla = _mla_prep_bwd(f"mla_prep_bwd{l}", s['z'], tabs, s['mla_small'], s['w_uq'], s['w_ukv'], (o_cq, o_ckv, o_kr),
                            dq, dk, dv)
        d_cq, d_ckv, d_kr = mla[:3]
        gsmall['q_a_norm'][l], gsmall['kv_a_norm'][l], gsmall['q_norm'][l], gsmall['k_norm'][l] = mla[3:7]
        dz = jnp.concatenate([d_u, d_v, d_cq, d_ckv, d_kr, jnp.zeros((T, LANE), BF16)], axis=-1)
        g_in = _mm(f"dw_in{l}", dz, s['n'], "tn", out_dtype=BF16)
        g_in = jnp.concatenate([g_in[o_cq:o_cq + q_rank], g_in[o_ckv:o_ckv + kv_rank], g_in[o_kr:o_kr + ROPE],
                                g_in[o_u:o_u + gw], g_in[o_v:o_v + gw]], axis=0)
        d_n = _mm(f"d_n{l}", dz, s['w_in'], "nt")
        dh, gsmall['mix_norm'][l] = _rms_bwd(f"rms_mix_bwd{l}", s['h1'], g2('mix_norm', l), d_n, dh)
        deps = rs_start(f"mix_{l}", l, [('w_in', _shard_rows(g_in)), ('w_uq', mla[7].transpose(0, 2, 1).astype(BF16)),
                                        ('w_ukv', mla[8].astype(BF16)), ('w_out', _shard_rows(g_out))])
        if l == 0:
            gsmall['ffn_a_norm'][0] = jnp.zeros((1, D), F32)
            gs = pack({n: jnp.stack([gsmall[n][k].reshape(w[n].shape[1:]) for k in range(L)]) for n in SMALL})
            small = _xchg_start("small_start", "gather", [gs[0]], [_landing(gs[0])], [])
            deps = deps + [small['token']]
        dh, gsmall['ffn_a_norm'][l] = ffn_backward('a', l, dh, s['xn_a'], s['h0'], s['a_h1'], s['a_h3'], s['wa'],
                                                   'ffn_a_norm', deps)
        deps = []
    grad_x = dh[None]
    head = gsmall['ffn_a_norm'][0].reshape(D // LANE, LANE)
    small_head = _xchg_start("small_head_start", "gather", [head], [_landing(head)], [])

    out = {}
    after = [dh, small_head['token']]
    landed = {}

    def partials(n, l):
        key, i = where[(n, l)]
        if key not in landed:
            landed[key] = _xchg_wait("rs_wait_" + key, rs[key], after)
        return landed[key][i]

    swap = lambda a: a.transpose(0, 2, 1)
    for stage in (3, 2, 1, 0):
        for n in STAGES[stage]:
            parts = [partials(n, l) for l in reversed(range(L))][::-1]
            if n in TRANSPOSED:
                out[n] = [swap(r) for r in _adamw_sum("adamw_" + n, parts, swap(w[n]), swap(m[n]), swap(v[n]))]
            else:
                out[n] = _adamw_sum("adamw_" + n, parts, w[n], m[n], v[n])
            after = [out[n][0]]

    body_parts = _xchg_wait("small_wait", small, after)[0]
    head_parts = _xchg_wait("small_head_wait", small_head, after)[0]
    parts = lax.dynamic_update_slice(body_parts, head_parts, (0, 0, 0))
    res = _adamw_sum("adamw_small", [parts], pack(w), pack(m), pack(v))
    off = 0
    for n, sz in zip(SMALL, sizes):
        out[n] = [r.reshape(-1)[off:off + sz].reshape(w[n].shape) for r in res]
        off += sz

    return (loss, grad_x, *[out[n][0] for n in WEIGHTS], *[out[n][1] for n in WEIGHTS],
            *[out[n][2] for n in WEIGHTS], *[out[n][3] for n in WEIGHTS])


def kernel(x, p, positions, ffn_a_norm, ffn_a_w1, ffn_a_w3, ffn_a_w2, mix_norm, w_in, q_a_norm, w_uq, kv_a_norm, w_ukv, q_norm, k_norm, gm_v_norm, gm_ws, gm_bs, attn_out_norm, gm_out_norm, w_out, ffn_b_norm, ffn_b_w1, ffn_b_w3, ffn_b_w2, ple_gate_norm, w_ple_gate, w_ple, ple_norm, loss_target, m_ffn_a_norm, m_ffn_a_w1, m_ffn_a_w3, m_ffn_a_w2, m_mix_norm, m_w_in, m_q_a_norm, m_w_uq, m_kv_a_norm, m_w_ukv, m_q_norm, m_k_norm, m_gm_v_norm, m_gm_ws, m_gm_bs, m_attn_out_norm, m_gm_out_norm, m_w_out, m_ffn_b_norm, m_ffn_b_w1, m_ffn_b_w3, m_ffn_b_w2, m_ple_gate_norm, m_w_ple_gate, m_w_ple, m_ple_norm, v_ffn_a_norm, v_ffn_a_w1, v_ffn_a_w3, v_ffn_a_w2, v_mix_norm, v_w_in, v_q_a_norm, v_w_uq, v_kv_a_norm, v_w_ukv, v_q_norm, v_k_norm, v_gm_v_norm, v_gm_ws, v_gm_bs, v_attn_out_norm, v_gm_out_norm, v_w_out, v_ffn_b_norm, v_ffn_b_w1, v_ffn_b_w3, v_ffn_b_w2, v_ple_gate_norm, v_w_ple_gate, v_w_ple, v_ple_norm):
    args = locals()
    w = {n: args[n] for n in WEIGHTS}
    m = {n: args["m_" + n] for n in WEIGHTS}
    v = {n: args["v_" + n] for n in WEIGHTS}
    return _step(x, p, positions, loss_target, w, m, v)
```
